```python
import math
import jax, jax.numpy as jnp
from jax import lax
import numpy as np

D_MODEL = 1024
BATCH = 2
SEQ = 8192
DEPTH = 2

CHUNK = 64
Q_BLOCK = 128
N_MIXERS = 2
MEM_LEN = 256
MEM_WIDTH = D_MODEL // 4
MEM_HEADS = 4
MEM_HEAD_DIM = MEM_WIDTH // MEM_HEADS
TOKEN_WIDTH = D_MODEL - MEM_WIDTH
MIX_WIDTH = TOKEN_WIDTH + MEM_WIDTH
DIFF_HEAD_DIM = 64
DIFF_HEADS = TOKEN_WIDTH // (2 * DIFF_HEAD_DIM)
DIFF_PROJ = 3 * TOKEN_WIDTH + MEM_WIDTH
GLA_HEADS = 4
GLA_V_DIM = TOKEN_WIDTH // GLA_HEADS
GLA_K_DIM = GLA_V_DIM // 2
GLA_GATE_RANK = 16
GLA_GATE_TAU = 16.0
GLA_PROJ = 2 * GLA_HEADS * GLA_K_DIM + 2 * TOKEN_WIDTH + GLA_GATE_RANK + MEM_WIDTH
REL_BUCKETS = 32
REL_MAX_DIST = 128
D_FF = ((8 * D_MODEL // 3 + 127) // 128) * 128
CONV_WIDTH = 3
EPS = 1e-6
N_DIFF_LAYERS = (DEPTH + N_MIXERS - 1) // N_MIXERS
N_GLA_LAYERS = DEPTH // N_MIXERS

kernel_name = 'hybrid_diffattn_gla_memxattn_convffn'


def rms_norm(x, g):
    xf = x.astype(jnp.float32)
    y = xf * lax.rsqrt(jnp.mean(xf * xf, axis=-1, keepdims=True) + EPS)
    return (y * g.astype(jnp.float32)).astype(x.dtype)


def t5_bucket(rel):
    half = REL_BUCKETS // 2
    max_exact = half // 2
    ret = jnp.where(rel > 0, half, 0)
    n = jnp.abs(rel)
    nf = jnp.maximum(n, 1).astype(jnp.float32)
    large = max_exact + (jnp.log(nf / max_exact) / math.log(REL_MAX_DIST / max_exact)
                         * (half - max_exact)).astype(jnp.int32)
    large = jnp.minimum(large, half - 1)
    return ret + jnp.where(n < max_exact, n, large)


def diff_attention(q, k, v, qk_norm_g, lam_vecs, out_norm_g, rel_bias, lam_init):
    B, S = q.shape[0], q.shape[1]
    H, d = DIFF_HEADS, DIFF_HEAD_DIM
    nb = S // Q_BLOCK
    q = rms_norm(q, qk_norm_g[0]) * (d ** -0.5)
    k = rms_norm(k, qk_norm_g[1])
    q = q.transpose(0, 2, 3, 1, 4)
    k = k.transpose(0, 2, 3, 1, 4)
    v = v.transpose(0, 2, 1, 3)
    lv = lam_vecs.astype(jnp.float32)
    lam = jnp.exp(jnp.sum(lv[0] * lv[1])) - jnp.exp(jnp.sum(lv[2] * lv[3])) + lam_init
    table = rel_bias.astype(jnp.float32)
    key_pos = jnp.arange(S)
    key_chunk = key_pos // CHUNK
    q_blocks = jnp.moveaxis(q.reshape(B, H, 2, nb, Q_BLOCK, d), 3, 0)

    def attend_block(args):
        qb, b_idx = args
        q_pos = b_idx * Q_BLOCK + jnp.arange(Q_BLOCK)
        logits = jnp.einsum('bhmqd,bhmkd->bhmqk', qb, k).astype(jnp.float32)
        bias = table[t5_bucket(key_pos[None, :] - q_pos[:, None])]
        logits = logits + jnp.transpose(bias, (2, 0, 1))[None, :, None]
        visible = key_chunk[None, :] <= (q_pos // CHUNK)[:, None]
        logits = jnp.where(visible, logits, -jnp.inf)
        p = jax.nn.softmax(logits, axis=-1)
        w = p[:, :, 0] - lam * p[:, :, 1]
        return jnp.einsum('bhqk,bhkv->bhqv', w.astype(v.dtype), v)

    o = lax.map(attend_block, (q_blocks, jnp.arange(nb)))
    o = o.transpose(1, 0, 3, 2, 4).reshape(B, S, H, 2 * d)
    o = rms_norm(o, out_norm_g) * (1.0 - lam_init)
    return o.reshape(B, S, H * 2 * d)


def gla_attention(q, k, v, r, gate_low, gate_w, gate_b, out_norm_g):
    B, S = q.shape[0], q.shape[1]
    H, dk, dv = GLA_HEADS, GLA_K_DIM, GLA_V_DIM
    nc = S // CHUNK
    f32 = jnp.float32
    log_a = jax.nn.log_sigmoid((gate_low @ gate_w + gate_b).astype(f32)) / GLA_GATE_TAU

    def to_chunks(t, dim):
        return t.astype(f32).reshape(B, nc, CHUNK, H, dim).transpose(0, 3, 1, 2, 4)

    qc = to_chunks(q, dk) * (dk ** -0.5)
    kc = to_chunks(k, dk)
    vc = to_chunks(v, dv)
    g = jnp.cumsum(to_chunks(log_a, dk), axis=3)
    g_end = g[:, :, :, -1:, :]
    eg, ieg = jnp.exp(g), jnp.exp(-g)
    a_past = jnp.einsum('bhnqd,bhnkd->bhnqk', qc * eg, kc * ieg)
    a_future = jnp.einsum('bhnqd,bhnkd->bhnqk', qc * ieg, kc * eg)
    t_idx = jnp.arange(CHUNK)
    scores = jnp.where(t_idx[None, :] <= t_idx[:, None], a_past, a_future)
    intra = jnp.einsum('bhnqk,bhnkv->bhnqv', scores, vc)
    kv = jnp.einsum('bhnkd,bhnkv->bhndv', kc * jnp.exp(g_end - g), vc)
    decay = jnp.exp(g_end[:, :, :, 0, :])

    def step(state, inp):
        kv_n, dec_n = inp
        return dec_n[..., None] * state + kv_n, state

    _, s_prev = lax.scan(step, jnp.zeros((B, H, dk, dv), f32),
                         (jnp.moveaxis(kv, 2, 0), jnp.moveaxis(decay, 2, 0)))
    inter = jnp.einsum('bhnqd,nbhdv->bhnqv', qc * eg, s_prev)
    o = (intra + inter).transpose(0, 2, 3, 1, 4).reshape(B, S, H, dv)
    o = rms_norm(o, out_norm_g).reshape(B, S, H * dv)
    return (o * jax.nn.silu(r.astype(f32))).astype(r.dtype)


def memory_cross_attention(q, mem, mem_norm_g, w_kv, qk_norm_g):
    B, S = q.shape[0], q.shape[1]
    M = mem.shape[1]
    q = q.reshape(B, S, MEM_HEADS, MEM_HEAD_DIM)
    kv = rms_norm(mem, mem_norm_g) @ w_kv
    k, v = jnp.split(kv, 2, axis=-1)
    k = k.reshape(B, M, MEM_HEADS, MEM_HEAD_DIM)
    v = v.reshape(B, M, MEM_HEADS, MEM_HEAD_DIM)
    q = rms_norm(q, qk_norm_g[0]) * (MEM_HEAD_DIM ** -0.5)
    k = rms_norm(k, qk_norm_g[1])
    logits = jnp.einsum('bqhd,bkhd->bhqk', q, k).astype(jnp.float32)
    p = jax.nn.softmax(logits, axis=-1)
    o = jnp.einsum('bhqk,bkhd->bqhd', p.astype(v.dtype), v)
    return o.reshape(B, S, MEM_WIDTH)


def conv_ffn(h, w_up, conv_w, conv_b, w_down):
    u = h @ w_up
    c = lax.conv_general_dilated(u, conv_w[:, None, :].astype(u.dtype), window_strides=(1,),
                                 padding=[(CONV_WIDTH - 1, 0)],
                                 dimension_numbers=('NWC', 'WIO', 'NWC'),
                                 feature_group_count=u.shape[-1]) + conv_b
    a, g = jnp.split(c, 2, axis=-1)
    return (jax.nn.silu(g) * a) @ w_down


def setup_inputs(seed: int = 0) -> dict:
    key = jax.random.key(seed)
    ks = jax.random.split(key, 21)
    f32 = jnp.float32

    def nrm(k, shape, scale):
        return jax.random.normal(k, shape, f32) * scale

    def gain(k, shape):
        return 1.0 + 0.05 * jax.random.normal(k, shape, f32)

    return {
        'x': nrm(ks[0], (BATCH, SEQ, D_MODEL), 1.0),
        'mem': nrm(ks[1], (BATCH, MEM_LEN, D_MODEL), 1.0),
        'rel_bias': nrm(ks[2], (REL_BUCKETS, DIFF_HEADS), 0.5),
        'attn_norm': gain(ks[3], (DEPTH, D_MODEL)),
        'ffn_norm': gain(ks[4], (DEPTH, D_MODEL)),
        'mem_norm': gain(ks[5], (DEPTH, D_MODEL)),
        'w_in_diff': nrm(ks[6], (N_DIFF_LAYERS, D_MODEL, DIFF_PROJ), D_MODEL ** -0.5),
        'diff_qk_norm': gain(ks[7], (N_DIFF_LAYERS, 2, DIFF_HEAD_DIM)),
        'diff_lambda': nrm(ks[8], (N_DIFF_LAYERS, 4, DIFF_HEAD_DIM), 0.1),
        'diff_out_norm': gain(ks[9], (N_DIFF_LAYERS, 2 * DIFF_HEAD_DIM)),
        'w_in_gla': nrm(ks[10], (N_GLA_LAYERS, D_MODEL, GLA_PROJ), D_MODEL ** -0.5),
        'gla_gate_w': nrm(ks[11], (N_GLA_LAYERS, GLA_GATE_RANK, GLA_HEADS * GLA_K_DIM), GLA_GATE_RANK ** -0.5),
        'gla_gate_b': nrm(ks[12], (N_GLA_LAYERS, GLA_HEADS * GLA_K_DIM), 0.1),
        'gla_out_norm': gain(ks[13], (N_GLA_LAYERS, GLA_V_DIM)),
        'w_mem_kv': nrm(ks[14], (DEPTH, D_MODEL, 2 * MEM_WIDTH), D_MODEL ** -0.5),
        'mem_qk_norm': gain(ks[15], (DEPTH, 2, MEM_HEAD_DIM)),
        'w_out': nrm(ks[16], (DEPTH, MIX_WIDTH, D_MODEL), MIX_WIDTH ** -0.5),
        'w_up': nrm(ks[17], (DEPTH, D_MODEL, 2 * D_FF), D_MODEL ** -0.5),
        'conv_w': nrm(ks[18], (DEPTH, CONV_WIDTH, 2 * D_FF), CONV_WIDTH ** -0.5),
        'conv_b': nrm(ks[19], (DEPTH, 2 * D_FF), 0.02),
        'w_down': nrm(ks[20], (DEPTH, D_FF, D_MODEL), D_FF ** -0.5),
    }


def reference(x, mem, rel_bias, attn_norm, ffn_norm, mem_norm, w_in_diff, diff_qk_norm,
              diff_lambda, diff_out_norm, w_in_gla, gla_gate_w, gla_gate_b, gla_out_norm,
              w_mem_kv, mem_qk_norm, w_out, w_up, conv_w, conv_b, w_down):
    B, S = x.shape[0], x.shape[1]
    tw = TOKEN_WIDTH
    for i in range(DEPTH):
        h = rms_norm(x, attn_norm[i])
        j = i // N_MIXERS
        if i % N_MIXERS == 0:
            proj = h @ w_in_diff[j]
            q = proj[..., :tw].reshape(B, S, DIFF_HEADS, 2, DIFF_HEAD_DIM)
            k = proj[..., tw:2 * tw].reshape(B, S, DIFF_HEADS, 2, DIFF_HEAD_DIM)
            v = proj[..., 2 * tw:3 * tw].reshape(B, S, DIFF_HEADS, 2 * DIFF_HEAD_DIM)
            mem_q = proj[..., 3 * tw:]
            lam_init = 0.8 - 0.6 * math.exp(-0.3 * i)
            mix = diff_attention(q, k, v, diff_qk_norm[j], diff_lambda[j], diff_out_norm[j],
                                 rel_bias, lam_init)
        else:
            proj = h @ w_in_gla[j]
            kw = GLA_HEADS * GLA_K_DIM
            q = proj[..., :kw]
            k = proj[..., kw:2 * kw]
            v = proj[..., 2 * kw:2 * kw + tw]
            r = proj[..., 2 * kw + tw:2 * kw + 2 * tw]
            gate_low = proj[..., 2 * kw + 2 * tw:2 * kw + 2 * tw + GLA_GATE_RANK]
            mem_q = proj[..., 2 * kw + 2 * tw + GLA_GATE_RANK:]
            mix = gla_attention(q, k, v, r, gate_low, gla_gate_w[j], gla_gate_b[j], gla_out_norm[j])
        cross = memory_cross_attention(mem_q, mem, mem_norm[i], w_mem_kv[i], mem_qk_norm[i])
        x = x + jnp.concatenate([mix.astype(x.dtype), cross.astype(x.dtype)], axis=-1) @ w_out[i]
        h = rms_norm(x, ffn_norm[i])
        x = x + conv_ffn(h, w_up[i], conv_w[i], conv_b[i], w_down[i])
    return x
```

```python
import functools
import math

import jax
import jax.numpy as jnp
from jax import lax
from jax.experimental import pallas as pl
from jax.experimental.pallas import tpu as pltpu

F32 = jnp.float32
BF16 = jnp.bfloat16

D_MODEL = 1024
DEPTH = 2
CHUNK = 64
MEM_WIDTH = D_MODEL // 4
MEM_HEADS = 4
MEM_HEAD_DIM = MEM_WIDTH // MEM_HEADS
TOKEN_WIDTH = D_MODEL - MEM_WIDTH
DIFF_HEAD_DIM = 64
DIFF_HEADS = TOKEN_WIDTH // (2 * DIFF_HEAD_DIM)
GLA_HEADS = 4
GLA_V_DIM = TOKEN_WIDTH // GLA_HEADS
GLA_K_DIM = GLA_V_DIM // 2
GLA_GATE_RANK = 16
GLA_GATE_TAU = 16.0
REL_BUCKETS = 32
REL_MAX_DIST = 128
D_FF = ((8 * D_MODEL // 3 + 127) // 128) * 128
EPS = 1e-6

LANES = 128
MXU_DIM = 256
VMEM_LIMIT_BYTES = 56 * 1024 * 1024

ROW_TILE = 512
ATT_BLK = 256
GLA_TILE = 256
FFN_TILE = 256
FFN_COLS = 256
GLA_KP = 128
GLA_VP = 256
CARRY_ROWS = 8


def _cparams(*sem):
    return pltpu.CompilerParams(dimension_semantics=sem, vmem_limit_bytes=VMEM_LIMIT_BYTES)


def _const_spec(shape):
    n = len(shape)
    return pl.BlockSpec(shape, lambda *_: (0,) * n)


def _rms_rows(x, g):
    ms = jnp.mean(x * x, axis=-1, keepdims=True)
    return x * lax.rsqrt(ms + EPS) * g


def _group_rms(t, gsum, gain):
    cols = []
    for c in range(t.shape[1] // MXU_DIM):
        blk = t[:, c * MXU_DIM:(c + 1) * MXU_DIM]
        ss = jnp.dot((blk * blk).astype(BF16), gsum, preferred_element_type=F32)
        cols.append(blk * lax.rsqrt(ss * (1.0 / 64) + EPS))
    out = cols[0] if len(cols) == 1 else jnp.concatenate(cols, axis=1)
    return out * gain


def _dot(a, b):
    return jnp.dot(a, b, preferred_element_type=F32)


def _dot_nt(a, b):
    return lax.dot_general(a, b, (((1,), (1,)), ((), ())), preferred_element_type=F32)


def _pre_diff_kernel(x_ref, g_ref, w_ref, gsum_ref, qg_ref, kg_ref, mg_ref,
                     q_ref, k_ref, v_ref, mq_ref):
    tw = TOKEN_WIDTH
    h = _rms_rows(x_ref[...], g_ref[...]).astype(BF16)
    gsum = gsum_ref[...]
    q_ref[...] = _group_rms(_dot(h, w_ref[:, 0:tw]), gsum, qg_ref[...]).astype(BF16)
    k_ref[...] = _group_rms(_dot(h, w_ref[:, tw:2 * tw]), gsum, kg_ref[...]).astype(BF16)
    v_ref[...] = _dot(h, w_ref[:, 2 * tw:3 * tw]).astype(BF16)
    mq_ref[...] = _group_rms(_dot(h, w_ref[:, 3 * tw:]), gsum, mg_ref[...]).astype(BF16)


def _pre_diff(x2, g, w, gsum, qg, kg, mg):
    t = x2.shape[0]
    tw = TOKEN_WIDTH
    row = lambda n: pl.BlockSpec((ROW_TILE, n), lambda i: (i, 0))
    return pl.pallas_call(
        _pre_diff_kernel,
        grid=(t // ROW_TILE,),
        in_specs=[row(D_MODEL), _const_spec((1, D_MODEL)), _const_spec(w.shape),
                  _const_spec(gsum.shape), _const_spec((1, tw)), _const_spec((1, tw)),
                  _const_spec((1, MEM_WIDTH))],
        out_specs=[row(tw), row(tw), row(tw), row(MEM_WIDTH)],
        out_shape=[jax.ShapeDtypeStruct((t, tw), BF16)] * 3 + [jax.ShapeDtypeStruct((t, MEM_WIDTH), BF16)],
        compiler_params=_cparams("parallel"),
        name="pre_diff",
    )(x2, g, w, gsum, qg, kg, mg)


def _diff_attn_kernel(lam_ref, qt_ref, k_ref, vt_ref, bias_ref, g_ref, o_ref,
                      m_scr, l_scr, acc_scr, *, out_scale):
    blk = ATT_BLK
    qi = pl.program_id(2)
    qt = qt_ref[0, 0, 0]
    row = lax.broadcasted_iota(jnp.int32, qt.shape, 0)
    zero = jnp.zeros_like(qt)
    qp = jnp.concatenate([jnp.where(row < DIFF_HEAD_DIM, qt, zero),
                          jnp.where(row >= DIFF_HEAD_DIM, qt, zero)], axis=1)

    m_scr[...] = jnp.full(m_scr.shape, -jnp.inf, F32)
    l_scr[...] = jnp.zeros(l_scr.shape, F32)
    acc_scr[...] = jnp.zeros(acc_scr.shape, F32)

    def block(j, bias):
        start = pl.multiple_of(j * blk, blk)
        kb = k_ref[0, pl.ds(start, blk), :]
        s = _dot(kb, qp)
        if bias is not None:
            s = s + jnp.concatenate([bias, bias], axis=1)
        m_old = m_scr[...]
        m_new = jnp.maximum(m_old, jnp.max(s, axis=0, keepdims=True))
        alpha = jnp.exp(m_old - m_new)
        p = jnp.exp(s - m_new)
        l_scr[...] = alpha * l_scr[...] + jnp.sum(p, axis=0, keepdims=True)
        acc_scr[...] = alpha * acc_scr[...] + _dot(vt_ref[0, 0, j], p.astype(BF16))
        m_scr[...] = m_new

    def far(j, c):
        block(j, None)
        return c

    lax.fori_loop(0, jnp.maximum(qi - 1, 0), far, 0)

    @pl.when(qi >= 1)
    def _():
        block(qi - 1, bias_ref[0, 1])

    block(qi, bias_ref[0, 0])

    lam = lam_ref[0, 0]
    acc = acc_scr[...]
    inv = 1.0 / l_scr[...]
    ot = acc[:, :blk] * inv[:, :blk] - lam * (acc[:, blk:] * inv[:, blk:])
    ms = jnp.mean(ot * ot, axis=0, keepdims=True)
    y = (ot * lax.rsqrt(ms + EPS)).T * (g_ref[...] * out_scale)
    o_ref[0] = y.astype(BF16)


def _diff_attn(lam, qt, k, vt, bias, g, out_scale):
    b, nh, nq = qt.shape[0], qt.shape[1], qt.shape[2]
    s = k.shape[1]
    blk = ATT_BLK
    return pl.pallas_call(
        functools.partial(_diff_attn_kernel, out_scale=out_scale),
        grid=(b, nh, nq),
        in_specs=[
            pl.BlockSpec(memory_space=pltpu.SMEM),
            pl.BlockSpec((1, 1, 1, 2 * DIFF_HEAD_DIM, blk), lambda bi, hi, qi: (bi, hi, qi, 0, 0)),
            pl.BlockSpec((1, s, 2 * DIFF_HEAD_DIM), lambda bi, hi, qi: (bi, 0, hi)),
            pl.BlockSpec((1, 1, nq, 2 * DIFF_HEAD_DIM, blk), lambda bi, hi, qi: (bi, hi, 0, 0, 0)),
            pl.BlockSpec((1, 2, blk, blk), lambda bi, hi, qi: (hi, 0, 0, 0)),
            _const_spec((1, 2 * DIFF_HEAD_DIM)),
        ],
        out_specs=pl.BlockSpec((1, blk, 2 * DIFF_HEAD_DIM), lambda bi, hi, qi: (bi, qi, hi)),
        out_shape=jax.ShapeDtypeStruct((b, s, nh * 2 * DIFF_HEAD_DIM), BF16),
        scratch_shapes=[pltpu.VMEM((1, 2 * blk), F32), pltpu.VMEM((1, 2 * blk), F32),
                        pltpu.VMEM((2 * DIFF_HEAD_DIM, 2 * blk), F32)],
        compiler_params=_cparams("parallel", "parallel", "arbitrary"),
        name="diff_attn",
    )(lam, qt, k, vt, bias, g)


def _pre_gla_kernel(x_ref, g_ref, w_ref, gsum_ref, mg_ref, gw_ref, gb_ref, tri_ref,
                    q_ref, k_ref, v_ref, r_ref, gc_ref, mq_ref):
    kw = GLA_HEADS * GLA_KP
    vw = GLA_HEADS * GLA_VP
    h = _rms_rows(x_ref[...], g_ref[...]).astype(BF16)
    q_ref[...] = _dot(h, w_ref[:, 0:kw]).astype(BF16)
    k_ref[...] = _dot(h, w_ref[:, kw:2 * kw]).astype(BF16)
    o = 2 * kw
    v_ref[...] = _dot(h, w_ref[:, o:o + vw]).astype(BF16)
    r_ref[...] = _dot(h, w_ref[:, o + vw:o + 2 * vw]).astype(BF16)
    o = o + 2 * vw
    gate_low = _dot(h, w_ref[:, o:o + LANES]).astype(BF16)
    mq_ref[...] = _group_rms(_dot(h, w_ref[:, o + LANES:]), gsum_ref[...], mg_ref[...]).astype(BF16)
    z = _dot(gate_low, gw_ref[...]) + gb_ref[...]
    log_a = (jnp.minimum(z, 0.0) - jnp.log1p(jnp.exp(-jnp.abs(z)))) * (1.0 / GLA_GATE_TAU)
    hi = log_a.astype(BF16)
    rem = log_a - hi.astype(F32)
    mid = rem.astype(BF16)
    lo = (rem - mid.astype(F32)).astype(BF16)
    tri = tri_ref[...]
    n = tri.shape[0]
    for c in range(log_a.shape[0] // n):
        rows = slice(c * n, (c + 1) * n)
        gc_ref[rows, :] = _dot(tri, hi[rows]) + _dot(tri, mid[rows]) + _dot(tri, lo[rows])


def _pre_gla(x2, g, w, gsum, mg, gw, gb, tri):
    t = x2.shape[0]
    kw = GLA_HEADS * GLA_KP
    vw = GLA_HEADS * GLA_VP
    row = lambda n: pl.BlockSpec((ROW_TILE, n), lambda i: (i, 0))
    return pl.pallas_call(
        _pre_gla_kernel,
        grid=(t // ROW_TILE,),
        in_specs=[row(D_MODEL), _const_spec((1, D_MODEL)), _const_spec(w.shape),
                  _const_spec(gsum.shape), _const_spec((1, MEM_WIDTH)), _const_spec(gw.shape),
                  _const_spec(gb.shape), _const_spec(tri.shape)],
        out_specs=[row(kw), row(kw), row(vw), row(vw), row(kw), row(MEM_WIDTH)],
        out_shape=[jax.ShapeDtypeStruct((t, kw), BF16), jax.ShapeDtypeStruct((t, kw), BF16),
                   jax.ShapeDtypeStruct((t, vw), BF16), jax.ShapeDtypeStruct((t, vw), BF16),
                   jax.ShapeDtypeStruct((t, kw), F32), jax.ShapeDtypeStruct((t, MEM_WIDTH), BF16)],
        compiler_params=_cparams("parallel"),
        name="pre_gla",
    )(x2, g, w, gsum, mg, gw, gb, tri)


def _gla_kernel(q_ref, k_ref, v_ref, r_ref, gc_ref, gain_ref, o_ref, s_scr):
    tg = GLA_TILE
    nchunk = tg // CHUNK

    @pl.when(pl.program_id(1) == 0)
    def _():
        s_scr[...] = jnp.zeros(s_scr.shape, F32)

    ri = lax.broadcasted_iota(jnp.int32, (tg, tg), 0)
    ci = lax.broadcasted_iota(jnp.int32, (tg, tg), 1)
    same_chunk = (ri // CHUNK) == (ci // CHUNK)
    past = ci <= ri
    row_chunk = lax.broadcasted_iota(jnp.int32, (tg, GLA_KP), 0) // CHUNK

    for h in range(GLA_HEADS):
        ks = slice(h * GLA_KP, (h + 1) * GLA_KP)
        vs = slice(h * GLA_VP, (h + 1) * GLA_VP)
        qh = q_ref[0, :, ks].astype(F32) * (GLA_K_DIM ** -0.5)
        kh = k_ref[0, :, ks].astype(F32)
        g = gc_ref[0, :, ks]
        vh = v_ref[0, :, vs]
        eg = jnp.exp(g)
        ieg = jnp.exp(-g)
        qe = (qh * eg).astype(BF16)
        a_past = _dot_nt(qe, (kh * ieg).astype(BF16))
        a_fut = _dot_nt((qh * ieg).astype(BF16), (kh * eg).astype(BF16))
        scores = jnp.where(same_chunk, jnp.where(past, a_past, a_fut), 0.0)
        o = _dot(scores.astype(BF16), vh)
        vt = vh.astype(F32).T.astype(BF16)
        g_end = jnp.concatenate(
            [jnp.broadcast_to(g[c * CHUNK + CHUNK - 1:c * CHUNK + CHUNK, :], (CHUNK, GLA_KP))
             for c in range(nchunk)], axis=0)
        kdec = kh * jnp.exp(g_end - g)
        inter = []
        for c in range(nchunk):
            rows = slice(c * CHUNK, (c + 1) * CHUNK)
            st = s_scr[h]
            inter.append(_dot_nt(qe[rows], st.astype(BF16)))
            kd_c = jnp.where(row_chunk == c, kdec, 0.0).astype(BF16)
            s_scr[h] = st * jnp.exp(g_end[rows][0:1, :]) + _dot(vt, kd_c)
        o = o + jnp.concatenate(inter, axis=0)
        ms = jnp.sum(o * o, axis=-1, keepdims=True) * (1.0 / GLA_V_DIM)
        y = o * lax.rsqrt(ms + EPS) * gain_ref[:, vs]
        rh = r_ref[0, :, vs].astype(F32)
        o_ref[0, :, vs] = (y * (rh / (1.0 + jnp.exp(-rh)))).astype(BF16)


def _gla(q, k, v, r, gc, gain):
    b, s = q.shape[0], q.shape[1]
    kw = GLA_HEADS * GLA_KP
    vw = GLA_HEADS * GLA_VP
    spec = lambda n: pl.BlockSpec((1, GLA_TILE, n), lambda bi, i: (bi, i, 0))
    return pl.pallas_call(
        _gla_kernel,
        grid=(b, s // GLA_TILE),
        in_specs=[spec(kw), spec(kw), spec(vw), spec(vw), spec(kw), _const_spec((1, vw))],
        out_specs=spec(vw),
        out_shape=jax.ShapeDtypeStruct((b, s, vw), BF16),
        scratch_shapes=[pltpu.VMEM((GLA_HEADS, GLA_VP, GLA_KP), F32)],
        compiler_params=_cparams("parallel", "arbitrary"),
        name="gla",
    )(q, k, v, r, gc, gain)


def _mem_kv_kernel(mem_ref, g_ref, w_ref, gsum_ref, kg_ref, k_ref, v_ref):
    h = _rms_rows(mem_ref[...], g_ref[...]).astype(BF16)
    k_ref[...] = _group_rms(_dot(h, w_ref[:, :MEM_WIDTH]), gsum_ref[...], kg_ref[...]).astype(BF16)
    v_ref[...] = _dot(h, w_ref[:, MEM_WIDTH:]).astype(BF16)


def _mem_kv(mem2, g, w, gsum, kg):
    n = mem2.shape[0]
    return pl.pallas_call(
        _mem_kv_kernel,
        grid=(1,),
        in_specs=[_const_spec(mem2.shape), _const_spec((1, D_MODEL)), _const_spec(w.shape),
                  _const_spec(gsum.shape), _const_spec((1, MEM_WIDTH))],
        out_specs=[_const_spec((n, MEM_WIDTH)), _const_spec((n, MEM_WIDTH))],
        out_shape=[jax.ShapeDtypeStruct((n, MEM_WIDTH), BF16)] * 2,
        compiler_params=_cparams("arbitrary"),
        name="mem_kv",
    )(mem2, g, w, gsum, kg)


def _mix_out_kernel(x_ref, mix_ref, mq_ref, kbd_ref, vbd_ref, wa_ref, wb_ref, o_ref):
    m = kbd_ref.shape[2] // MEM_HEADS
    logits = _dot(mq_ref[0], kbd_ref[0])
    ps = []
    for h in range(MEM_HEADS):
        s = logits[:, h * m:(h + 1) * m]
        e = jnp.exp(s - jnp.max(s, axis=-1, keepdims=True))
        ps.append((e * (1.0 / jnp.sum(e, axis=-1, keepdims=True))).astype(BF16))
    cross = _dot(jnp.concatenate(ps, axis=1), vbd_ref[0])
    o_ref[0] = x_ref[0] + _dot(mix_ref[0], wa_ref[...]) + _dot(cross.astype(BF16), wb_ref[...])


def _mix_out(x, mix, mq, kbd, vbd, wa, wb):
    b, s = x.shape[0], x.shape[1]
    spec = lambda n: pl.BlockSpec((1, ROW_TILE, n), lambda bi, i: (bi, i, 0))
    per_b = lambda a: pl.BlockSpec((1,) + a.shape[1:], lambda bi, i: (bi, 0, 0))
    return pl.pallas_call(
        _mix_out_kernel,
        grid=(b, s // ROW_TILE),
        in_specs=[spec(D_MODEL), spec(mix.shape[2]), spec(MEM_WIDTH), per_b(kbd), per_b(vbd),
                  _const_spec(wa.shape), _const_spec(wb.shape)],
        out_specs=spec(D_MODEL),
        out_shape=jax.ShapeDtypeStruct(x.shape, F32),
        compiler_params=_cparams("parallel", "parallel"),
        name="mix_out",
    )(x, mix, mq, kbd, vbd, wa, wb)


def _ffn_kernel(x_ref, g_ref, wu_ref, cw_ref, cb_ref, wd_ref, o_ref, u_scr, act_scr):
    tm = FFN_TILE
    cr = CARRY_ROWS

    @pl.when(pl.program_id(1) == 0)
    def _():
        u_scr[0:cr, :] = jnp.zeros((cr, u_scr.shape[1]), F32)

    x = x_ref[0]
    h = _rms_rows(x, g_ref[...]).astype(BF16)

    def conv(cols):
        u_scr[cr:cr + tm, cols] = _dot(h, wu_ref[:, cols])
        c = (cw_ref[0:1, cols] * u_scr[cr - 2:cr - 2 + tm, cols]
             + cw_ref[1:2, cols] * u_scr[cr - 1:cr - 1 + tm, cols]
             + cw_ref[2:3, cols] * u_scr[cr:cr + tm, cols]
             + cb_ref[:, cols])
        u_scr[0:cr, cols] = u_scr[tm:tm + cr, cols]
        return c

    for j in range(D_FF // FFN_COLS):
        a = conv(slice(j * FFN_COLS, (j + 1) * FFN_COLS))
        gte = conv(slice(D_FF + j * FFN_COLS, D_FF + (j + 1) * FFN_COLS))
        act_scr[:, j * FFN_COLS:(j + 1) * FFN_COLS] = (a * (gte / (1.0 + jnp.exp(-gte)))).astype(BF16)

    o_ref[0] = x + _dot(act_scr[...], wd_ref[...])


def _ffn(x, g, wu, cw, cb, wd):
    b, s = x.shape[0], x.shape[1]
    spec = pl.BlockSpec((1, FFN_TILE, D_MODEL), lambda bi, i: (bi, i, 0))
    single = lambda a: pl.BlockSpec(a.shape, lambda bi, i: (0,) * a.ndim, pipeline_mode=pl.Buffered(1))
    return pl.pallas_call(
        _ffn_kernel,
        grid=(b, s // FFN_TILE),
        in_specs=[spec, _const_spec((1, D_MODEL)), single(wu), _const_spec(cw.shape),
                  _const_spec(cb.shape), single(wd)],
        out_specs=spec,
        out_shape=jax.ShapeDtypeStruct(x.shape, F32),
        scratch_shapes=[pltpu.VMEM((FFN_TILE + CARRY_ROWS, 2 * D_FF), F32),
                        pltpu.VMEM((FFN_TILE, D_FF), BF16)],
        compiler_params=_cparams("parallel", "arbitrary"),
        name="ffn",
    )(x, g, wu, cw, cb, wd)


def _t5_bucket(rel):
    half = REL_BUCKETS // 2
    max_exact = half // 2
    ret = jnp.where(rel > 0, half, 0)
    n = jnp.abs(rel)
    nf = jnp.maximum(n, 1).astype(jnp.float32)
    large = max_exact + (jnp.log(nf / max_exact) / math.log(REL_MAX_DIST / max_exact)
                         * (half - max_exact)).astype(jnp.int32)
    large = jnp.minimum(large, half - 1)
    return ret + jnp.where(n < max_exact, n, large)


def _bias_tiles(rel_bias):
    blk = ATT_BLK
    assert blk >= REL_MAX_DIST
    table = rel_bias.astype(F32)
    kq = jnp.arange(blk)[:, None] - jnp.arange(blk)[None, :]
    far = table[_t5_bucket(jnp.full((1, 1), -2 * blk))]
    diag = table[_t5_bucket(kq)] - far
    near = table[_t5_bucket(kq - blk)] - far
    visible = (jnp.arange(blk)[:, None] // CHUNK) <= (jnp.arange(blk)[None, :] // CHUNK)
    diag = jnp.where(visible[:, :, None], diag, -jnp.inf)
    return jnp.stack([diag, near], axis=0).transpose(3, 0, 1, 2)


def _group_sum_matrix():
    i = jnp.arange(MXU_DIM)
    return ((i[:, None] // 64) == (i[None, :] // 64)).astype(BF16)


def _chunk_tri_matrix(n):
    i = jnp.arange(n)
    return (((i[:, None] // CHUNK) == (i[None, :] // CHUNK)) & (i[None, :] <= i[:, None])).astype(BF16)


def _pad_heads(w, heads, dim, pad, axis):
    shape = list(w.shape)
    shape[axis:axis + 1] = [heads, dim]
    w = w.reshape(shape)
    widths = [(0, 0)] * w.ndim
    widths[axis + 1] = (0, pad - dim)
    w = jnp.pad(w, widths)
    shape[axis:axis + 2] = [heads * pad]
    return w.reshape(shape)


def _tile_gain(g, reps, scale=1.0):
    return (jnp.tile(g.astype(F32), reps) * scale)[None, :]


def _mem_block_diag(kn, v, b):
    m = kn.shape[0] // b
    eye = jnp.eye(MEM_HEADS, dtype=BF16)
    knt = kn.reshape(b, m, MEM_WIDTH).transpose(0, 2, 1)
    kbd = (knt.reshape(b, MEM_HEADS, MEM_HEAD_DIM, 1, m) * eye.reshape(1, MEM_HEADS, 1, MEM_HEADS, 1))
    kbd = kbd.reshape(b, MEM_WIDTH, MEM_HEADS * m)
    vbd = (v.reshape(b, 1, m, MEM_HEADS, MEM_HEAD_DIM) * eye.reshape(1, MEM_HEADS, 1, MEM_HEADS, 1))
    vbd = vbd.reshape(b, MEM_HEADS * m, MEM_WIDTH)
    return kbd, vbd


def kernel(x, mem, rel_bias, attn_norm, ffn_norm, mem_norm, w_in_diff, diff_qk_norm, diff_lambda,
           diff_out_norm, w_in_gla, gla_gate_w, gla_gate_b, gla_out_norm, w_mem_kv, mem_qk_norm,
           w_out, w_up, conv_w, conv_b, w_down):
    b, s, d = x.shape
    t = b * s
    tw = TOKEN_WIDTH
    gsum = _group_sum_matrix()
    mem2 = mem.reshape(b * mem.shape[1], d)
    x = x.astype(F32)

    for i in range(DEPTH):
        j = i // 2
        x2 = x.reshape(t, d)
        mq_gain = _tile_gain(mem_qk_norm[i, 0], MEM_HEADS, MEM_HEAD_DIM ** -0.5)
        if i % 2 == 0:
            nh, hd = DIFF_HEADS, DIFF_HEAD_DIM
            q, k, v, mq = _pre_diff(
                x2, attn_norm[i][None, :], w_in_diff[j].astype(BF16), gsum,
                _tile_gain(diff_qk_norm[j, 0], 2 * nh, hd ** -0.5),
                _tile_gain(diff_qk_norm[j, 1], 2 * nh), mq_gain)
            nq = s // ATT_BLK
            qt = q.reshape(b, nq, ATT_BLK, nh, 2 * hd).transpose(0, 3, 1, 4, 2)
            vt = v.reshape(b, nq, ATT_BLK, nh, 2 * hd).transpose(0, 3, 1, 4, 2)
            lv = diff_lambda[j].astype(F32)
            lam_init = 0.8 - 0.6 * math.exp(-0.3 * i)
            lam = jnp.exp(jnp.sum(lv[0] * lv[1])) - jnp.exp(jnp.sum(lv[2] * lv[3])) + lam_init
            mix = _diff_attn(lam.reshape(1, 1), qt, k.reshape(b, s, tw), vt, _bias_tiles(rel_bias),
                             diff_out_norm[j].astype(F32)[None, :], 1.0 - lam_init)
            w_mix = w_out[i, :tw]
        else:
            kw = GLA_HEADS * GLA_K_DIM
            w = w_in_gla[j]
            hp = functools.partial(_pad_heads, heads=GLA_HEADS, axis=1)
            w_p = jnp.concatenate([
                hp(w[:, :kw], dim=GLA_K_DIM, pad=GLA_KP),
                hp(w[:, kw:2 * kw], dim=GLA_K_DIM, pad=GLA_KP),
                hp(w[:, 2 * kw:2 * kw + tw], dim=GLA_V_DIM, pad=GLA_VP),
                hp(w[:, 2 * kw + tw:2 * kw + 2 * tw], dim=GLA_V_DIM, pad=GLA_VP),
                jnp.pad(w[:, 2 * kw + 2 * tw:2 * kw + 2 * tw + GLA_GATE_RANK],
                        ((0, 0), (0, LANES - GLA_GATE_RANK))),
                w[:, 2 * kw + 2 * tw + GLA_GATE_RANK:]], axis=1).astype(BF16)
            gw = jnp.pad(hp(gla_gate_w[j], dim=GLA_K_DIM, pad=GLA_KP),
                         ((0, LANES - GLA_GATE_RANK), (0, 0))).astype(BF16)
            gb = _pad_heads(gla_gate_b[j].astype(F32)[None, :], GLA_HEADS, GLA_K_DIM, GLA_KP, 1)
            q, k, v, r, gc, mq = _pre_gla(x2, attn_norm[i][None, :], w_p, gsum, mq_gain, gw, gb,
                                          _chunk_tri_matrix(MXU_DIM))
            gain = _pad_heads(jnp.tile(gla_out_norm[j].astype(F32), GLA_HEADS)[None, :],
                              GLA_HEADS, GLA_V_DIM, GLA_VP, 1)
            sh = lambda a: a.reshape(b, s, a.shape[1])
            mix = _gla(sh(q), sh(k), sh(v), sh(r), sh(gc), gain)
            w_mix = _pad_heads(w_out[i, :tw], GLA_HEADS, GLA_V_DIM, GLA_VP, 0)

        kn, vm = _mem_kv(mem2, mem_norm[i][None, :], w_mem_kv[i].astype(BF16), gsum,
                         _tile_gain(mem_qk_norm[i, 1], MEM_HEADS))
        kbd, vbd = _mem_block_diag(kn, vm, b)
        x = _mix_out(x, mix.reshape(b, s, -1), mq.reshape(b, s, MEM_WIDTH), kbd, vbd,
                     w_mix.astype(BF16), w_out[i, tw:].astype(BF16))
        x = _ffn(x, ffn_norm[i][None, :], w_up[i].astype(BF16), conv_w[i].astype(F32),
                 conv_b[i].astype(F32)[None, :], w_down[i].astype(BF16))
    return x
```

```python
import functools
import math

import jax
import jax.numpy as jnp
from jax import lax
from jax.experimental import pallas as pl
from jax.experimental.pallas import tpu as pltpu

F32 = jnp.float32
BF16 = jnp.bfloat16

D_MODEL = 1024
DEPTH = 2
CHUNK = 64
MEM_WIDTH = D_MODEL // 4
MEM_HEADS = 4
MEM_HEAD_DIM = MEM_WIDTH // MEM_HEADS
TOKEN_WIDTH = D_MODEL - MEM_WIDTH
DIFF_HEAD_DIM = 64
DIFF_HEADS = TOKEN_WIDTH // (2 * DIFF_HEAD_DIM)
GLA_HEADS = 4
GLA_V_DIM = TOKEN_WIDTH // GLA_HEADS
GLA_K_DIM = GLA_V_DIM // 2
GLA_GATE_RANK = 16
GLA_GATE_TAU = 16.0
REL_BUCKETS = 32
REL_MAX_DIST = 128
D_FF = ((8 * D_MODEL // 3 + 127) // 128) * 128
EPS = 1e-6
LOG2E = math.log2(math.e)

LANES = 128
MXU_DIM = 256
VMEM_LIMIT_BYTES = 56 * 1024 * 1024

ROW_TILE = 512
ATT_BLK = 512
GLA_TILE = 256
FFN_TILE = 256
FFN_COLS = 256
GLA_KP = 128
GLA_VP = 256
CARRY_ROWS = 8


def _cparams(*sem):
    return pltpu.CompilerParams(dimension_semantics=sem, vmem_limit_bytes=VMEM_LIMIT_BYTES)


def _const_spec(shape):
    n = len(shape)
    return pl.BlockSpec(shape, lambda *_: (0,) * n)


def _rms_rows(x, g):
    ms = jnp.mean(x * x, axis=-1, keepdims=True)
    return x * lax.rsqrt(ms + EPS) * g


def _group_rms(t, gsum, gain):
    cols = []
    for c in range(t.shape[1] // MXU_DIM):
        blk = t[:, c * MXU_DIM:(c + 1) * MXU_DIM]
        ss = jnp.dot((blk * blk).astype(BF16), gsum, preferred_element_type=F32)
        cols.append(blk * lax.rsqrt(ss * (1.0 / 64) + EPS))
    out = cols[0] if len(cols) == 1 else jnp.concatenate(cols, axis=1)
    return out * gain


def _dot(a, b):
    return jnp.dot(a, b, preferred_element_type=F32)


def _dot_nt(a, b):
    return lax.dot_general(a, b, (((1,), (1,)), ((), ())), preferred_element_type=F32)


def _pre_diff_kernel(x_ref, g_ref, w_ref, gsum_ref, qg_ref, kg_ref, mg_ref,
                     q_ref, k_ref, v_ref, mq_ref):
    tw = TOKEN_WIDTH
    h = _rms_rows(x_ref[...], g_ref[...]).astype(BF16)
    gsum = gsum_ref[...]
    q_ref[...] = _group_rms(_dot(h, w_ref[:, 0:tw]), gsum, qg_ref[...]).astype(BF16)
    k_ref[...] = _group_rms(_dot(h, w_ref[:, tw:2 * tw]), gsum, kg_ref[...]).astype(BF16)
    v_ref[...] = _dot(h, w_ref[:, 2 * tw:3 * tw]).astype(BF16)
    mq_ref[...] = _group_rms(_dot(h, w_ref[:, 3 * tw:]), gsum, mg_ref[...]).astype(BF16)


def _pre_diff(x2, g, w, gsum, qg, kg, mg):
    t = x2.shape[0]
    tw = TOKEN_WIDTH
    row = lambda n: pl.BlockSpec((ROW_TILE, n), lambda i: (i, 0))
    return pl.pallas_call(
        _pre_diff_kernel,
        grid=(t // ROW_TILE,),
        in_specs=[row(D_MODEL), _const_spec((1, D_MODEL)), _const_spec(w.shape),
                  _const_spec(gsum.shape), _const_spec((1, tw)), _const_spec((1, tw)),
                  _const_spec((1, MEM_WIDTH))],
        out_specs=[row(tw), row(tw), row(tw), row(MEM_WIDTH)],
        out_shape=[jax.ShapeDtypeStruct((t, tw), BF16)] * 3 + [jax.ShapeDtypeStruct((t, MEM_WIDTH), BF16)],
        compiler_params=_cparams("parallel"),
        name="pre_diff",
    )(x2, g, w, gsum, qg, kg, mg)


def _diff_attn_kernel(lam_ref, qt_ref, k_ref, vt_ref, bias_ref, g_ref, o_ref,
                      m_scr, l_scr, acc_scr, *, out_scale):
    blk = ATT_BLK
    qi = pl.program_id(2)
    qt = qt_ref[0, 0, 0]
    row = lax.broadcasted_iota(jnp.int32, qt.shape, 0)
    zero = jnp.zeros_like(qt)
    qp = jnp.concatenate([jnp.where(row < DIFF_HEAD_DIM, qt, zero),
                          jnp.where(row >= DIFF_HEAD_DIM, qt, zero)], axis=1)

    m_scr[...] = jnp.full(m_scr.shape, -jnp.inf, F32)
    l_scr[...] = jnp.zeros(l_scr.shape, F32)
    acc_scr[...] = jnp.zeros(acc_scr.shape, F32)

    def block(j, bias):
        start = pl.multiple_of(j * blk, blk)
        kb = k_ref[0, pl.ds(start, blk), :]
        s = _dot(kb, qp)
        if bias is not None:
            s = s + jnp.concatenate([bias, bias], axis=1)
        m_old = m_scr[...]
        m_new = jnp.maximum(m_old, jnp.max(s, axis=0, keepdims=True))
        alpha = jnp.exp2(m_old - m_new)
        p = jnp.exp2(s - m_new)
        l_scr[...] = alpha * l_scr[...] + jnp.sum(p, axis=0, keepdims=True)
        acc_scr[...] = alpha * acc_scr[...] + _dot(vt_ref[0, 0, j], p.astype(BF16))
        m_scr[...] = m_new

    def far(j, c):
        block(j, None)
        return c

    lax.fori_loop(0, jnp.maximum(qi - 1, 0), far, 0)

    @pl.when(qi >= 1)
    def _():
        block(qi - 1, bias_ref[0, 1])

    block(qi, bias_ref[0, 0])

    lam = lam_ref[0, 0]
    acc = acc_scr[...]
    inv = 1.0 / l_scr[...]
    ot = acc[:, :blk] * inv[:, :blk] - lam * (acc[:, blk:] * inv[:, blk:])
    ms = jnp.mean(ot * ot, axis=0, keepdims=True)
    y = (ot * lax.rsqrt(ms + EPS)).T * (g_ref[...] * out_scale)
    o_ref[0] = y.astype(BF16)


def _diff_attn(lam, qt, k, vt, bias, g, out_scale):
    b, nh, nq = qt.shape[0], qt.shape[1], qt.shape[2]
    s = k.shape[1]
    blk = ATT_BLK
    return pl.pallas_call(
        functools.partial(_diff_attn_kernel, out_scale=out_scale),
        grid=(b, nh, nq),
        in_specs=[
            pl.BlockSpec(memory_space=pltpu.SMEM),
            pl.BlockSpec((1, 1, 1, 2 * DIFF_HEAD_DIM, blk), lambda bi, hi, qi: (bi, hi, qi, 0, 0)),
            pl.BlockSpec((1, s, 2 * DIFF_HEAD_DIM), lambda bi, hi, qi: (bi, 0, hi)),
            pl.BlockSpec((1, 1, nq, 2 * DIFF_HEAD_DIM, blk), lambda bi, hi, qi: (bi, hi, 0, 0, 0)),
            pl.BlockSpec((1, 2, blk, blk), lambda bi, hi, qi: (hi, 0, 0, 0)),
            _const_spec((1, 2 * DIFF_HEAD_DIM)),
        ],
        out_specs=pl.BlockSpec((1, blk, 2 * DIFF_HEAD_DIM), lambda bi, hi, qi: (bi, qi, hi)),
        out_shape=jax.ShapeDtypeStruct((b, s, nh * 2 * DIFF_HEAD_DIM), BF16),
        scratch_shapes=[pltpu.VMEM((1, 2 * blk), F32), pltpu.VMEM((1, 2 * blk), F32),
                        pltpu.VMEM((2 * DIFF_HEAD_DIM, 2 * blk), F32)],
        compiler_params=_cparams("parallel", "parallel", "arbitrary"),
        name="diff_attn",
    )(lam, qt, k, vt, bias, g)


def _pre_gla_kernel(x_ref, g_ref, w_ref, gsum_ref, mg_ref, gw_ref, gb_ref, tri_ref,
                    q_ref, k_ref, v_ref, r_ref, gc_ref, mq_ref):
    kw = GLA_HEADS * GLA_KP
    vw = GLA_HEADS * GLA_VP
    h = _rms_rows(x_ref[...], g_ref[...]).astype(BF16)
    q_ref[...] = _dot(h, w_ref[:, 0:kw]).astype(BF16)
    k_ref[...] = _dot(h, w_ref[:, kw:2 * kw]).astype(BF16)
    o = 2 * kw
    v_ref[...] = _dot(h, w_ref[:, o:o + vw]).astype(BF16)
    r_ref[...] = _dot(h, w_ref[:, o + vw:o + 2 * vw]).astype(BF16)
    o = o + 2 * vw
    gate_low = _dot(h, w_ref[:, o:o + LANES]).astype(BF16)
    mq_ref[...] = _group_rms(_dot(h, w_ref[:, o + LANES:]), gsum_ref[...], mg_ref[...]).astype(BF16)
    z = _dot(gate_low, gw_ref[...]) + gb_ref[...]
    log_a = (jnp.minimum(z, 0.0) - jnp.log1p(jnp.exp(-jnp.abs(z)))) * (1.0 / GLA_GATE_TAU)
    hi = log_a.astype(BF16)
    rem = log_a - hi.astype(F32)
    mid = rem.astype(BF16)
    lo = (rem - mid.astype(F32)).astype(BF16)
    tri = tri_ref[...]
    n = tri.shape[0]
    for c in range(log_a.shape[0] // n):
        rows = slice(c * n, (c + 1) * n)
        gc_ref[rows, :] = _dot(tri, hi[rows]) + _dot(tri, mid[rows]) + _dot(tri, lo[rows])


def _pre_gla(x2, g, w, gsum, mg, gw, gb, tri):
    t = x2.shape[0]
    kw = GLA_HEADS * GLA_KP
    vw = GLA_HEADS * GLA_VP
    row = lambda n: pl.BlockSpec((ROW_TILE, n), lambda i: (i, 0))
    return pl.pallas_call(
        _pre_gla_kernel,
        grid=(t // ROW_TILE,),
        in_specs=[row(D_MODEL), _const_spec((1, D_MODEL)), _const_spec(w.shape),
                  _const_spec(gsum.shape), _const_spec((1, MEM_WIDTH)), _const_spec(gw.shape),
                  _const_spec(gb.shape), _const_spec(tri.shape)],
        out_specs=[row(kw), row(kw), row(vw), row(vw), row(kw), row(MEM_WIDTH)],
        out_shape=[jax.ShapeDtypeStruct((t, kw), BF16), jax.ShapeDtypeStruct((t, kw), BF16),
                   jax.ShapeDtypeStruct((t, vw), BF16), jax.ShapeDtypeStruct((t, vw), BF16),
                   jax.ShapeDtypeStruct((t, kw), F32), jax.ShapeDtypeStruct((t, MEM_WIDTH), BF16)],
        compiler_params=_cparams("parallel"),
        name="pre_gla",
    )(x2, g, w, gsum, mg, gw, gb, tri)


def _gla_kernel(q_ref, k_ref, v_ref, r_ref, gc_ref, gain_ref, o_ref, s_scr):
    tg = GLA_TILE
    nchunk = tg // CHUNK

    @pl.when(pl.program_id(1) == 0)
    def _():
        s_scr[...] = jnp.zeros(s_scr.shape, F32)

    ri = lax.broadcasted_iota(jnp.int32, (tg, tg), 0)
    ci = lax.broadcasted_iota(jnp.int32, (tg, tg), 1)
    same_chunk = (ri // CHUNK) == (ci // CHUNK)
    past = ci <= ri
    row_chunk = lax.broadcasted_iota(jnp.int32, (tg, GLA_KP), 0) // CHUNK

    for h in range(GLA_HEADS):
        ks = slice(h * GLA_KP, (h + 1) * GLA_KP)
        vs = slice(h * GLA_VP, (h + 1) * GLA_VP)
        qh = q_ref[0, :, ks].astype(F32) * (GLA_K_DIM ** -0.5)
        kh = k_ref[0, :, ks].astype(F32)
        g = gc_ref[0, :, ks]
        vh = v_ref[0, :, vs]
        eg = jnp.exp(g)
        ieg = jnp.exp(-g)
        qe = (qh * eg).astype(BF16)
        a_past = _dot_nt(qe, (kh * ieg).astype(BF16))
        a_fut = _dot_nt((qh * ieg).astype(BF16), (kh * eg).astype(BF16))
        scores = jnp.where(same_chunk, jnp.where(past, a_past, a_fut), 0.0)
        o = _dot(scores.astype(BF16), vh)
        vt = vh.astype(F32).T.astype(BF16)
        g_end = jnp.concatenate(
            [jnp.broadcast_to(g[c * CHUNK + CHUNK - 1:c * CHUNK + CHUNK, :], (CHUNK, GLA_KP))
             for c in range(nchunk)], axis=0)
        kdec = kh * jnp.exp(g_end - g)
        inter = []
        for c in range(nchunk):
            rows = slice(c * CHUNK, (c + 1) * CHUNK)
            st = s_scr[h]
            inter.append(_dot_nt(qe[rows], st.astype(BF16)))
            kd_c = jnp.where(row_chunk == c, kdec, 0.0).astype(BF16)
            s_scr[h] = st * jnp.exp(g_end[rows][0:1, :]) + _dot(vt, kd_c)
        o = o + jnp.concatenate(inter, axis=0)
        ms = jnp.sum(o * o, axis=-1, keepdims=True) * (1.0 / GLA_V_DIM)
        y = o * lax.rsqrt(ms + EPS) * gain_ref[:, vs]
        rh = r_ref[0, :, vs].astype(F32)
        o_ref[0, :, vs] = (y * (rh / (1.0 + jnp.exp(-rh)))).astype(BF16)


def _gla(q, k, v, r, gc, gain):
    b, s = q.shape[0], q.shape[1]
    kw = GLA_HEADS * GLA_KP
    vw = GLA_HEADS * GLA_VP
    spec = lambda n: pl.BlockSpec((1, GLA_TILE, n), lambda bi, i: (bi, i, 0))
    return pl.pallas_call(
        _gla_kernel,
        grid=(b, s // GLA_TILE),
        in_specs=[spec(kw), spec(kw), spec(vw), spec(vw), spec(kw), _const_spec((1, vw))],
        out_specs=spec(vw),
        out_shape=jax.ShapeDtypeStruct((b, s, vw), BF16),
        scratch_shapes=[pltpu.VMEM((GLA_HEADS, GLA_VP, GLA_KP), F32)],
        compiler_params=_cparams("parallel", "arbitrary"),
        name="gla",
    )(q, k, v, r, gc, gain)


def _mem_kv_kernel(mem_ref, g_ref, w_ref, gsum_ref, kg_ref, k_ref, v_ref):
    h = _rms_rows(mem_ref[...], g_ref[...]).astype(BF16)
    k_ref[...] = _group_rms(_dot(h, w_ref[:, :MEM_WIDTH]), gsum_ref[...], kg_ref[...]).astype(BF16)
    v_ref[...] = _dot(h, w_ref[:, MEM_WIDTH:]).astype(BF16)


def _mem_kv(mem2, g, w, gsum, kg):
    n = mem2.shape[0]
    return pl.pallas_call(
        _mem_kv_kernel,
        grid=(1,),
        in_specs=[_const_spec(mem2.shape), _const_spec((1, D_MODEL)), _const_spec(w.shape),
                  _const_spec(gsum.shape), _const_spec((1, MEM_WIDTH))],
        out_specs=[_const_spec((n, MEM_WIDTH)), _const_spec((n, MEM_WIDTH))],
        out_shape=[jax.ShapeDtypeStruct((n, MEM_WIDTH), BF16)] * 2,
        compiler_params=_cparams("arbitrary"),
        name="mem_kv",
    )(mem2, g, w, gsum, kg)


def _mix_out_kernel(x_ref, mix_ref, mq_ref, kbd_ref, vbd_ref, wa_ref, wb_ref, o_ref):
    m = kbd_ref.shape[2] // MEM_HEADS
    logits = _dot(mq_ref[0], kbd_ref[0])
    ps = []
    for h in range(MEM_HEADS):
        s = logits[:, h * m:(h + 1) * m]
        e = jnp.exp(s - jnp.max(s, axis=-1, keepdims=True))
        ps.append((e * (1.0 / jnp.sum(e, axis=-1, keepdims=True))).astype(BF16))
    cross = _dot(jnp.concatenate(ps, axis=1), vbd_ref[0])
    o_ref[0] = x_ref[0] + _dot(mix_ref[0], wa_ref[...]) + _dot(cross.astype(BF16), wb_ref[...])


def _mix_out(x, mix, mq, kbd, vbd, wa, wb):
    b, s = x.shape[0], x.shape[1]
    spec = lambda n: pl.BlockSpec((1, ROW_TILE, n), lambda bi, i: (bi, i, 0))
    per_b = lambda a: pl.BlockSpec((1,) + a.shape[1:], lambda bi, i: (bi, 0, 0))
    return pl.pallas_call(
        _mix_out_kernel,
        grid=(b, s // ROW_TILE),
        in_specs=[spec(D_MODEL), spec(mix.shape[2]), spec(MEM_WIDTH), per_b(kbd), per_b(vbd),
                  _const_spec(wa.shape), _const_spec(wb.shape)],
        out_specs=spec(D_MODEL),
        out_shape=jax.ShapeDtypeStruct(x.shape, F32),
        compiler_params=_cparams("parallel", "parallel"),
        name="mix_out",
    )(x, mix, mq, kbd, vbd, wa, wb)


def _ffn_kernel(x_ref, g_ref, wu_ref, cw_ref, cb_ref, wd_ref, o_ref, u_scr, act_scr):
    tm = FFN_TILE
    cr = CARRY_ROWS

    @pl.when(pl.program_id(1) == 0)
    def _():
        u_scr[0:cr, :] = jnp.zeros((cr, u_scr.shape[1]), F32)

    x = x_ref[0]
    h = _rms_rows(x, g_ref[...]).astype(BF16)

    def conv(cols):
        u_scr[cr:cr + tm, cols] = _dot(h, wu_ref[:, cols])
        c = (cw_ref[0:1, cols] * u_scr[cr - 2:cr - 2 + tm, cols]
             + cw_ref[1:2, cols] * u_scr[cr - 1:cr - 1 + tm, cols]
             + cw_ref[2:3, cols] * u_scr[cr:cr + tm, cols]
             + cb_ref[:, cols])
        u_scr[0:cr, cols] = u_scr[tm:tm + cr, cols]
        return c

    for j in range(D_FF // FFN_COLS):
        a = conv(slice(j * FFN_COLS, (j + 1) * FFN_COLS))
        gte = conv(slice(D_FF + j * FFN_COLS, D_FF + (j + 1) * FFN_COLS))
        act_scr[:, j * FFN_COLS:(j + 1) * FFN_COLS] = (a * (gte / (1.0 + jnp.exp(-gte)))).astype(BF16)

    o_ref[0] = x + _dot(act_scr[...], wd_ref[...])


def _ffn(x, g, wu, cw, cb, wd):
    b, s = x.shape[0], x.shape[1]
    spec = pl.BlockSpec((1, FFN_TILE, D_MODEL), lambda bi, i: (bi, i, 0))
    single = lambda a: pl.BlockSpec(a.shape, lambda bi, i: (0,) * a.ndim, pipeline_mode=pl.Buffered(1))
    return pl.pallas_call(
        _ffn_kernel,
        grid=(b, s // FFN_TILE),
        in_specs=[spec, _const_spec((1, D_MODEL)), single(wu), _const_spec(cw.shape),
                  _const_spec(cb.shape), single(wd)],
        out_specs=spec,
        out_shape=jax.ShapeDtypeStruct(x.shape, F32),
        scratch_shapes=[pltpu.VMEM((FFN_TILE + CARRY_ROWS, 2 * D_FF), F32),
                        pltpu.VMEM((FFN_TILE, D_FF), BF16)],
        compiler_params=_cparams("parallel", "arbitrary"),
        name="ffn",
    )(x, g, wu, cw, cb, wd)


def _t5_bucket(rel):
    half = REL_BUCKETS // 2
    max_exact = half // 2
    ret = jnp.where(rel > 0, half, 0)
    n = jnp.abs(rel)
    nf = jnp.maximum(n, 1).astype(jnp.float32)
    large = max_exact + (jnp.log(nf / max_exact) / math.log(REL_MAX_DIST / max_exact)
                         * (half - max_exact)).astype(jnp.int32)
    large = jnp.minimum(large, half - 1)
    return ret + jnp.where(n < max_exact, n, large)


def _bias_tiles(rel_bias):
    blk = ATT_BLK
    assert blk >= REL_MAX_DIST
    table = rel_bias.astype(F32).T[:, :, None, None]

    def lookup(rel):
        bucket = _t5_bucket(rel)
        out = jnp.zeros((table.shape[0],) + rel.shape, F32)
        for i in range(REL_BUCKETS):
            out = jnp.where(bucket == i, table[:, i], out)
        return out

    kq = jnp.arange(blk)[:, None] - jnp.arange(blk)[None, :]
    far = lookup(jnp.full((1, 1), -2 * blk))
    visible = (jnp.arange(blk)[:, None] // CHUNK) <= (jnp.arange(blk)[None, :] // CHUNK)
    diag = jnp.where(visible, (lookup(kq) - far) * LOG2E, -jnp.inf)
    near = (lookup(kq - blk) - far) * LOG2E
    return jnp.stack([diag, near], axis=1)


def _group_sum_matrix():
    i = jnp.arange(MXU_DIM)
    return ((i[:, None] // 64) == (i[None, :] // 64)).astype(BF16)


def _chunk_tri_matrix(n):
    i = jnp.arange(n)
    return (((i[:, None] // CHUNK) == (i[None, :] // CHUNK)) & (i[None, :] <= i[:, None])).astype(BF16)


def _pad_heads(w, heads, dim, pad, axis):
    shape = list(w.shape)
    shape[axis:axis + 1] = [heads, dim]
    w = w.reshape(shape)
    widths = [(0, 0)] * w.ndim
    widths[axis + 1] = (0, pad - dim)
    w = jnp.pad(w, widths)
    shape[axis:axis + 2] = [heads * pad]
    return w.reshape(shape)


def _tile_gain(g, reps, scale=1.0):
    return (jnp.tile(g.astype(F32), reps) * scale)[None, :]


def _mem_block_diag(kn, v, b):
    m = kn.shape[0] // b
    eye = jnp.eye(MEM_HEADS, dtype=BF16)
    knt = kn.reshape(b, m, MEM_WIDTH).transpose(0, 2, 1)
    kbd = (knt.reshape(b, MEM_HEADS, MEM_HEAD_DIM, 1, m) * eye.reshape(1, MEM_HEADS, 1, MEM_HEADS, 1))
    kbd = kbd.reshape(b, MEM_WIDTH, MEM_HEADS * m)
    vbd = (v.reshape(b, 1, m, MEM_HEADS, MEM_HEAD_DIM) * eye.reshape(1, MEM_HEADS, 1, MEM_HEADS, 1))
    vbd = vbd.reshape(b, MEM_HEADS * m, MEM_WIDTH)
    return kbd, vbd


def kernel(x, mem, rel_bias, attn_norm, ffn_norm, mem_norm, w_in_diff, diff_qk_norm, diff_lambda,
           diff_out_norm, w_in_gla, gla_gate_w, gla_gate_b, gla_out_norm, w_mem_kv, mem_qk_norm,
           w_out, w_up, conv_w, conv_b, w_down):
    b, s, d = x.shape
    t = b * s
    tw = TOKEN_WIDTH
    gsum = _group_sum_matrix()
    mem2 = mem.reshape(b * mem.shape[1], d)
    x = x.astype(F32)

    for i in range(DEPTH):
        j = i // 2
        x2 = x.reshape(t, d)
        mq_gain = _tile_gain(mem_qk_norm[i, 0], MEM_HEADS, MEM_HEAD_DIM ** -0.5)
        if i % 2 == 0:
            nh, hd = DIFF_HEADS, DIFF_HEAD_DIM
            q, k, v, mq = _pre_diff(
                x2, attn_norm[i][None, :], w_in_diff[j].astype(BF16), gsum,
                _tile_gain(diff_qk_norm[j, 0], 2 * nh, hd ** -0.5 * LOG2E),
                _tile_gain(diff_qk_norm[j, 1], 2 * nh), mq_gain)
            nq = s // ATT_BLK
            qt = q.reshape(b, nq, ATT_BLK, nh, 2 * hd).transpose(0, 3, 1, 4, 2)
            vt = v.reshape(b, nq, ATT_BLK, nh, 2 * hd).transpose(0, 3, 1, 4, 2)
            lv = diff_lambda[j].astype(F32)
            lam_init = 0.8 - 0.6 * math.exp(-0.3 * i)
            lam = jnp.exp(jnp.sum(lv[0] * lv[1])) - jnp.exp(jnp.sum(lv[2] * lv[3])) + lam_init
            mix = _diff_attn(lam.reshape(1, 1), qt, k.reshape(b, s, tw), vt, _bias_tiles(rel_bias),
                             diff_out_norm[j].astype(F32)[None, :], 1.0 - lam_init)
            w_mix = w_out[i, :tw]
        else:
            kw = GLA_HEADS * GLA_K_DIM
            w = w_in_gla[j]
            hp = functools.partial(_pad_heads, heads=GLA_HEADS, axis=1)
            w_p = jnp.concatenate([
                hp(w[:, :kw], dim=GLA_K_DIM, pad=GLA_KP),
                hp(w[:, kw:2 * kw], dim=GLA_K_DIM, pad=GLA_KP),
                hp(w[:, 2 * kw:2 * kw + tw], dim=GLA_V_DIM, pad=GLA_VP),
                hp(w[:, 2 * kw + tw:2 * kw + 2 * tw], dim=GLA_V_DIM, pad=GLA_VP),
                jnp.pad(w[:, 2 * kw + 2 * tw:2 * kw + 2 * tw + GLA_GATE_RANK],
                        ((0, 0), (0, LANES - GLA_GATE_RANK))),
                w[:, 2 * kw + 2 * tw + GLA_GATE_RANK:]], axis=1).astype(BF16)
            gw = jnp.pad(hp(gla_gate_w[j], dim=GLA_K_DIM, pad=GLA_KP),
                         ((0, LANES - GLA_GATE_RANK), (0, 0))).astype(BF16)
            gb = _pad_heads(gla_gate_b[j].astype(F32)[None, :], GLA_HEADS, GLA_K_DIM, GLA_KP, 1)
            q, k, v, r, gc, mq = _pre_gla(x2, attn_norm[i][None, :], w_p, gsum, mq_gain, gw, gb,
                                          _chunk_tri_matrix(MXU_DIM))
            gain = _pad_heads(jnp.tile(gla_out_norm[j].astype(F32), GLA_HEADS)[None, :],
                              GLA_HEADS, GLA_V_DIM, GLA_VP, 1)
            sh = lambda a: a.reshape(b, s, a.shape[1])
            mix = _gla(sh(q), sh(k), sh(v), sh(r), sh(gc), gain)
            w_mix = _pad_heads(w_out[i, :tw], GLA_HEADS, GLA_V_DIM, GLA_VP, 0)

        kn, vm = _mem_kv(mem2, mem_norm[i][None, :], w_mem_kv[i].astype(BF16), gsum,
                         _tile_gain(mem_qk_norm[i, 1], MEM_HEADS))
        kbd, vbd = _mem_block_diag(kn, vm, b)
        x = _mix_out(x, mix.reshape(b, s, -1), mq.reshape(b, s, MEM_WIDTH), kbd, vbd,
                     w_mix.astype(BF16), w_out[i, tw:].astype(BF16))
        x = _ffn(x, ffn_norm[i][None, :], w_up[i].astype(BF16), conv_w[i].astype(F32),
                 conv_b[i].astype(F32)[None, :], w_down[i].astype(BF16))
    return x
```

```python
import functools
import math

import jax
import jax.numpy as jnp
from jax import lax
from jax.experimental import pallas as pl
from jax.experimental.pallas import tpu as pltpu

F32 = jnp.float32
BF16 = jnp.bfloat16

D_MODEL = 1024
DEPTH = 2
CHUNK = 64
MEM_WIDTH = D_MODEL // 4
MEM_HEADS = 4
MEM_HEAD_DIM = MEM_WIDTH // MEM_HEADS
TOKEN_WIDTH = D_MODEL - MEM_WIDTH
DIFF_HEAD_DIM = 64
DIFF_HEADS = TOKEN_WIDTH // (2 * DIFF_HEAD_DIM)
GLA_HEADS = 4
GLA_V_DIM = TOKEN_WIDTH // GLA_HEADS
GLA_K_DIM = GLA_V_DIM // 2
GLA_GATE_RANK = 16
GLA_GATE_TAU = 16.0
REL_BUCKETS = 32
REL_MAX_DIST = 128
D_FF = ((8 * D_MODEL // 3 + 127) // 128) * 128
EPS = 1e-6
LOG2E = math.log2(math.e)

LANES = 128
MXU_DIM = 256
VMEM_LIMIT_BYTES = 56 * 1024 * 1024

ROW_TILE = 512
ATT_BLK = 512
ATT_V_ROWS = 2 * DIFF_HEAD_DIM + 16
GLA_TILE = 256
FFN_TILE = 256
FFN_COLS = 256
GLA_KP = 128
GLA_VP = 256
CARRY_ROWS = 8


def _cparams(*sem):
    return pltpu.CompilerParams(dimension_semantics=sem, vmem_limit_bytes=VMEM_LIMIT_BYTES)


def _const_spec(shape):
    n = len(shape)
    return pl.BlockSpec(shape, lambda *_: (0,) * n)


def _rms_rows(x, g):
    ms = jnp.mean(x * x, axis=-1, keepdims=True)
    return x * lax.rsqrt(ms + EPS) * g


def _group_rms(t, gsum, gain):
    cols = []
    for c in range(t.shape[1] // MXU_DIM):
        blk = t[:, c * MXU_DIM:(c + 1) * MXU_DIM]
        ss = jnp.dot((blk * blk).astype(BF16), gsum, preferred_element_type=F32)
        cols.append(blk * lax.rsqrt(ss * (1.0 / 64) + EPS))
    out = cols[0] if len(cols) == 1 else jnp.concatenate(cols, axis=1)
    return out * gain


def _dot(a, b):
    return jnp.dot(a, b, preferred_element_type=F32)


def _dot_nt(a, b):
    return lax.dot_general(a, b, (((1,), (1,)), ((), ())), preferred_element_type=F32)


def _pre_diff_kernel(x_ref, g_ref, w_ref, gsum_ref, qg_ref, kg_ref, mg_ref,
                     q_ref, k_ref, v_ref, mq_ref):
    tw = TOKEN_WIDTH
    h = _rms_rows(x_ref[...], g_ref[...]).astype(BF16)
    gsum = gsum_ref[...]
    q_ref[...] = _group_rms(_dot(h, w_ref[:, 0:tw]), gsum, qg_ref[...]).astype(BF16)
    k_ref[...] = _group_rms(_dot(h, w_ref[:, tw:2 * tw]), gsum, kg_ref[...]).astype(BF16)
    v_ref[...] = _dot(h, w_ref[:, 2 * tw:3 * tw]).astype(BF16)
    mq_ref[...] = _group_rms(_dot(h, w_ref[:, 3 * tw:]), gsum, mg_ref[...]).astype(BF16)


def _pre_diff(x2, g, w, gsum, qg, kg, mg):
    t = x2.shape[0]
    tw = TOKEN_WIDTH
    row = lambda n: pl.BlockSpec((ROW_TILE, n), lambda i: (i, 0))
    return pl.pallas_call(
        _pre_diff_kernel,
        grid=(t // ROW_TILE,),
        in_specs=[row(D_MODEL), _const_spec((1, D_MODEL)), _const_spec(w.shape),
                  _const_spec(gsum.shape), _const_spec((1, tw)), _const_spec((1, tw)),
                  _const_spec((1, MEM_WIDTH))],
        out_specs=[row(tw), row(tw), row(tw), row(MEM_WIDTH)],
        out_shape=[jax.ShapeDtypeStruct((t, tw), BF16)] * 3 + [jax.ShapeDtypeStruct((t, MEM_WIDTH), BF16)],
        compiler_params=_cparams("parallel"),
        name="pre_diff",
    )(x2, g, w, gsum, qg, kg, mg)


def _diff_attn_kernel(lam_ref, qt_ref, k_ref, vt_ref, bias_ref, g_ref, o_ref,
                      qp_scr, s_scr, p_scr, a_scr, m_scr, acc_scr, *, out_scale):
    blk = ATT_BLK
    hd = DIFF_HEAD_DIM
    qi = pl.program_id(2)
    qt = qt_ref[0, 0, 0]
    row = lax.broadcasted_iota(jnp.int32, qt.shape, 0)
    zero = jnp.zeros_like(qt)
    qp_scr[0] = jnp.where(row < hd, qt, zero)
    qp_scr[1] = jnp.where(row >= hd, qt, zero)
    m_scr[...] = jnp.full(m_scr.shape, -jnp.inf, F32)
    acc_scr[...] = jnp.zeros(acc_scr.shape, F32)

    def key_block(t):
        return jnp.maximum(qi - t, 0)

    def logits(x, t, bias=None):
        start = pl.multiple_of(key_block(t) * blk, blk)
        s = _dot(k_ref[0, pl.ds(start, blk), :], qp_scr[x])
        s_scr[x] = s if bias is None else s + bias

    def softmax(x):
        m_old = m_scr[x]
        col_max = s_scr[x, 0:8, :]
        for r in range(8, blk, 8):
            col_max = jnp.maximum(col_max, s_scr[x, r:r + 8, :])
        m_new = jnp.maximum(m_old, jnp.max(col_max, axis=0, keepdims=True))
        a_scr[x] = jnp.exp2(m_old - m_new)
        m_scr[x] = m_new
        for r in range(0, blk, 16):
            p_scr[x, r:r + 16, :] = jnp.exp2((s_scr[x, r:r + 16, :] - m_new).astype(BF16))

    def values(x, t):
        acc_scr[x] = a_scr[x] * acc_scr[x] + _dot(vt_ref[0, 0, key_block(t)], p_scr[x])

    def step(t, bias=None):
        values(0, t)
        logits(0, t + 1, bias)
        softmax(1)
        softmax(0)
        values(1, t)
        logits(1, t + 1, bias)

    logits(0, 0, bias_ref[0, 0])
    logits(1, 0, bias_ref[0, 0])
    softmax(0)
    step(0, bias_ref[0, 1] + jnp.where(qi == 0, -jnp.inf, 0.0).astype(F32))

    def far(t, c):
        step(t)
        return c

    t_last = jnp.maximum(qi, 1)
    lax.fori_loop(1, t_last, far, 0)
    values(0, t_last)
    softmax(1)
    values(1, t_last)

    lam = lam_ref[0, 0]
    ot = (acc_scr[0, 0:2 * hd, :] * (1.0 / acc_scr[0, 2 * hd:2 * hd + 1, :])
          - lam * (acc_scr[1, 0:2 * hd, :] * (1.0 / acc_scr[1, 2 * hd:2 * hd + 1, :])))
    ms = jnp.mean(ot * ot, axis=0, keepdims=True)
    y = (ot * lax.rsqrt(ms + EPS)).T * (g_ref[...] * out_scale)
    o_ref[0] = y.astype(BF16)


def _diff_attn(lam, qt, k, vt, bias, g, out_scale):
    b, nh, nq = qt.shape[0], qt.shape[1], qt.shape[2]
    s = k.shape[1]
    blk = ATT_BLK
    return pl.pallas_call(
        functools.partial(_diff_attn_kernel, out_scale=out_scale),
        grid=(b, nh, nq),
        in_specs=[
            pl.BlockSpec(memory_space=pltpu.SMEM),
            pl.BlockSpec((1, 1, 1, 2 * DIFF_HEAD_DIM, blk), lambda bi, hi, qi: (bi, hi, qi, 0, 0)),
            pl.BlockSpec((1, s, 2 * DIFF_HEAD_DIM), lambda bi, hi, qi: (bi, 0, hi)),
            pl.BlockSpec((1, 1, nq, ATT_V_ROWS, blk), lambda bi, hi, qi: (bi, hi, 0, 0, 0)),
            pl.BlockSpec((1, 2, blk, blk), lambda bi, hi, qi: (hi, 0, 0, 0)),
            _const_spec((1, 2 * DIFF_HEAD_DIM)),
        ],
        out_specs=pl.BlockSpec((1, blk, 2 * DIFF_HEAD_DIM), lambda bi, hi, qi: (bi, qi, hi)),
        out_shape=jax.ShapeDtypeStruct((b, s, nh * 2 * DIFF_HEAD_DIM), BF16),
        scratch_shapes=[pltpu.VMEM((2, 2 * DIFF_HEAD_DIM, blk), BF16), pltpu.VMEM((2, blk, blk), F32),
                        pltpu.VMEM((2, blk, blk), BF16), pltpu.VMEM((2, 1, blk), F32),
                        pltpu.VMEM((2, 1, blk), F32), pltpu.VMEM((2, ATT_V_ROWS, blk), F32)],
        compiler_params=_cparams("parallel", "parallel", "arbitrary"),
        name="diff_attn",
    )(lam, qt, k, vt, bias, g)


def _pre_gla_kernel(x_ref, g_ref, w_ref, gsum_ref, mg_ref, gw_ref, gb_ref, tri_ref,
                    q_ref, k_ref, v_ref, r_ref, gc_ref, mq_ref):
    kw = GLA_HEADS * GLA_KP
    vw = GLA_HEADS * GLA_VP
    h = _rms_rows(x_ref[...], g_ref[...]).astype(BF16)
    q_ref[...] = _dot(h, w_ref[:, 0:kw]).astype(BF16)
    k_ref[...] = _dot(h, w_ref[:, kw:2 * kw]).astype(BF16)
    o = 2 * kw
    v_ref[...] = _dot(h, w_ref[:, o:o + vw]).astype(BF16)
    r_ref[...] = _dot(h, w_ref[:, o + vw:o + 2 * vw]).astype(BF16)
    o = o + 2 * vw
    gate_low = _dot(h, w_ref[:, o:o + LANES]).astype(BF16)
    mq_ref[...] = _group_rms(_dot(h, w_ref[:, o + LANES:]), gsum_ref[...], mg_ref[...]).astype(BF16)
    z = _dot(gate_low, gw_ref[...]) + gb_ref[...]
    log_a = (jnp.minimum(z, 0.0) - jnp.log1p(jnp.exp(-jnp.abs(z)))) * (1.0 / GLA_GATE_TAU)
    hi = log_a.astype(BF16)
    rem = log_a - hi.astype(F32)
    mid = rem.astype(BF16)
    lo = (rem - mid.astype(F32)).astype(BF16)
    tri = tri_ref[...]
    n = tri.shape[0]
    for c in range(log_a.shape[0] // n):
        rows = slice(c * n, (c + 1) * n)
        gc_ref[rows, :] = _dot(tri, hi[rows]) + _dot(tri, mid[rows]) + _dot(tri, lo[rows])


def _pre_gla(x2, g, w, gsum, mg, gw, gb, tri):
    t = x2.shape[0]
    kw = GLA_HEADS * GLA_KP
    vw = GLA_HEADS * GLA_VP
    row = lambda n: pl.BlockSpec((ROW_TILE, n), lambda i: (i, 0))
    return pl.pallas_call(
        _pre_gla_kernel,
        grid=(t // ROW_TILE,),
        in_specs=[row(D_MODEL), _const_spec((1, D_MODEL)), _const_spec(w.shape),
                  _const_spec(gsum.shape), _const_spec((1, MEM_WIDTH)), _const_spec(gw.shape),
                  _const_spec(gb.shape), _const_spec(tri.shape)],
        out_specs=[row(kw), row(kw), row(vw), row(vw), row(kw), row(MEM_WIDTH)],
        out_shape=[jax.ShapeDtypeStruct((t, kw), BF16), jax.ShapeDtypeStruct((t, kw), BF16),
                   jax.ShapeDtypeStruct((t, vw), BF16), jax.ShapeDtypeStruct((t, vw), BF16),
                   jax.ShapeDtypeStruct((t, kw), F32), jax.ShapeDtypeStruct((t, MEM_WIDTH), BF16)],
        compiler_params=_cparams("parallel"),
        name="pre_gla",
    )(x2, g, w, gsum, mg, gw, gb, tri)


def _gla_kernel(q_ref, k_ref, v_ref, r_ref, gc_ref, gain_ref, o_ref, s_scr):
    tg = GLA_TILE
    nchunk = tg // CHUNK

    @pl.when(pl.program_id(1) == 0)
    def _():
        s_scr[...] = jnp.zeros(s_scr.shape, F32)

    ri = lax.broadcasted_iota(jnp.int32, (tg, tg), 0)
    ci = lax.broadcasted_iota(jnp.int32, (tg, tg), 1)
    same_chunk = (ri // CHUNK) == (ci // CHUNK)
    past = ci <= ri
    row_chunk = lax.broadcasted_iota(jnp.int32, (tg, GLA_KP), 0) // CHUNK

    for h in range(GLA_HEADS):
        ks = slice(h * GLA_KP, (h + 1) * GLA_KP)
        vs = slice(h * GLA_VP, (h + 1) * GLA_VP)
        qh = q_ref[0, :, ks].astype(F32) * (GLA_K_DIM ** -0.5)
        kh = k_ref[0, :, ks].astype(F32)
        g = gc_ref[0, :, ks]
        vh = v_ref[0, :, vs]
        eg = jnp.exp(g)
        ieg = jnp.exp(-g)
        qe = (qh * eg).astype(BF16)
        a_past = _dot_nt(qe, (kh * ieg).astype(BF16))
        a_fut = _dot_nt((qh * ieg).astype(BF16), (kh * eg).astype(BF16))
        scores = jnp.where(same_chunk, jnp.where(past, a_past, a_fut), 0.0)
        o = _dot(scores.astype(BF16), vh)
        vt = vh.astype(F32).T.astype(BF16)
        g_end = jnp.concatenate(
            [jnp.broadcast_to(g[c * CHUNK + CHUNK - 1:c * CHUNK + CHUNK, :], (CHUNK, GLA_KP))
             for c in range(nchunk)], axis=0)
        kdec = kh * jnp.exp(g_end - g)
        inter = []
        for c in range(nchunk):
            rows = slice(c * CHUNK, (c + 1) * CHUNK)
            st = s_scr[h]
            inter.append(_dot_nt(qe[rows], st.astype(BF16)))
            kd_c = jnp.where(row_chunk == c, kdec, 0.0).astype(BF16)
            s_scr[h] = st * jnp.exp(g_end[rows][0:1, :]) + _dot(vt, kd_c)
        o = o + jnp.concatenate(inter, axis=0)
        ms = jnp.sum(o * o, axis=-1, keepdims=True) * (1.0 / GLA_V_DIM)
        y = o * lax.rsqrt(ms + EPS) * gain_ref[:, vs]
        rh = r_ref[0, :, vs].astype(F32)
        o_ref[0, :, vs] = (y * (rh / (1.0 + jnp.exp(-rh)))).astype(BF16)


def _gla(q, k, v, r, gc, gain):
    b, s = q.shape[0], q.shape[1]
    kw = GLA_HEADS * GLA_KP
    vw = GLA_HEADS * GLA_VP
    spec = lambda n: pl.BlockSpec((1, GLA_TILE, n), lambda bi, i: (bi, i, 0))
    return pl.pallas_call(
        _gla_kernel,
        grid=(b, s // GLA_TILE),
        in_specs=[spec(kw), spec(kw), spec(vw), spec(vw), spec(kw), _const_spec((1, vw))],
        out_specs=spec(vw),
        out_shape=jax.ShapeDtypeStruct((b, s, vw), BF16),
        scratch_shapes=[pltpu.VMEM((GLA_HEADS, GLA_VP, GLA_KP), F32)],
        compiler_params=_cparams("parallel", "arbitrary"),
        name="gla",
    )(q, k, v, r, gc, gain)


def _mem_kv_kernel(mem_ref, g_ref, w_ref, gsum_ref, kg_ref, k_ref, v_ref):
    h = _rms_rows(mem_ref[...], g_ref[...]).astype(BF16)
    k_ref[...] = _group_rms(_dot(h, w_ref[:, :MEM_WIDTH]), gsum_ref[...], kg_ref[...]).astype(BF16)
    v_ref[...] = _dot(h, w_ref[:, MEM_WIDTH:]).astype(BF16)


def _mem_kv(mem2, g, w, gsum, kg):
    n = mem2.shape[0]
    return pl.pallas_call(
        _mem_kv_kernel,
        grid=(1,),
        in_specs=[_const_spec(mem2.shape), _const_spec((1, D_MODEL)), _const_spec(w.shape),
                  _const_spec(gsum.shape), _const_spec((1, MEM_WIDTH))],
        out_specs=[_const_spec((n, MEM_WIDTH)), _const_spec((n, MEM_WIDTH))],
        out_shape=[jax.ShapeDtypeStruct((n, MEM_WIDTH), BF16)] * 2,
        compiler_params=_cparams("arbitrary"),
        name="mem_kv",
    )(mem2, g, w, gsum, kg)


def _mix_out_kernel(x_ref, mix_ref, mq_ref, kbd_ref, vbd_ref, wa_ref, wb_ref, o_ref):
    m = kbd_ref.shape[2] // MEM_HEADS
    logits = _dot(mq_ref[0], kbd_ref[0])
    ps = []
    for h in range(MEM_HEADS):
        s = logits[:, h * m:(h + 1) * m]
        e = jnp.exp(s - jnp.max(s, axis=-1, keepdims=True))
        ps.append((e * (1.0 / jnp.sum(e, axis=-1, keepdims=True))).astype(BF16))
    cross = _dot(jnp.concatenate(ps, axis=1), vbd_ref[0])
    o_ref[0] = x_ref[0] + _dot(mix_ref[0], wa_ref[...]) + _dot(cross.astype(BF16), wb_ref[...])


def _mix_out(x, mix, mq, kbd, vbd, wa, wb):
    b, s = x.shape[0], x.shape[1]
    spec = lambda n: pl.BlockSpec((1, ROW_TILE, n), lambda bi, i: (bi, i, 0))
    per_b = lambda a: pl.BlockSpec((1,) + a.shape[1:], lambda bi, i: (bi, 0, 0))
    return pl.pallas_call(
        _mix_out_kernel,
        grid=(b, s // ROW_TILE),
        in_specs=[spec(D_MODEL), spec(mix.shape[2]), spec(MEM_WIDTH), per_b(kbd), per_b(vbd),
                  _const_spec(wa.shape), _const_spec(wb.shape)],
        out_specs=spec(D_MODEL),
        out_shape=jax.ShapeDtypeStruct(x.shape, F32),
        compiler_params=_cparams("parallel", "parallel"),
        name="mix_out",
    )(x, mix, mq, kbd, vbd, wa, wb)


def _ffn_kernel(x_ref, g_ref, wu_ref, cw_ref, cb_ref, wd_ref, o_ref, u_scr, act_scr):
    tm = FFN_TILE
    cr = CARRY_ROWS

    @pl.when(pl.program_id(1) == 0)
    def _():
        u_scr[0:cr, :] = jnp.zeros((cr, u_scr.shape[1]), F32)

    x = x_ref[0]
    h = _rms_rows(x, g_ref[...]).astype(BF16)

    def conv(cols):
        u_scr[cr:cr + tm, cols] = _dot(h, wu_ref[:, cols])
        c = (cw_ref[0:1, cols] * u_scr[cr - 2:cr - 2 + tm, cols]
             + cw_ref[1:2, cols] * u_scr[cr - 1:cr - 1 + tm, cols]
             + cw_ref[2:3, cols] * u_scr[cr:cr + tm, cols]
             + cb_ref[:, cols])
        u_scr[0:cr, cols] = u_scr[tm:tm + cr, cols]
        return c

    for j in range(D_FF // FFN_COLS):
        a = conv(slice(j * FFN_COLS, (j + 1) * FFN_COLS))
        gte = conv(slice(D_FF + j * FFN_COLS, D_FF + (j + 1) * FFN_COLS))
        act_scr[:, j * FFN_COLS:(j + 1) * FFN_COLS] = (a * (gte / (1.0 + jnp.exp(-gte)))).astype(BF16)

    o_ref[0] = x + _dot(act_scr[...], wd_ref[...])


def _ffn(x, g, wu, cw, cb, wd):
    b, s = x.shape[0], x.shape[1]
    spec = pl.BlockSpec((1, FFN_TILE, D_MODEL), lambda bi, i: (bi, i, 0))
    single = lambda a: pl.BlockSpec(a.shape, lambda bi, i: (0,) * a.ndim, pipeline_mode=pl.Buffered(1))
    return pl.pallas_call(
        _ffn_kernel,
        grid=(b, s // FFN_TILE),
        in_specs=[spec, _const_spec((1, D_MODEL)), single(wu), _const_spec(cw.shape),
                  _const_spec(cb.shape), single(wd)],
        out_specs=spec,
        out_shape=jax.ShapeDtypeStruct(x.shape, F32),
        scratch_shapes=[pltpu.VMEM((FFN_TILE + CARRY_ROWS, 2 * D_FF), F32),
                        pltpu.VMEM((FFN_TILE, D_FF), BF16)],
        compiler_params=_cparams("parallel", "arbitrary"),
        name="ffn",
    )(x, g, wu, cw, cb, wd)


def _t5_bucket(rel):
    half = REL_BUCKETS // 2
    max_exact = half // 2
    ret = jnp.where(rel > 0, half, 0)
    n = jnp.abs(rel)
    nf = jnp.maximum(n, 1).astype(jnp.float32)
    large = max_exact + (jnp.log(nf / max_exact) / math.log(REL_MAX_DIST / max_exact)
                         * (half - max_exact)).astype(jnp.int32)
    large = jnp.minimum(large, half - 1)
    return ret + jnp.where(n < max_exact, n, large)


def _bias_tiles(rel_bias):
    blk = ATT_BLK
    assert blk >= REL_MAX_DIST
    table = rel_bias.astype(F32).T[:, :, None, None]

    def lookup(rel):
        bucket = _t5_bucket(rel)
        out = jnp.zeros((table.shape[0],) + rel.shape, F32)
        for i in range(REL_BUCKETS):
            out = jnp.where(bucket == i, table[:, i], out)
        return out

    kq = jnp.arange(blk)[:, None] - jnp.arange(blk)[None, :]
    far = lookup(jnp.full((1, 1), -2 * blk))
    visible = (jnp.arange(blk)[:, None] // CHUNK) <= (jnp.arange(blk)[None, :] // CHUNK)
    diag = jnp.where(visible, (lookup(kq) - far) * LOG2E, -jnp.inf)
    near = (lookup(kq - blk) - far) * LOG2E
    return jnp.stack([diag, near], axis=1)


def _group_sum_matrix():
    i = jnp.arange(MXU_DIM)
    return ((i[:, None] // 64) == (i[None, :] // 64)).astype(BF16)


def _chunk_tri_matrix(n):
    i = jnp.arange(n)
    return (((i[:, None] // CHUNK) == (i[None, :] // CHUNK)) & (i[None, :] <= i[:, None])).astype(BF16)


def _pad_heads(w, heads, dim, pad, axis):
    shape = list(w.shape)
    shape[axis:axis + 1] = [heads, dim]
    w = w.reshape(shape)
    widths = [(0, 0)] * w.ndim
    widths[axis + 1] = (0, pad - dim)
    w = jnp.pad(w, widths)
    shape[axis:axis + 2] = [heads * pad]
    return w.reshape(shape)


def _tile_gain(g, reps, scale=1.0):
    return (jnp.tile(g.astype(F32), reps) * scale)[None, :]


def _mem_block_diag(kn, v, b):
    m = kn.shape[0] // b
    eye = jnp.eye(MEM_HEADS, dtype=BF16)
    knt = kn.reshape(b, m, MEM_WIDTH).transpose(0, 2, 1)
    kbd = (knt.reshape(b, MEM_HEADS, MEM_HEAD_DIM, 1, m) * eye.reshape(1, MEM_HEADS, 1, MEM_HEADS, 1))
    kbd = kbd.reshape(b, MEM_WIDTH, MEM_HEADS * m)
    vbd = (v.reshape(b, 1, m, MEM_HEADS, MEM_HEAD_DIM) * eye.reshape(1, MEM_HEADS, 1, MEM_HEADS, 1))
    vbd = vbd.reshape(b, MEM_HEADS * m, MEM_WIDTH)
    return kbd, vbd


def kernel(x, mem, rel_bias, attn_norm, ffn_norm, mem_norm, w_in_diff, diff_qk_norm, diff_lambda,
           diff_out_norm, w_in_gla, gla_gate_w, gla_gate_b, gla_out_norm, w_mem_kv, mem_qk_norm,
           w_out, w_up, conv_w, conv_b, w_down):
    b, s, d = x.shape
    t = b * s
    tw = TOKEN_WIDTH
    gsum = _group_sum_matrix()
    mem2 = mem.reshape(b * mem.shape[1], d)
    x = x.astype(F32)

    for i in range(DEPTH):
        j = i // 2
        x2 = x.reshape(t, d)
        mq_gain = _tile_gain(mem_qk_norm[i, 0], MEM_HEADS, MEM_HEAD_DIM ** -0.5)
        if i % 2 == 0:
            nh, hd = DIFF_HEADS, DIFF_HEAD_DIM
            q, k, v, mq = _pre_diff(
                x2, attn_norm[i][None, :], w_in_diff[j].astype(BF16), gsum,
                _tile_gain(diff_qk_norm[j, 0], 2 * nh, hd ** -0.5 * LOG2E),
                _tile_gain(diff_qk_norm[j, 1], 2 * nh), mq_gain)
            nq = s // ATT_BLK
            qt = q.reshape(b, nq, ATT_BLK, nh, 2 * hd).transpose(0, 3, 1, 4, 2)
            vt = v.reshape(b, nq, ATT_BLK, nh, 2 * hd).transpose(0, 3, 1, 4, 2)
            vt = jnp.pad(vt, ((0, 0),) * 3 + ((0, ATT_V_ROWS - 2 * hd), (0, 0)), constant_values=1.0)
            lv = diff_lambda[j].astype(F32)
            lam_init = 0.8 - 0.6 * math.exp(-0.3 * i)
            lam = jnp.exp(jnp.sum(lv[0] * lv[1])) - jnp.exp(jnp.sum(lv[2] * lv[3])) + lam_init
            mix = _diff_attn(lam.reshape(1, 1), qt, k.reshape(b, s, tw), vt, _bias_tiles(rel_bias),
                             diff_out_norm[j].astype(F32)[None, :], 1.0 - lam_init)
            w_mix = w_out[i, :tw]
        else:
            kw = GLA_HEADS * GLA_K_DIM
            w = w_in_gla[j]
            hp = functools.partial(_pad_heads, heads=GLA_HEADS, axis=1)
            w_p = jnp.concatenate([
                hp(w[:, :kw], dim=GLA_K_DIM, pad=GLA_KP),
                hp(w[:, kw:2 * kw], dim=GLA_K_DIM, pad=GLA_KP),
                hp(w[:, 2 * kw:2 * kw + tw], dim=GLA_V_DIM, pad=GLA_VP),
                hp(w[:, 2 * kw + tw:2 * kw + 2 * tw], dim=GLA_V_DIM, pad=GLA_VP),
                jnp.pad(w[:, 2 * kw + 2 * tw:2 * kw + 2 * tw + GLA_GATE_RANK],
                        ((0, 0), (0, LANES - GLA_GATE_RANK))),
                w[:, 2 * kw + 2 * tw + GLA_GATE_RANK:]], axis=1).astype(BF16)
            gw = jnp.pad(hp(gla_gate_w[j], dim=GLA_K_DIM, pad=GLA_KP),
                         ((0, LANES - GLA_GATE_RANK), (0, 0))).astype(BF16)
            gb = _pad_heads(gla_gate_b[j].astype(F32)[None, :], GLA_HEADS, GLA_K_DIM, GLA_KP, 1)
            q, k, v, r, gc, mq = _pre_gla(x2, attn_norm[i][None, :], w_p, gsum, mq_gain, gw, gb,
                                          _chunk_tri_matrix(MXU_DIM))
            gain = _pad_heads(jnp.tile(gla_out_norm[j].astype(F32), GLA_HEADS)[None, :],
                              GLA_HEADS, GLA_V_DIM, GLA_VP, 1)
            sh = lambda a: a.reshape(b, s, a.shape[1])
            mix = _gla(sh(q), sh(k), sh(v), sh(r), sh(gc), gain)
            w_mix = _pad_heads(w_out[i, :tw], GLA_HEADS, GLA_V_DIM, GLA_VP, 0)

        kn, vm = _mem_kv(mem2, mem_norm[i][None, :], w_mem_kv[i].astype(BF16), gsum,
                         _tile_gain(mem_qk_norm[i, 1], MEM_HEADS))
        kbd, vbd = _mem_block_diag(kn, vm, b)
        x = _mix_out(x, mix.reshape(b, s, -1), mq.reshape(b, s, MEM_WIDTH), kbd, vbd,
                     w_mix.astype(BF16), w_out[i, tw:].astype(BF16))
        x = _ffn(x, ffn_norm[i][None, :], w_up[i].astype(BF16), conv_w[i].astype(F32),
                 conv_b[i].astype(F32)[None, :], w_down[i].astype(BF16))
    return x
```

```python
import functools
import math

import jax
import jax.numpy as jnp
from jax import lax
from jax.experimental import pallas as pl
from jax.experimental.pallas import tpu as pltpu

F32 = jnp.float32
BF16 = jnp.bfloat16

D_MODEL = 1024
DEPTH = 2
CHUNK = 64
MEM_WIDTH = D_MODEL // 4
MEM_HEADS = 4
MEM_HEAD_DIM = MEM_WIDTH // MEM_HEADS
TOKEN_WIDTH = D_MODEL - MEM_WIDTH
DIFF_HEAD_DIM = 64
DIFF_HEADS = TOKEN_WIDTH // (2 * DIFF_HEAD_DIM)
GLA_HEADS = 4
GLA_V_DIM = TOKEN_WIDTH // GLA_HEADS
GLA_K_DIM = GLA_V_DIM // 2
GLA_GATE_RANK = 16
GLA_GATE_TAU = 16.0
REL_BUCKETS = 32
REL_MAX_DIST = 128
D_FF = ((8 * D_MODEL // 3 + 127) // 128) * 128
EPS = 1e-6
LOG2E = math.log2(math.e)

LANES = 128
MXU_DIM = 256
VMEM_LIMIT_BYTES = 56 * 1024 * 1024

ROW_TILE = 512
ATT_BLK = 512
ATT_STREAMS = 4
ATT_V_ROWS = 2 * DIFF_HEAD_DIM + 16
GLA_TILE = 256
FFN_TILE = 256
FFN_COLS = 256
GLA_KP = 128
GLA_VP = 256
CARRY_ROWS = 8


def _cparams(*sem):
    return pltpu.CompilerParams(dimension_semantics=sem, vmem_limit_bytes=VMEM_LIMIT_BYTES)


def _const_spec(shape):
    n = len(shape)
    return pl.BlockSpec(shape, lambda *_: (0,) * n)


def _rms_rows(x, g):
    ms = jnp.mean(x * x, axis=-1, keepdims=True)
    return x * lax.rsqrt(ms + EPS) * g


def _group_rms(t, gsum, gain):
    cols = []
    for c in range(t.shape[1] // MXU_DIM):
        blk = t[:, c * MXU_DIM:(c + 1) * MXU_DIM]
        ss = jnp.dot((blk * blk).astype(BF16), gsum, preferred_element_type=F32)
        cols.append(blk * lax.rsqrt(ss * (1.0 / 64) + EPS))
    out = cols[0] if len(cols) == 1 else jnp.concatenate(cols, axis=1)
    return out * gain


def _dot(a, b):
    return jnp.dot(a, b, preferred_element_type=F32)


def _dot_nt(a, b):
    return lax.dot_general(a, b, (((1,), (1,)), ((), ())), preferred_element_type=F32)


def _pre_diff_kernel(x_ref, g_ref, w_ref, gsum_ref, qg_ref, kg_ref, mg_ref,
                     qt_ref, k_ref, vt_ref, mq_ref):
    tw = TOKEN_WIDTH
    hw = 2 * DIFF_HEAD_DIM
    h = _rms_rows(x_ref[...], g_ref[...]).astype(BF16)
    gsum = gsum_ref[...]
    q = _group_rms(_dot(h, w_ref[:, 0:tw]), gsum, qg_ref[...])
    k_ref[...] = _group_rms(_dot(h, w_ref[:, tw:2 * tw]), gsum, kg_ref[...]).astype(BF16)
    v = _dot(h, w_ref[:, 2 * tw:3 * tw])
    mq_ref[...] = _group_rms(_dot(h, w_ref[:, 3 * tw:]), gsum, mg_ref[...]).astype(BF16)
    ones = jnp.ones((ATT_V_ROWS - hw, ROW_TILE), BF16)
    for n in range(DIFF_HEADS):
        qt_ref[0, n, 0] = q[:, n * hw:(n + 1) * hw].T.astype(BF16)
        vt_ref[0, n, 0, 0:hw, :] = v[:, n * hw:(n + 1) * hw].T.astype(BF16)
        vt_ref[0, n, 0, hw:, :] = ones


def _pre_diff(x2, g, w, gsum, qg, kg, mg, b):
    t = x2.shape[0]
    tw = TOKEN_WIDTH
    hw = 2 * DIFF_HEAD_DIM
    assert ROW_TILE == ATT_BLK
    nq = t // b // ATT_BLK
    row = lambda n: pl.BlockSpec((ROW_TILE, n), lambda i: (i, 0))
    per_head = lambda r: pl.BlockSpec((1, DIFF_HEADS, 1, r, ATT_BLK), lambda i: (i // nq, 0, i % nq, 0, 0))
    return pl.pallas_call(
        _pre_diff_kernel,
        grid=(t // ROW_TILE,),
        in_specs=[row(D_MODEL), _const_spec((1, D_MODEL)), _const_spec(w.shape),
                  _const_spec(gsum.shape), _const_spec((1, tw)), _const_spec((1, tw)),
                  _const_spec((1, MEM_WIDTH))],
        out_specs=[per_head(hw), row(tw), per_head(ATT_V_ROWS), row(MEM_WIDTH)],
        out_shape=[jax.ShapeDtypeStruct((b, DIFF_HEADS, nq, hw, ATT_BLK), BF16),
                   jax.ShapeDtypeStruct((t, tw), BF16),
                   jax.ShapeDtypeStruct((b, DIFF_HEADS, nq, ATT_V_ROWS, ATT_BLK), BF16),
                   jax.ShapeDtypeStruct((t, MEM_WIDTH), BF16)],
        compiler_params=_cparams("parallel"),
        name="pre_diff",
    )(x2, g, w, gsum, qg, kg, mg)


def _diff_attn_kernel(lam_ref, qt_ref, k_ref, vt_ref, bias_ref, g_ref, o_ref,
                      *scratch, out_scale):
    blk = ATT_BLK
    hd = DIFF_HEAD_DIM
    ns = ATT_STREAMS
    qp_scr, s_scr, cm_scr, p_scr, a_scr, m_scr, acc_scr = (scratch[i * ns:(i + 1) * ns] for i in range(7))
    w = 2 * blk // ns
    qi = pl.program_id(2)
    qt = qt_ref[0, 0, 0]
    row = lax.broadcasted_iota(jnp.int32, qt.shape, 0)
    zero = jnp.zeros_like(qt)
    q_maps = (jnp.where(row < hd, qt, zero), jnp.where(row >= hd, qt, zero))
    cols = [slice((x % (ns // 2)) * w, (x % (ns // 2) + 1) * w) for x in range(ns)]
    for x in range(ns):
        qp_scr[x][...] = q_maps[x // (ns // 2)][:, cols[x]]
        m_scr[x][...] = jnp.full(m_scr[x].shape, -jnp.inf, F32)
        acc_scr[x][...] = jnp.zeros(acc_scr[x].shape, F32)

    def key_block(t):
        return jnp.maximum(qi - t, 0)

    def logits(x, t, bias=None):
        start = pl.multiple_of(key_block(t) * blk, blk)
        s = _dot(k_ref[0, pl.ds(start, blk), :], qp_scr[x][...])
        if bias is not None:
            s = s + bias[:, cols[x]]
        s_scr[x][...] = s
        part = s[0:8]
        for r in range(8, blk, 8):
            part = jnp.maximum(part, s[r:r + 8])
        cm_scr[x][...] = part

    def softmax(x):
        m_old = m_scr[x][...]
        m_new = jnp.maximum(m_old, jnp.max(cm_scr[x][...], axis=0, keepdims=True))
        a_scr[x][...] = jnp.exp2(m_old - m_new)
        m_scr[x][...] = m_new
        for r in range(0, blk, 16):
            p_scr[x][r:r + 16, :] = jnp.exp2((s_scr[x][r:r + 16, :] - m_new).astype(BF16))

    def values(x, t):
        acc_scr[x][...] = a_scr[x][...] * acc_scr[x][...] + _dot(vt_ref[0, 0, key_block(t)], p_scr[x][...])

    def step(t, bias=None):
        for x in range(ns):
            logits(x, t + 1, bias)
            values(x, t)
            softmax((x + 1) % ns)

    for x in range(ns):
        logits(x, 0, bias_ref[0, 0])
    softmax(0)
    step(0, bias_ref[0, 1] + jnp.where(qi == 0, -jnp.inf, 0.0).astype(F32))

    t_last = jnp.maximum(qi, 1)

    def far_pair(u, c):
        step(1 + 2 * u)
        step(2 + 2 * u)
        return c

    lax.fori_loop(0, (t_last - 1) // 2, far_pair, 0)

    @pl.when((t_last - 1) % 2 == 1)
    def _():
        step(t_last - 1)

    for x in range(ns):
        values(x, t_last)
        if x + 1 < ns:
            softmax(x + 1)

    lam = lam_ref[0, 0]
    half = ns // 2
    outs = []
    for x in range(half):
        o0 = acc_scr[x][0:2 * hd, :] * (1.0 / acc_scr[x][2 * hd:2 * hd + 1, :])
        o1 = acc_scr[half + x][0:2 * hd, :] * (1.0 / acc_scr[half + x][2 * hd:2 * hd + 1, :])
        outs.append(o0 - lam * o1)
    ot = outs[0] if half == 1 else jnp.concatenate(outs, axis=1)
    ms = jnp.mean(ot * ot, axis=0, keepdims=True)
    y = (ot * lax.rsqrt(ms + EPS)).T * (g_ref[...] * out_scale)
    o_ref[0] = y.astype(BF16)


def _diff_attn(lam, qt, k, vt, bias, g, out_scale):
    b, nh, nq = qt.shape[0], qt.shape[1], qt.shape[2]
    s = k.shape[1]
    blk = ATT_BLK
    ns = ATT_STREAMS
    w = 2 * blk // ns
    return pl.pallas_call(
        functools.partial(_diff_attn_kernel, out_scale=out_scale),
        grid=(b, nh, nq),
        in_specs=[
            pl.BlockSpec(memory_space=pltpu.SMEM),
            pl.BlockSpec((1, 1, 1, 2 * DIFF_HEAD_DIM, blk), lambda bi, hi, qi: (bi, hi, qi, 0, 0)),
            pl.BlockSpec((1, s, 2 * DIFF_HEAD_DIM), lambda bi, hi, qi: (bi, 0, hi)),
            pl.BlockSpec((1, 1, nq, ATT_V_ROWS, blk), lambda bi, hi, qi: (bi, hi, 0, 0, 0)),
            pl.BlockSpec((1, 2, blk, blk), lambda bi, hi, qi: (hi, 0, 0, 0)),
            _const_spec((1, 2 * DIFF_HEAD_DIM)),
        ],
        out_specs=pl.BlockSpec((1, blk, 2 * DIFF_HEAD_DIM), lambda bi, hi, qi: (bi, qi, hi)),
        out_shape=jax.ShapeDtypeStruct((b, s, nh * 2 * DIFF_HEAD_DIM), BF16),
        scratch_shapes=[pltpu.VMEM(shape, dtype) for shape, dtype in (
            ((2 * DIFF_HEAD_DIM, w), BF16), ((blk, w), F32), ((8, w), F32), ((blk, w), BF16),
            ((1, w), F32), ((1, w), F32), ((ATT_V_ROWS, w), F32)) for _ in range(ns)],
        compiler_params=_cparams("parallel", "parallel", "arbitrary"),
        name="diff_attn",
    )(lam, qt, k, vt, bias, g)


def _pre_gla_kernel(x_ref, g_ref, w_ref, gsum_ref, mg_ref, gw_ref, gb_ref, tri_ref,
                    q_ref, k_ref, v_ref, r_ref, gc_ref, mq_ref):
    kw = GLA_HEADS * GLA_KP
    vw = GLA_HEADS * GLA_VP
    h = _rms_rows(x_ref[...], g_ref[...]).astype(BF16)
    q_ref[...] = _dot(h, w_ref[:, 0:kw]).astype(BF16)
    k_ref[...] = _dot(h, w_ref[:, kw:2 * kw]).astype(BF16)
    o = 2 * kw
    v_ref[...] = _dot(h, w_ref[:, o:o + vw]).astype(BF16)
    r_ref[...] = _dot(h, w_ref[:, o + vw:o + 2 * vw]).astype(BF16)
    o = o + 2 * vw
    gate_low = _dot(h, w_ref[:, o:o + LANES]).astype(BF16)
    mq_ref[...] = _group_rms(_dot(h, w_ref[:, o + LANES:]), gsum_ref[...], mg_ref[...]).astype(BF16)
    z = _dot(gate_low, gw_ref[...]) + gb_ref[...]
    log_a = (jnp.minimum(z, 0.0) - jnp.log1p(jnp.exp(-jnp.abs(z)))) * (1.0 / GLA_GATE_TAU)
    hi = log_a.astype(BF16)
    rem = log_a - hi.astype(F32)
    mid = rem.astype(BF16)
    lo = (rem - mid.astype(F32)).astype(BF16)
    tri = tri_ref[...]
    n = tri.shape[0]
    for c in range(log_a.shape[0] // n):
        rows = slice(c * n, (c + 1) * n)
        gc_ref[rows, :] = _dot(tri, hi[rows]) + _dot(tri, mid[rows]) + _dot(tri, lo[rows])


def _pre_gla(x2, g, w, gsum, mg, gw, gb, tri):
    t = x2.shape[0]
    kw = GLA_HEADS * GLA_KP
    vw = GLA_HEADS * GLA_VP
    row = lambda n: pl.BlockSpec((ROW_TILE, n), lambda i: (i, 0))
    return pl.pallas_call(
        _pre_gla_kernel,
        grid=(t // ROW_TILE,),
        in_specs=[row(D_MODEL), _const_spec((1, D_MODEL)), _const_spec(w.shape),
                  _const_spec(gsum.shape), _const_spec((1, MEM_WIDTH)), _const_spec(gw.shape),
                  _const_spec(gb.shape), _const_spec(tri.shape)],
        out_specs=[row(kw), row(kw), row(vw), row(vw), row(kw), row(MEM_WIDTH)],
        out_shape=[jax.ShapeDtypeStruct((t, kw), BF16), jax.ShapeDtypeStruct((t, kw), BF16),
                   jax.ShapeDtypeStruct((t, vw), BF16), jax.ShapeDtypeStruct((t, vw), BF16),
                   jax.ShapeDtypeStruct((t, kw), F32), jax.ShapeDtypeStruct((t, MEM_WIDTH), BF16)],
        compiler_params=_cparams("parallel"),
        name="pre_gla",
    )(x2, g, w, gsum, mg, gw, gb, tri)


def _gla_kernel(q_ref, k_ref, v_ref, r_ref, gc_ref, gain_ref, o_ref, s_scr):
    tg = GLA_TILE
    nchunk = tg // CHUNK

    @pl.when(pl.program_id(1) == 0)
    def _():
        s_scr[...] = jnp.zeros(s_scr.shape, F32)

    ri = lax.broadcasted_iota(jnp.int32, (tg, tg), 0)
    ci = lax.broadcasted_iota(jnp.int32, (tg, tg), 1)
    same_chunk = (ri // CHUNK) == (ci // CHUNK)
    past = ci <= ri
    row_chunk = lax.broadcasted_iota(jnp.int32, (tg, GLA_KP), 0) // CHUNK

    for h in range(GLA_HEADS):
        ks = slice(h * GLA_KP, (h + 1) * GLA_KP)
        vs = slice(h * GLA_VP, (h + 1) * GLA_VP)
        qh = q_ref[0, :, ks].astype(F32) * (GLA_K_DIM ** -0.5)
        kh = k_ref[0, :, ks].astype(F32)
        g = gc_ref[0, :, ks]
        vh = v_ref[0, :, vs]
        eg = jnp.exp(g)
        ieg = jnp.exp(-g)
        qe = (qh * eg).astype(BF16)
        a_past = _dot_nt(qe, (kh * ieg).astype(BF16))
        a_fut = _dot_nt((qh * ieg).astype(BF16), (kh * eg).astype(BF16))
        scores = jnp.where(same_chunk, jnp.where(past, a_past, a_fut), 0.0)
        o = _dot(scores.astype(BF16), vh)
        vt = vh.astype(F32).T.astype(BF16)
        g_end = jnp.concatenate(
            [jnp.broadcast_to(g[c * CHUNK + CHUNK - 1:c * CHUNK + CHUNK, :], (CHUNK, GLA_KP))
             for c in range(nchunk)], axis=0)
        kdec = kh * jnp.exp(g_end - g)
        inter = []
        for c in range(nchunk):
            rows = slice(c * CHUNK, (c + 1) * CHUNK)
            st = s_scr[h]
            inter.append(_dot_nt(qe[rows], st.astype(BF16)))
            kd_c = jnp.where(row_chunk == c, kdec, 0.0).astype(BF16)
            s_scr[h] = st * jnp.exp(g_end[rows][0:1, :]) + _dot(vt, kd_c)
        o = o + jnp.concatenate(inter, axis=0)
        ms = jnp.sum(o * o, axis=-1, keepdims=True) * (1.0 / GLA_V_DIM)
        y = o * lax.rsqrt(ms + EPS) * gain_ref[:, vs]
        rh = r_ref[0, :, vs].astype(F32)
        o_ref[0, :, vs] = (y * (rh / (1.0 + jnp.exp(-rh)))).astype(BF16)


def _gla(q, k, v, r, gc, gain):
    b, s = q.shape[0], q.shape[1]
    kw = GLA_HEADS * GLA_KP
    vw = GLA_HEADS * GLA_VP
    spec = lambda n: pl.BlockSpec((1, GLA_TILE, n), lambda bi, i: (bi, i, 0))
    return pl.pallas_call(
        _gla_kernel,
        grid=(b, s // GLA_TILE),
        in_specs=[spec(kw), spec(kw), spec(vw), spec(vw), spec(kw), _const_spec((1, vw))],
        out_specs=spec(vw),
        out_shape=jax.ShapeDtypeStruct((b, s, vw), BF16),
        scratch_shapes=[pltpu.VMEM((GLA_HEADS, GLA_VP, GLA_KP), F32)],
        compiler_params=_cparams("parallel", "arbitrary"),
        name="gla",
    )(q, k, v, r, gc, gain)


def _mem_kv_kernel(mem_ref, g_ref, w_ref, gsum_ref, kg_ref, k_ref, v_ref):
    h = _rms_rows(mem_ref[...], g_ref[...]).astype(BF16)
    k_ref[...] = _group_rms(_dot(h, w_ref[:, :MEM_WIDTH]), gsum_ref[...], kg_ref[...]).astype(BF16)
    v_ref[...] = _dot(h, w_ref[:, MEM_WIDTH:]).astype(BF16)


def _mem_kv(mem2, g, w, gsum, kg):
    n = mem2.shape[0]
    return pl.pallas_call(
        _mem_kv_kernel,
        grid=(1,),
        in_specs=[_const_spec(mem2.shape), _const_spec((1, D_MODEL)), _const_spec(w.shape),
                  _const_spec(gsum.shape), _const_spec((1, MEM_WIDTH))],
        out_specs=[_const_spec((n, MEM_WIDTH)), _const_spec((n, MEM_WIDTH))],
        out_shape=[jax.ShapeDtypeStruct((n, MEM_WIDTH), BF16)] * 2,
        compiler_params=_cparams("arbitrary"),
        name="mem_kv",
    )(mem2, g, w, gsum, kg)


def _mix_out_kernel(x_ref, mix_ref, mq_ref, kbd_ref, vbd_ref, wa_ref, wb_ref, o_ref):
    m = kbd_ref.shape[2] // MEM_HEADS
    logits = _dot(mq_ref[0], kbd_ref[0])
    ps = []
    for h in range(MEM_HEADS):
        s = logits[:, h * m:(h + 1) * m]
        e = jnp.exp(s - jnp.max(s, axis=-1, keepdims=True))
        ps.append((e * (1.0 / jnp.sum(e, axis=-1, keepdims=True))).astype(BF16))
    cross = _dot(jnp.concatenate(ps, axis=1), vbd_ref[0])
    o_ref[0] = x_ref[0] + _dot(mix_ref[0], wa_ref[...]) + _dot(cross.astype(BF16), wb_ref[...])


def _mix_out(x, mix, mq, kbd, vbd, wa, wb):
    b, s = x.shape[0], x.shape[1]
    spec = lambda n: pl.BlockSpec((1, ROW_TILE, n), lambda bi, i: (bi, i, 0))
    per_b = lambda a: pl.BlockSpec((1,) + a.shape[1:], lambda bi, i: (bi, 0, 0))
    return pl.pallas_call(
        _mix_out_kernel,
        grid=(b, s // ROW_TILE),
        in_specs=[spec(D_MODEL), spec(mix.shape[2]), spec(MEM_WIDTH), per_b(kbd), per_b(vbd),
                  _const_spec(wa.shape), _const_spec(wb.shape)],
        out_specs=spec(D_MODEL),
        out_shape=jax.ShapeDtypeStruct(x.shape, F32),
        compiler_params=_cparams("parallel", "parallel"),
        name="mix_out",
    )(x, mix, mq, kbd, vbd, wa, wb)


def _ffn_kernel(x_ref, g_ref, wu_ref, cw_ref, cb_ref, wd_ref, o_ref, u_scr, act_scr):
    tm = FFN_TILE
    cr = CARRY_ROWS

    @pl.when(pl.program_id(1) == 0)
    def _():
        u_scr[0:cr, :] = jnp.zeros((cr, u_scr.shape[1]), F32)

    x = x_ref[0]
    h = _rms_rows(x, g_ref[...]).astype(BF16)

    def conv(cols):
        u_scr[cr:cr + tm, cols] = _dot(h, wu_ref[:, cols])
        c = (cw_ref[0:1, cols] * u_scr[cr - 2:cr - 2 + tm, cols]
             + cw_ref[1:2, cols] * u_scr[cr - 1:cr - 1 + tm, cols]
             + cw_ref[2:3, cols] * u_scr[cr:cr + tm, cols]
             + cb_ref[:, cols])
        u_scr[0:cr, cols] = u_scr[tm:tm + cr, cols]
        return c

    for j in range(D_FF // FFN_COLS):
        a = conv(slice(j * FFN_COLS, (j + 1) * FFN_COLS))
        gte = conv(slice(D_FF + j * FFN_COLS, D_FF + (j + 1) * FFN_COLS))
        act_scr[:, j * FFN_COLS:(j + 1) * FFN_COLS] = (a * (gte / (1.0 + jnp.exp(-gte)))).astype(BF16)

    o_ref[0] = x + _dot(act_scr[...], wd_ref[...])


def _ffn(x, g, wu, cw, cb, wd):
    b, s = x.shape[0], x.shape[1]
    spec = pl.BlockSpec((1, FFN_TILE, D_MODEL), lambda bi, i: (bi, i, 0))
    single = lambda a: pl.BlockSpec(a.shape, lambda bi, i: (0,) * a.ndim, pipeline_mode=pl.Buffered(1))
    return pl.pallas_call(
        _ffn_kernel,
        grid=(b, s // FFN_TILE),
        in_specs=[spec, _const_spec((1, D_MODEL)), single(wu), _const_spec(cw.shape),
                  _const_spec(cb.shape), single(wd)],
        out_specs=spec,
        out_shape=jax.ShapeDtypeStruct(x.shape, F32),
        scratch_shapes=[pltpu.VMEM((FFN_TILE + CARRY_ROWS, 2 * D_FF), F32),
                        pltpu.VMEM((FFN_TILE, D_FF), BF16)],
        compiler_params=_cparams("parallel", "arbitrary"),
        name="ffn",
    )(x, g, wu, cw, cb, wd)


def _t5_bucket(rel):
    half = REL_BUCKETS // 2
    max_exact = half // 2
    ret = jnp.where(rel > 0, half, 0)
    n = jnp.abs(rel)
    nf = jnp.maximum(n, 1).astype(jnp.float32)
    large = max_exact + (jnp.log(nf / max_exact) / math.log(REL_MAX_DIST / max_exact)
                         * (half - max_exact)).astype(jnp.int32)
    large = jnp.minimum(large, half - 1)
    return ret + jnp.where(n < max_exact, n, large)


def _bias_tiles(rel_bias):
    blk = ATT_BLK
    assert blk >= REL_MAX_DIST
    table = rel_bias.astype(F32).T[:, :, None]

    def lookup(rel):
        bucket = _t5_bucket(rel)
        out = jnp.zeros((table.shape[0],) + rel.shape, F32)
        for i in range(REL_BUCKETS):
            out = jnp.where(bucket == i, table[:, i], out)
        return out

    nh = table.shape[0]
    far = lookup(jnp.full((1,), -2 * blk))
    rel = jnp.arange(blk - 1, -2 * blk, -1)
    vec = (lookup(rel) - far) * LOG2E

    def toeplitz(g):
        gp = jnp.pad(g, ((0, 0), (0, 1)))
        rows = jnp.broadcast_to(gp[:, None, :], (nh, blk, 2 * blk)).reshape(nh, -1)
        skew = rows[:, :blk * (2 * blk - 1)].reshape(nh, blk, 2 * blk - 1)
        return skew[:, :, blk - 1:]

    diag = toeplitz(vec[:, :2 * blk - 1])
    near = toeplitz(vec[:, blk:])
    visible = (jnp.arange(blk)[:, None] // CHUNK) <= (jnp.arange(blk)[None, :] // CHUNK)
    diag = jnp.where(visible, diag, -jnp.inf)
    return jnp.stack([diag, near], axis=1)


def _group_sum_matrix():
    i = jnp.arange(MXU_DIM)
    return ((i[:, None] // 64) == (i[None, :] // 64)).astype(BF16)


def _chunk_tri_matrix(n):
    i = jnp.arange(n)
    return (((i[:, None] // CHUNK) == (i[None, :] // CHUNK)) & (i[None, :] <= i[:, None])).astype(BF16)


def _pad_heads(w, heads, dim, pad, axis):
    shape = list(w.shape)
    shape[axis:axis + 1] = [heads, dim]
    w = w.reshape(shape)
    widths = [(0, 0)] * w.ndim
    widths[axis + 1] = (0, pad - dim)
    w = jnp.pad(w, widths)
    shape[axis:axis + 2] = [heads * pad]
    return w.reshape(shape)


def _tile_gain(g, reps, scale=1.0):
    return (jnp.tile(g.astype(F32), reps) * scale)[None, :]


def _mem_block_diag(kn, v, b):
    m = kn.shape[0] // b
    eye = jnp.eye(MEM_HEADS, dtype=BF16)
    knt = kn.reshape(b, m, MEM_WIDTH).transpose(0, 2, 1)
    kbd = (knt.reshape(b, MEM_HEADS, MEM_HEAD_DIM, 1, m) * eye.reshape(1, MEM_HEADS, 1, MEM_HEADS, 1))
    kbd = kbd.reshape(b, MEM_WIDTH, MEM_HEADS * m)
    vbd = (v.reshape(b, 1, m, MEM_HEADS, MEM_HEAD_DIM) * eye.reshape(1, MEM_HEADS, 1, MEM_HEADS, 1))
    vbd = vbd.reshape(b, MEM_HEADS * m, MEM_WIDTH)
    return kbd, vbd


def kernel(x, mem, rel_bias, attn_norm, ffn_norm, mem_norm, w_in_diff, diff_qk_norm, diff_lambda,
           diff_out_norm, w_in_gla, gla_gate_w, gla_gate_b, gla_out_norm, w_mem_kv, mem_qk_norm,
           w_out, w_up, conv_w, conv_b, w_down):
    b, s, d = x.shape
    t = b * s
    tw = TOKEN_WIDTH
    gsum = _group_sum_matrix()
    mem2 = mem.reshape(b * mem.shape[1], d)
    x = x.astype(F32)

    for i in range(DEPTH):
        j = i // 2
        x2 = x.reshape(t, d)
        mq_gain = _tile_gain(mem_qk_norm[i, 0], MEM_HEADS, MEM_HEAD_DIM ** -0.5)
        if i % 2 == 0:
            nh, hd = DIFF_HEADS, DIFF_HEAD_DIM
            qt, k, vt, mq = _pre_diff(
                x2, attn_norm[i][None, :], w_in_diff[j].astype(BF16), gsum,
                _tile_gain(diff_qk_norm[j, 0], 2 * nh, hd ** -0.5 * LOG2E),
                _tile_gain(diff_qk_norm[j, 1], 2 * nh), mq_gain, b)
            lv = diff_lambda[j].astype(F32)
            lam_init = 0.8 - 0.6 * math.exp(-0.3 * i)
            lam = jnp.exp(jnp.sum(lv[0] * lv[1])) - jnp.exp(jnp.sum(lv[2] * lv[3])) + lam_init
            mix = _diff_attn(lam.reshape(1, 1), qt, k.reshape(b, s, tw), vt, _bias_tiles(rel_bias),
                             diff_out_norm[j].astype(F32)[None, :], 1.0 - lam_init)
            w_mix = w_out[i, :tw]
        else:
            kw = GLA_HEADS * GLA_K_DIM
            w = w_in_gla[j]
            hp = functools.partial(_pad_heads, heads=GLA_HEADS, axis=1)
            w_p = jnp.concatenate([
                hp(w[:, :kw], dim=GLA_K_DIM, pad=GLA_KP),
                hp(w[:, kw:2 * kw], dim=GLA_K_DIM, pad=GLA_KP),
                hp(w[:, 2 * kw:2 * kw + tw], dim=GLA_V_DIM, pad=GLA_VP),
                hp(w[:, 2 * kw + tw:2 * kw + 2 * tw], dim=GLA_V_DIM, pad=GLA_VP),
                jnp.pad(w[:, 2 * kw + 2 * tw:2 * kw + 2 * tw + GLA_GATE_RANK],
                        ((0, 0), (0, LANES - GLA_GATE_RANK))),
                w[:, 2 * kw + 2 * tw + GLA_GATE_RANK:]], axis=1).astype(BF16)
            gw = jnp.pad(hp(gla_gate_w[j], dim=GLA_K_DIM, pad=GLA_KP),
                         ((0, LANES - GLA_GATE_RANK), (0, 0))).astype(BF16)
            gb = _pad_heads(gla_gate_b[j].astype(F32)[None, :], GLA_HEADS, GLA_K_DIM, GLA_KP, 1)
            q, k, v, r, gc, mq = _pre_gla(x2, attn_norm[i][None, :], w_p, gsum, mq_gain, gw, gb,
                                          _chunk_tri_matrix(MXU_DIM))
            gain = _pad_heads(jnp.tile(gla_out_norm[j].astype(F32), GLA_HEADS)[None, :],
                              GLA_HEADS, GLA_V_DIM, GLA_VP, 1)
            sh = lambda a: a.reshape(b, s, a.shape[1])
            mix = _gla(sh(q), sh(k), sh(v), sh(r), sh(gc), gain)
            w_mix = _pad_heads(w_out[i, :tw], GLA_HEADS, GLA_V_DIM, GLA_VP, 0)

        kn, vm = _mem_kv(mem2, mem_norm[i][None, :], w_mem_kv[i].astype(BF16), gsum,
                         _tile_gain(mem_qk_norm[i, 1], MEM_HEADS))
        kbd, vbd = _mem_block_diag(kn, vm, b)
        x = _mix_out(x, mix.reshape(b, s, -1), mq.reshape(b, s, MEM_WIDTH), kbd, vbd,
                     w_mix.astype(BF16), w_out[i, tw:].astype(BF16))
        x = _ffn(x, ffn_norm[i][None, :], w_up[i].astype(BF16), conv_w[i].astype(F32),
                 conv_b[i].astype(F32)[None, :], w_down[i].astype(BF16))
    return x
```

```python
import functools
import math

import jax
import jax.numpy as jnp
from jax import lax
from jax.experimental import pallas as pl
from jax.experimental.pallas import tpu as pltpu

F32 = jnp.float32
BF16 = jnp.bfloat16

D_MODEL = 1024
DEPTH = 2
CHUNK = 64
MEM_WIDTH = D_MODEL // 4
MEM_HEADS = 4
MEM_HEAD_DIM = MEM_WIDTH // MEM_HEADS
TOKEN_WIDTH = D_MODEL - MEM_WIDTH
DIFF_HEAD_DIM = 64
DIFF_HEADS = TOKEN_WIDTH // (2 * DIFF_HEAD_DIM)
GLA_HEADS = 4
GLA_V_DIM = TOKEN_WIDTH // GLA_HEADS
GLA_K_DIM = GLA_V_DIM // 2
GLA_GATE_RANK = 16
GLA_GATE_TAU = 16.0
REL_BUCKETS = 32
REL_MAX_DIST = 128
D_FF = ((8 * D_MODEL // 3 + 127) // 128) * 128
EPS = 1e-6
LOG2E = math.log2(math.e)

LANES = 128
MXU_DIM = 256
VMEM_LIMIT_BYTES = 56 * 1024 * 1024

ROW_TILE = 512
ATT_BLK = 512
ATT_STREAMS = 4
ATT_V_ROWS = 2 * DIFF_HEAD_DIM + 16
GLA_TILE = 256
FFN_TILE = 512
FFN_COLS = 256
GLA_KP = 128
GLA_VP = 256
CARRY_ROWS = 8


def _cparams(*sem):
    return pltpu.CompilerParams(dimension_semantics=sem, vmem_limit_bytes=VMEM_LIMIT_BYTES)


def _const_spec(shape):
    n = len(shape)
    return pl.BlockSpec(shape, lambda *_: (0,) * n)


def _rms_rows(x, g):
    ms = jnp.mean(x * x, axis=-1, keepdims=True)
    return x * lax.rsqrt(ms + EPS) * g


def _group_rms(t, gsum, gain):
    cols = []
    for c in range(t.shape[1] // MXU_DIM):
        blk = t[:, c * MXU_DIM:(c + 1) * MXU_DIM]
        ss = jnp.dot((blk * blk).astype(BF16), gsum, preferred_element_type=F32)
        cols.append(blk * lax.rsqrt(ss * (1.0 / 64) + EPS))
    out = cols[0] if len(cols) == 1 else jnp.concatenate(cols, axis=1)
    return out * gain


def _dot(a, b):
    return jnp.dot(a, b, preferred_element_type=F32)


def _dot_nt(a, b):
    return lax.dot_general(a, b, (((1,), (1,)), ((), ())), preferred_element_type=F32)


def _pre_diff_kernel(x_ref, g_ref, w_ref, gsum_ref, qg_ref, kg_ref, mg_ref,
                     qt_ref, k_ref, vt_ref, mq_ref):
    tw = TOKEN_WIDTH
    hw = 2 * DIFF_HEAD_DIM
    h = _rms_rows(x_ref[...], g_ref[...]).astype(BF16)
    gsum = gsum_ref[...]
    q = _group_rms(_dot(h, w_ref[:, 0:tw]), gsum, qg_ref[...])
    k_ref[...] = _group_rms(_dot(h, w_ref[:, tw:2 * tw]), gsum, kg_ref[...]).astype(BF16)
    v = _dot(h, w_ref[:, 2 * tw:3 * tw])
    mq_ref[...] = _group_rms(_dot(h, w_ref[:, 3 * tw:]), gsum, mg_ref[...]).astype(BF16)
    ones = jnp.ones((ATT_V_ROWS - hw, ROW_TILE), BF16)
    for n in range(DIFF_HEADS):
        qt_ref[0, n, 0] = q[:, n * hw:(n + 1) * hw].T.astype(BF16)
        vt_ref[0, n, 0, 0:hw, :] = v[:, n * hw:(n + 1) * hw].T.astype(BF16)
        vt_ref[0, n, 0, hw:, :] = ones


def _pre_diff(x2, g, w, gsum, qg, kg, mg, b):
    t = x2.shape[0]
    tw = TOKEN_WIDTH
    hw = 2 * DIFF_HEAD_DIM
    assert ROW_TILE == ATT_BLK
    nq = t // b // ATT_BLK
    row = lambda n: pl.BlockSpec((ROW_TILE, n), lambda i: (i, 0))
    per_head = lambda r: pl.BlockSpec((1, DIFF_HEADS, 1, r, ATT_BLK), lambda i: (i // nq, 0, i % nq, 0, 0))
    return pl.pallas_call(
        _pre_diff_kernel,
        grid=(t // ROW_TILE,),
        in_specs=[row(D_MODEL), _const_spec((1, D_MODEL)), _const_spec(w.shape),
                  _const_spec(gsum.shape), _const_spec((1, tw)), _const_spec((1, tw)),
                  _const_spec((1, MEM_WIDTH))],
        out_specs=[per_head(hw), row(tw), per_head(ATT_V_ROWS), row(MEM_WIDTH)],
        out_shape=[jax.ShapeDtypeStruct((b, DIFF_HEADS, nq, hw, ATT_BLK), BF16),
                   jax.ShapeDtypeStruct((t, tw), BF16),
                   jax.ShapeDtypeStruct((b, DIFF_HEADS, nq, ATT_V_ROWS, ATT_BLK), BF16),
                   jax.ShapeDtypeStruct((t, MEM_WIDTH), BF16)],
        compiler_params=_cparams("parallel"),
        name="pre_diff",
    )(x2, g, w, gsum, qg, kg, mg)


def _diff_attn_kernel(lam_ref, qt_ref, k_ref, vt_ref, bvec_ref, g_ref, o_ref,
                      bias_scr, *scratch, out_scale):
    blk = ATT_BLK
    hd = DIFF_HEAD_DIM
    ns = ATT_STREAMS
    qp_scr, s_scr, cm_scr, p_scr, a_scr, m_scr, acc_scr = (scratch[i * ns:(i + 1) * ns] for i in range(7))
    w = 2 * blk // ns
    qi = pl.program_id(2)
    qt = qt_ref[0, 0, 0]
    row = lax.broadcasted_iota(jnp.int32, qt.shape, 0)
    zero = jnp.zeros_like(qt)
    q_maps = (jnp.where(row < hd, qt, zero), jnp.where(row >= hd, qt, zero))
    cols = [slice((x % (ns // 2)) * w, (x % (ns // 2) + 1) * w) for x in range(ns)]
    for x in range(ns):
        qp_scr[x][...] = q_maps[x // (ns // 2)][:, cols[x]]
        m_scr[x][...] = jnp.full(m_scr[x].shape, -jnp.inf, F32)
        acc_scr[x][...] = jnp.zeros(acc_scr[x].shape, F32)

    @pl.when(qi == 0)
    def _():
        kk = lax.broadcasted_iota(jnp.int32, (blk, blk), 0)
        qq = lax.broadcasted_iota(jnp.int32, (blk, blk), 1)
        visible = (kk // CHUNK) <= (qq // CHUNK)
        for i in range(2):
            rows = jnp.broadcast_to(bvec_ref[0, i], (blk, 2 * blk))
            tile = pltpu.roll(rows, 0, 1, stride=1, stride_axis=0)[:, :blk]
            bias_scr[i] = jnp.where(visible, tile, -jnp.inf) if i == 0 else tile

    def key_block(t):
        return jnp.maximum(qi - t, 0)

    def logits(x, t, bias=None):
        start = pl.multiple_of(key_block(t) * blk, blk)
        s = _dot(k_ref[0, pl.ds(start, blk), :], qp_scr[x][...])
        if bias is not None:
            s = s + bias[:, cols[x]]
        s_scr[x][...] = s
        part = s[0:8]
        for r in range(8, blk, 8):
            part = jnp.maximum(part, s[r:r + 8])
        cm_scr[x][...] = part

    def softmax(x):
        m_old = m_scr[x][...]
        m_new = jnp.maximum(m_old, jnp.max(cm_scr[x][...], axis=0, keepdims=True))
        a_scr[x][...] = jnp.exp2(m_old - m_new)
        m_scr[x][...] = m_new
        for r in range(0, blk, 16):
            p_scr[x][r:r + 16, :] = jnp.exp2((s_scr[x][r:r + 16, :] - m_new).astype(BF16))

    def values(x, t):
        acc_scr[x][...] = a_scr[x][...] * acc_scr[x][...] + _dot(vt_ref[0, 0, key_block(t)], p_scr[x][...])

    def step(t, bias=None):
        for x in range(ns):
            logits(x, t + 1, bias)
            values(x, t)
            softmax((x + 1) % ns)

    for x in range(ns):
        logits(x, 0, bias_scr[0])
    softmax(0)
    step(0, bias_scr[1] + jnp.where(qi == 0, -jnp.inf, 0.0).astype(F32))

    t_last = jnp.maximum(qi, 1)

    def far_pair(u, c):
        step(1 + 2 * u)
        step(2 + 2 * u)
        return c

    lax.fori_loop(0, (t_last - 1) // 2, far_pair, 0)

    @pl.when((t_last - 1) % 2 == 1)
    def _():
        step(t_last - 1)

    for x in range(ns):
        values(x, t_last)
        if x + 1 < ns:
            softmax(x + 1)

    lam = lam_ref[0, 0]
    half = ns // 2
    outs = []
    for x in range(half):
        o0 = acc_scr[x][0:2 * hd, :] * (1.0 / acc_scr[x][2 * hd:2 * hd + 1, :])
        o1 = acc_scr[half + x][0:2 * hd, :] * (1.0 / acc_scr[half + x][2 * hd:2 * hd + 1, :])
        outs.append(o0 - lam * o1)
    ot = outs[0] if half == 1 else jnp.concatenate(outs, axis=1)
    ms = jnp.mean(ot * ot, axis=0, keepdims=True)
    y = (ot * lax.rsqrt(ms + EPS)).T * (g_ref[...] * out_scale)
    o_ref[0] = y.astype(BF16)


def _diff_attn(lam, qt, k, vt, bias, g, out_scale):
    b, nh, nq = qt.shape[0], qt.shape[1], qt.shape[2]
    s = k.shape[1]
    blk = ATT_BLK
    ns = ATT_STREAMS
    w = 2 * blk // ns
    return pl.pallas_call(
        functools.partial(_diff_attn_kernel, out_scale=out_scale),
        grid=(b, nh, nq),
        in_specs=[
            pl.BlockSpec(memory_space=pltpu.SMEM),
            pl.BlockSpec((1, 1, 1, 2 * DIFF_HEAD_DIM, blk), lambda bi, hi, qi: (bi, hi, qi, 0, 0)),
            pl.BlockSpec((1, s, 2 * DIFF_HEAD_DIM), lambda bi, hi, qi: (bi, 0, hi)),
            pl.BlockSpec((1, 1, nq, ATT_V_ROWS, blk), lambda bi, hi, qi: (bi, hi, 0, 0, 0)),
            pl.BlockSpec((1, 2, 1, 2 * blk), lambda bi, hi, qi: (hi, 0, 0, 0)),
            _const_spec((1, 2 * DIFF_HEAD_DIM)),
        ],
        out_specs=pl.BlockSpec((1, blk, 2 * DIFF_HEAD_DIM), lambda bi, hi, qi: (bi, qi, hi)),
        out_shape=jax.ShapeDtypeStruct((b, s, nh * 2 * DIFF_HEAD_DIM), BF16),
        scratch_shapes=[pltpu.VMEM((2, blk, blk), F32)] + [pltpu.VMEM(shape, dtype) for shape, dtype in (
            ((2 * DIFF_HEAD_DIM, w), BF16), ((blk, w), F32), ((8, w), F32), ((blk, w), BF16),
            ((1, w), F32), ((1, w), F32), ((ATT_V_ROWS, w), F32)) for _ in range(ns)],
        compiler_params=_cparams("parallel", "parallel", "arbitrary"),
        name="diff_attn",
    )(lam, qt, k, vt, bias, g)


def _pre_gla_kernel(x_ref, g_ref, w_ref, gsum_ref, mg_ref, gw_ref, gb_ref, tri_ref,
                    q_ref, k_ref, v_ref, r_ref, gc_ref, mq_ref):
    kw = GLA_HEADS * GLA_KP
    vw = GLA_HEADS * GLA_VP
    h = _rms_rows(x_ref[...], g_ref[...]).astype(BF16)
    q_ref[...] = _dot(h, w_ref[:, 0:kw]).astype(BF16)
    k_ref[...] = _dot(h, w_ref[:, kw:2 * kw]).astype(BF16)
    o = 2 * kw
    v_ref[...] = _dot(h, w_ref[:, o:o + vw]).astype(BF16)
    r_ref[...] = _dot(h, w_ref[:, o + vw:o + 2 * vw]).astype(BF16)
    o = o + 2 * vw
    gate_low = _dot(h, w_ref[:, o:o + LANES]).astype(BF16)
    mq_ref[...] = _group_rms(_dot(h, w_ref[:, o + LANES:]), gsum_ref[...], mg_ref[...]).astype(BF16)
    z = _dot(gate_low, gw_ref[...]) + gb_ref[...]
    log_a = (jnp.minimum(z, 0.0) - jnp.log1p(jnp.exp(-jnp.abs(z)))) * (1.0 / GLA_GATE_TAU)
    hi = log_a.astype(BF16)
    rem = log_a - hi.astype(F32)
    mid = rem.astype(BF16)
    lo = (rem - mid.astype(F32)).astype(BF16)
    tri = tri_ref[...]
    n = tri.shape[0]
    for c in range(log_a.shape[0] // n):
        rows = slice(c * n, (c + 1) * n)
        gc_ref[rows, :] = _dot(tri, hi[rows]) + _dot(tri, mid[rows]) + _dot(tri, lo[rows])


def _pre_gla(x2, g, w, gsum, mg, gw, gb, tri):
    t = x2.shape[0]
    kw = GLA_HEADS * GLA_KP
    vw = GLA_HEADS * GLA_VP
    row = lambda n: pl.BlockSpec((ROW_TILE, n), lambda i: (i, 0))
    return pl.pallas_call(
        _pre_gla_kernel,
        grid=(t // ROW_TILE,),
        in_specs=[row(D_MODEL), _const_spec((1, D_MODEL)), _const_spec(w.shape),
                  _const_spec(gsum.shape), _const_spec((1, MEM_WIDTH)), _const_spec(gw.shape),
                  _const_spec(gb.shape), _const_spec(tri.shape)],
        out_specs=[row(kw), row(kw), row(vw), row(vw), row(kw), row(MEM_WIDTH)],
        out_shape=[jax.ShapeDtypeStruct((t, kw), BF16), jax.ShapeDtypeStruct((t, kw), BF16),
                   jax.ShapeDtypeStruct((t, vw), BF16), jax.ShapeDtypeStruct((t, vw), BF16),
                   jax.ShapeDtypeStruct((t, kw), F32), jax.ShapeDtypeStruct((t, MEM_WIDTH), BF16)],
        compiler_params=_cparams("parallel"),
        name="pre_gla",
    )(x2, g, w, gsum, mg, gw, gb, tri)


def _gla_kernel(q_ref, k_ref, v_ref, r_ref, gc_ref, gain_ref, o_ref, s_scr):
    tg = GLA_TILE
    nchunk = tg // CHUNK
    heads = range(GLA_HEADS)
    ks = [slice(h * GLA_KP, (h + 1) * GLA_KP) for h in heads]
    vs = [slice(h * GLA_VP, (h + 1) * GLA_VP) for h in heads]

    @pl.when(pl.program_id(1) == 0)
    def _():
        s_scr[...] = jnp.zeros(s_scr.shape, F32)

    ri = lax.broadcasted_iota(jnp.int32, (tg, tg), 0)
    ci = lax.broadcasted_iota(jnp.int32, (tg, tg), 1)
    same_chunk = (ri // CHUNK) == (ci // CHUNK)
    past = ci <= ri
    row_chunk = lax.broadcasted_iota(jnp.int32, (tg, GLA_KP), 0) // CHUNK

    qe, scores, kv, decay = [], [], [], []
    for h in heads:
        qh = q_ref[0, :, ks[h]].astype(F32) * (GLA_K_DIM ** -0.5)
        kh = k_ref[0, :, ks[h]].astype(F32)
        g = gc_ref[0, :, ks[h]]
        eg = jnp.exp(g)
        ieg = jnp.exp(-g)
        qe.append((qh * eg).astype(BF16))
        a_past = _dot_nt(qe[h], (kh * ieg).astype(BF16))
        a_fut = _dot_nt((qh * ieg).astype(BF16), (kh * eg).astype(BF16))
        scores.append(jnp.where(same_chunk, jnp.where(past, a_past, a_fut), 0.0).astype(BF16))
        vt = v_ref[0, :, vs[h]].astype(F32).T.astype(BF16)
        g_last = [g[c * CHUNK + CHUNK - 1:c * CHUNK + CHUNK, :] for c in range(nchunk)]
        g_end = jnp.concatenate([jnp.broadcast_to(gl, (CHUNK, GLA_KP)) for gl in g_last], axis=0)
        kdec = kh * jnp.exp(g_end - g)
        kv.append([_dot(vt, jnp.where(row_chunk == c, kdec, 0.0).astype(BF16)) for c in range(nchunk)])
        decay.append([jnp.exp(gl) for gl in g_last])

    starts = []
    for h in heads:
        st = s_scr[h]
        per_chunk = []
        for c in range(nchunk):
            per_chunk.append(st.astype(BF16))
            st = st * decay[h][c] + kv[h][c]
        s_scr[h] = st
        starts.append(per_chunk)

    for h in heads:
        inter = [_dot_nt(qe[h][c * CHUNK:(c + 1) * CHUNK], starts[h][c]) for c in range(nchunk)]
        o = _dot(scores[h], v_ref[0, :, vs[h]]) + jnp.concatenate(inter, axis=0)
        ms = jnp.sum(o * o, axis=-1, keepdims=True) * (1.0 / GLA_V_DIM)
        y = o * lax.rsqrt(ms + EPS) * gain_ref[:, vs[h]]
        rh = r_ref[0, :, vs[h]].astype(F32)
        o_ref[0, :, vs[h]] = (y * (rh / (1.0 + jnp.exp(-rh)))).astype(BF16)


def _gla(q, k, v, r, gc, gain):
    b, s = q.shape[0], q.shape[1]
    kw = GLA_HEADS * GLA_KP
    vw = GLA_HEADS * GLA_VP
    spec = lambda n: pl.BlockSpec((1, GLA_TILE, n), lambda bi, i: (bi, i, 0))
    return pl.pallas_call(
        _gla_kernel,
        grid=(b, s // GLA_TILE),
        in_specs=[spec(kw), spec(kw), spec(vw), spec(vw), spec(kw), _const_spec((1, vw))],
        out_specs=spec(vw),
        out_shape=jax.ShapeDtypeStruct((b, s, vw), BF16),
        scratch_shapes=[pltpu.VMEM((GLA_HEADS, GLA_VP, GLA_KP), F32)],
        compiler_params=_cparams("parallel", "arbitrary"),
        name="gla",
    )(q, k, v, r, gc, gain)


def _mem_kv_kernel(mem_ref, g_ref, w_ref, gsum_ref, kg_ref, k_ref, v_ref):
    h = _rms_rows(mem_ref[...], g_ref[...]).astype(BF16)
    k_ref[...] = _group_rms(_dot(h, w_ref[:, :MEM_WIDTH]), gsum_ref[...], kg_ref[...]).astype(BF16)
    v_ref[...] = _dot(h, w_ref[:, MEM_WIDTH:]).astype(BF16)


def _mem_kv(mem2, g, w, gsum, kg):
    n = mem2.shape[0]
    return pl.pallas_call(
        _mem_kv_kernel,
        grid=(1,),
        in_specs=[_const_spec(mem2.shape), _const_spec((1, D_MODEL)), _const_spec(w.shape),
                  _const_spec(gsum.shape), _const_spec((1, MEM_WIDTH))],
        out_specs=[_const_spec((n, MEM_WIDTH)), _const_spec((n, MEM_WIDTH))],
        out_shape=[jax.ShapeDtypeStruct((n, MEM_WIDTH), BF16)] * 2,
        compiler_params=_cparams("arbitrary"),
        name="mem_kv",
    )(mem2, g, w, gsum, kg)


def _mix_out_kernel(x_ref, mix_ref, mq_ref, kbd_ref, vbd_ref, wa_ref, wb_ref, o_ref):
    m = kbd_ref.shape[2] // MEM_HEADS
    logits = _dot(mq_ref[0], kbd_ref[0])
    ps = []
    for h in range(MEM_HEADS):
        s = logits[:, h * m:(h + 1) * m]
        e = jnp.exp(s - jnp.max(s, axis=-1, keepdims=True))
        ps.append((e * (1.0 / jnp.sum(e, axis=-1, keepdims=True))).astype(BF16))
    cross = _dot(jnp.concatenate(ps, axis=1), vbd_ref[0])
    o_ref[0] = x_ref[0] + _dot(mix_ref[0], wa_ref[...]) + _dot(cross.astype(BF16), wb_ref[...])


def _mix_out(x, mix, mq, kbd, vbd, wa, wb):
    b, s = x.shape[0], x.shape[1]
    spec = lambda n: pl.BlockSpec((1, ROW_TILE, n), lambda bi, i: (bi, i, 0))
    per_b = lambda a: pl.BlockSpec((1,) + a.shape[1:], lambda bi, i: (bi, 0, 0))
    return pl.pallas_call(
        _mix_out_kernel,
        grid=(b, s // ROW_TILE),
        in_specs=[spec(D_MODEL), spec(mix.shape[2]), spec(MEM_WIDTH), per_b(kbd), per_b(vbd),
                  _const_spec(wa.shape), _const_spec(wb.shape)],
        out_specs=spec(D_MODEL),
        out_shape=jax.ShapeDtypeStruct(x.shape, F32),
        compiler_params=_cparams("parallel", "parallel"),
        name="mix_out",
    )(x, mix, mq, kbd, vbd, wa, wb)


def _ffn_kernel(x_ref, g_ref, wu_ref, cw_ref, cb_ref, wd_ref, o_ref, u_scr, act_scr):
    tm = FFN_TILE
    cr = CARRY_ROWS

    @pl.when(pl.program_id(1) == 0)
    def _():
        u_scr[0:cr, :] = jnp.zeros((cr, u_scr.shape[1]), F32)

    x = x_ref[0]
    h = _rms_rows(x, g_ref[...]).astype(BF16)

    def conv(cols):
        u_scr[cr:cr + tm, cols] = _dot(h, wu_ref[:, cols])
        c = (cw_ref[0:1, cols] * u_scr[cr - 2:cr - 2 + tm, cols]
             + cw_ref[1:2, cols] * u_scr[cr - 1:cr - 1 + tm, cols]
             + cw_ref[2:3, cols] * u_scr[cr:cr + tm, cols]
             + cb_ref[:, cols])
        u_scr[0:cr, cols] = u_scr[tm:tm + cr, cols]
        return c

    for j in range(D_FF // FFN_COLS):
        a = conv(slice(j * FFN_COLS, (j + 1) * FFN_COLS))
        gte = conv(slice(D_FF + j * FFN_COLS, D_FF + (j + 1) * FFN_COLS))
        act_scr[:, j * FFN_COLS:(j + 1) * FFN_COLS] = (a * (gte / (1.0 + jnp.exp(-gte)))).astype(BF16)

    o_ref[0] = x + _dot(act_scr[...], wd_ref[...])


def _ffn(x, g, wu, cw, cb, wd):
    b, s = x.shape[0], x.shape[1]
    spec = pl.BlockSpec((1, FFN_TILE, D_MODEL), lambda bi, i: (bi, i, 0))
    single = lambda a: pl.BlockSpec(a.shape, lambda bi, i: (0,) * a.ndim, pipeline_mode=pl.Buffered(1))
    return pl.pallas_call(
        _ffn_kernel,
        grid=(b, s // FFN_TILE),
        in_specs=[spec, _const_spec((1, D_MODEL)), single(wu), _const_spec(cw.shape),
                  _const_spec(cb.shape), single(wd)],
        out_specs=spec,
        out_shape=jax.ShapeDtypeStruct(x.shape, F32),
        scratch_shapes=[pltpu.VMEM((FFN_TILE + CARRY_ROWS, 2 * D_FF), F32),
                        pltpu.VMEM((FFN_TILE, D_FF), BF16)],
        compiler_params=_cparams("parallel", "arbitrary"),
        name="ffn",
    )(x, g, wu, cw, cb, wd)


def _t5_bucket(rel):
    half = REL_BUCKETS // 2
    max_exact = half // 2
    ret = jnp.where(rel > 0, half, 0)
    n = jnp.abs(rel)
    nf = jnp.maximum(n, 1).astype(jnp.float32)
    large = max_exact + (jnp.log(nf / max_exact) / math.log(REL_MAX_DIST / max_exact)
                         * (half - max_exact)).astype(jnp.int32)
    large = jnp.minimum(large, half - 1)
    return ret + jnp.where(n < max_exact, n, large)


def _bias_vectors(rel_bias):
    blk = ATT_BLK
    assert blk >= REL_MAX_DIST
    table = rel_bias.astype(F32).T[:, :, None]

    def lookup(rel):
        bucket = _t5_bucket(rel)
        out = jnp.zeros((table.shape[0],) + rel.shape, F32)
        for i in range(REL_BUCKETS):
            out = jnp.where(bucket == i, table[:, i], out)
        return out

    far = lookup(jnp.full((1,), -2 * blk))
    j = jnp.arange(2 * blk)
    dist = jnp.where(j < blk, -j, 2 * blk - j)
    diag = (lookup(dist) - far) * LOG2E
    near = (lookup(dist - blk) - far) * LOG2E
    return jnp.stack([diag, near], axis=1)[:, :, None, :]


def _group_sum_matrix():
    i = jnp.arange(MXU_DIM)
    return ((i[:, None] // 64) == (i[None, :] // 64)).astype(BF16)


def _chunk_tri_matrix(n):
    i = jnp.arange(n)
    return (((i[:, None] // CHUNK) == (i[None, :] // CHUNK)) & (i[None, :] <= i[:, None])).astype(BF16)


def _pad_heads(w, heads, dim, pad, axis):
    shape = list(w.shape)
    shape[axis:axis + 1] = [heads, dim]
    w = w.reshape(shape)
    widths = [(0, 0)] * w.ndim
    widths[axis + 1] = (0, pad - dim)
    w = jnp.pad(w, widths)
    shape[axis:axis + 2] = [heads * pad]
    return w.reshape(shape)


def _tile_gain(g, reps, scale=1.0):
    return (jnp.tile(g.astype(F32), reps) * scale)[None, :]


def _mem_block_diag(kn, v, b):
    m = kn.shape[0] // b
    eye = jnp.eye(MEM_HEADS, dtype=BF16)
    knt = kn.reshape(b, m, MEM_WIDTH).transpose(0, 2, 1)
    kbd = (knt.reshape(b, MEM_HEADS, MEM_HEAD_DIM, 1, m) * eye.reshape(1, MEM_HEADS, 1, MEM_HEADS, 1))
    kbd = kbd.reshape(b, MEM_WIDTH, MEM_HEADS * m)
    vbd = (v.reshape(b, 1, m, MEM_HEADS, MEM_HEAD_DIM) * eye.reshape(1, MEM_HEADS, 1, MEM_HEADS, 1))
    vbd = vbd.reshape(b, MEM_HEADS * m, MEM_WIDTH)
    return kbd, vbd


def kernel(x, mem, rel_bias, attn_norm, ffn_norm, mem_norm, w_in_diff, diff_qk_norm, diff_lambda,
           diff_out_norm, w_in_gla, gla_gate_w, gla_gate_b, gla_out_norm, w_mem_kv, mem_qk_norm,
           w_out, w_up, conv_w, conv_b, w_down):
    b, s, d = x.shape
    t = b * s
    tw = TOKEN_WIDTH
    gsum = _group_sum_matrix()
    mem2 = mem.reshape(b * mem.shape[1], d)
    x = x.astype(F32)

    for i in range(DEPTH):
        j = i // 2
        x2 = x.reshape(t, d)
        mq_gain = _tile_gain(mem_qk_norm[i, 0], MEM_HEADS, MEM_HEAD_DIM ** -0.5)
        if i % 2 == 0:
            nh, hd = DIFF_HEADS, DIFF_HEAD_DIM
            qt, k, vt, mq = _pre_diff(
                x2, attn_norm[i][None, :], w_in_diff[j].astype(BF16), gsum,
                _tile_gain(diff_qk_norm[j, 0], 2 * nh, hd ** -0.5 * LOG2E),
                _tile_gain(diff_qk_norm[j, 1], 2 * nh), mq_gain, b)
            lv = diff_lambda[j].astype(F32)
            lam_init = 0.8 - 0.6 * math.exp(-0.3 * i)
            lam = jnp.exp(jnp.sum(lv[0] * lv[1])) - jnp.exp(jnp.sum(lv[2] * lv[3])) + lam_init
            mix = _diff_attn(lam.reshape(1, 1), qt, k.reshape(b, s, tw), vt, _bias_vectors(rel_bias),
                             diff_out_norm[j].astype(F32)[None, :], 1.0 - lam_init)
            w_mix = w_out[i, :tw]
        else:
            kw = GLA_HEADS * GLA_K_DIM
            w = w_in_gla[j]
            hp = functools.partial(_pad_heads, heads=GLA_HEADS, axis=1)
            w_p = jnp.concatenate([
                hp(w[:, :kw], dim=GLA_K_DIM, pad=GLA_KP),
                hp(w[:, kw:2 * kw], dim=GLA_K_DIM, pad=GLA_KP),
                hp(w[:, 2 * kw:2 * kw + tw], dim=GLA_V_DIM, pad=GLA_VP),
                hp(w[:, 2 * kw + tw:2 * kw + 2 * tw], dim=GLA_V_DIM, pad=GLA_VP),
                jnp.pad(w[:, 2 * kw + 2 * tw:2 * kw + 2 * tw + GLA_GATE_RANK],
                        ((0, 0), (0, LANES - GLA_GATE_RANK))),
                w[:, 2 * kw + 2 * tw + GLA_GATE_RANK:]], axis=1).astype(BF16)
            gw = jnp.pad(hp(gla_gate_w[j], dim=GLA_K_DIM, pad=GLA_KP),
                         ((0, LANES - GLA_GATE_RANK), (0, 0))).astype(BF16)
            gb = _pad_heads(gla_gate_b[j].astype(F32)[None, :], GLA_HEADS, GLA_K_DIM, GLA_KP, 1)
            q, k, v, r, gc, mq = _pre_gla(x2, attn_norm[i][None, :], w_p, gsum, mq_gain, gw, gb,
                                          _chunk_tri_matrix(MXU_DIM))
            gain = _pad_heads(jnp.tile(gla_out_norm[j].astype(F32), GLA_HEADS)[None, :],
                              GLA_HEADS, GLA_V_DIM, GLA_VP, 1)
            sh = lambda a: a.reshape(b, s, a.shape[1])
            mix = _gla(sh(q), sh(k), sh(v), sh(r), sh(gc), gain)
            w_mix = _pad_heads(w_out[i, :tw], GLA_HEADS, GLA_V_DIM, GLA_VP, 0)

        kn, vm = _mem_kv(mem2, mem_norm[i][None, :], w_mem_kv[i].astype(BF16), gsum,
                         _tile_gain(mem_qk_norm[i, 1], MEM_HEADS))
        kbd, vbd = _mem_block_diag(kn, vm, b)
        x = _mix_out(x, mix.reshape(b, s, -1), mq.reshape(b, s, MEM_WIDTH), kbd, vbd,
                     w_mix.astype(BF16), w_out[i, tw:].astype(BF16))
        x = _ffn(x, ffn_norm[i][None, :], w_up[i].astype(BF16), conv_w[i].astype(F32),
                 conv_b[i].astype(F32)[None, :], w_down[i].astype(BF16))
    return x
```

```python
import functools
import math

import jax
import jax.numpy as jnp
from jax import lax
from jax.experimental import pallas as pl
from jax.experimental.pallas import tpu as pltpu

F32 = jnp.float32
BF16 = jnp.bfloat16

D_MODEL = 1024
DEPTH = 2
CHUNK = 64
MEM_WIDTH = D_MODEL // 4
MEM_HEADS = 4
MEM_HEAD_DIM = MEM_WIDTH // MEM_HEADS
TOKEN_WIDTH = D_MODEL - MEM_WIDTH
DIFF_HEAD_DIM = 64
DIFF_HEADS = TOKEN_WIDTH // (2 * DIFF_HEAD_DIM)
GLA_HEADS = 4
GLA_V_DIM = TOKEN_WIDTH // GLA_HEADS
GLA_K_DIM = GLA_V_DIM // 2
GLA_GATE_RANK = 16
GLA_GATE_TAU = 16.0
REL_BUCKETS = 32
REL_MAX_DIST = 128
D_FF = ((8 * D_MODEL // 3 + 127) // 128) * 128
EPS = 1e-6
LOG2E = math.log2(math.e)

LANES = 128
MXU_DIM = 256
VMEM_LIMIT_BYTES = 56 * 1024 * 1024

ROW_TILE = 512
ATT_BLK = 512
ATT_STREAMS = 4
ATT_MAX_EXP2_SPAN = 100.0
ATT_ROUNDING_SLACK = 1.02
ATT_V_ROWS = 2 * DIFF_HEAD_DIM + 16
GLA_TILE = 256
FFN_TILE = 512
FFN_COLS = 256
GLA_KP = 128
GLA_VP = 256
CARRY_ROWS = 8


def _cparams(*sem):
    return pltpu.CompilerParams(dimension_semantics=sem, vmem_limit_bytes=VMEM_LIMIT_BYTES)


def _const_spec(shape):
    n = len(shape)
    return pl.BlockSpec(shape, lambda *_: (0,) * n)


def _rms_rows(x, g):
    ms = jnp.mean(x * x, axis=-1, keepdims=True)
    return x * lax.rsqrt(ms + EPS) * g


def _group_rms(t, gsum, gain):
    cols = []
    for c in range(t.shape[1] // MXU_DIM):
        blk = t[:, c * MXU_DIM:(c + 1) * MXU_DIM]
        ss = jnp.dot((blk * blk).astype(BF16), gsum, preferred_element_type=F32)
        cols.append(blk * lax.rsqrt(ss * (1.0 / 64) + EPS))
    out = cols[0] if len(cols) == 1 else jnp.concatenate(cols, axis=1)
    return out * gain


def _dot(a, b):
    return jnp.dot(a, b, preferred_element_type=F32)


def _dot_nt(a, b):
    return lax.dot_general(a, b, (((1,), (1,)), ((), ())), preferred_element_type=F32)


def _pre_diff_kernel(x_ref, g_ref, w_ref, gsum_ref, qg_ref, kg_ref, mg_ref,
                     qt_ref, k_ref, vt_ref, mq_ref):
    tw = TOKEN_WIDTH
    hw = 2 * DIFF_HEAD_DIM
    h = _rms_rows(x_ref[...], g_ref[...]).astype(BF16)
    gsum = gsum_ref[...]
    q = _group_rms(_dot(h, w_ref[:, 0:tw]), gsum, qg_ref[...])
    k_ref[...] = _group_rms(_dot(h, w_ref[:, tw:2 * tw]), gsum, kg_ref[...]).astype(BF16)
    v = _dot(h, w_ref[:, 2 * tw:3 * tw])
    mq_ref[...] = _group_rms(_dot(h, w_ref[:, 3 * tw:]), gsum, mg_ref[...]).astype(BF16)
    ones = jnp.ones((ATT_V_ROWS - hw, ROW_TILE), BF16)
    for n in range(DIFF_HEADS):
        qt_ref[0, n, 0] = q[:, n * hw:(n + 1) * hw].T.astype(BF16)
        vt_ref[0, n, 0, 0:hw, :] = v[:, n * hw:(n + 1) * hw].T.astype(BF16)
        vt_ref[0, n, 0, hw:, :] = ones


def _pre_diff(x2, g, w, gsum, qg, kg, mg, b):
    t = x2.shape[0]
    tw = TOKEN_WIDTH
    hw = 2 * DIFF_HEAD_DIM
    assert ROW_TILE == ATT_BLK
    nq = t // b // ATT_BLK
    row = lambda n: pl.BlockSpec((ROW_TILE, n), lambda i: (i, 0))
    per_head = lambda r: pl.BlockSpec((1, DIFF_HEADS, 1, r, ATT_BLK), lambda i: (i // nq, 0, i % nq, 0, 0))
    return pl.pallas_call(
        _pre_diff_kernel,
        grid=(t // ROW_TILE,),
        in_specs=[row(D_MODEL), _const_spec((1, D_MODEL)), _const_spec(w.shape),
                  _const_spec(gsum.shape), _const_spec((1, tw)), _const_spec((1, tw)),
                  _const_spec((1, MEM_WIDTH))],
        out_specs=[per_head(hw), row(tw), per_head(ATT_V_ROWS), row(MEM_WIDTH)],
        out_shape=[jax.ShapeDtypeStruct((b, DIFF_HEADS, nq, hw, ATT_BLK), BF16),
                   jax.ShapeDtypeStruct((t, tw), BF16),
                   jax.ShapeDtypeStruct((b, DIFF_HEADS, nq, ATT_V_ROWS, ATT_BLK), BF16),
                   jax.ShapeDtypeStruct((t, MEM_WIDTH), BF16)],
        compiler_params=_cparams("parallel"),
        name="pre_diff",
    )(x2, g, w, gsum, qg, kg, mg)


def _diff_attn_kernel(lam_ref, qt_ref, k_ref, vt_ref, bvec_ref, g_ref, o_ref,
                      bias_scr, *scratch, out_scale):
    blk = ATT_BLK
    hd = DIFF_HEAD_DIM
    ns = ATT_STREAMS
    qp_scr, s_scr, cm_scr, p_scr, a_scr, m_scr, acc_scr = (scratch[i * ns:(i + 1) * ns] for i in range(7))
    w = 2 * blk // ns
    qi = pl.program_id(2)
    qt = qt_ref[0, 0, 0]
    row = lax.broadcasted_iota(jnp.int32, qt.shape, 0)
    zero = jnp.zeros_like(qt)
    q_maps = (jnp.where(row < hd, qt, zero), jnp.where(row >= hd, qt, zero))
    cols = [slice((x % (ns // 2)) * w, (x % (ns // 2) + 1) * w) for x in range(ns)]
    for x in range(ns):
        qp_scr[x][...] = q_maps[x // (ns // 2)][:, cols[x]]
        m_scr[x][...] = jnp.full(m_scr[x].shape, -jnp.inf, F32)
        acc_scr[x][...] = jnp.zeros(acc_scr[x].shape, F32)

    @pl.when(qi == 0)
    def _():
        kk = lax.broadcasted_iota(jnp.int32, (blk, blk), 0)
        qq = lax.broadcasted_iota(jnp.int32, (blk, blk), 1)
        visible = (kk // CHUNK) <= (qq // CHUNK)
        for i in range(2):
            rows = jnp.broadcast_to(bvec_ref[0, i], (blk, 2 * blk))
            tile = pltpu.roll(rows, 0, 1, stride=1, stride_axis=0)[:, :blk]
            bias_scr[i] = jnp.where(visible, tile, -jnp.inf) if i == 0 else tile

    def key_block(t):
        return jnp.maximum(qi - t, 0)

    def logits(x, t, bias=None):
        start = pl.multiple_of(key_block(t) * blk, blk)
        s = _dot(k_ref[0, pl.ds(start, blk), :], qp_scr[x][...])
        if bias is not None:
            s = s + bias[:, cols[x]]
        s_scr[x][...] = s
        part = s[0:8]
        for r in range(8, blk, 8):
            part = jnp.maximum(part, s[r:r + 8])
        cm_scr[x][...] = part

    def softmax(x):
        m_old = m_scr[x][...]
        m_new = jnp.maximum(m_old, jnp.max(cm_scr[x][...], axis=0, keepdims=True))
        a_scr[x][...] = jnp.exp2(m_old - m_new)
        m_scr[x][...] = m_new
        for r in range(0, blk, 16):
            p_scr[x][r:r + 16, :] = jnp.exp2((s_scr[x][r:r + 16, :] - m_new).astype(BF16))

    def values(x, t):
        acc_scr[x][...] = a_scr[x][...] * acc_scr[x][...] + _dot(vt_ref[0, 0, key_block(t)], p_scr[x][...])

    def step(t, bias=None):
        for x in range(ns):
            logits(x, t + 1, bias)
            values(x, t)
            softmax((x + 1) % ns)

    for x in range(ns):
        logits(x, 0, bias_scr[0])
    softmax(0)
    step(0, bias_scr[1] + jnp.where(qi == 0, -jnp.inf, 0.0).astype(F32))

    t_last = jnp.maximum(qi, 1)

    def far_pair(u, c):
        step(1 + 2 * u)
        step(2 + 2 * u)
        return c

    lax.fori_loop(0, (t_last - 1) // 2, far_pair, 0)

    @pl.when((t_last - 1) % 2 == 1)
    def _():
        step(t_last - 1)

    for x in range(ns):
        values(x, t_last)
        if x + 1 < ns:
            softmax(x + 1)

    lam = lam_ref[0, 0]
    half = ns // 2
    outs = []
    for x in range(half):
        o0 = acc_scr[x][0:2 * hd, :] * (1.0 / acc_scr[x][2 * hd:2 * hd + 1, :])
        o1 = acc_scr[half + x][0:2 * hd, :] * (1.0 / acc_scr[half + x][2 * hd:2 * hd + 1, :])
        outs.append(o0 - lam * o1)
    ot = outs[0] if half == 1 else jnp.concatenate(outs, axis=1)
    ms = jnp.mean(ot * ot, axis=0, keepdims=True)
    y = (ot * lax.rsqrt(ms + EPS)).T * (g_ref[...] * out_scale)
    o_ref[0] = y.astype(BF16)


def _diff_attn_bounded_kernel(lam_ref, qt_ref, k_ref, vt_ref, bvec_ref, g_ref, o_ref,
                              bias_scr, *scratch, out_scale):
    blk = ATT_BLK
    hd = DIFF_HEAD_DIM
    ns = ATT_STREAMS
    qp_scr, p0_scr, p1_scr, acc_scr = (scratch[i * ns:(i + 1) * ns] for i in range(4))
    p_scr = (p0_scr, p1_scr)
    w = 2 * blk // ns
    qi = pl.program_id(2)
    m_ref = lam_ref[0, 1]
    qt = qt_ref[0, 0, 0]
    row = lax.broadcasted_iota(jnp.int32, qt.shape, 0)
    zero = jnp.zeros_like(qt)
    q_maps = (jnp.where(row < hd, qt, zero), jnp.where(row >= hd, qt, zero))
    cols = [slice((x % (ns // 2)) * w, (x % (ns // 2) + 1) * w) for x in range(ns)]
    for x in range(ns):
        qp_scr[x][...] = q_maps[x // (ns // 2)][:, cols[x]]
        acc_scr[x][...] = jnp.zeros(acc_scr[x].shape, F32)

    @pl.when(qi == 0)
    def _():
        kk = lax.broadcasted_iota(jnp.int32, (blk, blk), 0)
        qq = lax.broadcasted_iota(jnp.int32, (blk, blk), 1)
        visible = (kk // CHUNK) <= (qq // CHUNK)
        for i in range(2):
            rows = jnp.broadcast_to(bvec_ref[0, i], (blk, 2 * blk))
            tile = pltpu.roll(rows, 0, 1, stride=1, stride_axis=0)[:, :blk]
            bias_scr[i] = jnp.where(visible, tile, -jnp.inf) if i == 0 else tile

    def key_block(t):
        return jnp.maximum(qi - t, 0)

    def probs(x, t, bias=None):
        start = pl.multiple_of(key_block(t) * blk, blk)
        s = _dot(k_ref[0, pl.ds(start, blk), :], qp_scr[x][...])
        if bias is not None:
            s = s + bias[:, cols[x]]
        return jnp.exp2(s - m_ref).astype(BF16)

    def values(x, t, slot):
        acc_scr[x][...] += _dot(vt_ref[0, 0, key_block(t)], p_scr[slot][x][...])

    def step(t, slot, bias=None):
        for x in range(ns):
            p_scr[slot][x][...] = probs(x, t + 1, bias)
            values(x, t, 1 - slot)

    for x in range(ns):
        p0_scr[x][...] = probs(x, 0, bias_scr[0])
    step(0, 1, bias_scr[1] + jnp.where(qi == 0, -jnp.inf, 0.0).astype(F32))

    t_last = jnp.maximum(qi, 1)
    n_far = t_last - 1

    def far_pair(u, c):
        step(1 + 2 * u, 0)
        step(2 + 2 * u, 1)
        return c

    lax.fori_loop(0, n_far // 2, far_pair, 0)

    @pl.when(n_far % 2 == 1)
    def _():
        step(t_last - 1, 0)
        for x in range(ns):
            values(x, t_last, 0)

    @pl.when(n_far % 2 == 0)
    def _():
        for x in range(ns):
            values(x, t_last, 1)

    lam = lam_ref[0, 0]
    half = ns // 2
    outs = []
    for x in range(half):
        o0 = acc_scr[x][0:2 * hd, :] * (1.0 / acc_scr[x][2 * hd:2 * hd + 1, :])
        o1 = acc_scr[half + x][0:2 * hd, :] * (1.0 / acc_scr[half + x][2 * hd:2 * hd + 1, :])
        outs.append(o0 - lam * o1)
    ot = outs[0] if half == 1 else jnp.concatenate(outs, axis=1)
    ms = jnp.mean(ot * ot, axis=0, keepdims=True)
    y = (ot * lax.rsqrt(ms + EPS)).T * (g_ref[...] * out_scale)
    o_ref[0] = y.astype(BF16)


def _diff_attn(lam, qt, k, vt, bias, g, out_scale, bounded):
    b, nh, nq = qt.shape[0], qt.shape[1], qt.shape[2]
    s = k.shape[1]
    blk = ATT_BLK
    ns = ATT_STREAMS
    w = 2 * blk // ns
    if bounded:
        body = _diff_attn_bounded_kernel
        per_stream = (((2 * DIFF_HEAD_DIM, w), BF16), ((blk, w), BF16), ((blk, w), BF16), ((ATT_V_ROWS, w), F32))
    else:
        body = _diff_attn_kernel
        per_stream = (((2 * DIFF_HEAD_DIM, w), BF16), ((blk, w), F32), ((8, w), F32), ((blk, w), BF16),
                      ((1, w), F32), ((1, w), F32), ((ATT_V_ROWS, w), F32))
    return pl.pallas_call(
        functools.partial(body, out_scale=out_scale),
        grid=(b, nh, nq),
        in_specs=[
            pl.BlockSpec(memory_space=pltpu.SMEM),
            pl.BlockSpec((1, 1, 1, 2 * DIFF_HEAD_DIM, blk), lambda bi, hi, qi: (bi, hi, qi, 0, 0)),
            pl.BlockSpec((1, s, 2 * DIFF_HEAD_DIM), lambda bi, hi, qi: (bi, 0, hi)),
            pl.BlockSpec((1, 1, nq, ATT_V_ROWS, blk), lambda bi, hi, qi: (bi, hi, 0, 0, 0)),
            pl.BlockSpec((1, 2, 1, 2 * blk), lambda bi, hi, qi: (hi, 0, 0, 0)),
            _const_spec((1, 2 * DIFF_HEAD_DIM)),
        ],
        out_specs=pl.BlockSpec((1, blk, 2 * DIFF_HEAD_DIM), lambda bi, hi, qi: (bi, qi, hi)),
        out_shape=jax.ShapeDtypeStruct((b, s, nh * 2 * DIFF_HEAD_DIM), BF16),
        scratch_shapes=[pltpu.VMEM((2, blk, blk), F32)] + [pltpu.VMEM(shape, dtype)
                                                        for shape, dtype in per_stream for _ in range(ns)],
        compiler_params=_cparams("parallel", "parallel", "arbitrary"),
        name="diff_attn_bounded" if bounded else "diff_attn",
    )(lam, qt, k, vt, bias, g)


def _pre_gla_kernel(x_ref, g_ref, w_ref, gsum_ref, mg_ref, gw_ref, gb_ref, tri_ref,
                    q_ref, k_ref, v_ref, r_ref, gc_ref, mq_ref):
    kw = GLA_HEADS * GLA_KP
    vw = GLA_HEADS * GLA_VP
    h = _rms_rows(x_ref[...], g_ref[...]).astype(BF16)
    q_ref[...] = _dot(h, w_ref[:, 0:kw]).astype(BF16)
    k_ref[...] = _dot(h, w_ref[:, kw:2 * kw]).astype(BF16)
    o = 2 * kw
    v_ref[...] = _dot(h, w_ref[:, o:o + vw]).astype(BF16)
    r_ref[...] = _dot(h, w_ref[:, o + vw:o + 2 * vw]).astype(BF16)
    o = o + 2 * vw
    gate_low = _dot(h, w_ref[:, o:o + LANES]).astype(BF16)
    mq_ref[...] = _group_rms(_dot(h, w_ref[:, o + LANES:]), gsum_ref[...], mg_ref[...]).astype(BF16)
    z = _dot(gate_low, gw_ref[...]) + gb_ref[...]
    log_a = (jnp.minimum(z, 0.0) - jnp.log1p(jnp.exp(-jnp.abs(z)))) * (1.0 / GLA_GATE_TAU)
    hi = log_a.astype(BF16)
    rem = log_a - hi.astype(F32)
    mid = rem.astype(BF16)
    lo = (rem - mid.astype(F32)).astype(BF16)
    tri = tri_ref[...]
    n = tri.shape[0]
    for c in range(log_a.shape[0] // n):
        rows = slice(c * n, (c + 1) * n)
        gc_ref[rows, :] = _dot(tri, hi[rows]) + _dot(tri, mid[rows]) + _dot(tri, lo[rows])


def _pre_gla(x2, g, w, gsum, mg, gw, gb, tri):
    t = x2.shape[0]
    kw = GLA_HEADS * GLA_KP
    vw = GLA_HEADS * GLA_VP
    row = lambda n: pl.BlockSpec((ROW_TILE, n), lambda i: (i, 0))
    return pl.pallas_call(
        _pre_gla_kernel,
        grid=(t // ROW_TILE,),
        in_specs=[row(D_MODEL), _const_spec((1, D_MODEL)), _const_spec(w.shape),
                  _const_spec(gsum.shape), _const_spec((1, MEM_WIDTH)), _const_spec(gw.shape),
                  _const_spec(gb.shape), _const_spec(tri.shape)],
        out_specs=[row(kw), row(kw), row(vw), row(vw), row(kw), row(MEM_WIDTH)],
        out_shape=[jax.ShapeDtypeStruct((t, kw), BF16), jax.ShapeDtypeStruct((t, kw), BF16),
                   jax.ShapeDtypeStruct((t, vw), BF16), jax.ShapeDtypeStruct((t, vw), BF16),
                   jax.ShapeDtypeStruct((t, kw), F32), jax.ShapeDtypeStruct((t, MEM_WIDTH), BF16)],
        compiler_params=_cparams("parallel"),
        name="pre_gla",
    )(x2, g, w, gsum, mg, gw, gb, tri)


def _gla_kernel(q_ref, k_ref, v_ref, r_ref, gc_ref, gain_ref, o_ref, s_scr):
    tg = GLA_TILE
    nchunk = tg // CHUNK
    heads = range(GLA_HEADS)
    ks = [slice(h * GLA_KP, (h + 1) * GLA_KP) for h in heads]
    vs = [slice(h * GLA_VP, (h + 1) * GLA_VP) for h in heads]

    @pl.when(pl.program_id(1) == 0)
    def _():
        s_scr[...] = jnp.zeros(s_scr.shape, F32)

    ri = lax.broadcasted_iota(jnp.int32, (tg, tg), 0)
    ci = lax.broadcasted_iota(jnp.int32, (tg, tg), 1)
    same_chunk = (ri // CHUNK) == (ci // CHUNK)
    past = ci <= ri
    row_chunk = lax.broadcasted_iota(jnp.int32, (tg, GLA_KP), 0) // CHUNK

    qe, scores, kv, decay = [], [], [], []
    for h in heads:
        qh = q_ref[0, :, ks[h]].astype(F32) * (GLA_K_DIM ** -0.5)
        kh = k_ref[0, :, ks[h]].astype(F32)
        g = gc_ref[0, :, ks[h]]
        eg = jnp.exp(g)
        ieg = jnp.exp(-g)
        qe.append((qh * eg).astype(BF16))
        a_past = _dot_nt(qe[h], (kh * ieg).astype(BF16))
        a_fut = _dot_nt((qh * ieg).astype(BF16), (kh * eg).astype(BF16))
        scores.append(jnp.where(same_chunk, jnp.where(past, a_past, a_fut), 0.0).astype(BF16))
        vt = v_ref[0, :, vs[h]].astype(F32).T.astype(BF16)
        g_last = [g[c * CHUNK + CHUNK - 1:c * CHUNK + CHUNK, :] for c in range(nchunk)]
        g_end = jnp.concatenate([jnp.broadcast_to(gl, (CHUNK, GLA_KP)) for gl in g_last], axis=0)
        kdec = kh * jnp.exp(g_end - g)
        kv.append([_dot(vt, jnp.where(row_chunk == c, kdec, 0.0).astype(BF16)) for c in range(nchunk)])
        decay.append([jnp.exp(gl) for gl in g_last])

    starts = []
    for h in heads:
        st = s_scr[h]
        per_chunk = []
        for c in range(nchunk):
            per_chunk.append(st.astype(BF16))
            st = st * decay[h][c] + kv[h][c]
        s_scr[h] = st
        starts.append(per_chunk)

    for h in heads:
        inter = [_dot_nt(qe[h][c * CHUNK:(c + 1) * CHUNK], starts[h][c]) for c in range(nchunk)]
        o = _dot(scores[h], v_ref[0, :, vs[h]]) + jnp.concatenate(inter, axis=0)
        ms = jnp.sum(o * o, axis=-1, keepdims=True) * (1.0 / GLA_V_DIM)
        y = o * lax.rsqrt(ms + EPS) * gain_ref[:, vs[h]]
        rh = r_ref[0, :, vs[h]].astype(F32)
        o_ref[0, :, vs[h]] = (y * (rh / (1.0 + jnp.exp(-rh)))).astype(BF16)


def _gla(q, k, v, r, gc, gain):
    b, s = q.shape[0], q.shape[1]
    kw = GLA_HEADS * GLA_KP
    vw = GLA_HEADS * GLA_VP
    spec = lambda n: pl.BlockSpec((1, GLA_TILE, n), lambda bi, i: (bi, i, 0))
    return pl.pallas_call(
        _gla_kernel,
        grid=(b, s // GLA_TILE),
        in_specs=[spec(kw), spec(kw), spec(vw), spec(vw), spec(kw), _const_spec((1, vw))],
        out_specs=spec(vw),
        out_shape=jax.ShapeDtypeStruct((b, s, vw), BF16),
        scratch_shapes=[pltpu.VMEM((GLA_HEADS, GLA_VP, GLA_KP), F32)],
        compiler_params=_cparams("parallel", "arbitrary"),
        name="gla",
    )(q, k, v, r, gc, gain)


def _mem_kv_kernel(mem_ref, g_ref, w_ref, gsum_ref, kg_ref, k_ref, v_ref):
    h = _rms_rows(mem_ref[...], g_ref[...]).astype(BF16)
    k_ref[...] = _group_rms(_dot(h, w_ref[:, :MEM_WIDTH]), gsum_ref[...], kg_ref[...]).astype(BF16)
    v_ref[...] = _dot(h, w_ref[:, MEM_WIDTH:]).astype(BF16)


def _mem_kv(mem2, g, w, gsum, kg):
    n = mem2.shape[0]
    return pl.pallas_call(
        _mem_kv_kernel,
        grid=(1,),
        in_specs=[_const_spec(mem2.shape), _const_spec((1, D_MODEL)), _const_spec(w.shape),
                  _const_spec(gsum.shape), _const_spec((1, MEM_WIDTH))],
        out_specs=[_const_spec((n, MEM_WIDTH)), _const_spec((n, MEM_WIDTH))],
        out_shape=[jax.ShapeDtypeStruct((n, MEM_WIDTH), BF16)] * 2,
        compiler_params=_cparams("arbitrary"),
        name="mem_kv",
    )(mem2, g, w, gsum, kg)


def _mix_out_kernel(x_ref, mix_ref, mq_ref, kbd_ref, vbd_ref, wa_ref, wb_ref, o_ref):
    m = kbd_ref.shape[2] // MEM_HEADS
    logits = _dot(mq_ref[0], kbd_ref[0])
    ps = []
    for h in range(MEM_HEADS):
        s = logits[:, h * m:(h + 1) * m]
        e = jnp.exp(s - jnp.max(s, axis=-1, keepdims=True))
        ps.append((e * (1.0 / jnp.sum(e, axis=-1, keepdims=True))).astype(BF16))
    cross = _dot(jnp.concatenate(ps, axis=1), vbd_ref[0])
    o_ref[0] = x_ref[0] + _dot(mix_ref[0], wa_ref[...]) + _dot(cross.astype(BF16), wb_ref[...])


def _mix_out(x, mix, mq, kbd, vbd, wa, wb):
    b, s = x.shape[0], x.shape[1]
    spec = lambda n: pl.BlockSpec((1, ROW_TILE, n), lambda bi, i: (bi, i, 0))
    per_b = lambda a: pl.BlockSpec((1,) + a.shape[1:], lambda bi, i: (bi, 0, 0))
    return pl.pallas_call(
        _mix_out_kernel,
        grid=(b, s // ROW_TILE),
        in_specs=[spec(D_MODEL), spec(mix.shape[2]), spec(MEM_WIDTH), per_b(kbd), per_b(vbd),
                  _const_spec(wa.shape), _const_spec(wb.shape)],
        out_specs=spec(D_MODEL),
        out_shape=jax.ShapeDtypeStruct(x.shape, F32),
        compiler_params=_cparams("parallel", "parallel"),
        name="mix_out",
    )(x, mix, mq, kbd, vbd, wa, wb)


def _ffn_kernel(x_ref, g_ref, wu_ref, cw_ref, cb_ref, wd_ref, o_ref, u_scr, act_scr):
    tm = FFN_TILE
    cr = CARRY_ROWS

    @pl.when(pl.program_id(1) == 0)
    def _():
        u_scr[0:cr, :] = jnp.zeros((cr, u_scr.shape[1]), F32)

    x = x_ref[0]
    h = _rms_rows(x, g_ref[...]).astype(BF16)

    def up(j):
        for cols in (slice(j * FFN_COLS, (j + 1) * FFN_COLS),
                     slice(D_FF + j * FFN_COLS, D_FF + (j + 1) * FFN_COLS)):
            u_scr[cr:cr + tm, cols] = _dot(h, wu_ref[:, cols])

    def conv(cols):
        c = (cw_ref[0:1, cols] * u_scr[cr - 2:cr - 2 + tm, cols]
             + cw_ref[1:2, cols] * u_scr[cr - 1:cr - 1 + tm, cols]
             + cw_ref[2:3, cols] * u_scr[cr:cr + tm, cols]
             + cb_ref[:, cols])
        u_scr[0:cr, cols] = u_scr[tm:tm + cr, cols]
        return c

    n_cols = D_FF // FFN_COLS
    up(0)
    for j in range(n_cols):
        if j + 1 < n_cols:
            up(j + 1)
        a = conv(slice(j * FFN_COLS, (j + 1) * FFN_COLS))
        gte = conv(slice(D_FF + j * FFN_COLS, D_FF + (j + 1) * FFN_COLS))
        act_scr[:, j * FFN_COLS:(j + 1) * FFN_COLS] = (a * (gte / (1.0 + jnp.exp(-gte)))).astype(BF16)

    o_ref[0] = x + _dot(act_scr[...], wd_ref[...])


def _ffn(x, g, wu, cw, cb, wd):
    b, s = x.shape[0], x.shape[1]
    spec = pl.BlockSpec((1, FFN_TILE, D_MODEL), lambda bi, i: (bi, i, 0))
    single = lambda a: pl.BlockSpec(a.shape, lambda bi, i: (0,) * a.ndim, pipeline_mode=pl.Buffered(1))
    return pl.pallas_call(
        _ffn_kernel,
        grid=(b, s // FFN_TILE),
        in_specs=[spec, _const_spec((1, D_MODEL)), single(wu), _const_spec(cw.shape),
                  _const_spec(cb.shape), single(wd)],
        out_specs=spec,
        out_shape=jax.ShapeDtypeStruct(x.shape, F32),
        scratch_shapes=[pltpu.VMEM((FFN_TILE + CARRY_ROWS, 2 * D_FF), F32),
                        pltpu.VMEM((FFN_TILE, D_FF), BF16)],
        compiler_params=_cparams("parallel", "arbitrary"),
        name="ffn",
    )(x, g, wu, cw, cb, wd)


def _t5_bucket(rel):
    half = REL_BUCKETS // 2
    max_exact = half // 2
    ret = jnp.where(rel > 0, half, 0)
    n = jnp.abs(rel)
    nf = jnp.maximum(n, 1).astype(jnp.float32)
    large = max_exact + (jnp.log(nf / max_exact) / math.log(REL_MAX_DIST / max_exact)
                         * (half - max_exact)).astype(jnp.int32)
    large = jnp.minimum(large, half - 1)
    return ret + jnp.where(n < max_exact, n, large)


def _bias_vectors(rel_bias):
    blk = ATT_BLK
    assert blk >= REL_MAX_DIST
    table = rel_bias.astype(F32).T[:, :, None]

    def lookup(rel):
        bucket = _t5_bucket(rel)
        out = jnp.zeros((table.shape[0],) + rel.shape, F32)
        for i in range(REL_BUCKETS):
            out = jnp.where(bucket == i, table[:, i], out)
        return out

    far = lookup(jnp.full((1,), -2 * blk))
    j = jnp.arange(2 * blk)
    dist = jnp.where(j < blk, -j, 2 * blk - j)
    diag = (lookup(dist) - far) * LOG2E
    near = (lookup(dist - blk) - far) * LOG2E
    return jnp.stack([diag, near], axis=1)[:, :, None, :]


def _group_sum_matrix():
    i = jnp.arange(MXU_DIM)
    return ((i[:, None] // 64) == (i[None, :] // 64)).astype(BF16)


def _chunk_tri_matrix(n):
    i = jnp.arange(n)
    return (((i[:, None] // CHUNK) == (i[None, :] // CHUNK)) & (i[None, :] <= i[:, None])).astype(BF16)


def _pad_heads(w, heads, dim, pad, axis):
    shape = list(w.shape)
    shape[axis:axis + 1] = [heads, dim]
    w = w.reshape(shape)
    widths = [(0, 0)] * w.ndim
    widths[axis + 1] = (0, pad - dim)
    w = jnp.pad(w, widths)
    shape[axis:axis + 2] = [heads * pad]
    return w.reshape(shape)


def _tile_gain(g, reps, scale=1.0):
    return (jnp.tile(g.astype(F32), reps) * scale)[None, :]


def _mem_block_diag(kn, v, b):
    m = kn.shape[0] // b
    eye = jnp.eye(MEM_HEADS, dtype=BF16)
    knt = kn.reshape(b, m, MEM_WIDTH).transpose(0, 2, 1)
    kbd = (knt.reshape(b, MEM_HEADS, MEM_HEAD_DIM, 1, m) * eye.reshape(1, MEM_HEADS, 1, MEM_HEADS, 1))
    kbd = kbd.reshape(b, MEM_WIDTH, MEM_HEADS * m)
    vbd = (v.reshape(b, 1, m, MEM_HEADS, MEM_HEAD_DIM) * eye.reshape(1, MEM_HEADS, 1, MEM_HEADS, 1))
    vbd = vbd.reshape(b, MEM_HEADS * m, MEM_WIDTH)
    return kbd, vbd


def kernel(x, mem, rel_bias, attn_norm, ffn_norm, mem_norm, w_in_diff, diff_qk_norm, diff_lambda,
           diff_out_norm, w_in_gla, gla_gate_w, gla_gate_b, gla_out_norm, w_mem_kv, mem_qk_norm,
           w_out, w_up, conv_w, conv_b, w_down):
    b, s, d = x.shape
    t = b * s
    tw = TOKEN_WIDTH
    gsum = _group_sum_matrix()
    mem2 = mem.reshape(b * mem.shape[1], d)
    x = x.astype(F32)

    for i in range(DEPTH):
        j = i // 2
        x2 = x.reshape(t, d)
        mq_gain = _tile_gain(mem_qk_norm[i, 0], MEM_HEADS, MEM_HEAD_DIM ** -0.5)
        if i % 2 == 0:
            nh, hd = DIFF_HEADS, DIFF_HEAD_DIM
            qt, k, vt, mq = _pre_diff(
                x2, attn_norm[i][None, :], w_in_diff[j].astype(BF16), gsum,
                _tile_gain(diff_qk_norm[j, 0], 2 * nh, hd ** -0.5 * LOG2E),
                _tile_gain(diff_qk_norm[j, 1], 2 * nh), mq_gain, b)
            lv = diff_lambda[j].astype(F32)
            lam_init = 0.8 - 0.6 * math.exp(-0.3 * i)
            lam = jnp.exp(jnp.sum(lv[0] * lv[1])) - jnp.exp(jnp.sum(lv[2] * lv[3])) + lam_init
            bvec = _bias_vectors(rel_bias)
            qk_bound = (hd ** 0.5 * LOG2E * ATT_ROUNDING_SLACK
                        * jnp.max(jnp.abs(diff_qk_norm[j, 0])) * jnp.max(jnp.abs(diff_qk_norm[j, 1]))).astype(F32)
            hi = qk_bound + jnp.maximum(jnp.max(bvec), 0.0)
            lo = -qk_bound + jnp.minimum(jnp.min(bvec), 0.0)
            scalars = jnp.stack([lam.astype(F32), hi]).reshape(1, 2)
            attend = lambda bounded: functools.partial(
                _diff_attn, qt=qt, k=k.reshape(b, s, tw), vt=vt, bias=bvec,
                g=diff_out_norm[j].astype(F32)[None, :], out_scale=1.0 - lam_init, bounded=bounded)
            mix = lax.cond(hi - lo <= ATT_MAX_EXP2_SPAN, attend(True), attend(False), scalars)
            w_mix = w_out[i, :tw]
        else:
            kw = GLA_HEADS * GLA_K_DIM
            w = w_in_gla[j]
            hp = functools.partial(_pad_heads, heads=GLA_HEADS, axis=1)
            w_p = jnp.concatenate([
                hp(w[:, :kw], dim=GLA_K_DIM, pad=GLA_KP),
                hp(w[:, kw:2 * kw], dim=GLA_K_DIM, pad=GLA_KP),
                hp(w[:, 2 * kw:2 * kw + tw], dim=GLA_V_DIM, pad=GLA_VP),
                hp(w[:, 2 * kw + tw:2 * kw + 2 * tw], dim=GLA_V_DIM, pad=GLA_VP),
                jnp.pad(w[:, 2 * kw + 2 * tw:2 * kw + 2 * tw + GLA_GATE_RANK],
                        ((0, 0), (0, LANES - GLA_GATE_RANK))),
                w[:, 2 * kw + 2 * tw + GLA_GATE_RANK:]], axis=1).astype(BF16)
            gw = jnp.pad(hp(gla_gate_w[j], dim=GLA_K_DIM, pad=GLA_KP),
                         ((0, LANES - GLA_GATE_RANK), (0, 0))).astype(BF16)
            gb = _pad_heads(gla_gate_b[j].astype(F32)[None, :], GLA_HEADS, GLA_K_DIM, GLA_KP, 1)
            q, k, v, r, gc, mq = _pre_gla(x2, attn_norm[i][None, :], w_p, gsum, mq_gain, gw, gb,
                                          _chunk_tri_matrix(MXU_DIM))
            gain = _pad_heads(jnp.tile(gla_out_norm[j].astype(F32), GLA_HEADS)[None, :],
                              GLA_HEADS, GLA_V_DIM, GLA_VP, 1)
            sh = lambda a: a.reshape(b, s, a.shape[1])
            mix = _gla(sh(q), sh(k), sh(v), sh(r), sh(gc), gain)
            w_mix = _pad_heads(w_out[i, :tw], GLA_HEADS, GLA_V_DIM, GLA_VP, 0)

        kn, vm = _mem_kv(mem2, mem_norm[i][None, :], w_mem_kv[i].astype(BF16), gsum,
                         _tile_gain(mem_qk_norm[i, 1], MEM_HEADS))
        kbd, vbd = _mem_block_diag(kn, vm, b)
        x = _mix_out(x, mix.reshape(b, s, -1), mq.reshape(b, s, MEM_WIDTH), kbd, vbd,
                     w_mix.astype(BF16), w_out[i, tw:].astype(BF16))
        x = _ffn(x, ffn_norm[i][None, :], w_up[i].astype(BF16), conv_w[i].astype(F32),
                 conv_b[i].astype(F32)[None, :], w_down[i].astype(BF16))
    return x
```

```python
import functools
import math

import jax
import jax.numpy as jnp
from jax import lax
from jax.experimental import pallas as pl
from jax.experimental.pallas import tpu as pltpu

F32 = jnp.float32
BF16 = jnp.bfloat16

D_MODEL = 1024
DEPTH = 2
CHUNK = 64
MEM_WIDTH = D_MODEL // 4
MEM_HEADS = 4
MEM_HEAD_DIM = MEM_WIDTH // MEM_HEADS
TOKEN_WIDTH = D_MODEL - MEM_WIDTH
DIFF_HEAD_DIM = 64
DIFF_HEADS = TOKEN_WIDTH // (2 * DIFF_HEAD_DIM)
GLA_HEADS = 4
GLA_V_DIM = TOKEN_WIDTH // GLA_HEADS
GLA_K_DIM = GLA_V_DIM // 2
GLA_GATE_RANK = 16
GLA_GATE_TAU = 16.0
REL_BUCKETS = 32
REL_MAX_DIST = 128
D_FF = ((8 * D_MODEL // 3 + 127) // 128) * 128
EPS = 1e-6
LOG2E = math.log2(math.e)

LANES = 128
MXU_DIM = 256
VMEM_LIMIT_BYTES = 56 * 1024 * 1024

ROW_TILE = 512
ATT_BLK = 512
ATT_STREAMS = 4
ATT_MAX_EXP2_SPAN = 100.0
ATT_ROUNDING_SLACK = 1.02
ATT_V_ROWS = 2 * DIFF_HEAD_DIM + 16
GLA_TILE = 256
FFN_TILE = 512
FFN_COLS = 256
GLA_KP = 128
GLA_VP = 256
CARRY_ROWS = 8


def _cparams(*sem):
    return pltpu.CompilerParams(dimension_semantics=sem, vmem_limit_bytes=VMEM_LIMIT_BYTES)


def _const_spec(shape):
    n = len(shape)
    return pl.BlockSpec(shape, lambda *_: (0,) * n)


def _rms_rows(x, g):
    ms = jnp.mean(x * x, axis=-1, keepdims=True)
    return x * lax.rsqrt(ms + EPS) * g


def _group_rms(t, gsum, gain):
    cols = []
    for c in range(t.shape[1] // MXU_DIM):
        blk = t[:, c * MXU_DIM:(c + 1) * MXU_DIM]
        ss = jnp.dot((blk * blk).astype(BF16), gsum, preferred_element_type=F32)
        cols.append(blk * lax.rsqrt(ss * (1.0 / 64) + EPS))
    out = cols[0] if len(cols) == 1 else jnp.concatenate(cols, axis=1)
    return out * gain


def _dot(a, b):
    return jnp.dot(a, b, preferred_element_type=F32)


def _dot_nt(a, b):
    return lax.dot_general(a, b, (((1,), (1,)), ((), ())), preferred_element_type=F32)


def _pre_diff_kernel(x_ref, g_ref, w_ref, gsum_ref, qg_ref, kg_ref, mg_ref,
                     qt_ref, k_ref, vt_ref, mq_ref):
    tw = TOKEN_WIDTH
    hw = 2 * DIFF_HEAD_DIM
    h = _rms_rows(x_ref[...], g_ref[...]).astype(BF16)
    gsum = gsum_ref[...]
    q = _group_rms(_dot(h, w_ref[:, 0:tw]), gsum, qg_ref[...])
    k_ref[...] = _group_rms(_dot(h, w_ref[:, tw:2 * tw]), gsum, kg_ref[...]).astype(BF16)
    v = _dot(h, w_ref[:, 2 * tw:3 * tw])
    mq_ref[...] = _group_rms(_dot(h, w_ref[:, 3 * tw:]), gsum, mg_ref[...]).astype(BF16)
    ones = jnp.ones((ATT_V_ROWS - hw, ROW_TILE), BF16)
    for n in range(DIFF_HEADS):
        qt_ref[0, n, 0] = q[:, n * hw:(n + 1) * hw].T.astype(BF16)
        vt_ref[0, n, 0, 0:hw, :] = v[:, n * hw:(n + 1) * hw].T.astype(BF16)
        vt_ref[0, n, 0, hw:, :] = ones


def _pre_diff(x2, g, w, gsum, qg, kg, mg, b):
    t = x2.shape[0]
    tw = TOKEN_WIDTH
    hw = 2 * DIFF_HEAD_DIM
    assert ROW_TILE == ATT_BLK
    nq = t // b // ATT_BLK
    row = lambda n: pl.BlockSpec((ROW_TILE, n), lambda i: (i, 0))
    per_head = lambda r: pl.BlockSpec((1, DIFF_HEADS, 1, r, ATT_BLK), lambda i: (i // nq, 0, i % nq, 0, 0))
    return pl.pallas_call(
        _pre_diff_kernel,
        grid=(t // ROW_TILE,),
        in_specs=[row(D_MODEL), _const_spec((1, D_MODEL)), _const_spec(w.shape),
                  _const_spec(gsum.shape), _const_spec((1, tw)), _const_spec((1, tw)),
                  _const_spec((1, MEM_WIDTH))],
        out_specs=[per_head(hw), row(tw), per_head(ATT_V_ROWS), row(MEM_WIDTH)],
        out_shape=[jax.ShapeDtypeStruct((b, DIFF_HEADS, nq, hw, ATT_BLK), BF16),
                   jax.ShapeDtypeStruct((t, tw), BF16),
                   jax.ShapeDtypeStruct((b, DIFF_HEADS, nq, ATT_V_ROWS, ATT_BLK), BF16),
                   jax.ShapeDtypeStruct((t, MEM_WIDTH), BF16)],
        compiler_params=_cparams("parallel"),
        name="pre_diff",
    )(x2, g, w, gsum, qg, kg, mg)


def _diff_attn_kernel(lam_ref, qt_ref, k_ref, vt_ref, bvec_ref, g_ref, o_ref,
                      bias_scr, *scratch, out_scale):
    blk = ATT_BLK
    hd = DIFF_HEAD_DIM
    ns = ATT_STREAMS
    qp_scr, s_scr, cm_scr, p_scr, a_scr, m_scr, acc_scr = (scratch[i * ns:(i + 1) * ns] for i in range(7))
    w = 2 * blk // ns
    qi = pl.program_id(2)
    qt = qt_ref[0, 0, 0]
    row = lax.broadcasted_iota(jnp.int32, qt.shape, 0)
    zero = jnp.zeros_like(qt)
    q_maps = (jnp.where(row < hd, qt, zero), jnp.where(row >= hd, qt, zero))
    cols = [slice((x % (ns // 2)) * w, (x % (ns // 2) + 1) * w) for x in range(ns)]
    for x in range(ns):
        qp_scr[x][...] = q_maps[x // (ns // 2)][:, cols[x]]
        m_scr[x][...] = jnp.full(m_scr[x].shape, -jnp.inf, F32)
        acc_scr[x][...] = jnp.zeros(acc_scr[x].shape, F32)

    @pl.when(qi == 0)
    def _():
        kk = lax.broadcasted_iota(jnp.int32, (blk, blk), 0)
        qq = lax.broadcasted_iota(jnp.int32, (blk, blk), 1)
        visible = (kk // CHUNK) <= (qq // CHUNK)
        for i in range(2):
            rows = jnp.broadcast_to(bvec_ref[0, i], (blk, 2 * blk))
            tile = pltpu.roll(rows, 0, 1, stride=1, stride_axis=0)[:, :blk]
            bias_scr[i] = jnp.where(visible, tile, -jnp.inf) if i == 0 else tile

    def key_block(t):
        return jnp.maximum(qi - t, 0)

    def logits(x, t, bias=None):
        start = pl.multiple_of(key_block(t) * blk, blk)
        s = _dot(k_ref[0, pl.ds(start, blk), :], qp_scr[x][...])
        if bias is not None:
            s = s + bias[:, cols[x]]
        s_scr[x][...] = s
        part = s[0:8]
        for r in range(8, blk, 8):
            part = jnp.maximum(part, s[r:r + 8])
        cm_scr[x][...] = part

    def softmax(x):
        m_old = m_scr[x][...]
        m_new = jnp.maximum(m_old, jnp.max(cm_scr[x][...], axis=0, keepdims=True))
        a_scr[x][...] = jnp.exp2(m_old - m_new)
        m_scr[x][...] = m_new
        for r in range(0, blk, 16):
            p_scr[x][r:r + 16, :] = jnp.exp2((s_scr[x][r:r + 16, :] - m_new).astype(BF16))

    def values(x, t):
        acc_scr[x][...] = a_scr[x][...] * acc_scr[x][...] + _dot(vt_ref[0, 0, key_block(t)], p_scr[x][...])

    def step(t, bias=None):
        for x in range(ns):
            logits(x, t + 1, bias)
            values(x, t)
            softmax((x + 1) % ns)

    for x in range(ns):
        logits(x, 0, bias_scr[0])
    softmax(0)
    step(0, bias_scr[1] + jnp.where(qi == 0, -jnp.inf, 0.0).astype(F32))

    t_last = jnp.maximum(qi, 1)

    def far_pair(u, c):
        step(1 + 2 * u)
        step(2 + 2 * u)
        return c

    lax.fori_loop(0, (t_last - 1) // 2, far_pair, 0)

    @pl.when((t_last - 1) % 2 == 1)
    def _():
        step(t_last - 1)

    for x in range(ns):
        values(x, t_last)
        if x + 1 < ns:
            softmax(x + 1)

    lam = lam_ref[0, 0]
    half = ns // 2
    outs = []
    for x in range(half):
        o0 = acc_scr[x][0:2 * hd, :] * (1.0 / acc_scr[x][2 * hd:2 * hd + 1, :])
        o1 = acc_scr[half + x][0:2 * hd, :] * (1.0 / acc_scr[half + x][2 * hd:2 * hd + 1, :])
        outs.append(o0 - lam * o1)
    ot = outs[0] if half == 1 else jnp.concatenate(outs, axis=1)
    ms = jnp.mean(ot * ot, axis=0, keepdims=True)
    y = (ot * lax.rsqrt(ms + EPS)).T * (g_ref[...] * out_scale)
    o_ref[0] = y.astype(BF16)


def _diff_attn_bounded_kernel(lam_ref, qt_ref, k_ref, vt_ref, bvec_ref, g_ref, o_ref,
                              bias_scr, *scratch, out_scale):
    blk = ATT_BLK
    hd = DIFF_HEAD_DIM
    ns = ATT_STREAMS
    qp_scr, p0_scr, p1_scr, acc_scr = (scratch[i * ns:(i + 1) * ns] for i in range(4))
    p_scr = (p0_scr, p1_scr)
    w = 2 * blk // ns
    qi = pl.program_id(2)
    m_ref = lam_ref[0, 1]
    qt = qt_ref[0, 0, 0]
    row = lax.broadcasted_iota(jnp.int32, qt.shape, 0)
    zero = jnp.zeros_like(qt)
    q_maps = (jnp.where(row < hd, qt, zero), jnp.where(row >= hd, qt, zero))
    cols = [slice((x % (ns // 2)) * w, (x % (ns // 2) + 1) * w) for x in range(ns)]
    for x in range(ns):
        qp_scr[x][...] = q_maps[x // (ns // 2)][:, cols[x]]
        acc_scr[x][...] = jnp.zeros(acc_scr[x].shape, F32)

    @pl.when(qi == 0)
    def _():
        kk = lax.broadcasted_iota(jnp.int32, (blk, blk), 0)
        qq = lax.broadcasted_iota(jnp.int32, (blk, blk), 1)
        visible = (kk // CHUNK) <= (qq // CHUNK)
        for i in range(2):
            rows = jnp.broadcast_to(bvec_ref[0, i], (blk, 2 * blk))
            tile = pltpu.roll(rows, 0, 1, stride=1, stride_axis=0)[:, :blk]
            bias_scr[i] = jnp.where(visible, tile, -jnp.inf) if i == 0 else tile

    def key_block(t):
        return jnp.maximum(qi - t, 0)

    def probs(x, t, bias=None):
        start = pl.multiple_of(key_block(t) * blk, blk)
        s = _dot(k_ref[0, pl.ds(start, blk), :], qp_scr[x][...])
        if bias is not None:
            s = s + bias[:, cols[x]]
        return jnp.exp2(s - m_ref).astype(BF16)

    def values(x, t, slot):
        acc_scr[x][...] += _dot(vt_ref[0, 0, key_block(t)], p_scr[slot][x][...])

    def step(t, slot, bias=None):
        for x in range(ns):
            p_scr[slot][x][...] = probs(x, t + 1, bias)
            values(x, t, 1 - slot)

    for x in range(ns):
        p0_scr[x][...] = probs(x, 0, bias_scr[0])
    step(0, 1, bias_scr[1] + jnp.where(qi == 0, -jnp.inf, 0.0).astype(F32))

    t_last = jnp.maximum(qi, 1)
    n_far = t_last - 1

    def far_steps(n):
        def body(u, c):
            for i in range(n):
                step(1 + n * u + i, i % 2)
            return c
        return body

    n_quads = n_far // 4
    lax.fori_loop(0, n_quads, far_steps(4), 0)
    t_done = 1 + 4 * n_quads

    @pl.when(n_far % 4 >= 2)
    def _():
        step(t_done, 0)
        step(t_done + 1, 1)

    @pl.when(n_far % 2 == 1)
    def _():
        step(t_last - 1, 0)

    last_in_p1 = t_last % 2 == 1
    for x in range(ns):
        p_last = jnp.where(last_in_p1, p1_scr[x][...], p0_scr[x][...])
        acc_scr[x][...] += _dot(vt_ref[0, 0, key_block(t_last)], p_last)

    lam = lam_ref[0, 0]
    half = ns // 2
    outs = []
    for x in range(half):
        o0 = acc_scr[x][0:2 * hd, :] * (1.0 / acc_scr[x][2 * hd:2 * hd + 1, :])
        o1 = acc_scr[half + x][0:2 * hd, :] * (1.0 / acc_scr[half + x][2 * hd:2 * hd + 1, :])
        outs.append(o0 - lam * o1)
    ot = outs[0] if half == 1 else jnp.concatenate(outs, axis=1)
    ms = jnp.mean(ot * ot, axis=0, keepdims=True)
    y = (ot * lax.rsqrt(ms + EPS)).T * (g_ref[...] * out_scale)
    o_ref[0] = y.astype(BF16)


def _diff_attn(lam, qt, k, vt, bias, g, out_scale, bounded):
    b, nh, nq = qt.shape[0], qt.shape[1], qt.shape[2]
    s = k.shape[1]
    blk = ATT_BLK
    ns = ATT_STREAMS
    w = 2 * blk // ns
    if bounded:
        body = _diff_attn_bounded_kernel
        per_stream = (((2 * DIFF_HEAD_DIM, w), BF16), ((blk, w), BF16), ((blk, w), BF16), ((ATT_V_ROWS, w), F32))
    else:
        body = _diff_attn_kernel
        per_stream = (((2 * DIFF_HEAD_DIM, w), BF16), ((blk, w), F32), ((8, w), F32), ((blk, w), BF16),
                      ((1, w), F32), ((1, w), F32), ((ATT_V_ROWS, w), F32))
    return pl.pallas_call(
        functools.partial(body, out_scale=out_scale),
        grid=(b, nh, nq),
        in_specs=[
            pl.BlockSpec(memory_space=pltpu.SMEM),
            pl.BlockSpec((1, 1, 1, 2 * DIFF_HEAD_DIM, blk), lambda bi, hi, qi: (bi, hi, qi, 0, 0)),
            pl.BlockSpec((1, s, 2 * DIFF_HEAD_DIM), lambda bi, hi, qi: (bi, 0, hi)),
            pl.BlockSpec((1, 1, nq, ATT_V_ROWS, blk), lambda bi, hi, qi: (bi, hi, 0, 0, 0)),
            pl.BlockSpec((1, 2, 1, 2 * blk), lambda bi, hi, qi: (hi, 0, 0, 0)),
            _const_spec((1, 2 * DIFF_HEAD_DIM)),
        ],
        out_specs=pl.BlockSpec((1, blk, 2 * DIFF_HEAD_DIM), lambda bi, hi, qi: (bi, qi, hi)),
        out_shape=jax.ShapeDtypeStruct((b, s, nh * 2 * DIFF_HEAD_DIM), BF16),
        scratch_shapes=[pltpu.VMEM((2, blk, blk), F32)] + [pltpu.VMEM(shape, dtype)
                                                        for shape, dtype in per_stream for _ in range(ns)],
        compiler_params=_cparams("parallel", "parallel", "arbitrary"),
        name="diff_attn_bounded" if bounded else "diff_attn",
    )(lam, qt, k, vt, bias, g)


def _pre_gla_kernel(x_ref, g_ref, w_ref, gsum_ref, mg_ref, gw_ref, gb_ref, tri_ref,
                    q_ref, k_ref, v_ref, r_ref, gc_ref, mq_ref):
    kw = GLA_HEADS * GLA_KP
    vw = GLA_HEADS * GLA_VP
    h = _rms_rows(x_ref[...], g_ref[...]).astype(BF16)
    q_ref[...] = _dot(h, w_ref[:, 0:kw]).astype(BF16)
    k_ref[...] = _dot(h, w_ref[:, kw:2 * kw]).astype(BF16)
    o = 2 * kw
    v_ref[...] = _dot(h, w_ref[:, o:o + vw]).astype(BF16)
    r_ref[...] = _dot(h, w_ref[:, o + vw:o + 2 * vw]).astype(BF16)
    o = o + 2 * vw
    gate_low = _dot(h, w_ref[:, o:o + LANES]).astype(BF16)
    mq_ref[...] = _group_rms(_dot(h, w_ref[:, o + LANES:]), gsum_ref[...], mg_ref[...]).astype(BF16)
    z = _dot(gate_low, gw_ref[...]) + gb_ref[...]
    log_a = (jnp.minimum(z, 0.0) - jnp.log1p(jnp.exp(-jnp.abs(z)))) * (1.0 / GLA_GATE_TAU)
    hi = log_a.astype(BF16)
    rem = log_a - hi.astype(F32)
    mid = rem.astype(BF16)
    lo = (rem - mid.astype(F32)).astype(BF16)
    tri = tri_ref[...]
    n = tri.shape[0]
    for c in range(log_a.shape[0] // n):
        rows = slice(c * n, (c + 1) * n)
        gc_ref[rows, :] = _dot(tri, hi[rows]) + _dot(tri, mid[rows]) + _dot(tri, lo[rows])


def _pre_gla(x2, g, w, gsum, mg, gw, gb, tri):
    t = x2.shape[0]
    kw = GLA_HEADS * GLA_KP
    vw = GLA_HEADS * GLA_VP
    row = lambda n: pl.BlockSpec((ROW_TILE, n), lambda i: (i, 0))
    return pl.pallas_call(
        _pre_gla_kernel,
        grid=(t // ROW_TILE,),
        in_specs=[row(D_MODEL), _const_spec((1, D_MODEL)), _const_spec(w.shape),
                  _const_spec(gsum.shape), _const_spec((1, MEM_WIDTH)), _const_spec(gw.shape),
                  _const_spec(gb.shape), _const_spec(tri.shape)],
        out_specs=[row(kw), row(kw), row(vw), row(vw), row(kw), row(MEM_WIDTH)],
        out_shape=[jax.ShapeDtypeStruct((t, kw), BF16), jax.ShapeDtypeStruct((t, kw), BF16),
                   jax.ShapeDtypeStruct((t, vw), BF16), jax.ShapeDtypeStruct((t, vw), BF16),
                   jax.ShapeDtypeStruct((t, kw), F32), jax.ShapeDtypeStruct((t, MEM_WIDTH), BF16)],
        compiler_params=_cparams("parallel"),
        name="pre_gla",
    )(x2, g, w, gsum, mg, gw, gb, tri)


def _gla_kernel(q_ref, k_ref, v_ref, r_ref, gc_ref, gain_ref, o_ref, s_scr):
    tg = GLA_TILE
    nchunk = tg // CHUNK
    heads = range(GLA_HEADS)
    ks = [slice(h * GLA_KP, (h + 1) * GLA_KP) for h in heads]
    vs = [slice(h * GLA_VP, (h + 1) * GLA_VP) for h in heads]

    @pl.when(pl.program_id(1) == 0)
    def _():
        s_scr[...] = jnp.zeros(s_scr.shape, F32)

    ri = lax.broadcasted_iota(jnp.int32, (tg, tg), 0)
    ci = lax.broadcasted_iota(jnp.int32, (tg, tg), 1)
    same_chunk = (ri // CHUNK) == (ci // CHUNK)
    past = ci <= ri
    row_chunk = lax.broadcasted_iota(jnp.int32, (tg, GLA_KP), 0) // CHUNK

    qe, scores, kv, decay = [], [], [], []
    for h in heads:
        qh = q_ref[0, :, ks[h]].astype(F32) * (GLA_K_DIM ** -0.5)
        kh = k_ref[0, :, ks[h]].astype(F32)
        g = gc_ref[0, :, ks[h]]
        eg = jnp.exp(g)
        ieg = jnp.exp(-g)
        qe.append((qh * eg).astype(BF16))
        a_past = _dot_nt(qe[h], (kh * ieg).astype(BF16))
        a_fut = _dot_nt((qh * ieg).astype(BF16), (kh * eg).astype(BF16))
        scores.append(jnp.where(same_chunk, jnp.where(past, a_past, a_fut), 0.0).astype(BF16))
        vt = v_ref[0, :, vs[h]].astype(F32).T.astype(BF16)
        g_last = [g[c * CHUNK + CHUNK - 1:c * CHUNK + CHUNK, :] for c in range(nchunk)]
        g_end = jnp.concatenate([jnp.broadcast_to(gl, (CHUNK, GLA_KP)) for gl in g_last], axis=0)
        kdec = kh * jnp.exp(g_end - g)
        kv.append([_dot(vt, jnp.where(row_chunk == c, kdec, 0.0).astype(BF16)) for c in range(nchunk)])
        decay.append([jnp.exp(gl) for gl in g_last])

    starts = []
    for h in heads:
        st = s_scr[h]
        per_chunk = []
        for c in range(nchunk):
            per_chunk.append(st.astype(BF16))
            st = st * decay[h][c] + kv[h][c]
        s_scr[h] = st
        starts.append(per_chunk)

    for h in heads:
        inter = [_dot_nt(qe[h][c * CHUNK:(c + 1) * CHUNK], starts[h][c]) for c in range(nchunk)]
        o = _dot(scores[h], v_ref[0, :, vs[h]]) + jnp.concatenate(inter, axis=0)
        ms = jnp.sum(o * o, axis=-1, keepdims=True) * (1.0 / GLA_V_DIM)
        y = o * lax.rsqrt(ms + EPS) * gain_ref[:, vs[h]]
        rh = r_ref[0, :, vs[h]].astype(F32)
        o_ref[0, :, vs[h]] = (y * (rh / (1.0 + jnp.exp(-rh)))).astype(BF16)


def _gla(q, k, v, r, gc, gain):
    b, s = q.shape[0], q.shape[1]
    kw = GLA_HEADS * GLA_KP
    vw = GLA_HEADS * GLA_VP
    spec = lambda n: pl.BlockSpec((1, GLA_TILE, n), lambda bi, i: (bi, i, 0))
    return pl.pallas_call(
        _gla_kernel,
        grid=(b, s // GLA_TILE),
        in_specs=[spec(kw), spec(kw), spec(vw), spec(vw), spec(kw), _const_spec((1, vw))],
        out_specs=spec(vw),
        out_shape=jax.ShapeDtypeStruct((b, s, vw), BF16),
        scratch_shapes=[pltpu.VMEM((GLA_HEADS, GLA_VP, GLA_KP), F32)],
        compiler_params=_cparams("parallel", "arbitrary"),
        name="gla",
    )(q, k, v, r, gc, gain)


def _mem_kv_kernel(mem_ref, g_ref, w_ref, gsum_ref, kg_ref, k_ref, v_ref):
    h = _rms_rows(mem_ref[...], g_ref[...]).astype(BF16)
    k_ref[...] = _group_rms(_dot(h, w_ref[:, :MEM_WIDTH]), gsum_ref[...], kg_ref[...]).astype(BF16)
    v_ref[...] = _dot(h, w_ref[:, MEM_WIDTH:]).astype(BF16)


def _mem_kv(mem2, g, w, gsum, kg):
    n = mem2.shape[0]
    return pl.pallas_call(
        _mem_kv_kernel,
        grid=(1,),
        in_specs=[_const_spec(mem2.shape), _const_spec((1, D_MODEL)), _const_spec(w.shape),
                  _const_spec(gsum.shape), _const_spec((1, MEM_WIDTH))],
        out_specs=[_const_spec((n, MEM_WIDTH)), _const_spec((n, MEM_WIDTH))],
        out_shape=[jax.ShapeDtypeStruct((n, MEM_WIDTH), BF16)] * 2,
        compiler_params=_cparams("arbitrary"),
        name="mem_kv",
    )(mem2, g, w, gsum, kg)


def _mix_out_kernel(x_ref, mix_ref, mq_ref, kbd_ref, vbd_ref, wa_ref, wb_ref, o_ref):
    m = kbd_ref.shape[2] // MEM_HEADS
    logits = _dot(mq_ref[0], kbd_ref[0])
    ps = []
    for h in range(MEM_HEADS):
        s = logits[:, h * m:(h + 1) * m]
        e = jnp.exp(s - jnp.max(s, axis=-1, keepdims=True))
        ps.append((e * (1.0 / jnp.sum(e, axis=-1, keepdims=True))).astype(BF16))
    cross = _dot(jnp.concatenate(ps, axis=1), vbd_ref[0])
    o_ref[0] = x_ref[0] + _dot(mix_ref[0], wa_ref[...]) + _dot(cross.astype(BF16), wb_ref[...])


def _mix_out(x, mix, mq, kbd, vbd, wa, wb):
    b, s = x.shape[0], x.shape[1]
    spec = lambda n: pl.BlockSpec((1, ROW_TILE, n), lambda bi, i: (bi, i, 0))
    per_b = lambda a: pl.BlockSpec((1,) + a.shape[1:], lambda bi, i: (bi, 0, 0))
    return pl.pallas_call(
        _mix_out_kernel,
        grid=(b, s // ROW_TILE),
        in_specs=[spec(D_MODEL), spec(mix.shape[2]), spec(MEM_WIDTH), per_b(kbd), per_b(vbd),
                  _const_spec(wa.shape), _const_spec(wb.shape)],
        out_specs=spec(D_MODEL),
        out_shape=jax.ShapeDtypeStruct(x.shape, F32),
        compiler_params=_cparams("parallel", "parallel"),
        name="mix_out",
    )(x, mix, mq, kbd, vbd, wa, wb)


def _ffn_kernel(x_ref, g_ref, wu_ref, cw_ref, cb_ref, wd_ref, o_ref, carry_scr, act_scr, *shift_scr):
    tm = FFN_TILE
    cr = CARRY_ROWS

    @pl.when(pl.program_id(1) == 0)
    def _():
        carry_scr[...] = jnp.zeros(carry_scr.shape, F32)

    x = x_ref[0]
    h = _rms_rows(x, g_ref[...]).astype(BF16)

    def conv(cols, bufs):
        u = _dot(h, wu_ref[:, cols])
        prev = carry_scr[:, cols]
        for shift, buf in zip((1, 2), bufs):
            buf[shift:shift + cr, :] = prev
            buf[cr + shift:cr + shift + tm, :] = u
        carry_scr[:, cols] = u[tm - cr:tm]
        return (cw_ref[0:1, cols] * bufs[1][cr:cr + tm, :] + cw_ref[1:2, cols] * bufs[0][cr:cr + tm, :]
                + cw_ref[2:3, cols] * u + cb_ref[:, cols])

    for j in range(D_FF // FFN_COLS):
        bufs = shift_scr[4 * (j % 2):4 * (j % 2) + 4]
        a = conv(slice(j * FFN_COLS, (j + 1) * FFN_COLS), bufs[0:2])
        gte = conv(slice(D_FF + j * FFN_COLS, D_FF + (j + 1) * FFN_COLS), bufs[2:4])
        act_scr[:, j * FFN_COLS:(j + 1) * FFN_COLS] = (a * (gte / (1.0 + jnp.exp(-gte)))).astype(BF16)

    o_ref[0] = x + _dot(act_scr[...], wd_ref[...])


def _ffn(x, g, wu, cw, cb, wd):
    b, s = x.shape[0], x.shape[1]
    spec = pl.BlockSpec((1, FFN_TILE, D_MODEL), lambda bi, i: (bi, i, 0))
    single = lambda a: pl.BlockSpec(a.shape, lambda bi, i: (0,) * a.ndim, pipeline_mode=pl.Buffered(1))
    return pl.pallas_call(
        _ffn_kernel,
        grid=(b, s // FFN_TILE),
        in_specs=[spec, _const_spec((1, D_MODEL)), single(wu), _const_spec(cw.shape),
                  _const_spec(cb.shape), single(wd)],
        out_specs=spec,
        out_shape=jax.ShapeDtypeStruct(x.shape, F32),
        scratch_shapes=[pltpu.VMEM((CARRY_ROWS, 2 * D_FF), F32), pltpu.VMEM((FFN_TILE, D_FF), BF16)]
        + [pltpu.VMEM((FFN_TILE + 2 * CARRY_ROWS, FFN_COLS), F32)] * 8,
        compiler_params=_cparams("parallel", "arbitrary"),
        name="ffn",
    )(x, g, wu, cw, cb, wd)


def _t5_bucket(rel):
    half = REL_BUCKETS // 2
    max_exact = half // 2
    ret = jnp.where(rel > 0, half, 0)
    n = jnp.abs(rel)
    nf = jnp.maximum(n, 1).astype(jnp.float32)
    large = max_exact + (jnp.log(nf / max_exact) / math.log(REL_MAX_DIST / max_exact)
                         * (half - max_exact)).astype(jnp.int32)
    large = jnp.minimum(large, half - 1)
    return ret + jnp.where(n < max_exact, n, large)


def _bias_vectors(rel_bias):
    blk = ATT_BLK
    assert blk >= REL_MAX_DIST
    table = rel_bias.astype(F32).T[:, :, None]

    def lookup(rel):
        bucket = _t5_bucket(rel)
        out = jnp.zeros((table.shape[0],) + rel.shape, F32)
        for i in range(REL_BUCKETS):
            out = jnp.where(bucket == i, table[:, i], out)
        return out

    far = lookup(jnp.full((1,), -2 * blk))
    j = jnp.arange(2 * blk)
    dist = jnp.where(j < blk, -j, 2 * blk - j)
    diag = (lookup(dist) - far) * LOG2E
    near = (lookup(dist - blk) - far) * LOG2E
    return jnp.stack([diag, near], axis=1)[:, :, None, :]


def _group_sum_matrix():
    i = jnp.arange(MXU_DIM)
    return ((i[:, None] // 64) == (i[None, :] // 64)).astype(BF16)


def _chunk_tri_matrix(n):
    i = jnp.arange(n)
    return (((i[:, None] // CHUNK) == (i[None, :] // CHUNK)) & (i[None, :] <= i[:, None])).astype(BF16)


def _pad_heads(w, heads, dim, pad, axis):
    shape = list(w.shape)
    shape[axis:axis + 1] = [heads, dim]
    w = w.reshape(shape)
    widths = [(0, 0)] * w.ndim
    widths[axis + 1] = (0, pad - dim)
    w = jnp.pad(w, widths)
    shape[axis:axis + 2] = [heads * pad]
    return w.reshape(shape)


def _tile_gain(g, reps, scale=1.0):
    return (jnp.tile(g.astype(F32), reps) * scale)[None, :]


def _mem_block_diag(kn, v, b):
    m = kn.shape[0] // b
    eye = jnp.eye(MEM_HEADS, dtype=BF16)
    knt = kn.reshape(b, m, MEM_WIDTH).transpose(0, 2, 1)
    kbd = (knt.reshape(b, MEM_HEADS, MEM_HEAD_DIM, 1, m) * eye.reshape(1, MEM_HEADS, 1, MEM_HEADS, 1))
    kbd = kbd.reshape(b, MEM_WIDTH, MEM_HEADS * m)
    vbd = (v.reshape(b, 1, m, MEM_HEADS, MEM_HEAD_DIM) * eye.reshape(1, MEM_HEADS, 1, MEM_HEADS, 1))
    vbd = vbd.reshape(b, MEM_HEADS * m, MEM_WIDTH)
    return kbd, vbd


def kernel(x, mem, rel_bias, attn_norm, ffn_norm, mem_norm, w_in_diff, diff_qk_norm, diff_lambda,
           diff_out_norm, w_in_gla, gla_gate_w, gla_gate_b, gla_out_norm, w_mem_kv, mem_qk_norm,
           w_out, w_up, conv_w, conv_b, w_down):
    b, s, d = x.shape
    t = b * s
    tw = TOKEN_WIDTH
    gsum = _group_sum_matrix()
    mem2 = mem.reshape(b * mem.shape[1], d)
    x = x.astype(F32)

    for i in range(DEPTH):
        j = i // 2
        x2 = x.reshape(t, d)
        mq_gain = _tile_gain(mem_qk_norm[i, 0], MEM_HEADS, MEM_HEAD_DIM ** -0.5)
        if i % 2 == 0:
            nh, hd = DIFF_HEADS, DIFF_HEAD_DIM
            qt, k, vt, mq = _pre_diff(
                x2, attn_norm[i][None, :], w_in_diff[j].astype(BF16), gsum,
                _tile_gain(diff_qk_norm[j, 0], 2 * nh, hd ** -0.5 * LOG2E),
                _tile_gain(diff_qk_norm[j, 1], 2 * nh), mq_gain, b)
            lv = diff_lambda[j].astype(F32)
            lam_init = 0.8 - 0.6 * math.exp(-0.3 * i)
            lam = jnp.exp(jnp.sum(lv[0] * lv[1])) - jnp.exp(jnp.sum(lv[2] * lv[3])) + lam_init
            bvec = _bias_vectors(rel_bias)
            qk_bound = (hd ** 0.5 * LOG2E * ATT_ROUNDING_SLACK
                        * jnp.max(jnp.abs(diff_qk_norm[j, 0])) * jnp.max(jnp.abs(diff_qk_norm[j, 1]))).astype(F32)
            hi = qk_bound + jnp.maximum(jnp.max(bvec), 0.0)
            lo = -qk_bound + jnp.minimum(jnp.min(bvec), 0.0)
            scalars = jnp.stack([lam.astype(F32), hi]).reshape(1, 2)
            attend = lambda bounded: functools.partial(
                _diff_attn, qt=qt, k=k.reshape(b, s, tw), vt=vt, bias=bvec,
                g=diff_out_norm[j].astype(F32)[None, :], out_scale=1.0 - lam_init, bounded=bounded)
            mix = lax.cond(hi - lo <= ATT_MAX_EXP2_SPAN, attend(True), attend(False), scalars)
            w_mix = w_out[i, :tw]
        else:
            kw = GLA_HEADS * GLA_K_DIM
            w = w_in_gla[j]
            hp = functools.partial(_pad_heads, heads=GLA_HEADS, axis=1)
            w_p = jnp.concatenate([
                hp(w[:, :kw], dim=GLA_K_DIM, pad=GLA_KP),
                hp(w[:, kw:2 * kw], dim=GLA_K_DIM, pad=GLA_KP),
                hp(w[:, 2 * kw:2 * kw + tw], dim=GLA_V_DIM, pad=GLA_VP),
                hp(w[:, 2 * kw + tw:2 * kw + 2 * tw], dim=GLA_V_DIM, pad=GLA_VP),
                jnp.pad(w[:, 2 * kw + 2 * tw:2 * kw + 2 * tw + GLA_GATE_RANK],
                        ((0, 0), (0, LANES - GLA_GATE_RANK))),
                w[:, 2 * kw + 2 * tw + GLA_GATE_RANK:]], axis=1).astype(BF16)
            gw = jnp.pad(hp(gla_gate_w[j], dim=GLA_K_DIM, pad=GLA_KP),
                         ((0, LANES - GLA_GATE_RANK), (0, 0))).astype(BF16)
            gb = _pad_heads(gla_gate_b[j].astype(F32)[None, :], GLA_HEADS, GLA_K_DIM, GLA_KP, 1)
            q, k, v, r, gc, mq = _pre_gla(x2, attn_norm[i][None, :], w_p, gsum, mq_gain, gw, gb,
                                          _chunk_tri_matrix(MXU_DIM))
            gain = _pad_heads(jnp.tile(gla_out_norm[j].astype(F32), GLA_HEADS)[None, :],
                              GLA_HEADS, GLA_V_DIM, GLA_VP, 1)
            sh = lambda a: a.reshape(b, s, a.shape[1])
            mix = _gla(sh(q), sh(k), sh(v), sh(r), sh(gc), gain)
            w_mix = _pad_heads(w_out[i, :tw], GLA_HEADS, GLA_V_DIM, GLA_VP, 0)

        kn, vm = _mem_kv(mem2, mem_norm[i][None, :], w_mem_kv[i].astype(BF16), gsum,
                         _tile_gain(mem_qk_norm[i, 1], MEM_HEADS))
        kbd, vbd = _mem_block_diag(kn, vm, b)
        x = _mix_out(x, mix.reshape(b, s, -1), mq.reshape(b, s, MEM_WIDTH), kbd, vbd,
                     w_mix.astype(BF16), w_out[i, tw:].astype(BF16))
        x = _ffn(x, ffn_norm[i][None, :], w_up[i].astype(BF16), conv_w[i].astype(F32),
                 conv_b[i].astype(F32)[None, :], w_down[i].astype(BF16))
    return x
```

```python
import functools
import math

import jax
import jax.numpy as jnp
from jax import lax
from jax.experimental import pallas as pl
from jax.experimental.pallas import tpu as pltpu

F32 = jnp.float32
BF16 = jnp.bfloat16

D_MODEL = 1024
DEPTH = 2
CHUNK = 64
MEM_WIDTH = D_MODEL // 4
MEM_HEADS = 4
MEM_HEAD_DIM = MEM_WIDTH // MEM_HEADS
TOKEN_WIDTH = D_MODEL - MEM_WIDTH
DIFF_HEAD_DIM = 64
DIFF_HEADS = TOKEN_WIDTH // (2 * DIFF_HEAD_DIM)
GLA_HEADS = 4
GLA_V_DIM = TOKEN_WIDTH // GLA_HEADS
GLA_K_DIM = GLA_V_DIM // 2
GLA_GATE_RANK = 16
GLA_GATE_TAU = 16.0
REL_BUCKETS = 32
REL_MAX_DIST = 128
D_FF = ((8 * D_MODEL // 3 + 127) // 128) * 128
EPS = 1e-6
LOG2E = math.log2(math.e)

LANES = 128
MXU_DIM = 256
VMEM_LIMIT_BYTES = 56 * 1024 * 1024

ROW_TILE = 512
ATT_BLK = 512
ATT_STREAMS = 4
ATT_MAX_EXP2_SPAN = 100.0
ATT_ROUNDING_SLACK = 1.02
ATT_V_ROWS = 2 * DIFF_HEAD_DIM + 16
GLA_TILE = 256
FFN_TILE = 512
FFN_COLS = 256
GLA_KP = 128
GLA_VP = 256
CARRY_ROWS = 8


def _cparams(*sem):
    return pltpu.CompilerParams(dimension_semantics=sem, vmem_limit_bytes=VMEM_LIMIT_BYTES)


def _const_spec(shape):
    n = len(shape)
    return pl.BlockSpec(shape, lambda *_: (0,) * n)


def _rms_rows(x, g):
    ms = jnp.mean(x * x, axis=-1, keepdims=True)
    return x * lax.rsqrt(ms + EPS) * g


def _group_rms(t, gsum, gain):
    cols = []
    for c in range(t.shape[1] // MXU_DIM):
        blk = t[:, c * MXU_DIM:(c + 1) * MXU_DIM]
        ss = jnp.dot((blk * blk).astype(BF16), gsum, preferred_element_type=F32)
        cols.append(blk * lax.rsqrt(ss * (1.0 / 64) + EPS))
    out = cols[0] if len(cols) == 1 else jnp.concatenate(cols, axis=1)
    return out * gain


def _dot(a, b):
    return jnp.dot(a, b, preferred_element_type=F32)


def _dot_nt(a, b):
    return lax.dot_general(a, b, (((1,), (1,)), ((), ())), preferred_element_type=F32)


def _pre_diff_kernel(x_ref, g_ref, w_ref, gsum_ref, qg_ref, kg_ref, mg_ref,
                     qt_ref, k_ref, vt_ref, mq_ref):
    tw = TOKEN_WIDTH
    hw = 2 * DIFF_HEAD_DIM
    h = _rms_rows(x_ref[...], g_ref[...]).astype(BF16)
    gsum = gsum_ref[...]
    q = _group_rms(_dot(h, w_ref[:, 0:tw]), gsum, qg_ref[...])
    k_ref[...] = _group_rms(_dot(h, w_ref[:, tw:2 * tw]), gsum, kg_ref[...]).astype(BF16)
    v = _dot(h, w_ref[:, 2 * tw:3 * tw])
    mq_ref[...] = _group_rms(_dot(h, w_ref[:, 3 * tw:]), gsum, mg_ref[...]).astype(BF16)
    ones = jnp.ones((ATT_V_ROWS - hw, ROW_TILE), BF16)
    for n in range(DIFF_HEADS):
        qt_ref[0, n, 0] = q[:, n * hw:(n + 1) * hw].T.astype(BF16)
        vt_ref[0, n, 0, 0:hw, :] = v[:, n * hw:(n + 1) * hw].T.astype(BF16)
        vt_ref[0, n, 0, hw:, :] = ones


def _pre_diff(x2, g, w, gsum, qg, kg, mg, b):
    t = x2.shape[0]
    tw = TOKEN_WIDTH
    hw = 2 * DIFF_HEAD_DIM
    assert ROW_TILE == ATT_BLK
    nq = t // b // ATT_BLK
    row = lambda n: pl.BlockSpec((ROW_TILE, n), lambda i: (i, 0))
    per_head = lambda r: pl.BlockSpec((1, DIFF_HEADS, 1, r, ATT_BLK), lambda i: (i // nq, 0, i % nq, 0, 0))
    return pl.pallas_call(
        _pre_diff_kernel,
        grid=(t // ROW_TILE,),
        in_specs=[row(D_MODEL), _const_spec((1, D_MODEL)), _const_spec(w.shape),
                  _const_spec(gsum.shape), _const_spec((1, tw)), _const_spec((1, tw)),
                  _const_spec((1, MEM_WIDTH))],
        out_specs=[per_head(hw), row(tw), per_head(ATT_V_ROWS), row(MEM_WIDTH)],
        out_shape=[jax.ShapeDtypeStruct((b, DIFF_HEADS, nq, hw, ATT_BLK), BF16),
                   jax.ShapeDtypeStruct((t, tw), BF16),
                   jax.ShapeDtypeStruct((b, DIFF_HEADS, nq, ATT_V_ROWS, ATT_BLK), BF16),
                   jax.ShapeDtypeStruct((t, MEM_WIDTH), BF16)],
        compiler_params=_cparams("parallel"),
        name="pre_diff",
    )(x2, g, w, gsum, qg, kg, mg)


def _diff_attn_kernel(lam_ref, qt_ref, k_ref, vt_ref, bvec_ref, g_ref, o_ref,
                      bias_scr, *scratch, out_scale):
    blk = ATT_BLK
    hd = DIFF_HEAD_DIM
    ns = ATT_STREAMS
    qp_scr, s_scr, cm_scr, p_scr, a_scr, m_scr, acc_scr = (scratch[i * ns:(i + 1) * ns] for i in range(7))
    w = 2 * blk // ns
    qi = pl.program_id(2)
    qt = qt_ref[0, 0, 0]
    row = lax.broadcasted_iota(jnp.int32, qt.shape, 0)
    zero = jnp.zeros_like(qt)
    q_maps = (jnp.where(row < hd, qt, zero), jnp.where(row >= hd, qt, zero))
    cols = [slice((x % (ns // 2)) * w, (x % (ns // 2) + 1) * w) for x in range(ns)]
    for x in range(ns):
        qp_scr[x][...] = q_maps[x // (ns // 2)][:, cols[x]]
        m_scr[x][...] = jnp.full(m_scr[x].shape, -jnp.inf, F32)
        acc_scr[x][...] = jnp.zeros(acc_scr[x].shape, F32)

    @pl.when(qi == 0)
    def _():
        kk = lax.broadcasted_iota(jnp.int32, (blk, blk), 0)
        qq = lax.broadcasted_iota(jnp.int32, (blk, blk), 1)
        visible = (kk // CHUNK) <= (qq // CHUNK)
        for i in range(2):
            rows = jnp.broadcast_to(bvec_ref[0, i], (blk, 2 * blk))
            tile = pltpu.roll(rows, 0, 1, stride=1, stride_axis=0)[:, :blk]
            bias_scr[i] = jnp.where(visible, tile, -jnp.inf) if i == 0 else tile

    def key_block(t):
        return jnp.maximum(qi - t, 0)

    def logits(x, t, bias=None):
        start = pl.multiple_of(key_block(t) * blk, blk)
        s = _dot(k_ref[0, pl.ds(start, blk), :], qp_scr[x][...])
        if bias is not None:
            s = s + bias[:, cols[x]]
        s_scr[x][...] = s
        part = s[0:8]
        for r in range(8, blk, 8):
            part = jnp.maximum(part, s[r:r + 8])
        cm_scr[x][...] = part

    def softmax(x):
        m_old = m_scr[x][...]
        m_new = jnp.maximum(m_old, jnp.max(cm_scr[x][...], axis=0, keepdims=True))
        a_scr[x][...] = jnp.exp2(m_old - m_new)
        m_scr[x][...] = m_new
        for r in range(0, blk, 16):
            p_scr[x][r:r + 16, :] = jnp.exp2((s_scr[x][r:r + 16, :] - m_new).astype(BF16))

    def values(x, t):
        acc_scr[x][...] = a_scr[x][...] * acc_scr[x][...] + _dot(vt_ref[0, 0, key_block(t)], p_scr[x][...])

    def step(t, bias=None):
        for x in range(ns):
            logits(x, t + 1, bias)
            values(x, t)
            softmax((x + 1) % ns)

    for x in range(ns):
        logits(x, 0, bias_scr[0])
    softmax(0)
    step(0, bias_scr[1] + jnp.where(qi == 0, -jnp.inf, 0.0).astype(F32))

    t_last = jnp.maximum(qi, 1)

    def far_pair(u, c):
        step(1 + 2 * u)
        step(2 + 2 * u)
        return c

    lax.fori_loop(0, (t_last - 1) // 2, far_pair, 0)

    @pl.when((t_last - 1) % 2 == 1)
    def _():
        step(t_last - 1)

    for x in range(ns):
        values(x, t_last)
        if x + 1 < ns:
            softmax(x + 1)

    lam = lam_ref[0, 0]
    half = ns // 2
    outs = []
    for x in range(half):
        o0 = acc_scr[x][0:2 * hd, :] * (1.0 / acc_scr[x][2 * hd:2 * hd + 1, :])
        o1 = acc_scr[half + x][0:2 * hd, :] * (1.0 / acc_scr[half + x][2 * hd:2 * hd + 1, :])
        outs.append(o0 - lam * o1)
    ot = outs[0] if half == 1 else jnp.concatenate(outs, axis=1)
    ms = jnp.mean(ot * ot, axis=0, keepdims=True)
    y = (ot * lax.rsqrt(ms + EPS)).T * (g_ref[...] * out_scale)
    o_ref[0] = y.astype(BF16)


def _diff_attn_bounded_kernel(lam_ref, qt_ref, k_ref, vt_ref, bvec_ref, g_ref, o_ref,
                              bias_scr, *scratch, out_scale):
    blk = ATT_BLK
    hd = DIFF_HEAD_DIM
    ns = ATT_STREAMS
    nq = qt_ref.shape[2]
    qp_scr, p0_scr, p1_scr, acc_scr = (scratch[i * ns:(i + 1) * ns] for i in range(4))
    p_scr = (p0_scr, p1_scr)
    w = 2 * blk // ns
    lam = lam_ref[0, 0]
    m_ref = lam_ref[0, 1]
    cols = [slice((x % (ns // 2)) * w, (x % (ns // 2) + 1) * w) for x in range(ns)]

    kk = lax.broadcasted_iota(jnp.int32, (blk, blk), 0)
    qq = lax.broadcasted_iota(jnp.int32, (blk, blk), 1)
    visible = (kk // CHUNK) <= (qq // CHUNK)
    for i in range(2):
        rows = jnp.broadcast_to(bvec_ref[0, i], (blk, 2 * blk))
        tile = pltpu.roll(rows, 0, 1, stride=1, stride_axis=0)[:, :blk]
        bias_scr[i] = jnp.where(visible, tile, -jnp.inf) if i == 0 else tile

    def key_block(qi, t):
        return jnp.maximum(qi - t, 0)

    def probs(qi, x, t, bias=None):
        start = pl.multiple_of(key_block(qi, t) * blk, blk)
        s = _dot(k_ref[0, pl.ds(start, blk), :], qp_scr[x][...])
        if bias is not None:
            s = s + bias[:, cols[x]]
        return jnp.exp2(s - m_ref).astype(BF16)

    def step(qi, t, slot, bias=None):
        for x in range(ns):
            p_scr[slot][x][...] = probs(qi, x, t + 1, bias)
            acc_scr[x][...] += _dot(vt_ref[0, 0, key_block(qi, t)], p_scr[1 - slot][x][...])

    def begin(qi):
        qt = qt_ref[0, 0, qi]
        row = lax.broadcasted_iota(jnp.int32, qt.shape, 0)
        zero = jnp.zeros_like(qt)
        q_maps = (jnp.where(row < hd, qt, zero), jnp.where(row >= hd, qt, zero))
        for x in range(ns):
            qp_scr[x][...] = q_maps[x // (ns // 2)][:, cols[x]]
            acc_scr[x][...] = jnp.zeros(acc_scr[x].shape, F32)
        for x in range(ns):
            p0_scr[x][...] = probs(qi, x, 0, bias_scr[0])
        step(qi, 0, 1, bias_scr[1] + jnp.where(qi == 0, -jnp.inf, 0.0).astype(F32))

    def sweep(qi):
        n_far = qi - 1

        def far_steps(n):
            def body(u, c):
                for i in range(n):
                    step(qi, 1 + n * u + i, i % 2)
                return c
            return body

        n_quads = n_far // 4
        lax.fori_loop(0, n_quads, far_steps(4), 0)
        t_done = 1 + 4 * n_quads

        @pl.when(n_far % 4 >= 2)
        def _():
            step(qi, t_done, 0)
            step(qi, t_done + 1, 1)

        @pl.when(n_far % 2 == 1)
        def _():
            step(qi, n_far, 0)

    def finish(qi):
        t_last = jnp.maximum(qi, 1)
        last_in_p1 = t_last % 2 == 1
        for x in range(ns):
            p_last = jnp.where(last_in_p1, p1_scr[x][...], p0_scr[x][...])
            acc_scr[x][...] += _dot(vt_ref[0, 0, key_block(qi, t_last)], p_last)
        half = ns // 2
        outs = []
        for x in range(half):
            o0 = acc_scr[x][0:2 * hd, :] * (1.0 / acc_scr[x][2 * hd:2 * hd + 1, :])
            o1 = acc_scr[half + x][0:2 * hd, :] * (1.0 / acc_scr[half + x][2 * hd:2 * hd + 1, :])
            outs.append(o0 - lam * o1)
        ot = outs[0] if half == 1 else jnp.concatenate(outs, axis=1)
        ms = jnp.mean(ot * ot, axis=0, keepdims=True)
        y = (ot * lax.rsqrt(ms + EPS)).T * (g_ref[...] * out_scale)
        o_ref[0, pl.ds(pl.multiple_of(qi * blk, blk), blk), :] = y.astype(BF16)

    begin(0)

    def query_block(qi, c):
        finish(qi - 1)
        begin(qi)
        sweep(qi)
        return c

    lax.fori_loop(1, nq, query_block, 0)
    finish(nq - 1)


def _diff_attn(lam, qt, k, vt, bias, g, out_scale, bounded):
    b, nh, nq = qt.shape[0], qt.shape[1], qt.shape[2]
    s = k.shape[1]
    blk = ATT_BLK
    ns = ATT_STREAMS
    w = 2 * blk // ns
    hw = 2 * DIFF_HEAD_DIM
    if bounded:
        body = _diff_attn_bounded_kernel
        per_stream = (((hw, w), BF16), ((blk, w), BF16), ((blk, w), BF16), ((ATT_V_ROWS, w), F32))
        grid = (b, nh)
        q_spec = pl.BlockSpec((1, 1, nq, hw, blk), lambda bi, hi: (bi, hi, 0, 0, 0))
        o_spec = pl.BlockSpec((1, s, hw), lambda bi, hi: (bi, 0, hi))
        sem = ("parallel", "parallel")
    else:
        body = _diff_attn_kernel
        per_stream = (((hw, w), BF16), ((blk, w), F32), ((8, w), F32), ((blk, w), BF16),
                      ((1, w), F32), ((1, w), F32), ((ATT_V_ROWS, w), F32))
        grid = (b, nh, nq)
        q_spec = pl.BlockSpec((1, 1, 1, hw, blk), lambda bi, hi, qi: (bi, hi, qi, 0, 0))
        o_spec = pl.BlockSpec((1, blk, hw), lambda bi, hi, qi: (bi, qi, hi))
        sem = ("parallel", "parallel", "arbitrary")
    return pl.pallas_call(
        functools.partial(body, out_scale=out_scale),
        grid=grid,
        in_specs=[
            pl.BlockSpec(memory_space=pltpu.SMEM),
            q_spec,
            pl.BlockSpec((1, s, hw), lambda bi, hi, *_: (bi, 0, hi)),
            pl.BlockSpec((1, 1, nq, ATT_V_ROWS, blk), lambda bi, hi, *_: (bi, hi, 0, 0, 0)),
            pl.BlockSpec((1, 2, 1, 2 * blk), lambda bi, hi, *_: (hi, 0, 0, 0)),
            _const_spec((1, hw)),
        ],
        out_specs=o_spec,
        out_shape=jax.ShapeDtypeStruct((b, s, nh * hw), BF16),
        scratch_shapes=[pltpu.VMEM((2, blk, blk), F32)] + [pltpu.VMEM(shape, dtype)
                                                        for shape, dtype in per_stream for _ in range(ns)],
        compiler_params=_cparams(*sem),
        name="diff_attn_bounded" if bounded else "diff_attn",
    )(lam, qt, k, vt, bias, g)


def _pre_gla_kernel(x_ref, g_ref, w_ref, gsum_ref, mg_ref, gw_ref, gb_ref, tri_ref,
                    q_ref, k_ref, v_ref, r_ref, gc_ref, mq_ref):
    kw = GLA_HEADS * GLA_KP
    vw = GLA_HEADS * GLA_VP
    h = _rms_rows(x_ref[...], g_ref[...]).astype(BF16)
    q_ref[...] = _dot(h, w_ref[:, 0:kw]).astype(BF16)
    k_ref[...] = _dot(h, w_ref[:, kw:2 * kw]).astype(BF16)
    o = 2 * kw
    v_ref[...] = _dot(h, w_ref[:, o:o + vw]).astype(BF16)
    r_ref[...] = _dot(h, w_ref[:, o + vw:o + 2 * vw]).astype(BF16)
    o = o + 2 * vw
    gate_low = _dot(h, w_ref[:, o:o + LANES]).astype(BF16)
    mq_ref[...] = _group_rms(_dot(h, w_ref[:, o + LANES:]), gsum_ref[...], mg_ref[...]).astype(BF16)
    z = _dot(gate_low, gw_ref[...]) + gb_ref[...]
    log_a = (jnp.minimum(z, 0.0) - jnp.log1p(jnp.exp(-jnp.abs(z)))) * (1.0 / GLA_GATE_TAU)
    hi = log_a.astype(BF16)
    rem = log_a - hi.astype(F32)
    mid = rem.astype(BF16)
    lo = (rem - mid.astype(F32)).astype(BF16)
    tri = tri_ref[...]
    n = tri.shape[0]
    for c in range(log_a.shape[0] // n):
        rows = slice(c * n, (c + 1) * n)
        gc_ref[rows, :] = _dot(tri, hi[rows]) + _dot(tri, mid[rows]) + _dot(tri, lo[rows])


def _pre_gla(x2, g, w, gsum, mg, gw, gb, tri):
    t = x2.shape[0]
    kw = GLA_HEADS * GLA_KP
    vw = GLA_HEADS * GLA_VP
    row = lambda n: pl.BlockSpec((ROW_TILE, n), lambda i: (i, 0))
    return pl.pallas_call(
        _pre_gla_kernel,
        grid=(t // ROW_TILE,),
        in_specs=[row(D_MODEL), _const_spec((1, D_MODEL)), _const_spec(w.shape),
                  _const_spec(gsum.shape), _const_spec((1, MEM_WIDTH)), _const_spec(gw.shape),
                  _const_spec(gb.shape), _const_spec(tri.shape)],
        out_specs=[row(kw), row(kw), row(vw), row(vw), row(kw), row(MEM_WIDTH)],
        out_shape=[jax.ShapeDtypeStruct((t, kw), BF16), jax.ShapeDtypeStruct((t, kw), BF16),
                   jax.ShapeDtypeStruct((t, vw), BF16), jax.ShapeDtypeStruct((t, vw), BF16),
                   jax.ShapeDtypeStruct((t, kw), F32), jax.ShapeDtypeStruct((t, MEM_WIDTH), BF16)],
        compiler_params=_cparams("parallel"),
        name="pre_gla",
    )(x2, g, w, gsum, mg, gw, gb, tri)


def _gla_kernel(q_ref, k_ref, v_ref, r_ref, gc_ref, gain_ref, o_ref, s_scr):
    tg = GLA_TILE
    nchunk = tg // CHUNK
    heads = range(GLA_HEADS)
    ks = [slice(h * GLA_KP, (h + 1) * GLA_KP) for h in heads]
    vs = [slice(h * GLA_VP, (h + 1) * GLA_VP) for h in heads]

    @pl.when(pl.program_id(1) == 0)
    def _():
        s_scr[...] = jnp.zeros(s_scr.shape, F32)

    ri = lax.broadcasted_iota(jnp.int32, (tg, tg), 0)
    ci = lax.broadcasted_iota(jnp.int32, (tg, tg), 1)
    same_chunk = (ri // CHUNK) == (ci // CHUNK)
    past = ci <= ri
    row_chunk = lax.broadcasted_iota(jnp.int32, (tg, GLA_KP), 0) // CHUNK

    qe, scores, kv, decay = [], [], [], []
    for h in heads:
        qh = q_ref[0, :, ks[h]].astype(F32) * (GLA_K_DIM ** -0.5)
        kh = k_ref[0, :, ks[h]].astype(F32)
        g = gc_ref[0, :, ks[h]]
        eg = jnp.exp(g)
        ieg = jnp.exp(-g)
        qe.append((qh * eg).astype(BF16))
        a_past = _dot_nt(qe[h], (kh * ieg).astype(BF16))
        a_fut = _dot_nt((qh * ieg).astype(BF16), (kh * eg).astype(BF16))
        scores.append(jnp.where(same_chunk, jnp.where(past, a_past, a_fut), 0.0).astype(BF16))
        vt = v_ref[0, :, vs[h]].astype(F32).T.astype(BF16)
        g_last = [g[c * CHUNK + CHUNK - 1:c * CHUNK + CHUNK, :] for c in range(nchunk)]
        g_end = jnp.concatenate([jnp.broadcast_to(gl, (CHUNK, GLA_KP)) for gl in g_last], axis=0)
        kdec = kh * jnp.exp(g_end - g)
        kv.append([_dot(vt, jnp.where(row_chunk == c, kdec, 0.0).astype(BF16)) for c in range(nchunk)])
        decay.append([jnp.exp(gl) for gl in g_last])

    starts = []
    for h in heads:
        st = s_scr[h]
        per_chunk = []
        for c in range(nchunk):
            per_chunk.append(st.astype(BF16))
            st = st * decay[h][c] + kv[h][c]
        s_scr[h] = st
        starts.append(per_chunk)

    for h in heads:
        inter = [_dot_nt(qe[h][c * CHUNK:(c + 1) * CHUNK], starts[h][c]) for c in range(nchunk)]
        o = _dot(scores[h], v_ref[0, :, vs[h]]) + jnp.concatenate(inter, axis=0)
        ms = jnp.sum(o * o, axis=-1, keepdims=True) * (1.0 / GLA_V_DIM)
        y = o * lax.rsqrt(ms + EPS) * gain_ref[:, vs[h]]
        rh = r_ref[0, :, vs[h]].astype(F32)
        o_ref[0, :, vs[h]] = (y * (rh / (1.0 + jnp.exp(-rh)))).astype(BF16)


def _gla(q, k, v, r, gc, gain):
    b, s = q.shape[0], q.shape[1]
    kw = GLA_HEADS * GLA_KP
    vw = GLA_HEADS * GLA_VP
    spec = lambda n: pl.BlockSpec((1, GLA_TILE, n), lambda bi, i: (bi, i, 0))
    return pl.pallas_call(
        _gla_kernel,
        grid=(b, s // GLA_TILE),
        in_specs=[spec(kw), spec(kw), spec(vw), spec(vw), spec(kw), _const_spec((1, vw))],
        out_specs=spec(vw),
        out_shape=jax.ShapeDtypeStruct((b, s, vw), BF16),
        scratch_shapes=[pltpu.VMEM((GLA_HEADS, GLA_VP, GLA_KP), F32)],
        compiler_params=_cparams("parallel", "arbitrary"),
        name="gla",
    )(q, k, v, r, gc, gain)


def _mem_kv_kernel(mem_ref, g_ref, w_ref, gsum_ref, kg_ref, k_ref, v_ref):
    h = _rms_rows(mem_ref[...], g_ref[...]).astype(BF16)
    k_ref[...] = _group_rms(_dot(h, w_ref[:, :MEM_WIDTH]), gsum_ref[...], kg_ref[...]).astype(BF16)
    v_ref[...] = _dot(h, w_ref[:, MEM_WIDTH:]).astype(BF16)


def _mem_kv(mem2, g, w, gsum, kg):
    n = mem2.shape[0]
    return pl.pallas_call(
        _mem_kv_kernel,
        grid=(1,),
        in_specs=[_const_spec(mem2.shape), _const_spec((1, D_MODEL)), _const_spec(w.shape),
                  _const_spec(gsum.shape), _const_spec((1, MEM_WIDTH))],
        out_specs=[_const_spec((n, MEM_WIDTH)), _const_spec((n, MEM_WIDTH))],
        out_shape=[jax.ShapeDtypeStruct((n, MEM_WIDTH), BF16)] * 2,
        compiler_params=_cparams("arbitrary"),
        name="mem_kv",
    )(mem2, g, w, gsum, kg)


def _mix_out_kernel(x_ref, mix_ref, mq_ref, kbd_ref, vbd_ref, wa_ref, wb_ref, o_ref):
    m = kbd_ref.shape[2] // MEM_HEADS
    logits = _dot(mq_ref[0], kbd_ref[0])
    ps = []
    for h in range(MEM_HEADS):
        s = logits[:, h * m:(h + 1) * m]
        e = jnp.exp(s - jnp.max(s, axis=-1, keepdims=True))
        ps.append((e * (1.0 / jnp.sum(e, axis=-1, keepdims=True))).astype(BF16))
    cross = _dot(jnp.concatenate(ps, axis=1), vbd_ref[0])
    o_ref[0] = x_ref[0] + _dot(mix_ref[0], wa_ref[...]) + _dot(cross.astype(BF16), wb_ref[...])


def _mix_out(x, mix, mq, kbd, vbd, wa, wb):
    b, s = x.shape[0], x.shape[1]
    spec = lambda n: pl.BlockSpec((1, ROW_TILE, n), lambda bi, i: (bi, i, 0))
    per_b = lambda a: pl.BlockSpec((1,) + a.shape[1:], lambda bi, i: (bi, 0, 0))
    return pl.pallas_call(
        _mix_out_kernel,
        grid=(b, s // ROW_TILE),
        in_specs=[spec(D_MODEL), spec(mix.shape[2]), spec(MEM_WIDTH), per_b(kbd), per_b(vbd),
                  _const_spec(wa.shape), _const_spec(wb.shape)],
        out_specs=spec(D_MODEL),
        out_shape=jax.ShapeDtypeStruct(x.shape, F32),
        compiler_params=_cparams("parallel", "parallel"),
        name="mix_out",
    )(x, mix, mq, kbd, vbd, wa, wb)


def _ffn_kernel(x_ref, g_ref, wu_ref, cw_ref, cb_ref, wd_ref, o_ref, carry_scr, act_scr, *shift_scr):
    tm = FFN_TILE
    cr = CARRY_ROWS

    @pl.when(pl.program_id(1) == 0)
    def _():
        carry_scr[...] = jnp.zeros(carry_scr.shape, F32)

    x = x_ref[0]
    h = _rms_rows(x, g_ref[...]).astype(BF16)

    def conv(cols, bufs):
        u = _dot(h, wu_ref[:, cols])
        prev = carry_scr[:, cols]
        for shift, buf in zip((1, 2), bufs):
            buf[shift:shift + cr, :] = prev
            buf[cr + shift:cr + shift + tm, :] = u
        carry_scr[:, cols] = u[tm - cr:tm]
        return (cw_ref[0:1, cols] * bufs[1][cr:cr + tm, :] + cw_ref[1:2, cols] * bufs[0][cr:cr + tm, :]
                + cw_ref[2:3, cols] * u + cb_ref[:, cols])

    for j in range(D_FF // FFN_COLS):
        bufs = shift_scr[4 * (j % 2):4 * (j % 2) + 4]
        a = conv(slice(j * FFN_COLS, (j + 1) * FFN_COLS), bufs[0:2])
        gte = conv(slice(D_FF + j * FFN_COLS, D_FF + (j + 1) * FFN_COLS), bufs[2:4])
        act_scr[:, j * FFN_COLS:(j + 1) * FFN_COLS] = (a * (gte / (1.0 + jnp.exp(-gte)))).astype(BF16)

    o_ref[0] = x + _dot(act_scr[...], wd_ref[...])


def _ffn(x, g, wu, cw, cb, wd):
    b, s = x.shape[0], x.shape[1]
    spec = pl.BlockSpec((1, FFN_TILE, D_MODEL), lambda bi, i: (bi, i, 0))
    single = lambda a: pl.BlockSpec(a.shape, lambda bi, i: (0,) * a.ndim, pipeline_mode=pl.Buffered(1))
    return pl.pallas_call(
        _ffn_kernel,
        grid=(b, s // FFN_TILE),
        in_specs=[spec, _const_spec((1, D_MODEL)), single(wu), _const_spec(cw.shape),
                  _const_spec(cb.shape), single(wd)],
        out_specs=spec,
        out_shape=jax.ShapeDtypeStruct(x.shape, F32),
        scratch_shapes=[pltpu.VMEM((CARRY_ROWS, 2 * D_FF), F32), pltpu.VMEM((FFN_TILE, D_FF), BF16)]
        + [pltpu.VMEM((FFN_TILE + 2 * CARRY_ROWS, FFN_COLS), F32)] * 8,
        compiler_params=_cparams("parallel", "arbitrary"),
        name="ffn",
    )(x, g, wu, cw, cb, wd)


def _t5_bucket(rel):
    half = REL_BUCKETS // 2
    max_exact = half // 2
    ret = jnp.where(rel > 0, half, 0)
    n = jnp.abs(rel)
    nf = jnp.maximum(n, 1).astype(jnp.float32)
    large = max_exact + (jnp.log(nf / max_exact) / math.log(REL_MAX_DIST / max_exact)
                         * (half - max_exact)).astype(jnp.int32)
    large = jnp.minimum(large, half - 1)
    return ret + jnp.where(n < max_exact, n, large)


def _bias_vectors(rel_bias):
    blk = ATT_BLK
    assert blk >= REL_MAX_DIST
    table = rel_bias.astype(F32).T[:, :, None]

    def lookup(rel):
        bucket = _t5_bucket(rel)
        out = jnp.zeros((table.shape[0],) + rel.shape, F32)
        for i in range(REL_BUCKETS):
            out = jnp.where(bucket == i, table[:, i], out)
        return out

    far = lookup(jnp.full((1,), -2 * blk))
    j = jnp.arange(2 * blk)
    dist = jnp.where(j < blk, -j, 2 * blk - j)
    diag = (lookup(dist) - far) * LOG2E
    near = (lookup(dist - blk) - far) * LOG2E
    return jnp.stack([diag, near], axis=1)[:, :, None, :]


def _group_sum_matrix():
    i = jnp.arange(MXU_DIM)
    return ((i[:, None] // 64) == (i[None, :] // 64)).astype(BF16)


def _chunk_tri_matrix(n):
    i = jnp.arange(n)
    return (((i[:, None] // CHUNK) == (i[None, :] // CHUNK)) & (i[None, :] <= i[:, None])).astype(BF16)


def _pad_heads(w, heads, dim, pad, axis):
    shape = list(w.shape)
    shape[axis:axis + 1] = [heads, dim]
    w = w.reshape(shape)
    widths = [(0, 0)] * w.ndim
    widths[axis + 1] = (0, pad - dim)
    w = jnp.pad(w, widths)
    shape[axis:axis + 2] = [heads * pad]
    return w.reshape(shape)


def _tile_gain(g, reps, scale=1.0):
    return (jnp.tile(g.astype(F32), reps) * scale)[None, :]


def _mem_block_diag(kn, v, b):
    m = kn.shape[0] // b
    eye = jnp.eye(MEM_HEADS, dtype=BF16)
    knt = kn.reshape(b, m, MEM_WIDTH).transpose(0, 2, 1)
    kbd = (knt.reshape(b, MEM_HEADS, MEM_HEAD_DIM, 1, m) * eye.reshape(1, MEM_HEADS, 1, MEM_HEADS, 1))
    kbd = kbd.reshape(b, MEM_WIDTH, MEM_HEADS * m)
    vbd = (v.reshape(b, 1, m, MEM_HEADS, MEM_HEAD_DIM) * eye.reshape(1, MEM_HEADS, 1, MEM_HEADS, 1))
    vbd = vbd.reshape(b, MEM_HEADS * m, MEM_WIDTH)
    return kbd, vbd


def kernel(x, mem, rel_bias, attn_norm, ffn_norm, mem_norm, w_in_diff, diff_qk_norm, diff_lambda,
           diff_out_norm, w_in_gla, gla_gate_w, gla_gate_b, gla_out_norm, w_mem_kv, mem_qk_norm,
           w_out, w_up, conv_w, conv_b, w_down):
    b, s, d = x.shape
    t = b * s
    tw = TOKEN_WIDTH
    gsum = _group_sum_matrix()
    mem2 = mem.reshape(b * mem.shape[1], d)
    x = x.astype(F32)

    for i in range(DEPTH):
        j = i // 2
        x2 = x.reshape(t, d)
        mq_gain = _tile_gain(mem_qk_norm[i, 0], MEM_HEADS, MEM_HEAD_DIM ** -0.5)
        if i % 2 == 0:
            nh, hd = DIFF_HEADS, DIFF_HEAD_DIM
            qt, k, vt, mq = _pre_diff(
                x2, attn_norm[i][None, :], w_in_diff[j].astype(BF16), gsum,
                _tile_gain(diff_qk_norm[j, 0], 2 * nh, hd ** -0.5 * LOG2E),
                _tile_gain(diff_qk_norm[j, 1], 2 * nh), mq_gain, b)
            lv = diff_lambda[j].astype(F32)
            lam_init = 0.8 - 0.6 * math.exp(-0.3 * i)
            lam = jnp.exp(jnp.sum(lv[0] * lv[1])) - jnp.exp(jnp.sum(lv[2] * lv[3])) + lam_init
            bvec = _bias_vectors(rel_bias)
            qk_bound = (hd ** 0.5 * LOG2E * ATT_ROUNDING_SLACK
                        * jnp.max(jnp.abs(diff_qk_norm[j, 0])) * jnp.max(jnp.abs(diff_qk_norm[j, 1]))).astype(F32)
            hi = qk_bound + jnp.maximum(jnp.max(bvec), 0.0)
            lo = -qk_bound + jnp.minimum(jnp.min(bvec), 0.0)
            scalars = jnp.stack([lam.astype(F32), hi]).reshape(1, 2)
            attend = lambda bounded: functools.partial(
                _diff_attn, qt=qt, k=k.reshape(b, s, tw), vt=vt, bias=bvec,
                g=diff_out_norm[j].astype(F32)[None, :], out_scale=1.0 - lam_init, bounded=bounded)
            mix = lax.cond(hi - lo <= ATT_MAX_EXP2_SPAN, attend(True), attend(False), scalars)
            w_mix = w_out[i, :tw]
        else:
            kw = GLA_HEADS * GLA_K_DIM
            w = w_in_gla[j]
            hp = functools.partial(_pad_heads, heads=GLA_HEADS, axis=1)
            w_p = jnp.concatenate([
                hp(w[:, :kw], dim=GLA_K_DIM, pad=GLA_KP),
                hp(w[:, kw:2 * kw], dim=GLA_K_DIM, pad=GLA_KP),
                hp(w[:, 2 * kw:2 * kw + tw], dim=GLA_V_DIM, pad=GLA_VP),
                hp(w[:, 2 * kw + tw:2 * kw + 2 * tw], dim=GLA_V_DIM, pad=GLA_VP),
                jnp.pad(w[:, 2 * kw + 2 * tw:2 * kw + 2 * tw + GLA_GATE_RANK],
                        ((0, 0), (0, LANES - GLA_GATE_RANK))),
                w[:, 2 * kw + 2 * tw + GLA_GATE_RANK:]], axis=1).astype(BF16)
            gw = jnp.pad(hp(gla_gate_w[j], dim=GLA_K_DIM, pad=GLA_KP),
                         ((0, LANES - GLA_GATE_RANK), (0, 0))).astype(BF16)
            gb = _pad_heads(gla_gate_b[j].astype(F32)[None, :], GLA_HEADS, GLA_K_DIM, GLA_KP, 1)
            q, k, v, r, gc, mq = _pre_gla(x2, attn_norm[i][None, :], w_p, gsum, mq_gain, gw, gb,
                                          _chunk_tri_matrix(MXU_DIM))
            gain = _pad_heads(jnp.tile(gla_out_norm[j].astype(F32), GLA_HEADS)[None, :],
                              GLA_HEADS, GLA_V_DIM, GLA_VP, 1)
            sh = lambda a: a.reshape(b, s, a.shape[1])
            mix = _gla(sh(q), sh(k), sh(v), sh(r), sh(gc), gain)
            w_mix = _pad_heads(w_out[i, :tw], GLA_HEADS, GLA_V_DIM, GLA_VP, 0)

        kn, vm = _mem_kv(mem2, mem_norm[i][None, :], w_mem_kv[i].astype(BF16), gsum,
                         _tile_gain(mem_qk_norm[i, 1], MEM_HEADS))
        kbd, vbd = _mem_block_diag(kn, vm, b)
        x = _mix_out(x, mix.reshape(b, s, -1), mq.reshape(b, s, MEM_WIDTH), kbd, vbd,
                     w_mix.astype(BF16), w_out[i, tw:].astype(BF16))
        x = _ffn(x, ffn_norm[i][None, :], w_up[i].astype(BF16), conv_w[i].astype(F32),
                 conv_b[i].astype(F32)[None, :], w_down[i].astype(BF16))
    return x
```

```python
import functools
import math

import jax
import jax.numpy as jnp
from jax import lax
from jax.experimental import pallas as pl
from jax.experimental.pallas import tpu as pltpu

F32 = jnp.float32
BF16 = jnp.bfloat16

D_MODEL = 1024
DEPTH = 2
CHUNK = 64
MEM_WIDTH = D_MODEL // 4
MEM_HEADS = 4
MEM_HEAD_DIM = MEM_WIDTH // MEM_HEADS
TOKEN_WIDTH = D_MODEL - MEM_WIDTH
DIFF_HEAD_DIM = 64
DIFF_HEADS = TOKEN_WIDTH // (2 * DIFF_HEAD_DIM)
GLA_HEADS = 4
GLA_V_DIM = TOKEN_WIDTH // GLA_HEADS
GLA_K_DIM = GLA_V_DIM // 2
GLA_GATE_RANK = 16
GLA_GATE_TAU = 16.0
REL_BUCKETS = 32
REL_MAX_DIST = 128
D_FF = ((8 * D_MODEL // 3 + 127) // 128) * 128
EPS = 1e-6
LOG2E = math.log2(math.e)

LANES = 128
MXU_DIM = 256
VMEM_LIMIT_BYTES = 56 * 1024 * 1024

ROW_TILE = 512
ATT_BLK = 512
ATT_STREAMS = 4
ATT_MAX_EXP2_SPAN = 100.0
ATT_ROUNDING_SLACK = 1.02
ATT_V_ROWS = 2 * DIFF_HEAD_DIM + 16
GLA_TILE = 256
FFN_TILE = 512
FFN_COLS = 256
GLA_KP = 128
GLA_VP = 256
CARRY_ROWS = 8


def _cparams(*sem):
    return pltpu.CompilerParams(dimension_semantics=sem, vmem_limit_bytes=VMEM_LIMIT_BYTES)


def _const_spec(shape):
    n = len(shape)
    return pl.BlockSpec(shape, lambda *_: (0,) * n)


def _rms_rows(x, g):
    ms = jnp.mean(x * x, axis=-1, keepdims=True)
    return x * lax.rsqrt(ms + EPS) * g


def _group_rms(t, gsum, gain):
    cols = []
    for c in range(t.shape[1] // MXU_DIM):
        blk = t[:, c * MXU_DIM:(c + 1) * MXU_DIM]
        ss = jnp.dot((blk * blk).astype(BF16), gsum, preferred_element_type=F32)
        cols.append(blk * lax.rsqrt(ss * (1.0 / 64) + EPS))
    out = cols[0] if len(cols) == 1 else jnp.concatenate(cols, axis=1)
    return out * gain


def _dot(a, b):
    return jnp.dot(a, b, preferred_element_type=F32)


def _dot_nt(a, b):
    return lax.dot_general(a, b, (((1,), (1,)), ((), ())), preferred_element_type=F32)


def _pre_diff_kernel(x_ref, g_ref, w_ref, gsum_ref, qg_ref, kg_ref, mg_ref,
                     qt_ref, k_ref, vt_ref, mq_ref):
    tw = TOKEN_WIDTH
    hw = 2 * DIFF_HEAD_DIM
    h = _rms_rows(x_ref[...], g_ref[...]).astype(BF16)
    gsum = gsum_ref[...]
    q = _group_rms(_dot(h, w_ref[:, 0:tw]), gsum, qg_ref[...])
    k_ref[...] = _group_rms(_dot(h, w_ref[:, tw:2 * tw]), gsum, kg_ref[...]).astype(BF16)
    v = _dot(h, w_ref[:, 2 * tw:3 * tw])
    mq_ref[...] = _group_rms(_dot(h, w_ref[:, 3 * tw:]), gsum, mg_ref[...]).astype(BF16)
    ones = jnp.ones((ATT_V_ROWS - hw, ROW_TILE), BF16)
    for n in range(DIFF_HEADS):
        qt_ref[0, n, 0] = q[:, n * hw:(n + 1) * hw].T.astype(BF16)
        vt_ref[0, n, 0, 0:hw, :] = v[:, n * hw:(n + 1) * hw].T.astype(BF16)
        vt_ref[0, n, 0, hw:, :] = ones


def _pre_diff(x2, g, w, gsum, qg, kg, mg, b):
    t = x2.shape[0]
    tw = TOKEN_WIDTH
    hw = 2 * DIFF_HEAD_DIM
    assert ROW_TILE == ATT_BLK
    nq = t // b // ATT_BLK
    row = lambda n: pl.BlockSpec((ROW_TILE, n), lambda i: (i, 0))
    per_head = lambda r: pl.BlockSpec((1, DIFF_HEADS, 1, r, ATT_BLK), lambda i: (i // nq, 0, i % nq, 0, 0))
    return pl.pallas_call(
        _pre_diff_kernel,
        grid=(t // ROW_TILE,),
        in_specs=[row(D_MODEL), _const_spec((1, D_MODEL)), _const_spec(w.shape),
                  _const_spec(gsum.shape), _const_spec((1, tw)), _const_spec((1, tw)),
                  _const_spec((1, MEM_WIDTH))],
        out_specs=[per_head(hw), row(tw), per_head(ATT_V_ROWS), row(MEM_WIDTH)],
        out_shape=[jax.ShapeDtypeStruct((b, DIFF_HEADS, nq, hw, ATT_BLK), BF16),
                   jax.ShapeDtypeStruct((t, tw), BF16),
                   jax.ShapeDtypeStruct((b, DIFF_HEADS, nq, ATT_V_ROWS, ATT_BLK), BF16),
                   jax.ShapeDtypeStruct((t, MEM_WIDTH), BF16)],
        compiler_params=_cparams("parallel"),
        name="pre_diff",
    )(x2, g, w, gsum, qg, kg, mg)


def _diff_attn_kernel(lam_ref, qt_ref, k_ref, vt_ref, bvec_ref, g_ref, o_ref,
                      bias_scr, *scratch, out_scale):
    blk = ATT_BLK
    hd = DIFF_HEAD_DIM
    ns = ATT_STREAMS
    qp_scr, s_scr, cm_scr, p_scr, a_scr, m_scr, acc_scr = (scratch[i * ns:(i + 1) * ns] for i in range(7))
    w = 2 * blk // ns
    qi = pl.program_id(2)
    qt = qt_ref[0, 0, 0]
    row = lax.broadcasted_iota(jnp.int32, qt.shape, 0)
    zero = jnp.zeros_like(qt)
    q_maps = (jnp.where(row < hd, qt, zero), jnp.where(row >= hd, qt, zero))
    cols = [slice((x % (ns // 2)) * w, (x % (ns // 2) + 1) * w) for x in range(ns)]
    for x in range(ns):
        qp_scr[x][...] = q_maps[x // (ns // 2)][:, cols[x]]
        m_scr[x][...] = jnp.full(m_scr[x].shape, -jnp.inf, F32)
        acc_scr[x][...] = jnp.zeros(acc_scr[x].shape, F32)

    @pl.when(qi == 0)
    def _():
        kk = lax.broadcasted_iota(jnp.int32, (blk, blk), 0)
        qq = lax.broadcasted_iota(jnp.int32, (blk, blk), 1)
        visible = (kk // CHUNK) <= (qq // CHUNK)
        for i in range(2):
            rows = jnp.broadcast_to(bvec_ref[0, i], (blk, 2 * blk))
            tile = pltpu.roll(rows, 0, 1, stride=1, stride_axis=0)[:, :blk]
            bias_scr[i] = jnp.where(visible, tile, -jnp.inf) if i == 0 else tile

    def key_block(t):
        return jnp.maximum(qi - t, 0)

    def logits(x, t, bias=None):
        start = pl.multiple_of(key_block(t) * blk, blk)
        s = _dot(k_ref[0, pl.ds(start, blk), :], qp_scr[x][...])
        if bias is not None:
            s = s + bias[:, cols[x]]
        s_scr[x][...] = s
        part = s[0:8]
        for r in range(8, blk, 8):
            part = jnp.maximum(part, s[r:r + 8])
        cm_scr[x][...] = part

    def softmax(x):
        m_old = m_scr[x][...]
        m_new = jnp.maximum(m_old, jnp.max(cm_scr[x][...], axis=0, keepdims=True))
        a_scr[x][...] = jnp.exp2(m_old - m_new)
        m_scr[x][...] = m_new
        for r in range(0, blk, 16):
            p_scr[x][r:r + 16, :] = jnp.exp2((s_scr[x][r:r + 16, :] - m_new).astype(BF16))

    def values(x, t):
        acc_scr[x][...] = a_scr[x][...] * acc_scr[x][...] + _dot(vt_ref[0, 0, key_block(t)], p_scr[x][...])

    def step(t, bias=None):
        for x in range(ns):
            logits(x, t + 1, bias)
            values(x, t)
            softmax((x + 1) % ns)

    for x in range(ns):
        logits(x, 0, bias_scr[0])
    softmax(0)
    step(0, bias_scr[1] + jnp.where(qi == 0, -jnp.inf, 0.0).astype(F32))

    t_last = jnp.maximum(qi, 1)

    def far_pair(u, c):
        step(1 + 2 * u)
        step(2 + 2 * u)
        return c

    lax.fori_loop(0, (t_last - 1) // 2, far_pair, 0)

    @pl.when((t_last - 1) % 2 == 1)
    def _():
        step(t_last - 1)

    for x in range(ns):
        values(x, t_last)
        if x + 1 < ns:
            softmax(x + 1)

    lam = lam_ref[0, 0]
    half = ns // 2
    outs = []
    for x in range(half):
        o0 = acc_scr[x][0:2 * hd, :] * (1.0 / acc_scr[x][2 * hd:2 * hd + 1, :])
        o1 = acc_scr[half + x][0:2 * hd, :] * (1.0 / acc_scr[half + x][2 * hd:2 * hd + 1, :])
        outs.append(o0 - lam * o1)
    ot = outs[0] if half == 1 else jnp.concatenate(outs, axis=1)
    ms = jnp.mean(ot * ot, axis=0, keepdims=True)
    y = (ot * lax.rsqrt(ms + EPS)).T * (g_ref[...] * out_scale)
    o_ref[0] = y.astype(BF16)


def _diff_attn_bounded_kernel(lam_ref, qt_ref, k_ref, vt_ref, bvec_ref, g_ref, o_ref,
                              bias_scr, *scratch, out_scale):
    blk = ATT_BLK
    hd = DIFF_HEAD_DIM
    ns = ATT_STREAMS
    nq = qt_ref.shape[2]
    qp_scr, p0_scr, p1_scr, acc_scr, l_scr = (scratch[i * ns:(i + 1) * ns] for i in range(5))
    p_scr = (p0_scr, p1_scr)
    w = 2 * blk // ns
    lam = lam_ref[0, 0]
    m_ref = lam_ref[0, 1]
    cols = [slice((x % (ns // 2)) * w, (x % (ns // 2) + 1) * w) for x in range(ns)]

    kk = lax.broadcasted_iota(jnp.int32, (blk, blk), 0)
    qq = lax.broadcasted_iota(jnp.int32, (blk, blk), 1)
    visible = (kk // CHUNK) <= (qq // CHUNK)
    for i in range(2):
        rows = jnp.broadcast_to(bvec_ref[0, i], (blk, 2 * blk))
        tile = pltpu.roll(rows, 0, 1, stride=1, stride_axis=0)[:, :blk]
        bias_scr[i] = jnp.where(visible, tile, -jnp.inf) if i == 0 else tile

    def key_block(qi, t):
        return jnp.maximum(qi - t, 0)

    def probs(qi, x, t, bias=None):
        start = pl.multiple_of(key_block(qi, t) * blk, blk)
        s = _dot(k_ref[0, pl.ds(start, blk), :], qp_scr[x][...])
        if bias is not None:
            s = s + bias[:, cols[x]]
        p = jnp.exp2(s - m_ref)
        l_scr[x][...] += jnp.sum(p.reshape(blk // 8, 8, w), axis=0)
        return p.astype(BF16)

    def step(qi, t, slot, bias=None):
        for x in range(ns):
            p_scr[slot][x][...] = probs(qi, x, t + 1, bias)
            acc_scr[x][...] += _dot(vt_ref[0, 0, key_block(qi, t), 0:2 * hd, :], p_scr[1 - slot][x][...])

    def begin(qi):
        qt = qt_ref[0, 0, qi]
        row = lax.broadcasted_iota(jnp.int32, qt.shape, 0)
        zero = jnp.zeros_like(qt)
        q_maps = (jnp.where(row < hd, qt, zero), jnp.where(row >= hd, qt, zero))
        for x in range(ns):
            qp_scr[x][...] = q_maps[x // (ns // 2)][:, cols[x]]
            acc_scr[x][...] = jnp.zeros(acc_scr[x].shape, F32)
            l_scr[x][...] = jnp.zeros(l_scr[x].shape, F32)
        for x in range(ns):
            p0_scr[x][...] = probs(qi, x, 0, bias_scr[0])
        step(qi, 0, 1, bias_scr[1] + jnp.where(qi == 0, -jnp.inf, 0.0).astype(F32))

    def sweep(qi):
        n_far = qi - 1

        def far_steps(n):
            def body(u, c):
                for i in range(n):
                    step(qi, 1 + n * u + i, i % 2)
                return c
            return body

        n_quads = n_far // 4
        lax.fori_loop(0, n_quads, far_steps(4), 0)
        t_done = 1 + 4 * n_quads

        @pl.when(n_far % 4 >= 2)
        def _():
            step(qi, t_done, 0)
            step(qi, t_done + 1, 1)

        @pl.when(n_far % 2 == 1)
        def _():
            step(qi, n_far, 0)

    def finish(qi):
        t_last = jnp.maximum(qi, 1)
        last_in_p1 = t_last % 2 == 1
        for x in range(ns):
            p_last = jnp.where(last_in_p1, p1_scr[x][...], p0_scr[x][...])
            acc_scr[x][...] += _dot(vt_ref[0, 0, key_block(qi, t_last), 0:2 * hd, :], p_last)
        half = ns // 2
        inv = [1.0 / jnp.sum(l_scr[x][...], axis=0, keepdims=True) for x in range(ns)]
        outs = []
        for x in range(half):
            outs.append(acc_scr[x][...] * inv[x] - lam * (acc_scr[half + x][...] * inv[half + x]))
        ot = outs[0] if half == 1 else jnp.concatenate(outs, axis=1)
        ms = jnp.mean(ot * ot, axis=0, keepdims=True)
        y = (ot * lax.rsqrt(ms + EPS)).T * (g_ref[...] * out_scale)
        o_ref[0, pl.ds(pl.multiple_of(qi * blk, blk), blk), :] = y.astype(BF16)

    begin(0)

    def query_block(qi, c):
        finish(qi - 1)
        begin(qi)
        sweep(qi)
        return c

    lax.fori_loop(1, nq, query_block, 0)
    finish(nq - 1)


def _diff_attn(lam, qt, k, vt, bias, g, out_scale, bounded):
    b, nh, nq = qt.shape[0], qt.shape[1], qt.shape[2]
    s = k.shape[1]
    blk = ATT_BLK
    ns = ATT_STREAMS
    w = 2 * blk // ns
    hw = 2 * DIFF_HEAD_DIM
    if bounded:
        body = _diff_attn_bounded_kernel
        per_stream = (((hw, w), BF16), ((blk, w), BF16), ((blk, w), BF16), ((hw, w), F32), ((8, w), F32))
        grid = (b, nh)
        q_spec = pl.BlockSpec((1, 1, nq, hw, blk), lambda bi, hi: (bi, hi, 0, 0, 0))
        o_spec = pl.BlockSpec((1, s, hw), lambda bi, hi: (bi, 0, hi))
        sem = ("parallel", "parallel")
    else:
        body = _diff_attn_kernel
        per_stream = (((hw, w), BF16), ((blk, w), F32), ((8, w), F32), ((blk, w), BF16),
                      ((1, w), F32), ((1, w), F32), ((ATT_V_ROWS, w), F32))
        grid = (b, nh, nq)
        q_spec = pl.BlockSpec((1, 1, 1, hw, blk), lambda bi, hi, qi: (bi, hi, qi, 0, 0))
        o_spec = pl.BlockSpec((1, blk, hw), lambda bi, hi, qi: (bi, qi, hi))
        sem = ("parallel", "parallel", "arbitrary")
    return pl.pallas_call(
        functools.partial(body, out_scale=out_scale),
        grid=grid,
        in_specs=[
            pl.BlockSpec(memory_space=pltpu.SMEM),
            q_spec,
            pl.BlockSpec((1, s, hw), lambda bi, hi, *_: (bi, 0, hi)),
            pl.BlockSpec((1, 1, nq, ATT_V_ROWS, blk), lambda bi, hi, *_: (bi, hi, 0, 0, 0)),
            pl.BlockSpec((1, 2, 1, 2 * blk), lambda bi, hi, *_: (hi, 0, 0, 0)),
            _const_spec((1, hw)),
        ],
        out_specs=o_spec,
        out_shape=jax.ShapeDtypeStruct((b, s, nh * hw), BF16),
        scratch_shapes=[pltpu.VMEM((2, blk, blk), F32)] + [pltpu.VMEM(shape, dtype)
                                                        for shape, dtype in per_stream for _ in range(ns)],
        compiler_params=_cparams(*sem),
        name="diff_attn_bounded" if bounded else "diff_attn",
    )(lam, qt, k, vt, bias, g)


def _pre_gla_kernel(x_ref, g_ref, w_ref, gsum_ref, mg_ref, gw_ref, gb_ref, tri_ref,
                    q_ref, k_ref, v_ref, r_ref, gc_ref, mq_ref):
    kw = GLA_HEADS * GLA_KP
    vw = GLA_HEADS * GLA_VP
    h = _rms_rows(x_ref[...], g_ref[...]).astype(BF16)
    q_ref[...] = _dot(h, w_ref[:, 0:kw]).astype(BF16)
    k_ref[...] = _dot(h, w_ref[:, kw:2 * kw]).astype(BF16)
    o = 2 * kw
    v_ref[...] = _dot(h, w_ref[:, o:o + vw]).astype(BF16)
    r_ref[...] = _dot(h, w_ref[:, o + vw:o + 2 * vw]).astype(BF16)
    o = o + 2 * vw
    gate_low = _dot(h, w_ref[:, o:o + LANES]).astype(BF16)
    mq_ref[...] = _group_rms(_dot(h, w_ref[:, o + LANES:]), gsum_ref[...], mg_ref[...]).astype(BF16)
    z = _dot(gate_low, gw_ref[...]) + gb_ref[...]
    log_a = (jnp.minimum(z, 0.0) - jnp.log1p(jnp.exp(-jnp.abs(z)))) * (1.0 / GLA_GATE_TAU)
    hi = log_a.astype(BF16)
    rem = log_a - hi.astype(F32)
    mid = rem.astype(BF16)
    lo = (rem - mid.astype(F32)).astype(BF16)
    tri = tri_ref[...]
    n = tri.shape[0]
    for c in range(log_a.shape[0] // n):
        rows = slice(c * n, (c + 1) * n)
        gc_ref[rows, :] = _dot(tri, hi[rows]) + _dot(tri, mid[rows]) + _dot(tri, lo[rows])


def _pre_gla(x2, g, w, gsum, mg, gw, gb, tri):
    t = x2.shape[0]
    kw = GLA_HEADS * GLA_KP
    vw = GLA_HEADS * GLA_VP
    row = lambda n: pl.BlockSpec((ROW_TILE, n), lambda i: (i, 0))
    return pl.pallas_call(
        _pre_gla_kernel,
        grid=(t // ROW_TILE,),
        in_specs=[row(D_MODEL), _const_spec((1, D_MODEL)), _const_spec(w.shape),
                  _const_spec(gsum.shape), _const_spec((1, MEM_WIDTH)), _const_spec(gw.shape),
                  _const_spec(gb.shape), _const_spec(tri.shape)],
        out_specs=[row(kw), row(kw), row(vw), row(vw), row(kw), row(MEM_WIDTH)],
        out_shape=[jax.ShapeDtypeStruct((t, kw), BF16), jax.ShapeDtypeStruct((t, kw), BF16),
                   jax.ShapeDtypeStruct((t, vw), BF16), jax.ShapeDtypeStruct((t, vw), BF16),
                   jax.ShapeDtypeStruct((t, kw), F32), jax.ShapeDtypeStruct((t, MEM_WIDTH), BF16)],
        compiler_params=_cparams("parallel"),
        name="pre_gla",
    )(x2, g, w, gsum, mg, gw, gb, tri)


def _gla_kernel(q_ref, k_ref, v_ref, r_ref, gc_ref, gain_ref, o_ref, s_scr):
    tg = GLA_TILE
    nchunk = tg // CHUNK
    heads = range(GLA_HEADS)
    ks = [slice(h * GLA_KP, (h + 1) * GLA_KP) for h in heads]
    vs = [slice(h * GLA_VP, (h + 1) * GLA_VP) for h in heads]

    @pl.when(pl.program_id(1) == 0)
    def _():
        s_scr[...] = jnp.zeros(s_scr.shape, F32)

    ri = lax.broadcasted_iota(jnp.int32, (tg, tg), 0)
    ci = lax.broadcasted_iota(jnp.int32, (tg, tg), 1)
    same_chunk = (ri // CHUNK) == (ci // CHUNK)
    past = ci <= ri
    row_chunk = lax.broadcasted_iota(jnp.int32, (tg, GLA_KP), 0) // CHUNK

    qe, scores, kv, decay = [], [], [], []
    for h in heads:
        qh = q_ref[0, :, ks[h]].astype(F32) * (GLA_K_DIM ** -0.5)
        kh = k_ref[0, :, ks[h]].astype(F32)
        g = gc_ref[0, :, ks[h]]
        eg = jnp.exp(g)
        ieg = jnp.exp(-g)
        qe.append((qh * eg).astype(BF16))
        a_past = _dot_nt(qe[h], (kh * ieg).astype(BF16))
        a_fut = _dot_nt((qh * ieg).astype(BF16), (kh * eg).astype(BF16))
        scores.append(jnp.where(same_chunk, jnp.where(past, a_past, a_fut), 0.0).astype(BF16))
        vt = v_ref[0, :, vs[h]].astype(F32).T.astype(BF16)
        g_last = [g[c * CHUNK + CHUNK - 1:c * CHUNK + CHUNK, :] for c in range(nchunk)]
        g_end = jnp.concatenate([jnp.broadcast_to(gl, (CHUNK, GLA_KP)) for gl in g_last], axis=0)
        kdec = kh * jnp.exp(g_end - g)
        kv.append([_dot(vt, jnp.where(row_chunk == c, kdec, 0.0).astype(BF16)) for c in range(nchunk)])
        decay.append([jnp.exp(gl) for gl in g_last])

    starts = []
    for h in heads:
        st = s_scr[h]
        per_chunk = []
        for c in range(nchunk):
            per_chunk.append(st.astype(BF16))
            st = st * decay[h][c] + kv[h][c]
        s_scr[h] = st
        starts.append(per_chunk)

    for h in heads:
        inter = [_dot_nt(qe[h][c * CHUNK:(c + 1) * CHUNK], starts[h][c]) for c in range(nchunk)]
        o = _dot(scores[h], v_ref[0, :, vs[h]]) + jnp.concatenate(inter, axis=0)
        ms = jnp.sum(o * o, axis=-1, keepdims=True) * (1.0 / GLA_V_DIM)
        y = o * lax.rsqrt(ms + EPS) * gain_ref[:, vs[h]]
        rh = r_ref[0, :, vs[h]].astype(F32)
        o_ref[0, :, vs[h]] = (y * (rh / (1.0 + jnp.exp(-rh)))).astype(BF16)


def _gla(q, k, v, r, gc, gain):
    b, s = q.shape[0], q.shape[1]
    kw = GLA_HEADS * GLA_KP
    vw = GLA_HEADS * GLA_VP
    spec = lambda n: pl.BlockSpec((1, GLA_TILE, n), lambda bi, i: (bi, i, 0))
    return pl.pallas_call(
        _gla_kernel,
        grid=(b, s // GLA_TILE),
        in_specs=[spec(kw), spec(kw), spec(vw), spec(vw), spec(kw), _const_spec((1, vw))],
        out_specs=spec(vw),
        out_shape=jax.ShapeDtypeStruct((b, s, vw), BF16),
        scratch_shapes=[pltpu.VMEM((GLA_HEADS, GLA_VP, GLA_KP), F32)],
        compiler_params=_cparams("parallel", "arbitrary"),
        name="gla",
    )(q, k, v, r, gc, gain)


def _mem_kv_kernel(mem_ref, g_ref, w_ref, gsum_ref, kg_ref, k_ref, v_ref):
    h = _rms_rows(mem_ref[...], g_ref[...]).astype(BF16)
    k_ref[...] = _group_rms(_dot(h, w_ref[:, :MEM_WIDTH]), gsum_ref[...], kg_ref[...]).astype(BF16)
    v_ref[...] = _dot(h, w_ref[:, MEM_WIDTH:]).astype(BF16)


def _mem_kv(mem2, g, w, gsum, kg):
    n = mem2.shape[0]
    return pl.pallas_call(
        _mem_kv_kernel,
        grid=(1,),
        in_specs=[_const_spec(mem2.shape), _const_spec((1, D_MODEL)), _const_spec(w.shape),
                  _const_spec(gsum.shape), _const_spec((1, MEM_WIDTH))],
        out_specs=[_const_spec((n, MEM_WIDTH)), _const_spec((n, MEM_WIDTH))],
        out_shape=[jax.ShapeDtypeStruct((n, MEM_WIDTH), BF16)] * 2,
        compiler_params=_cparams("arbitrary"),
        name="mem_kv",
    )(mem2, g, w, gsum, kg)


def _mix_out_kernel(x_ref, mix_ref, mq_ref, kbd_ref, vbd_ref, wa_ref, wb_ref, o_ref):
    m = kbd_ref.shape[2] // MEM_HEADS
    logits = _dot(mq_ref[0], kbd_ref[0])
    ps = []
    for h in range(MEM_HEADS):
        s = logits[:, h * m:(h + 1) * m]
        e = jnp.exp(s - jnp.max(s, axis=-1, keepdims=True))
        ps.append((e * (1.0 / jnp.sum(e, axis=-1, keepdims=True))).astype(BF16))
    cross = _dot(jnp.concatenate(ps, axis=1), vbd_ref[0])
    o_ref[0] = x_ref[0] + _dot(mix_ref[0], wa_ref[...]) + _dot(cross.astype(BF16), wb_ref[...])


def _mix_out(x, mix, mq, kbd, vbd, wa, wb):
    b, s = x.shape[0], x.shape[1]
    spec = lambda n: pl.BlockSpec((1, ROW_TILE, n), lambda bi, i: (bi, i, 0))
    per_b = lambda a: pl.BlockSpec((1,) + a.shape[1:], lambda bi, i: (bi, 0, 0))
    return pl.pallas_call(
        _mix_out_kernel,
        grid=(b, s // ROW_TILE),
        in_specs=[spec(D_MODEL), spec(mix.shape[2]), spec(MEM_WIDTH), per_b(kbd), per_b(vbd),
                  _const_spec(wa.shape), _const_spec(wb.shape)],
        out_specs=spec(D_MODEL),
        out_shape=jax.ShapeDtypeStruct(x.shape, F32),
        compiler_params=_cparams("parallel", "parallel"),
        name="mix_out",
    )(x, mix, mq, kbd, vbd, wa, wb)


def _ffn_kernel(x_ref, g_ref, wu_ref, cw_ref, cb_ref, wd_ref, o_ref, carry_scr, act_scr, *shift_scr):
    tm = FFN_TILE
    cr = CARRY_ROWS

    @pl.when(pl.program_id(1) == 0)
    def _():
        carry_scr[...] = jnp.zeros(carry_scr.shape, F32)

    x = x_ref[0]
    h = _rms_rows(x, g_ref[...]).astype(BF16)

    def conv(cols, bufs):
        u = _dot(h, wu_ref[:, cols])
        prev = carry_scr[:, cols]
        for shift, buf in zip((1, 2), bufs):
            buf[shift:shift + cr, :] = prev
            buf[cr + shift:cr + shift + tm, :] = u
        carry_scr[:, cols] = u[tm - cr:tm]
        return (cw_ref[0:1, cols] * bufs[1][cr:cr + tm, :] + cw_ref[1:2, cols] * bufs[0][cr:cr + tm, :]
                + cw_ref[2:3, cols] * u + cb_ref[:, cols])

    for j in range(D_FF // FFN_COLS):
        bufs = shift_scr[4 * (j % 2):4 * (j % 2) + 4]
        a = conv(slice(j * FFN_COLS, (j + 1) * FFN_COLS), bufs[0:2])
        half_g = 0.5 * conv(slice(D_FF + j * FFN_COLS, D_FF + (j + 1) * FFN_COLS), bufs[2:4])
        act_scr[:, j * FFN_COLS:(j + 1) * FFN_COLS] = (a * half_g * (1.0 + jnp.tanh(half_g))).astype(BF16)

    o_ref[0] = x + _dot(act_scr[...], wd_ref[...])


def _ffn(x, g, wu, cw, cb, wd):
    b, s = x.shape[0], x.shape[1]
    spec = pl.BlockSpec((1, FFN_TILE, D_MODEL), lambda bi, i: (bi, i, 0))
    single = lambda a: pl.BlockSpec(a.shape, lambda bi, i: (0,) * a.ndim, pipeline_mode=pl.Buffered(1))
    return pl.pallas_call(
        _ffn_kernel,
        grid=(b, s // FFN_TILE),
        in_specs=[spec, _const_spec((1, D_MODEL)), single(wu), _const_spec(cw.shape),
                  _const_spec(cb.shape), single(wd)],
        out_specs=spec,
        out_shape=jax.ShapeDtypeStruct(x.shape, F32),
        scratch_shapes=[pltpu.VMEM((CARRY_ROWS, 2 * D_FF), F32), pltpu.VMEM((FFN_TILE, D_FF), BF16)]
        + [pltpu.VMEM((FFN_TILE + 2 * CARRY_ROWS, FFN_COLS), F32)] * 8,
        compiler_params=_cparams("parallel", "arbitrary"),
        name="ffn",
    )(x, g, wu, cw, cb, wd)


def _t5_bucket(rel):
    half = REL_BUCKETS // 2
    max_exact = half // 2
    ret = jnp.where(rel > 0, half, 0)
    n = jnp.abs(rel)
    nf = jnp.maximum(n, 1).astype(jnp.float32)
    large = max_exact + (jnp.log(nf / max_exact) / math.log(REL_MAX_DIST / max_exact)
                         * (half - max_exact)).astype(jnp.int32)
    large = jnp.minimum(large, half - 1)
    return ret + jnp.where(n < max_exact, n, large)


def _bias_vectors(rel_bias):
    blk = ATT_BLK
    assert blk >= REL_MAX_DIST
    table = rel_bias.astype(F32).T[:, :, None]

    def lookup(rel):
        bucket = _t5_bucket(rel)
        out = jnp.zeros((table.shape[0],) + rel.shape, F32)
        for i in range(REL_BUCKETS):
            out = jnp.where(bucket == i, table[:, i], out)
        return out

    far = lookup(jnp.full((1,), -2 * blk))
    j = jnp.arange(2 * blk)
    dist = jnp.where(j < blk, -j, 2 * blk - j)
    diag = (lookup(dist) - far) * LOG2E
    near = (lookup(dist - blk) - far) * LOG2E
    return jnp.stack([diag, near], axis=1)[:, :, None, :]


def _group_sum_matrix():
    i = jnp.arange(MXU_DIM)
    return ((i[:, None] // 64) == (i[None, :] // 64)).astype(BF16)


def _chunk_tri_matrix(n):
    i = jnp.arange(n)
    return (((i[:, None] // CHUNK) == (i[None, :] // CHUNK)) & (i[None, :] <= i[:, None])).astype(BF16)


def _pad_heads(w, heads, dim, pad, axis):
    shape = list(w.shape)
    shape[axis:axis + 1] = [heads, dim]
    w = w.reshape(shape)
    widths = [(0, 0)] * w.ndim
    widths[axis + 1] = (0, pad - dim)
    w = jnp.pad(w, widths)
    shape[axis:axis + 2] = [heads * pad]
    return w.reshape(shape)


def _tile_gain(g, reps, scale=1.0):
    return (jnp.tile(g.astype(F32), reps) * scale)[None, :]


def _mem_block_diag(kn, v, b):
    m = kn.shape[0] // b
    eye = jnp.eye(MEM_HEADS, dtype=BF16)
    knt = kn.reshape(b, m, MEM_WIDTH).transpose(0, 2, 1)
    kbd = (knt.reshape(b, MEM_HEADS, MEM_HEAD_DIM, 1, m) * eye.reshape(1, MEM_HEADS, 1, MEM_HEADS, 1))
    kbd = kbd.reshape(b, MEM_WIDTH, MEM_HEADS * m)
    vbd = (v.reshape(b, 1, m, MEM_HEADS, MEM_HEAD_DIM) * eye.reshape(1, MEM_HEADS, 1, MEM_HEADS, 1))
    vbd = vbd.reshape(b, MEM_HEADS * m, MEM_WIDTH)
    return kbd, vbd


def kernel(x, mem, rel_bias, attn_norm, ffn_norm, mem_norm, w_in_diff, diff_qk_norm, diff_lambda,
           diff_out_norm, w_in_gla, gla_gate_w, gla_gate_b, gla_out_norm, w_mem_kv, mem_qk_norm,
           w_out, w_up, conv_w, conv_b, w_down):
    b, s, d = x.shape
    t = b * s
    tw = TOKEN_WIDTH
    gsum = _group_sum_matrix()
    mem2 = mem.reshape(b * mem.shape[1], d)
    x = x.astype(F32)

    for i in range(DEPTH):
        j = i // 2
        x2 = x.reshape(t, d)
        mq_gain = _tile_gain(mem_qk_norm[i, 0], MEM_HEADS, MEM_HEAD_DIM ** -0.5)
        if i % 2 == 0:
            nh, hd = DIFF_HEADS, DIFF_HEAD_DIM
            qt, k, vt, mq = _pre_diff(
                x2, attn_norm[i][None, :], w_in_diff[j].astype(BF16), gsum,
                _tile_gain(diff_qk_norm[j, 0], 2 * nh, hd ** -0.5 * LOG2E),
                _tile_gain(diff_qk_norm[j, 1], 2 * nh), mq_gain, b)
            lv = diff_lambda[j].astype(F32)
            lam_init = 0.8 - 0.6 * math.exp(-0.3 * i)
            lam = jnp.exp(jnp.sum(lv[0] * lv[1])) - jnp.exp(jnp.sum(lv[2] * lv[3])) + lam_init
            bvec = _bias_vectors(rel_bias)
            qk_bound = (hd ** 0.5 * LOG2E * ATT_ROUNDING_SLACK
                        * jnp.max(jnp.abs(diff_qk_norm[j, 0])) * jnp.max(jnp.abs(diff_qk_norm[j, 1]))).astype(F32)
            hi = qk_bound + jnp.maximum(jnp.max(bvec), 0.0)
            lo = -qk_bound + jnp.minimum(jnp.min(bvec), 0.0)
            scalars = jnp.stack([lam.astype(F32), hi]).reshape(1, 2)
            attend = lambda bounded: functools.partial(
                _diff_attn, qt=qt, k=k.reshape(b, s, tw), vt=vt, bias=bvec,
                g=diff_out_norm[j].astype(F32)[None, :], out_scale=1.0 - lam_init, bounded=bounded)
            mix = lax.cond(hi - lo <= ATT_MAX_EXP2_SPAN, attend(True), attend(False), scalars)
            w_mix = w_out[i, :tw]
        else:
            kw = GLA_HEADS * GLA_K_DIM
            w = w_in_gla[j]
            hp = functools.partial(_pad_heads, heads=GLA_HEADS, axis=1)
            w_p = jnp.concatenate([
                hp(w[:, :kw], dim=GLA_K_DIM, pad=GLA_KP),
                hp(w[:, kw:2 * kw], dim=GLA_K_DIM, pad=GLA_KP),
                hp(w[:, 2 * kw:2 * kw + tw], dim=GLA_V_DIM, pad=GLA_VP),
                hp(w[:, 2 * kw + tw:2 * kw + 2 * tw], dim=GLA_V_DIM, pad=GLA_VP),
                jnp.pad(w[:, 2 * kw + 2 * tw:2 * kw + 2 * tw + GLA_GATE_RANK],
                        ((0, 0), (0, LANES - GLA_GATE_RANK))),
                w[:, 2 * kw + 2 * tw + GLA_GATE_RANK:]], axis=1).astype(BF16)
            gw = jnp.pad(hp(gla_gate_w[j], dim=GLA_K_DIM, pad=GLA_KP),
                         ((0, LANES - GLA_GATE_RANK), (0, 0))).astype(BF16)
            gb = _pad_heads(gla_gate_b[j].astype(F32)[None, :], GLA_HEADS, GLA_K_DIM, GLA_KP, 1)
            q, k, v, r, gc, mq = _pre_gla(x2, attn_norm[i][None, :], w_p, gsum, mq_gain, gw, gb,
                                          _chunk_tri_matrix(MXU_DIM))
            gain = _pad_heads(jnp.tile(gla_out_norm[j].astype(F32), GLA_HEADS)[None, :],
                              GLA_HEADS, GLA_V_DIM, GLA_VP, 1)
            sh = lambda a: a.reshape(b, s, a.shape[1])
            mix = _gla(sh(q), sh(k), sh(v), sh(r), sh(gc), gain)
            w_mix = _pad_heads(w_out[i, :tw], GLA_HEADS, GLA_V_DIM, GLA_VP, 0)

        kn, vm = _mem_kv(mem2, mem_norm[i][None, :], w_mem_kv[i].astype(BF16), gsum,
                         _tile_gain(mem_qk_norm[i, 1], MEM_HEADS))
        kbd, vbd = _mem_block_diag(kn, vm, b)
        x = _mix_out(x, mix.reshape(b, s, -1), mq.reshape(b, s, MEM_WIDTH), kbd, vbd,
                     w_mix.astype(BF16), w_out[i, tw:].astype(BF16))
        x = _ffn(x, ffn_norm[i][None, :], w_up[i].astype(BF16), conv_w[i].astype(F32),
                 conv_b[i].astype(F32)[None, :], w_down[i].astype(BF16))
    return x
```

```python
import functools
import math

import jax
import jax.numpy as jnp
from jax import lax
from jax.experimental import pallas as pl
from jax.experimental.pallas import tpu as pltpu

F32 = jnp.float32
BF16 = jnp.bfloat16

D_MODEL = 1024
DEPTH = 2
CHUNK = 64
MEM_WIDTH = D_MODEL // 4
MEM_HEADS = 4
MEM_HEAD_DIM = MEM_WIDTH // MEM_HEADS
TOKEN_WIDTH = D_MODEL - MEM_WIDTH
DIFF_HEAD_DIM = 64
DIFF_HEADS = TOKEN_WIDTH // (2 * DIFF_HEAD_DIM)
GLA_HEADS = 4
GLA_V_DIM = TOKEN_WIDTH // GLA_HEADS
GLA_K_DIM = GLA_V_DIM // 2
GLA_GATE_RANK = 16
GLA_GATE_TAU = 16.0
REL_BUCKETS = 32
REL_MAX_DIST = 128
D_FF = ((8 * D_MODEL // 3 + 127) // 128) * 128
EPS = 1e-6
LOG2E = math.log2(math.e)

LANES = 128
MXU_DIM = 256
VMEM_LIMIT_BYTES = 56 * 1024 * 1024

ROW_TILE = 512
ATT_BLK = 512
ATT_STREAMS = 4
ATT_MAX_EXP2_SPAN = 100.0
ATT_ROUNDING_SLACK = 1.02
ATT_V_ROWS = 2 * DIFF_HEAD_DIM + 16
GLA_TILE = 256
FFN_TILE = 512
FFN_COLS = 256
GLA_KP = 128
GLA_VP = 256
CARRY_ROWS = 8
CAST_BLOCK_BYTES = 4 * 1024 * 1024


def _cparams(*sem):
    return pltpu.CompilerParams(dimension_semantics=sem, vmem_limit_bytes=VMEM_LIMIT_BYTES)


def _const_spec(shape):
    n = len(shape)
    return pl.BlockSpec(shape, lambda *_: (0,) * n)


def _layer_spec(w_all, layer, rows=None, single=False):
    first, n_rows = (0, w_all.shape[1]) if rows is None else rows
    assert first % n_rows == 0
    mode = dict(pipeline_mode=pl.Buffered(1)) if single else {}
    return pl.BlockSpec((None, n_rows, w_all.shape[2]), lambda *_: (layer, first // n_rows, 0), **mode)


def _cast_kernel(w_ref, o_ref):
    o_ref[...] = w_ref[...].astype(BF16)


def _to_bf16(w):
    n, r, c = w.shape
    rows = max(rb for rb in range(16, r + 1, 16) if r % rb == 0 and rb * c * 4 <= CAST_BLOCK_BYTES)
    spec = pl.BlockSpec((1, rows, c), lambda a, i: (a, i, 0))
    return pl.pallas_call(
        _cast_kernel, grid=(n, r // rows), in_specs=[spec], out_specs=spec,
        out_shape=jax.ShapeDtypeStruct(w.shape, BF16),
        compiler_params=_cparams("parallel", "parallel"), name="to_bf16",
    )(w.astype(F32))


def _rms_rows(x, g):
    ms = jnp.mean(x * x, axis=-1, keepdims=True)
    return x * lax.rsqrt(ms + EPS) * g


def _group_rms(t, gsum, gain):
    cols = []
    for c in range(t.shape[1] // MXU_DIM):
        blk = t[:, c * MXU_DIM:(c + 1) * MXU_DIM]
        ss = jnp.dot((blk * blk).astype(BF16), gsum, preferred_element_type=F32)
        cols.append(blk * lax.rsqrt(ss * (1.0 / 64) + EPS))
    out = cols[0] if len(cols) == 1 else jnp.concatenate(cols, axis=1)
    return out * gain


def _dot(a, b):
    return jnp.dot(a, b, preferred_element_type=F32)


def _dot_nt(a, b):
    return lax.dot_general(a, b, (((1,), (1,)), ((), ())), preferred_element_type=F32)


def _pre_diff_kernel(x_ref, g_ref, w_ref, gsum_ref, qg_ref, kg_ref, mg_ref,
                     qt_ref, k_ref, vt_ref, mq_ref):
    tw = TOKEN_WIDTH
    hw = 2 * DIFF_HEAD_DIM
    h = _rms_rows(x_ref[...], g_ref[...]).astype(BF16)
    gsum = gsum_ref[...]
    q = _group_rms(_dot(h, w_ref[:, 0:tw]), gsum, qg_ref[...])
    k_ref[...] = _group_rms(_dot(h, w_ref[:, tw:2 * tw]), gsum, kg_ref[...]).astype(BF16)
    v = _dot(h, w_ref[:, 2 * tw:3 * tw])
    mq_ref[...] = _group_rms(_dot(h, w_ref[:, 3 * tw:]), gsum, mg_ref[...]).astype(BF16)
    ones = jnp.ones((ATT_V_ROWS - hw, ROW_TILE), BF16)
    for n in range(DIFF_HEADS):
        qt_ref[0, n, 0] = q[:, n * hw:(n + 1) * hw].T.astype(BF16)
        vt_ref[0, n, 0, 0:hw, :] = v[:, n * hw:(n + 1) * hw].T.astype(BF16)
        vt_ref[0, n, 0, hw:, :] = ones


def _pre_diff(x2, g, w_all, layer, gsum, qg, kg, mg, b):
    t = x2.shape[0]
    tw = TOKEN_WIDTH
    hw = 2 * DIFF_HEAD_DIM
    assert ROW_TILE == ATT_BLK
    nq = t // b // ATT_BLK
    row = lambda n: pl.BlockSpec((ROW_TILE, n), lambda i: (i, 0))
    per_head = lambda r: pl.BlockSpec((1, DIFF_HEADS, 1, r, ATT_BLK), lambda i: (i // nq, 0, i % nq, 0, 0))
    return pl.pallas_call(
        _pre_diff_kernel,
        grid=(t // ROW_TILE,),
        in_specs=[row(D_MODEL), _const_spec((1, D_MODEL)), _layer_spec(w_all, layer),
                  _const_spec(gsum.shape), _const_spec((1, tw)), _const_spec((1, tw)),
                  _const_spec((1, MEM_WIDTH))],
        out_specs=[per_head(hw), row(tw), per_head(ATT_V_ROWS), row(MEM_WIDTH)],
        out_shape=[jax.ShapeDtypeStruct((b, DIFF_HEADS, nq, hw, ATT_BLK), BF16),
                   jax.ShapeDtypeStruct((t, tw), BF16),
                   jax.ShapeDtypeStruct((b, DIFF_HEADS, nq, ATT_V_ROWS, ATT_BLK), BF16),
                   jax.ShapeDtypeStruct((t, MEM_WIDTH), BF16)],
        compiler_params=_cparams("parallel"),
        name="pre_diff",
    )(x2, g, w_all, gsum, qg, kg, mg)


def _diff_attn_kernel(lam_ref, qt_ref, k_ref, vt_ref, bvec_ref, g_ref, o_ref,
                      bias_scr, *scratch, out_scale):
    blk = ATT_BLK
    hd = DIFF_HEAD_DIM
    ns = ATT_STREAMS
    qp_scr, s_scr, cm_scr, p_scr, a_scr, m_scr, acc_scr = (scratch[i * ns:(i + 1) * ns] for i in range(7))
    w = 2 * blk // ns
    qi = pl.program_id(2)
    qt = qt_ref[0, 0, 0]
    row = lax.broadcasted_iota(jnp.int32, qt.shape, 0)
    zero = jnp.zeros_like(qt)
    q_maps = (jnp.where(row < hd, qt, zero), jnp.where(row >= hd, qt, zero))
    cols = [slice((x % (ns // 2)) * w, (x % (ns // 2) + 1) * w) for x in range(ns)]
    for x in range(ns):
        qp_scr[x][...] = q_maps[x // (ns // 2)][:, cols[x]]
        m_scr[x][...] = jnp.full(m_scr[x].shape, -jnp.inf, F32)
        acc_scr[x][...] = jnp.zeros(acc_scr[x].shape, F32)

    @pl.when(qi == 0)
    def _():
        kk = lax.broadcasted_iota(jnp.int32, (blk, blk), 0)
        qq = lax.broadcasted_iota(jnp.int32, (blk, blk), 1)
        visible = (kk // CHUNK) <= (qq // CHUNK)
        for i in range(2):
            rows = jnp.broadcast_to(bvec_ref[0, i], (blk, 2 * blk))
            tile = pltpu.roll(rows, 0, 1, stride=1, stride_axis=0)[:, :blk]
            bias_scr[i] = jnp.where(visible, tile, -jnp.inf) if i == 0 else tile

    def key_block(t):
        return jnp.maximum(qi - t, 0)

    def logits(x, t, bias=None):
        start = pl.multiple_of(key_block(t) * blk, blk)
        s = _dot(k_ref[0, pl.ds(start, blk), :], qp_scr[x][...])
        if bias is not None:
            s = s + bias[:, cols[x]]
        s_scr[x][...] = s
        part = s[0:8]
        for r in range(8, blk, 8):
            part = jnp.maximum(part, s[r:r + 8])
        cm_scr[x][...] = part

    def softmax(x):
        m_old = m_scr[x][...]
        m_new = jnp.maximum(m_old, jnp.max(cm_scr[x][...], axis=0, keepdims=True))
        a_scr[x][...] = jnp.exp2(m_old - m_new)
        m_scr[x][...] = m_new
        for r in range(0, blk, 16):
            p_scr[x][r:r + 16, :] = jnp.exp2((s_scr[x][r:r + 16, :] - m_new).astype(BF16))

    def values(x, t):
        acc_scr[x][...] = a_scr[x][...] * acc_scr[x][...] + _dot(vt_ref[0, 0, key_block(t)], p_scr[x][...])

    def step(t, bias=None):
        for x in range(ns):
            logits(x, t + 1, bias)
            values(x, t)
            softmax((x + 1) % ns)

    for x in range(ns):
        logits(x, 0, bias_scr[0])
    softmax(0)
    step(0, bias_scr[1] + jnp.where(qi == 0, -jnp.inf, 0.0).astype(F32))

    t_last = jnp.maximum(qi, 1)

    def far_pair(u, c):
        step(1 + 2 * u)
        step(2 + 2 * u)
        return c

    lax.fori_loop(0, (t_last - 1) // 2, far_pair, 0)

    @pl.when((t_last - 1) % 2 == 1)
    def _():
        step(t_last - 1)

    for x in range(ns):
        values(x, t_last)
        if x + 1 < ns:
            softmax(x + 1)

    lam = lam_ref[0, 0]
    half = ns // 2
    outs = []
    for x in range(half):
        o0 = acc_scr[x][0:2 * hd, :] * (1.0 / acc_scr[x][2 * hd:2 * hd + 1, :])
        o1 = acc_scr[half + x][0:2 * hd, :] * (1.0 / acc_scr[half + x][2 * hd:2 * hd + 1, :])
        outs.append(o0 - lam * o1)
    ot = outs[0] if half == 1 else jnp.concatenate(outs, axis=1)
    ms = jnp.mean(ot * ot, axis=0, keepdims=True)
    y = (ot * lax.rsqrt(ms + EPS)).T * (g_ref[...] * out_scale)
    o_ref[0] = y.astype(BF16)


def _diff_attn_bounded_kernel(lam_ref, qt_ref, k_ref, vt_ref, bvec_ref, g_ref, o_ref,
                              bias_scr, *scratch, out_scale):
    blk = ATT_BLK
    hd = DIFF_HEAD_DIM
    ns = ATT_STREAMS
    nq = qt_ref.shape[2]
    qp_scr, p0_scr, p1_scr, acc_scr, l_scr = (scratch[i * ns:(i + 1) * ns] for i in range(5))
    p_scr = (p0_scr, p1_scr)
    w = 2 * blk // ns
    lam = lam_ref[0, 0]
    m_ref = lam_ref[0, 1]
    cols = [slice((x % (ns // 2)) * w, (x % (ns // 2) + 1) * w) for x in range(ns)]

    kk = lax.broadcasted_iota(jnp.int32, (blk, blk), 0)
    qq = lax.broadcasted_iota(jnp.int32, (blk, blk), 1)
    visible = (kk // CHUNK) <= (qq // CHUNK)
    for i in range(2):
        rows = jnp.broadcast_to(bvec_ref[0, i], (blk, 2 * blk))
        tile = pltpu.roll(rows, 0, 1, stride=1, stride_axis=0)[:, :blk]
        bias_scr[i] = jnp.where(visible, tile, -jnp.inf) if i == 0 else tile

    def key_block(qi, t):
        return jnp.maximum(qi - t, 0)

    diag_keys = [(x % (ns // 2) + 1) * w for x in range(ns)]
    assert w % CHUNK == 0

    def probs(qi, x, t, bias=None, keys=blk):
        start = pl.multiple_of(key_block(qi, t) * blk, blk)
        s = _dot(k_ref[0, pl.ds(start, keys), :], qp_scr[x][...])
        if bias is not None:
            s = s + bias[0:keys, cols[x]]
        p = jnp.exp2(s - m_ref)
        l_scr[x][...] += jnp.sum(p.reshape(keys // 8, 8, w), axis=0)
        return p.astype(BF16)

    def step(qi, t, slot, bias=None, keys_t=None):
        for x in range(ns):
            p_scr[slot][x][...] = probs(qi, x, t + 1, bias)
            n = blk if keys_t is None else keys_t[x]
            acc_scr[x][...] += _dot(vt_ref[0, 0, key_block(qi, t), 0:2 * hd, 0:n], p_scr[1 - slot][x][0:n, :])

    def begin(qi):
        qt = qt_ref[0, 0, qi]
        row = lax.broadcasted_iota(jnp.int32, qt.shape, 0)
        zero = jnp.zeros_like(qt)
        q_maps = (jnp.where(row < hd, qt, zero), jnp.where(row >= hd, qt, zero))
        for x in range(ns):
            qp_scr[x][...] = q_maps[x // (ns // 2)][:, cols[x]]
            acc_scr[x][...] = jnp.zeros(acc_scr[x].shape, F32)
            l_scr[x][...] = jnp.zeros(l_scr[x].shape, F32)
        for x in range(ns):
            p0_scr[x][0:diag_keys[x], :] = probs(qi, x, 0, bias_scr[0], diag_keys[x])
        step(qi, 0, 1, bias_scr[1] + jnp.where(qi == 0, -jnp.inf, 0.0).astype(F32), diag_keys)

    def sweep(qi):
        n_far = qi - 1

        def far_steps(n):
            def body(u, c):
                for i in range(n):
                    step(qi, 1 + n * u + i, i % 2)
                return c
            return body

        n_quads = n_far // 4
        lax.fori_loop(0, n_quads, far_steps(4), 0)
        t_done = 1 + 4 * n_quads

        @pl.when(n_far % 4 >= 2)
        def _():
            step(qi, t_done, 0)
            step(qi, t_done + 1, 1)

        @pl.when(n_far % 2 == 1)
        def _():
            step(qi, n_far, 0)

    def finish(qi):
        t_last = jnp.maximum(qi, 1)
        last_in_p1 = t_last % 2 == 1
        for x in range(ns):
            p_last = jnp.where(last_in_p1, p1_scr[x][...], p0_scr[x][...])
            acc_scr[x][...] += _dot(vt_ref[0, 0, key_block(qi, t_last), 0:2 * hd, :], p_last)
        half = ns // 2
        inv = [1.0 / jnp.sum(l_scr[x][...], axis=0, keepdims=True) for x in range(ns)]
        outs = []
        for x in range(half):
            outs.append(acc_scr[x][...] * inv[x] - lam * (acc_scr[half + x][...] * inv[half + x]))
        ot = outs[0] if half == 1 else jnp.concatenate(outs, axis=1)
        ms = jnp.mean(ot * ot, axis=0, keepdims=True)
        y = (ot * lax.rsqrt(ms + EPS)).T * (g_ref[...] * out_scale)
        o_ref[0, pl.ds(pl.multiple_of(qi * blk, blk), blk), :] = y.astype(BF16)

    begin(0)

    def query_block(qi, c):
        finish(qi - 1)
        begin(qi)
        sweep(qi)
        return c

    lax.fori_loop(1, nq, query_block, 0)
    finish(nq - 1)


def _diff_attn(lam, qt, k, vt, bias, g, out_scale, bounded):
    b, nh, nq = qt.shape[0], qt.shape[1], qt.shape[2]
    s = k.shape[1]
    blk = ATT_BLK
    ns = ATT_STREAMS
    w = 2 * blk // ns
    hw = 2 * DIFF_HEAD_DIM
    if bounded:
        body = _diff_attn_bounded_kernel
        per_stream = (((hw, w), BF16), ((blk, w), BF16), ((blk, w), BF16), ((hw, w), F32), ((8, w), F32))
        grid = (b, nh)
        q_spec = pl.BlockSpec((1, 1, nq, hw, blk), lambda bi, hi: (bi, hi, 0, 0, 0))
        o_spec = pl.BlockSpec((1, s, hw), lambda bi, hi: (bi, 0, hi))
        sem = ("parallel", "parallel")
    else:
        body = _diff_attn_kernel
        per_stream = (((hw, w), BF16), ((blk, w), F32), ((8, w), F32), ((blk, w), BF16),
                      ((1, w), F32), ((1, w), F32), ((ATT_V_ROWS, w), F32))
        grid = (b, nh, nq)
        q_spec = pl.BlockSpec((1, 1, 1, hw, blk), lambda bi, hi, qi: (bi, hi, qi, 0, 0))
        o_spec = pl.BlockSpec((1, blk, hw), lambda bi, hi, qi: (bi, qi, hi))
        sem = ("parallel", "parallel", "arbitrary")
    return pl.pallas_call(
        functools.partial(body, out_scale=out_scale),
        grid=grid,
        in_specs=[
            pl.BlockSpec(memory_space=pltpu.SMEM),
            q_spec,
            pl.BlockSpec((1, s, hw), lambda bi, hi, *_: (bi, 0, hi)),
            pl.BlockSpec((1, 1, nq, ATT_V_ROWS, blk), lambda bi, hi, *_: (bi, hi, 0, 0, 0)),
            pl.BlockSpec((1, 2, 1, 2 * blk), lambda bi, hi, *_: (hi, 0, 0, 0)),
            _const_spec((1, hw)),
        ],
        out_specs=o_spec,
        out_shape=jax.ShapeDtypeStruct((b, s, nh * hw), BF16),
        scratch_shapes=[pltpu.VMEM((2, blk, blk), F32)] + [pltpu.VMEM(shape, dtype)
                                                        for shape, dtype in per_stream for _ in range(ns)],
        compiler_params=_cparams(*sem),
        name="diff_attn_bounded" if bounded else "diff_attn",
    )(lam, qt, k, vt, bias, g)


def _pre_gla_kernel(x_ref, g_ref, w_ref, gsum_ref, mg_ref, gw_ref, gb_ref, tri_ref,
                    q_ref, k_ref, v_ref, r_ref, gc_ref, mq_ref):
    kw = GLA_HEADS * GLA_KP
    vw = GLA_HEADS * GLA_VP
    h = _rms_rows(x_ref[...], g_ref[...]).astype(BF16)
    q_ref[...] = _dot(h, w_ref[:, 0:kw]).astype(BF16)
    k_ref[...] = _dot(h, w_ref[:, kw:2 * kw]).astype(BF16)
    o = 2 * kw
    v_ref[...] = _dot(h, w_ref[:, o:o + vw]).astype(BF16)
    r_ref[...] = _dot(h, w_ref[:, o + vw:o + 2 * vw]).astype(BF16)
    o = o + 2 * vw
    gate_low = _dot(h, w_ref[:, o:o + LANES]).astype(BF16)
    mq_ref[...] = _group_rms(_dot(h, w_ref[:, o + LANES:]), gsum_ref[...], mg_ref[...]).astype(BF16)
    z = _dot(gate_low, gw_ref[...]) + gb_ref[...]
    log_a = (jnp.minimum(z, 0.0) - jnp.log1p(jnp.exp(-jnp.abs(z)))) * (1.0 / GLA_GATE_TAU)
    hi = log_a.astype(BF16)
    rem = log_a - hi.astype(F32)
    mid = rem.astype(BF16)
    lo = (rem - mid.astype(F32)).astype(BF16)
    tri = tri_ref[...]
    n = tri.shape[0]
    for c in range(log_a.shape[0] // n):
        rows = slice(c * n, (c + 1) * n)
        gc_ref[rows, :] = _dot(tri, hi[rows]) + _dot(tri, mid[rows]) + _dot(tri, lo[rows])


def _pre_gla(x2, g, w, gsum, mg, gw, gb, tri):
    t = x2.shape[0]
    kw = GLA_HEADS * GLA_KP
    vw = GLA_HEADS * GLA_VP
    row = lambda n: pl.BlockSpec((ROW_TILE, n), lambda i: (i, 0))
    return pl.pallas_call(
        _pre_gla_kernel,
        grid=(t // ROW_TILE,),
        in_specs=[row(D_MODEL), _const_spec((1, D_MODEL)), _const_spec(w.shape),
                  _const_spec(gsum.shape), _const_spec((1, MEM_WIDTH)), _const_spec(gw.shape),
                  _const_spec(gb.shape), _const_spec(tri.shape)],
        out_specs=[row(kw), row(kw), row(vw), row(vw), row(kw), row(MEM_WIDTH)],
        out_shape=[jax.ShapeDtypeStruct((t, kw), BF16), jax.ShapeDtypeStruct((t, kw), BF16),
                   jax.ShapeDtypeStruct((t, vw), BF16), jax.ShapeDtypeStruct((t, vw), BF16),
                   jax.ShapeDtypeStruct((t, kw), F32), jax.ShapeDtypeStruct((t, MEM_WIDTH), BF16)],
        compiler_params=_cparams("parallel"),
        name="pre_gla",
    )(x2, g, w, gsum, mg, gw, gb, tri)


def _gla_kernel(q_ref, k_ref, v_ref, r_ref, gc_ref, gain_ref, o_ref, s_scr):
    tg = GLA_TILE
    nchunk = tg // CHUNK
    heads = range(GLA_HEADS)
    ks = [slice(h * GLA_KP, (h + 1) * GLA_KP) for h in heads]
    vs = [slice(h * GLA_VP, (h + 1) * GLA_VP) for h in heads]

    @pl.when(pl.program_id(1) == 0)
    def _():
        s_scr[...] = jnp.zeros(s_scr.shape, F32)

    ri = lax.broadcasted_iota(jnp.int32, (tg, tg), 0)
    ci = lax.broadcasted_iota(jnp.int32, (tg, tg), 1)
    same_chunk = (ri // CHUNK) == (ci // CHUNK)
    past = ci <= ri
    row_chunk = lax.broadcasted_iota(jnp.int32, (tg, GLA_KP), 0) // CHUNK

    qe, scores, kv, decay = [], [], [], []
    for h in heads:
        qh = q_ref[0, :, ks[h]].astype(F32) * (GLA_K_DIM ** -0.5)
        kh = k_ref[0, :, ks[h]].astype(F32)
        g = gc_ref[0, :, ks[h]]
        eg = jnp.exp(g)
        ieg = jnp.exp(-g)
        qe.append((qh * eg).astype(BF16))
        a_past = _dot_nt(qe[h], (kh * ieg).astype(BF16))
        a_fut = _dot_nt((qh * ieg).astype(BF16), (kh * eg).astype(BF16))
        scores.append(jnp.where(same_chunk, jnp.where(past, a_past, a_fut), 0.0).astype(BF16))
        vt = v_ref[0, :, vs[h]].astype(F32).T.astype(BF16)
        g_last = [g[c * CHUNK + CHUNK - 1:c * CHUNK + CHUNK, :] for c in range(nchunk)]
        g_end = jnp.concatenate([jnp.broadcast_to(gl, (CHUNK, GLA_KP)) for gl in g_last], axis=0)
        kdec = kh * jnp.exp(g_end - g)
        kv.append([_dot(vt, jnp.where(row_chunk == c, kdec, 0.0).astype(BF16)) for c in range(nchunk)])
        decay.append([jnp.exp(gl) for gl in g_last])

    starts = []
    for h in heads:
        st = s_scr[h]
        per_chunk = []
        for c in range(nchunk):
            per_chunk.append(st.astype(BF16))
            st = st * decay[h][c] + kv[h][c]
        s_scr[h] = st
        starts.append(per_chunk)

    for h in heads:
        inter = [_dot_nt(qe[h][c * CHUNK:(c + 1) * CHUNK], starts[h][c]) for c in range(nchunk)]
        o = _dot(scores[h], v_ref[0, :, vs[h]]) + jnp.concatenate(inter, axis=0)
        ms = jnp.sum(o * o, axis=-1, keepdims=True) * (1.0 / GLA_V_DIM)
        y = o * lax.rsqrt(ms + EPS) * gain_ref[:, vs[h]]
        rh = r_ref[0, :, vs[h]].astype(F32)
        o_ref[0, :, vs[h]] = (y * (rh / (1.0 + jnp.exp(-rh)))).astype(BF16)


def _gla(q, k, v, r, gc, gain):
    b, s = q.shape[0], q.shape[1]
    kw = GLA_HEADS * GLA_KP
    vw = GLA_HEADS * GLA_VP
    spec = lambda n: pl.BlockSpec((1, GLA_TILE, n), lambda bi, i: (bi, i, 0))
    return pl.pallas_call(
        _gla_kernel,
        grid=(b, s // GLA_TILE),
        in_specs=[spec(kw), spec(kw), spec(vw), spec(vw), spec(kw), _const_spec((1, vw))],
        out_specs=spec(vw),
        out_shape=jax.ShapeDtypeStruct((b, s, vw), BF16),
        scratch_shapes=[pltpu.VMEM((GLA_HEADS, GLA_VP, GLA_KP), F32)],
        compiler_params=_cparams("parallel", "arbitrary"),
        name="gla",
    )(q, k, v, r, gc, gain)


def _mem_kv_kernel(mem_ref, g_ref, w_ref, gsum_ref, kg_ref, k_ref, v_ref):
    h = _rms_rows(mem_ref[...], g_ref[...]).astype(BF16)
    k_ref[...] = _group_rms(_dot(h, w_ref[:, :MEM_WIDTH]), gsum_ref[...], kg_ref[...]).astype(BF16)
    v_ref[...] = _dot(h, w_ref[:, MEM_WIDTH:]).astype(BF16)


def _mem_kv(mem2, g, w_all, layer, gsum, kg):
    n = mem2.shape[0]
    return pl.pallas_call(
        _mem_kv_kernel,
        grid=(1,),
        in_specs=[_const_spec(mem2.shape), _const_spec((1, D_MODEL)), _layer_spec(w_all, layer),
                  _const_spec(gsum.shape), _const_spec((1, MEM_WIDTH))],
        out_specs=[_const_spec((n, MEM_WIDTH)), _const_spec((n, MEM_WIDTH))],
        out_shape=[jax.ShapeDtypeStruct((n, MEM_WIDTH), BF16)] * 2,
        compiler_params=_cparams("arbitrary"),
        name="mem_kv",
    )(mem2, g, w_all, gsum, kg)


def _mix_out_kernel(x_ref, mix_ref, mq_ref, kbd_ref, vbd_ref, wa_ref, wb_ref, o_ref):
    m = kbd_ref.shape[2] // MEM_HEADS
    logits = _dot(mq_ref[0], kbd_ref[0])
    ps = []
    for h in range(MEM_HEADS):
        s = logits[:, h * m:(h + 1) * m]
        e = jnp.exp(s - jnp.max(s, axis=-1, keepdims=True))
        ps.append((e * (1.0 / jnp.sum(e, axis=-1, keepdims=True))).astype(BF16))
    cross = _dot(jnp.concatenate(ps, axis=1), vbd_ref[0])
    o_ref[0] = x_ref[0] + _dot(mix_ref[0], wa_ref[...]) + _dot(cross.astype(BF16), wb_ref[...])


def _mix_out(x, mix, mq, kbd, vbd, wa, wa_spec, wb, wb_spec):
    b, s = x.shape[0], x.shape[1]
    spec = lambda n: pl.BlockSpec((1, ROW_TILE, n), lambda bi, i: (bi, i, 0))
    per_b = lambda a: pl.BlockSpec((1,) + a.shape[1:], lambda bi, i: (bi, 0, 0))
    return pl.pallas_call(
        _mix_out_kernel,
        grid=(b, s // ROW_TILE),
        in_specs=[spec(D_MODEL), spec(mix.shape[2]), spec(MEM_WIDTH), per_b(kbd), per_b(vbd),
                  wa_spec, wb_spec],
        out_specs=spec(D_MODEL),
        out_shape=jax.ShapeDtypeStruct(x.shape, F32),
        compiler_params=_cparams("parallel", "parallel"),
        name="mix_out",
    )(x, mix, mq, kbd, vbd, wa, wb)


def _ffn_kernel(x_ref, g_ref, wu_ref, cw_ref, cb_ref, wd_ref, o_ref, carry_scr, act_scr, *shift_scr):
    tm = FFN_TILE
    cr = CARRY_ROWS

    @pl.when(pl.program_id(1) == 0)
    def _():
        carry_scr[...] = jnp.zeros(carry_scr.shape, F32)

    x = x_ref[0]
    h = _rms_rows(x, g_ref[...]).astype(BF16)

    def conv(cols, bufs):
        u = _dot(h, wu_ref[:, cols])
        prev = carry_scr[:, cols]
        for shift, buf in zip((1, 2), bufs):
            buf[shift:shift + cr, :] = prev
            buf[cr + shift:cr + shift + tm, :] = u
        carry_scr[:, cols] = u[tm - cr:tm]
        return (cw_ref[0:1, cols] * bufs[1][cr:cr + tm, :] + cw_ref[1:2, cols] * bufs[0][cr:cr + tm, :]
                + cw_ref[2:3, cols] * u + cb_ref[:, cols])

    for j in range(D_FF // FFN_COLS):
        bufs = shift_scr[4 * (j % 2):4 * (j % 2) + 4]
        a = conv(slice(j * FFN_COLS, (j + 1) * FFN_COLS), bufs[0:2])
        half_g = 0.5 * conv(slice(D_FF + j * FFN_COLS, D_FF + (j + 1) * FFN_COLS), bufs[2:4])
        act_scr[:, j * FFN_COLS:(j + 1) * FFN_COLS] = (a * half_g * (1.0 + jnp.tanh(half_g))).astype(BF16)

    o_ref[0] = x + _dot(act_scr[...], wd_ref[...])


def _ffn(x, g, wu_all, cw, cb, wd_all, layer):
    b, s = x.shape[0], x.shape[1]
    spec = pl.BlockSpec((1, FFN_TILE, D_MODEL), lambda bi, i: (bi, i, 0))
    return pl.pallas_call(
        _ffn_kernel,
        grid=(b, s // FFN_TILE),
        in_specs=[spec, _const_spec((1, D_MODEL)), _layer_spec(wu_all, layer, single=True), _const_spec(cw.shape),
                  _const_spec(cb.shape), _layer_spec(wd_all, layer, single=True)],
        out_specs=spec,
        out_shape=jax.ShapeDtypeStruct(x.shape, F32),
        scratch_shapes=[pltpu.VMEM((CARRY_ROWS, 2 * D_FF), F32), pltpu.VMEM((FFN_TILE, D_FF), BF16)]
        + [pltpu.VMEM((FFN_TILE + 2 * CARRY_ROWS, FFN_COLS), F32)] * 8,
        compiler_params=_cparams("parallel", "arbitrary"),
        name="ffn",
    )(x, g, wu_all, cw, cb, wd_all)


def _t5_bucket(rel):
    half = REL_BUCKETS // 2
    max_exact = half // 2
    ret = jnp.where(rel > 0, half, 0)
    n = jnp.abs(rel)
    nf = jnp.maximum(n, 1).astype(jnp.float32)
    large = max_exact + (jnp.log(nf / max_exact) / math.log(REL_MAX_DIST / max_exact)
                         * (half - max_exact)).astype(jnp.int32)
    large = jnp.minimum(large, half - 1)
    return ret + jnp.where(n < max_exact, n, large)


def _bias_vectors(rel_bias):
    blk = ATT_BLK
    assert blk >= REL_MAX_DIST
    table = rel_bias.astype(F32).T[:, :, None]

    def lookup(rel):
        bucket = _t5_bucket(rel)
        out = jnp.zeros((table.shape[0],) + rel.shape, F32)
        for i in range(REL_BUCKETS):
            out = jnp.where(bucket == i, table[:, i], out)
        return out

    far = lookup(jnp.full((1,), -2 * blk))
    j = jnp.arange(2 * blk)
    dist = jnp.where(j < blk, -j, 2 * blk - j)
    diag = (lookup(dist) - far) * LOG2E
    near = (lookup(dist - blk) - far) * LOG2E
    return jnp.stack([diag, near], axis=1)[:, :, None, :]


def _group_sum_matrix():
    i = jnp.arange(MXU_DIM)
    return ((i[:, None] // 64) == (i[None, :] // 64)).astype(BF16)


def _chunk_tri_matrix(n):
    i = jnp.arange(n)
    return (((i[:, None] // CHUNK) == (i[None, :] // CHUNK)) & (i[None, :] <= i[:, None])).astype(BF16)


def _pad_heads(w, heads, dim, pad, axis):
    shape = list(w.shape)
    shape[axis:axis + 1] = [heads, dim]
    w = w.reshape(shape)
    widths = [(0, 0)] * w.ndim
    widths[axis + 1] = (0, pad - dim)
    w = jnp.pad(w, widths)
    shape[axis:axis + 2] = [heads * pad]
    return w.reshape(shape)


def _tile_gain(g, reps, scale=1.0):
    return (jnp.tile(g.astype(F32), reps) * scale)[None, :]


def _mem_block_diag(kn, v, b):
    m = kn.shape[0] // b
    eye = jnp.eye(MEM_HEADS, dtype=BF16)
    knt = kn.reshape(b, m, MEM_WIDTH).transpose(0, 2, 1)
    kbd = (knt.reshape(b, MEM_HEADS, MEM_HEAD_DIM, 1, m) * eye.reshape(1, MEM_HEADS, 1, MEM_HEADS, 1))
    kbd = kbd.reshape(b, MEM_WIDTH, MEM_HEADS * m)
    vbd = (v.reshape(b, 1, m, MEM_HEADS, MEM_HEAD_DIM) * eye.reshape(1, MEM_HEADS, 1, MEM_HEADS, 1))
    vbd = vbd.reshape(b, MEM_HEADS * m, MEM_WIDTH)
    return kbd, vbd


def kernel(x, mem, rel_bias, attn_norm, ffn_norm, mem_norm, w_in_diff, diff_qk_norm, diff_lambda,
           diff_out_norm, w_in_gla, gla_gate_w, gla_gate_b, gla_out_norm, w_mem_kv, mem_qk_norm,
           w_out, w_up, conv_w, conv_b, w_down):
    b, s, d = x.shape
    t = b * s
    tw = TOKEN_WIDTH
    gsum = _group_sum_matrix()
    mem2 = mem.reshape(b * mem.shape[1], d)
    x = x.astype(F32)
    w_in_diff, w_mem_kv, w_out, w_up, w_down = map(_to_bf16, (w_in_diff, w_mem_kv, w_out, w_up, w_down))

    for i in range(DEPTH):
        j = i // 2
        x2 = x.reshape(t, d)
        mq_gain = _tile_gain(mem_qk_norm[i, 0], MEM_HEADS, MEM_HEAD_DIM ** -0.5)
        if i % 2 == 0:
            nh, hd = DIFF_HEADS, DIFF_HEAD_DIM
            qt, k, vt, mq = _pre_diff(
                x2, attn_norm[i][None, :], w_in_diff, j, gsum,
                _tile_gain(diff_qk_norm[j, 0], 2 * nh, hd ** -0.5 * LOG2E),
                _tile_gain(diff_qk_norm[j, 1], 2 * nh), mq_gain, b)
            lv = diff_lambda[j].astype(F32)
            lam_init = 0.8 - 0.6 * math.exp(-0.3 * i)
            lam = jnp.exp(jnp.sum(lv[0] * lv[1])) - jnp.exp(jnp.sum(lv[2] * lv[3])) + lam_init
            bvec = _bias_vectors(rel_bias)
            qk_bound = (hd ** 0.5 * LOG2E * ATT_ROUNDING_SLACK
                        * jnp.max(jnp.abs(diff_qk_norm[j, 0])) * jnp.max(jnp.abs(diff_qk_norm[j, 1]))).astype(F32)
            hi = qk_bound + jnp.maximum(jnp.max(bvec), 0.0)
            lo = -qk_bound + jnp.minimum(jnp.min(bvec), 0.0)
            scalars = jnp.stack([lam.astype(F32), hi]).reshape(1, 2)
            attend = lambda bounded: functools.partial(
                _diff_attn, qt=qt, k=k.reshape(b, s, tw), vt=vt, bias=bvec,
                g=diff_out_norm[j].astype(F32)[None, :], out_scale=1.0 - lam_init, bounded=bounded)
            mix = lax.cond(hi - lo <= ATT_MAX_EXP2_SPAN, attend(True), attend(False), scalars)
            w_mix, w_mix_spec = w_out, _layer_spec(w_out, i, rows=(0, tw))
        else:
            kw = GLA_HEADS * GLA_K_DIM
            w = w_in_gla[j]
            hp = functools.partial(_pad_heads, heads=GLA_HEADS, axis=1)
            w_p = jnp.concatenate([
                hp(w[:, :kw], dim=GLA_K_DIM, pad=GLA_KP),
                hp(w[:, kw:2 * kw], dim=GLA_K_DIM, pad=GLA_KP),
                hp(w[:, 2 * kw:2 * kw + tw], dim=GLA_V_DIM, pad=GLA_VP),
                hp(w[:, 2 * kw + tw:2 * kw + 2 * tw], dim=GLA_V_DIM, pad=GLA_VP),
                jnp.pad(w[:, 2 * kw + 2 * tw:2 * kw + 2 * tw + GLA_GATE_RANK],
                        ((0, 0), (0, LANES - GLA_GATE_RANK))),
                w[:, 2 * kw + 2 * tw + GLA_GATE_RANK:]], axis=1).astype(BF16)
            gw = jnp.pad(hp(gla_gate_w[j], dim=GLA_K_DIM, pad=GLA_KP),
                         ((0, LANES - GLA_GATE_RANK), (0, 0))).astype(BF16)
            gb = _pad_heads(gla_gate_b[j].astype(F32)[None, :], GLA_HEADS, GLA_K_DIM, GLA_KP, 1)
            q, k, v, r, gc, mq = _pre_gla(x2, attn_norm[i][None, :], w_p, gsum, mq_gain, gw, gb,
                                          _chunk_tri_matrix(MXU_DIM))
            gain = _pad_heads(jnp.tile(gla_out_norm[j].astype(F32), GLA_HEADS)[None, :],
                              GLA_HEADS, GLA_V_DIM, GLA_VP, 1)
            sh = lambda a: a.reshape(b, s, a.shape[1])
            mix = _gla(sh(q), sh(k), sh(v), sh(r), sh(gc), gain)
            w_mix = _pad_heads(w_out[i, :tw], GLA_HEADS, GLA_V_DIM, GLA_VP, 0)
            w_mix_spec = _const_spec(w_mix.shape)

        kn, vm = _mem_kv(mem2, mem_norm[i][None, :], w_mem_kv, i, gsum,
                         _tile_gain(mem_qk_norm[i, 1], MEM_HEADS))
        kbd, vbd = _mem_block_diag(kn, vm, b)
        x = _mix_out(x, mix.reshape(b, s, -1), mq.reshape(b, s, MEM_WIDTH), kbd, vbd,
                     w_mix, w_mix_spec, w_out, _layer_spec(w_out, i, rows=(tw, MEM_WIDTH)))
        x = _ffn(x, ffn_norm[i][None, :], w_up, conv_w[i].astype(F32), conv_b[i].astype(F32)[None, :], w_down, i)
    return x
```

```python
import functools
import math

import jax
import jax.numpy as jnp
from jax import lax
from jax.experimental import pallas as pl
from jax.experimental.pallas import tpu as pltpu

F32 = jnp.float32
BF16 = jnp.bfloat16

D_MODEL = 1024
DEPTH = 2
CHUNK = 64
MEM_WIDTH = D_MODEL // 4
MEM_HEADS = 4
MEM_HEAD_DIM = MEM_WIDTH // MEM_HEADS
TOKEN_WIDTH = D_MODEL - MEM_WIDTH
DIFF_HEAD_DIM = 64
DIFF_HEADS = TOKEN_WIDTH // (2 * DIFF_HEAD_DIM)
GLA_HEADS = 4
GLA_V_DIM = TOKEN_WIDTH // GLA_HEADS
GLA_K_DIM = GLA_V_DIM // 2
GLA_GATE_RANK = 16
GLA_GATE_TAU = 16.0
REL_BUCKETS = 32
REL_MAX_DIST = 128
D_FF = ((8 * D_MODEL // 3 + 127) // 128) * 128
EPS = 1e-6
LOG2E = math.log2(math.e)

LANES = 128
MXU_DIM = 256
VMEM_LIMIT_BYTES = 56 * 1024 * 1024

ROW_TILE = 512
ATT_BLK = 512
ATT_STREAMS = 4
ATT_MAX_EXP2_SPAN = 100.0
ATT_ROUNDING_SLACK = 1.02
ATT_V_ROWS = 2 * DIFF_HEAD_DIM + 16
GLA_TILE = 256
FFN_TILE = 512
FFN_COLS = 256
GLA_KP = 128
GLA_VP = 256
CARRY_ROWS = 8
CAST_BLOCK_BYTES = 4 * 1024 * 1024


def _cparams(*sem):
    return pltpu.CompilerParams(dimension_semantics=sem, vmem_limit_bytes=VMEM_LIMIT_BYTES)


def _const_spec(shape):
    n = len(shape)
    return pl.BlockSpec(shape, lambda *_: (0,) * n)


def _layer_spec(w_all, layer, rows=None, single=False):
    first, n_rows = (0, w_all.shape[1]) if rows is None else rows
    assert first % n_rows == 0
    mode = dict(pipeline_mode=pl.Buffered(1)) if single else {}
    return pl.BlockSpec((None, n_rows, w_all.shape[2]), lambda *_: (layer, first // n_rows, 0), **mode)


def _cast_kernel(w_ref, o_ref):
    o_ref[...] = w_ref[...].astype(BF16)


def _to_bf16(w):
    n, r, c = w.shape
    rows = max(rb for rb in range(16, r + 1, 16) if r % rb == 0 and rb * c * 4 <= CAST_BLOCK_BYTES)
    spec = pl.BlockSpec((1, rows, c), lambda a, i: (a, i, 0))
    return pl.pallas_call(
        _cast_kernel, grid=(n, r // rows), in_specs=[spec], out_specs=spec,
        out_shape=jax.ShapeDtypeStruct(w.shape, BF16),
        compiler_params=_cparams("parallel", "parallel"), name="to_bf16",
    )(w.astype(F32))


def _rms_rows(x, g):
    ms = jnp.mean(x * x, axis=-1, keepdims=True)
    return x * lax.rsqrt(ms + EPS) * g


def _group_rms(t, gsum, gain):
    cols = []
    for c in range(t.shape[1] // MXU_DIM):
        blk = t[:, c * MXU_DIM:(c + 1) * MXU_DIM]
        ss = jnp.dot((blk * blk).astype(BF16), gsum, preferred_element_type=F32)
        cols.append(blk * lax.rsqrt(ss * (1.0 / 64) + EPS))
    out = cols[0] if len(cols) == 1 else jnp.concatenate(cols, axis=1)
    return out * gain


def _dot(a, b):
    return jnp.dot(a, b, preferred_element_type=F32)


def _dot_nt(a, b):
    return lax.dot_general(a, b, (((1,), (1,)), ((), ())), preferred_element_type=F32)


def _pre_diff_kernel(x_ref, g_ref, w_ref, gsum_ref, qg_ref, kg_ref, mg_ref,
                     qt_ref, k_ref, vt_ref, mq_ref):
    tw = TOKEN_WIDTH
    hw = 2 * DIFF_HEAD_DIM
    h = _rms_rows(x_ref[...], g_ref[...]).astype(BF16)
    gsum = gsum_ref[...]
    q = _group_rms(_dot(h, w_ref[:, 0:tw]), gsum, qg_ref[...])
    k_ref[...] = _group_rms(_dot(h, w_ref[:, tw:2 * tw]), gsum, kg_ref[...]).astype(BF16)
    v = _dot(h, w_ref[:, 2 * tw:3 * tw])
    mq_ref[...] = _group_rms(_dot(h, w_ref[:, 3 * tw:]), gsum, mg_ref[...]).astype(BF16)
    ones = jnp.ones((ATT_V_ROWS - hw, ROW_TILE), BF16)
    for n in range(DIFF_HEADS):
        qt_ref[0, n, 0] = q[:, n * hw:(n + 1) * hw].T.astype(BF16)
        vt_ref[0, n, 0, 0:hw, :] = v[:, n * hw:(n + 1) * hw].T.astype(BF16)
        vt_ref[0, n, 0, hw:, :] = ones


def _pre_diff(x2, g, w_all, layer, gsum, qg, kg, mg, b):
    t = x2.shape[0]
    tw = TOKEN_WIDTH
    hw = 2 * DIFF_HEAD_DIM
    assert ROW_TILE == ATT_BLK
    nq = t // b // ATT_BLK
    row = lambda n: pl.BlockSpec((ROW_TILE, n), lambda i: (i, 0))
    per_head = lambda r: pl.BlockSpec((1, DIFF_HEADS, 1, r, ATT_BLK), lambda i: (i // nq, 0, i % nq, 0, 0))
    return pl.pallas_call(
        _pre_diff_kernel,
        grid=(t // ROW_TILE,),
        in_specs=[row(D_MODEL), _const_spec((1, D_MODEL)), _layer_spec(w_all, layer),
                  _const_spec(gsum.shape), _const_spec((1, tw)), _const_spec((1, tw)),
                  _const_spec((1, MEM_WIDTH))],
        out_specs=[per_head(hw), row(tw), per_head(ATT_V_ROWS), row(MEM_WIDTH)],
        out_shape=[jax.ShapeDtypeStruct((b, DIFF_HEADS, nq, hw, ATT_BLK), BF16),
                   jax.ShapeDtypeStruct((t, tw), BF16),
                   jax.ShapeDtypeStruct((b, DIFF_HEADS, nq, ATT_V_ROWS, ATT_BLK), BF16),
                   jax.ShapeDtypeStruct((t, MEM_WIDTH), BF16)],
        compiler_params=_cparams("parallel"),
        name="pre_diff",
    )(x2, g, w_all, gsum, qg, kg, mg)


def _diff_attn_kernel(lam_ref, qt_ref, k_ref, vt_ref, bvec_ref, g_ref, o_ref,
                      bias_scr, *scratch, out_scale):
    blk = ATT_BLK
    hd = DIFF_HEAD_DIM
    ns = ATT_STREAMS
    qp_scr, s_scr, cm_scr, p_scr, a_scr, m_scr, acc_scr = (scratch[i * ns:(i + 1) * ns] for i in range(7))
    w = 2 * blk // ns
    qi = pl.program_id(2)
    qt = qt_ref[0, 0, 0]
    row = lax.broadcasted_iota(jnp.int32, qt.shape, 0)
    zero = jnp.zeros_like(qt)
    q_maps = (jnp.where(row < hd, qt, zero), jnp.where(row >= hd, qt, zero))
    cols = [slice((x % (ns // 2)) * w, (x % (ns // 2) + 1) * w) for x in range(ns)]
    for x in range(ns):
        qp_scr[x][...] = q_maps[x // (ns // 2)][:, cols[x]]
        m_scr[x][...] = jnp.full(m_scr[x].shape, -jnp.inf, F32)
        acc_scr[x][...] = jnp.zeros(acc_scr[x].shape, F32)

    @pl.when(qi == 0)
    def _():
        kk = lax.broadcasted_iota(jnp.int32, (blk, blk), 0)
        qq = lax.broadcasted_iota(jnp.int32, (blk, blk), 1)
        visible = (kk // CHUNK) <= (qq // CHUNK)
        for i in range(2):
            rows = jnp.broadcast_to(bvec_ref[0, i], (blk, 2 * blk))
            tile = pltpu.roll(rows, 0, 1, stride=1, stride_axis=0)[:, :blk]
            bias_scr[i] = jnp.where(visible, tile, -jnp.inf) if i == 0 else tile

    def key_block(t):
        return jnp.maximum(qi - t, 0)

    def logits(x, t, bias=None):
        start = pl.multiple_of(key_block(t) * blk, blk)
        s = _dot(k_ref[0, pl.ds(start, blk), :], qp_scr[x][...])
        if bias is not None:
            s = s + bias[:, cols[x]]
        s_scr[x][...] = s
        part = s[0:8]
        for r in range(8, blk, 8):
            part = jnp.maximum(part, s[r:r + 8])
        cm_scr[x][...] = part

    def softmax(x):
        m_old = m_scr[x][...]
        m_new = jnp.maximum(m_old, jnp.max(cm_scr[x][...], axis=0, keepdims=True))
        a_scr[x][...] = jnp.exp2(m_old - m_new)
        m_scr[x][...] = m_new
        for r in range(0, blk, 16):
            p_scr[x][r:r + 16, :] = jnp.exp2((s_scr[x][r:r + 16, :] - m_new).astype(BF16))

    def values(x, t):
        acc_scr[x][...] = a_scr[x][...] * acc_scr[x][...] + _dot(vt_ref[0, 0, key_block(t)], p_scr[x][...])

    def step(t, bias=None):
        for x in range(ns):
            logits(x, t + 1, bias)
            values(x, t)
            softmax((x + 1) % ns)

    for x in range(ns):
        logits(x, 0, bias_scr[0])
    softmax(0)
    step(0, bias_scr[1] + jnp.where(qi == 0, -jnp.inf, 0.0).astype(F32))

    t_last = jnp.maximum(qi, 1)

    def far_pair(u, c):
        step(1 + 2 * u)
        step(2 + 2 * u)
        return c

    lax.fori_loop(0, (t_last - 1) // 2, far_pair, 0)

    @pl.when((t_last - 1) % 2 == 1)
    def _():
        step(t_last - 1)

    for x in range(ns):
        values(x, t_last)
        if x + 1 < ns:
            softmax(x + 1)

    lam = lam_ref[0, 0]
    half = ns // 2
    outs = []
    for x in range(half):
        o0 = acc_scr[x][0:2 * hd, :] * (1.0 / acc_scr[x][2 * hd:2 * hd + 1, :])
        o1 = acc_scr[half + x][0:2 * hd, :] * (1.0 / acc_scr[half + x][2 * hd:2 * hd + 1, :])
        outs.append(o0 - lam * o1)
    ot = outs[0] if half == 1 else jnp.concatenate(outs, axis=1)
    ms = jnp.mean(ot * ot, axis=0, keepdims=True)
    y = (ot * lax.rsqrt(ms + EPS)).T * (g_ref[...] * out_scale)
    o_ref[0] = y.astype(BF16)


def _diff_attn_bounded_kernel(lam_ref, qt_ref, k_ref, vt_ref, bvec_ref, g_ref, o_ref,
                              bias_scr, *scratch, out_scale):
    blk = ATT_BLK
    hd = DIFF_HEAD_DIM
    ns = ATT_STREAMS
    nq = qt_ref.shape[2]
    qp_scr, p0_scr, p1_scr, acc_scr, l_scr = (scratch[i * ns:(i + 1) * ns] for i in range(5))
    p_scr = (p0_scr, p1_scr)
    w = 2 * blk // ns
    lam = lam_ref[0, 0]
    m_ref = lam_ref[0, 1]
    cols = [slice((x % (ns // 2)) * w, (x % (ns // 2) + 1) * w) for x in range(ns)]

    kk = lax.broadcasted_iota(jnp.int32, (blk, blk), 0)
    qq = lax.broadcasted_iota(jnp.int32, (blk, blk), 1)
    visible = (kk // CHUNK) <= (qq // CHUNK)
    for i in range(2):
        rows = jnp.broadcast_to(bvec_ref[0, i], (blk, 2 * blk))
        tile = pltpu.roll(rows, 0, 1, stride=1, stride_axis=0)[:, :blk]
        bias_scr[i] = jnp.where(visible, tile, -jnp.inf) if i == 0 else tile

    def key_block(qi, t):
        return jnp.maximum(qi - t, 0)

    diag_keys = [(x % (ns // 2) + 1) * w for x in range(ns)]
    assert w % CHUNK == 0

    def probs(qi, x, t, bias=None, keys=blk):
        start = pl.multiple_of(key_block(qi, t) * blk, blk)
        s = _dot(k_ref[0, pl.ds(start, keys), :], qp_scr[x][...])
        if bias is not None:
            s = s + bias[0:keys, cols[x]]
        p = jnp.exp2(s - m_ref)
        l_scr[x][...] += jnp.sum(p.reshape(keys // 8, 8, w), axis=0)
        return p.astype(BF16)

    def step(qi, t, slot, bias=None, keys_t=None):
        for x in range(ns):
            p_scr[slot][x][...] = probs(qi, x, t + 1, bias)
            n = blk if keys_t is None else keys_t[x]
            acc_scr[x][...] += _dot(vt_ref[0, 0, key_block(qi, t), 0:2 * hd, 0:n], p_scr[1 - slot][x][0:n, :])

    def begin(qi):
        qt = qt_ref[0, 0, qi]
        row = lax.broadcasted_iota(jnp.int32, qt.shape, 0)
        zero = jnp.zeros_like(qt)
        q_maps = (jnp.where(row < hd, qt, zero), jnp.where(row >= hd, qt, zero))
        for x in range(ns):
            qp_scr[x][...] = q_maps[x // (ns // 2)][:, cols[x]]
            acc_scr[x][...] = jnp.zeros(acc_scr[x].shape, F32)
            l_scr[x][...] = jnp.zeros(l_scr[x].shape, F32)
        for x in range(ns):
            p0_scr[x][0:diag_keys[x], :] = probs(qi, x, 0, bias_scr[0], diag_keys[x])
        step(qi, 0, 1, bias_scr[1] + jnp.where(qi == 0, -jnp.inf, 0.0).astype(F32), diag_keys)

    def sweep(qi):
        n_far = qi - 1

        def far_steps(n):
            def body(u, c):
                for i in range(n):
                    step(qi, 1 + n * u + i, i % 2)
                return c
            return body

        n_quads = n_far // 4
        lax.fori_loop(0, n_quads, far_steps(4), 0)
        t_done = 1 + 4 * n_quads

        @pl.when(n_far % 4 >= 2)
        def _():
            step(qi, t_done, 0)
            step(qi, t_done + 1, 1)

        @pl.when(n_far % 2 == 1)
        def _():
            step(qi, n_far, 0)

    def finish(qi):
        t_last = jnp.maximum(qi, 1)
        last_in_p1 = t_last % 2 == 1
        for x in range(ns):
            p_last = jnp.where(last_in_p1, p1_scr[x][...], p0_scr[x][...])
            acc_scr[x][...] += _dot(vt_ref[0, 0, key_block(qi, t_last), 0:2 * hd, :], p_last)
        half = ns // 2
        inv = [1.0 / jnp.sum(l_scr[x][...], axis=0, keepdims=True) for x in range(ns)]
        outs = []
        for x in range(half):
            outs.append(acc_scr[x][...] * inv[x] - lam * (acc_scr[half + x][...] * inv[half + x]))
        ot = outs[0] if half == 1 else jnp.concatenate(outs, axis=1)
        ms = jnp.mean(ot * ot, axis=0, keepdims=True)
        y = (ot * lax.rsqrt(ms + EPS)).T * (g_ref[...] * out_scale)
        o_ref[0, pl.ds(pl.multiple_of(qi * blk, blk), blk), :] = y.astype(BF16)

    begin(0)

    def query_block(qi, c):
        finish(qi - 1)
        begin(qi)
        sweep(qi)
        return c

    lax.fori_loop(1, nq, query_block, 0)
    finish(nq - 1)


def _diff_attn(lam, qt, k, vt, bias, g, out_scale, bounded):
    b, nh, nq = qt.shape[0], qt.shape[1], qt.shape[2]
    s = k.shape[1]
    blk = ATT_BLK
    ns = ATT_STREAMS
    w = 2 * blk // ns
    hw = 2 * DIFF_HEAD_DIM
    if bounded:
        body = _diff_attn_bounded_kernel
        per_stream = (((hw, w), BF16), ((blk, w), BF16), ((blk, w), BF16), ((hw, w), F32), ((8, w), F32))
        grid = (b, nh)
        q_spec = pl.BlockSpec((1, 1, nq, hw, blk), lambda bi, hi: (bi, hi, 0, 0, 0))
        o_spec = pl.BlockSpec((1, s, hw), lambda bi, hi: (bi, 0, hi))
        sem = ("parallel", "parallel")
    else:
        body = _diff_attn_kernel
        per_stream = (((hw, w), BF16), ((blk, w), F32), ((8, w), F32), ((blk, w), BF16),
                      ((1, w), F32), ((1, w), F32), ((ATT_V_ROWS, w), F32))
        grid = (b, nh, nq)
        q_spec = pl.BlockSpec((1, 1, 1, hw, blk), lambda bi, hi, qi: (bi, hi, qi, 0, 0))
        o_spec = pl.BlockSpec((1, blk, hw), lambda bi, hi, qi: (bi, qi, hi))
        sem = ("parallel", "parallel", "arbitrary")
    return pl.pallas_call(
        functools.partial(body, out_scale=out_scale),
        grid=grid,
        in_specs=[
            pl.BlockSpec(memory_space=pltpu.SMEM),
            q_spec,
            pl.BlockSpec((1, s, hw), lambda bi, hi, *_: (bi, 0, hi)),
            pl.BlockSpec((1, 1, nq, ATT_V_ROWS, blk), lambda bi, hi, *_: (bi, hi, 0, 0, 0)),
            pl.BlockSpec((1, 2, 1, 2 * blk), lambda bi, hi, *_: (hi, 0, 0, 0)),
            _const_spec((1, hw)),
        ],
        out_specs=o_spec,
        out_shape=jax.ShapeDtypeStruct((b, s, nh * hw), BF16),
        scratch_shapes=[pltpu.VMEM((2, blk, blk), F32)] + [pltpu.VMEM(shape, dtype)
                                                        for shape, dtype in per_stream for _ in range(ns)],
        compiler_params=_cparams(*sem),
        name="diff_attn_bounded" if bounded else "diff_attn",
    )(lam, qt, k, vt, bias, g)


def _pre_gla_kernel(x_ref, g_ref, w_ref, gsum_ref, mg_ref, gw_ref, gb_ref, tri_ref,
                    q_ref, k_ref, v_ref, r_ref, gc_ref, mq_ref):
    kw = GLA_HEADS * GLA_KP
    vw = TOKEN_WIDTH
    h = _rms_rows(x_ref[...], g_ref[...]).astype(BF16)
    q_ref[...] = _dot(h, w_ref[:, 0:kw]).astype(BF16)
    k_ref[...] = _dot(h, w_ref[:, kw:2 * kw]).astype(BF16)
    o = 2 * kw
    v_ref[...] = _dot(h, w_ref[:, o:o + vw]).astype(BF16)
    r_ref[...] = _dot(h, w_ref[:, o + vw:o + 2 * vw]).astype(BF16)
    o = o + 2 * vw
    gate_low = _dot(h, w_ref[:, o:o + LANES]).astype(BF16)
    mq_ref[...] = _group_rms(_dot(h, w_ref[:, o + LANES:]), gsum_ref[...], mg_ref[...]).astype(BF16)
    z = _dot(gate_low, gw_ref[...]) + gb_ref[...]
    log_a = (jnp.minimum(z, 0.0) - jnp.log1p(jnp.exp(-jnp.abs(z)))) * (1.0 / GLA_GATE_TAU)
    hi = log_a.astype(BF16)
    rem = log_a - hi.astype(F32)
    mid = rem.astype(BF16)
    lo = (rem - mid.astype(F32)).astype(BF16)
    tri = tri_ref[...]
    n = tri.shape[0]
    for c in range(log_a.shape[0] // n):
        rows = slice(c * n, (c + 1) * n)
        gc_ref[rows, :] = _dot(tri, hi[rows]) + _dot(tri, mid[rows]) + _dot(tri, lo[rows])


def _pre_gla(x2, g, w, gsum, mg, gw, gb, tri):
    t = x2.shape[0]
    kw = GLA_HEADS * GLA_KP
    vw = TOKEN_WIDTH
    row = lambda n: pl.BlockSpec((ROW_TILE, n), lambda i: (i, 0))
    return pl.pallas_call(
        _pre_gla_kernel,
        grid=(t // ROW_TILE,),
        in_specs=[row(D_MODEL), _const_spec((1, D_MODEL)), _const_spec(w.shape),
                  _const_spec(gsum.shape), _const_spec((1, MEM_WIDTH)), _const_spec(gw.shape),
                  _const_spec(gb.shape), _const_spec(tri.shape)],
        out_specs=[row(kw), row(kw), row(vw), row(vw), row(kw), row(MEM_WIDTH)],
        out_shape=[jax.ShapeDtypeStruct((t, kw), BF16), jax.ShapeDtypeStruct((t, kw), BF16),
                   jax.ShapeDtypeStruct((t, vw), BF16), jax.ShapeDtypeStruct((t, vw), BF16),
                   jax.ShapeDtypeStruct((t, kw), F32), jax.ShapeDtypeStruct((t, MEM_WIDTH), BF16)],
        compiler_params=_cparams("parallel"),
        name="pre_gla",
    )(x2, g, w, gsum, mg, gw, gb, tri)


def _gla_kernel(q_ref, k_ref, v_ref, r_ref, gc_ref, gain_ref, o_ref, s_scr):
    tg = GLA_TILE
    nchunk = tg // CHUNK
    heads = range(GLA_HEADS)
    ks = [slice(h * GLA_KP, (h + 1) * GLA_KP) for h in heads]
    vs = [slice(h * GLA_VP, (h + 1) * GLA_VP) for h in heads]
    pad_lane = lax.broadcasted_iota(jnp.int32, (tg, GLA_VP), 1) >= GLA_V_DIM

    def head_cols(ref, h):
        first = h * GLA_V_DIM
        aligned = first // LANES * LANES
        win = ref[0, :, aligned:aligned + GLA_VP].astype(F32)
        if first != aligned:
            win = pltpu.roll(win, GLA_VP - (first - aligned), 1)
        return jnp.where(pad_lane, 0.0, win)

    @pl.when(pl.program_id(1) == 0)
    def _():
        s_scr[...] = jnp.zeros(s_scr.shape, F32)

    ri = lax.broadcasted_iota(jnp.int32, (tg, tg), 0)
    ci = lax.broadcasted_iota(jnp.int32, (tg, tg), 1)
    same_chunk = (ri // CHUNK) == (ci // CHUNK)
    past = ci <= ri
    row_chunk = lax.broadcasted_iota(jnp.int32, (tg, GLA_KP), 0) // CHUNK

    qe, scores, kv, decay, v_pad = [], [], [], [], []
    for h in heads:
        qh = q_ref[0, :, ks[h]].astype(F32) * (GLA_K_DIM ** -0.5)
        kh = k_ref[0, :, ks[h]].astype(F32)
        g = gc_ref[0, :, ks[h]]
        eg = jnp.exp(g)
        ieg = jnp.exp(-g)
        qe.append((qh * eg).astype(BF16))
        a_past = _dot_nt(qe[h], (kh * ieg).astype(BF16))
        a_fut = _dot_nt((qh * ieg).astype(BF16), (kh * eg).astype(BF16))
        scores.append(jnp.where(same_chunk, jnp.where(past, a_past, a_fut), 0.0).astype(BF16))
        v_pad.append(head_cols(v_ref, h))
        vt = v_pad[h].T.astype(BF16)
        g_last = [g[c * CHUNK + CHUNK - 1:c * CHUNK + CHUNK, :] for c in range(nchunk)]
        g_end = jnp.concatenate([jnp.broadcast_to(gl, (CHUNK, GLA_KP)) for gl in g_last], axis=0)
        kdec = kh * jnp.exp(g_end - g)
        kv.append([_dot(vt, jnp.where(row_chunk == c, kdec, 0.0).astype(BF16)) for c in range(nchunk)])
        decay.append([jnp.exp(gl) for gl in g_last])

    starts = []
    for h in heads:
        st = s_scr[h]
        per_chunk = []
        for c in range(nchunk):
            per_chunk.append(st.astype(BF16))
            st = st * decay[h][c] + kv[h][c]
        s_scr[h] = st
        starts.append(per_chunk)

    gated = []
    for h in heads:
        inter = [_dot_nt(qe[h][c * CHUNK:(c + 1) * CHUNK], starts[h][c]) for c in range(nchunk)]
        o = _dot(scores[h], v_pad[h].astype(BF16)) + jnp.concatenate(inter, axis=0)
        ms = jnp.sum(o * o, axis=-1, keepdims=True) * (1.0 / GLA_V_DIM)
        y = o * lax.rsqrt(ms + EPS) * gain_ref[:, vs[h]]
        rh = head_cols(r_ref, h)
        gated.append(y * (rh / (1.0 + jnp.exp(-rh))))

    pair = 2 * GLA_V_DIM
    blank = jnp.zeros((tg, GLA_VP), F32)
    for p in range(GLA_HEADS // 2):
        even = jnp.concatenate([gated[2 * p], blank], axis=1)
        odd = pltpu.roll(jnp.concatenate([gated[2 * p + 1], blank], axis=1), GLA_V_DIM, 1)
        o_ref[0, :, p * pair:(p + 1) * pair] = (even + odd)[:, :pair].astype(BF16)


def _gla(q, k, v, r, gc, gain):
    b, s = q.shape[0], q.shape[1]
    kw = GLA_HEADS * GLA_KP
    vw = GLA_HEADS * GLA_VP
    spec = lambda n: pl.BlockSpec((1, GLA_TILE, n), lambda bi, i: (bi, i, 0))
    return pl.pallas_call(
        _gla_kernel,
        grid=(b, s // GLA_TILE),
        in_specs=[spec(kw), spec(kw), spec(TOKEN_WIDTH), spec(TOKEN_WIDTH), spec(kw), _const_spec((1, vw))],
        out_specs=spec(TOKEN_WIDTH),
        out_shape=jax.ShapeDtypeStruct((b, s, TOKEN_WIDTH), BF16),
        scratch_shapes=[pltpu.VMEM((GLA_HEADS, GLA_VP, GLA_KP), F32)],
        compiler_params=_cparams("parallel", "arbitrary"),
        name="gla",
    )(q, k, v, r, gc, gain)


def _mem_kv_kernel(mem_ref, g_ref, w_ref, gsum_ref, kg_ref, k_ref, v_ref):
    h = _rms_rows(mem_ref[...], g_ref[...]).astype(BF16)
    k_ref[...] = _group_rms(_dot(h, w_ref[:, :MEM_WIDTH]), gsum_ref[...], kg_ref[...]).astype(BF16)
    v_ref[...] = _dot(h, w_ref[:, MEM_WIDTH:]).astype(BF16)


def _mem_kv(mem2, g, w_all, layer, gsum, kg):
    n = mem2.shape[0]
    return pl.pallas_call(
        _mem_kv_kernel,
        grid=(1,),
        in_specs=[_const_spec(mem2.shape), _const_spec((1, D_MODEL)), _layer_spec(w_all, layer),
                  _const_spec(gsum.shape), _const_spec((1, MEM_WIDTH))],
        out_specs=[_const_spec((n, MEM_WIDTH)), _const_spec((n, MEM_WIDTH))],
        out_shape=[jax.ShapeDtypeStruct((n, MEM_WIDTH), BF16)] * 2,
        compiler_params=_cparams("arbitrary"),
        name="mem_kv",
    )(mem2, g, w_all, gsum, kg)


def _mix_out_kernel(x_ref, mix_ref, mq_ref, kbd_ref, vbd_ref, wa_ref, wb_ref, o_ref):
    m = kbd_ref.shape[2] // MEM_HEADS
    logits = _dot(mq_ref[0], kbd_ref[0])
    ps = []
    for h in range(MEM_HEADS):
        s = logits[:, h * m:(h + 1) * m]
        e = jnp.exp(s - jnp.max(s, axis=-1, keepdims=True))
        ps.append((e * (1.0 / jnp.sum(e, axis=-1, keepdims=True))).astype(BF16))
    cross = _dot(jnp.concatenate(ps, axis=1), vbd_ref[0])
    o_ref[0] = x_ref[0] + _dot(mix_ref[0], wa_ref[...]) + _dot(cross.astype(BF16), wb_ref[...])


def _mix_out(x, mix, mq, kbd, vbd, wa, wa_spec, wb, wb_spec):
    b, s = x.shape[0], x.shape[1]
    spec = lambda n: pl.BlockSpec((1, ROW_TILE, n), lambda bi, i: (bi, i, 0))
    per_b = lambda a: pl.BlockSpec((1,) + a.shape[1:], lambda bi, i: (bi, 0, 0))
    return pl.pallas_call(
        _mix_out_kernel,
        grid=(b, s // ROW_TILE),
        in_specs=[spec(D_MODEL), spec(mix.shape[2]), spec(MEM_WIDTH), per_b(kbd), per_b(vbd),
                  wa_spec, wb_spec],
        out_specs=spec(D_MODEL),
        out_shape=jax.ShapeDtypeStruct(x.shape, F32),
        compiler_params=_cparams("parallel", "parallel"),
        name="mix_out",
    )(x, mix, mq, kbd, vbd, wa, wb)


def _ffn_kernel(x_ref, g_ref, wu_ref, cw_ref, cb_ref, wd_ref, o_ref, carry_scr, act_scr, *shift_scr):
    tm = FFN_TILE
    cr = CARRY_ROWS

    @pl.when(pl.program_id(1) == 0)
    def _():
        carry_scr[...] = jnp.zeros(carry_scr.shape, F32)

    x = x_ref[0]
    h = _rms_rows(x, g_ref[...]).astype(BF16)

    def conv(cols, bufs):
        u = _dot(h, wu_ref[:, cols])
        prev = carry_scr[:, cols]
        for shift, buf in zip((1, 2), bufs):
            buf[shift:shift + cr, :] = prev
            buf[cr + shift:cr + shift + tm, :] = u
        carry_scr[:, cols] = u[tm - cr:tm]
        return (cw_ref[0:1, cols] * bufs[1][cr:cr + tm, :] + cw_ref[1:2, cols] * bufs[0][cr:cr + tm, :]
                + cw_ref[2:3, cols] * u + cb_ref[:, cols])

    for j in range(D_FF // FFN_COLS):
        bufs = shift_scr[4 * (j % 2):4 * (j % 2) + 4]
        a = conv(slice(j * FFN_COLS, (j + 1) * FFN_COLS), bufs[0:2])
        half_g = 0.5 * conv(slice(D_FF + j * FFN_COLS, D_FF + (j + 1) * FFN_COLS), bufs[2:4])
        act_scr[:, j * FFN_COLS:(j + 1) * FFN_COLS] = (a * half_g * (1.0 + jnp.tanh(half_g))).astype(BF16)

    o_ref[0] = x + _dot(act_scr[...], wd_ref[...])


def _ffn(x, g, wu_all, cw, cb, wd_all, layer):
    b, s = x.shape[0], x.shape[1]
    spec = pl.BlockSpec((1, FFN_TILE, D_MODEL), lambda bi, i: (bi, i, 0))
    return pl.pallas_call(
        _ffn_kernel,
        grid=(b, s // FFN_TILE),
        in_specs=[spec, _const_spec((1, D_MODEL)), _layer_spec(wu_all, layer, single=True), _const_spec(cw.shape),
                  _const_spec(cb.shape), _layer_spec(wd_all, layer, single=True)],
        out_specs=spec,
        out_shape=jax.ShapeDtypeStruct(x.shape, F32),
        scratch_shapes=[pltpu.VMEM((CARRY_ROWS, 2 * D_FF), F32), pltpu.VMEM((FFN_TILE, D_FF), BF16)]
        + [pltpu.VMEM((FFN_TILE + 2 * CARRY_ROWS, FFN_COLS), F32)] * 8,
        compiler_params=_cparams("parallel", "arbitrary"),
        name="ffn",
    )(x, g, wu_all, cw, cb, wd_all)


def _t5_bucket(rel):
    half = REL_BUCKETS // 2
    max_exact = half // 2
    ret = jnp.where(rel > 0, half, 0)
    n = jnp.abs(rel)
    nf = jnp.maximum(n, 1).astype(jnp.float32)
    large = max_exact + (jnp.log(nf / max_exact) / math.log(REL_MAX_DIST / max_exact)
                         * (half - max_exact)).astype(jnp.int32)
    large = jnp.minimum(large, half - 1)
    return ret + jnp.where(n < max_exact, n, large)


def _bias_vectors(rel_bias):
    blk = ATT_BLK
    assert blk >= REL_MAX_DIST
    table = rel_bias.astype(F32).T[:, :, None]

    def lookup(rel):
        bucket = _t5_bucket(rel)
        out = jnp.zeros((table.shape[0],) + rel.shape, F32)
        for i in range(REL_BUCKETS):
            out = jnp.where(bucket == i, table[:, i], out)
        return out

    far = lookup(jnp.full((1,), -2 * blk))
    j = jnp.arange(2 * blk)
    dist = jnp.where(j < blk, -j, 2 * blk - j)
    diag = (lookup(dist) - far) * LOG2E
    near = (lookup(dist - blk) - far) * LOG2E
    return jnp.stack([diag, near], axis=1)[:, :, None, :]


def _group_sum_matrix():
    i = jnp.arange(MXU_DIM)
    return ((i[:, None] // 64) == (i[None, :] // 64)).astype(BF16)


def _chunk_tri_matrix(n):
    i = jnp.arange(n)
    return (((i[:, None] // CHUNK) == (i[None, :] // CHUNK)) & (i[None, :] <= i[:, None])).astype(BF16)


def _pad_heads(w, heads, dim, pad, axis):
    shape = list(w.shape)
    shape[axis:axis + 1] = [heads, dim]
    w = w.reshape(shape)
    widths = [(0, 0)] * w.ndim
    widths[axis + 1] = (0, pad - dim)
    w = jnp.pad(w, widths)
    shape[axis:axis + 2] = [heads * pad]
    return w.reshape(shape)


def _tile_gain(g, reps, scale=1.0):
    return (jnp.tile(g.astype(F32), reps) * scale)[None, :]


def _mem_block_diag(kn, v, b):
    m = kn.shape[0] // b
    eye = jnp.eye(MEM_HEADS, dtype=BF16)
    knt = kn.reshape(b, m, MEM_WIDTH).transpose(0, 2, 1)
    kbd = (knt.reshape(b, MEM_HEADS, MEM_HEAD_DIM, 1, m) * eye.reshape(1, MEM_HEADS, 1, MEM_HEADS, 1))
    kbd = kbd.reshape(b, MEM_WIDTH, MEM_HEADS * m)
    vbd = (v.reshape(b, 1, m, MEM_HEADS, MEM_HEAD_DIM) * eye.reshape(1, MEM_HEADS, 1, MEM_HEADS, 1))
    vbd = vbd.reshape(b, MEM_HEADS * m, MEM_WIDTH)
    return kbd, vbd


def kernel(x, mem, rel_bias, attn_norm, ffn_norm, mem_norm, w_in_diff, diff_qk_norm, diff_lambda,
           diff_out_norm, w_in_gla, gla_gate_w, gla_gate_b, gla_out_norm, w_mem_kv, mem_qk_norm,
           w_out, w_up, conv_w, conv_b, w_down):
    b, s, d = x.shape
    t = b * s
    tw = TOKEN_WIDTH
    gsum = _group_sum_matrix()
    mem2 = mem.reshape(b * mem.shape[1], d)
    x = x.astype(F32)
    w_in_diff, w_mem_kv, w_out, w_up, w_down = map(_to_bf16, (w_in_diff, w_mem_kv, w_out, w_up, w_down))

    for i in range(DEPTH):
        j = i // 2
        x2 = x.reshape(t, d)
        mq_gain = _tile_gain(mem_qk_norm[i, 0], MEM_HEADS, MEM_HEAD_DIM ** -0.5)
        if i % 2 == 0:
            nh, hd = DIFF_HEADS, DIFF_HEAD_DIM
            qt, k, vt, mq = _pre_diff(
                x2, attn_norm[i][None, :], w_in_diff, j, gsum,
                _tile_gain(diff_qk_norm[j, 0], 2 * nh, hd ** -0.5 * LOG2E),
                _tile_gain(diff_qk_norm[j, 1], 2 * nh), mq_gain, b)
            lv = diff_lambda[j].astype(F32)
            lam_init = 0.8 - 0.6 * math.exp(-0.3 * i)
            lam = jnp.exp(jnp.sum(lv[0] * lv[1])) - jnp.exp(jnp.sum(lv[2] * lv[3])) + lam_init
            bvec = _bias_vectors(rel_bias)
            qk_bound = (hd ** 0.5 * LOG2E * ATT_ROUNDING_SLACK
                        * jnp.max(jnp.abs(diff_qk_norm[j, 0])) * jnp.max(jnp.abs(diff_qk_norm[j, 1]))).astype(F32)
            hi = qk_bound + jnp.maximum(jnp.max(bvec), 0.0)
            lo = -qk_bound + jnp.minimum(jnp.min(bvec), 0.0)
            scalars = jnp.stack([lam.astype(F32), hi]).reshape(1, 2)
            attend = lambda bounded: functools.partial(
                _diff_attn, qt=qt, k=k.reshape(b, s, tw), vt=vt, bias=bvec,
                g=diff_out_norm[j].astype(F32)[None, :], out_scale=1.0 - lam_init, bounded=bounded)
            mix = lax.cond(hi - lo <= ATT_MAX_EXP2_SPAN, attend(True), attend(False), scalars)
        else:
            kw = GLA_HEADS * GLA_K_DIM
            w = w_in_gla[j]
            hp = functools.partial(_pad_heads, heads=GLA_HEADS, axis=1)
            w_p = jnp.concatenate([
                hp(w[:, :kw], dim=GLA_K_DIM, pad=GLA_KP),
                hp(w[:, kw:2 * kw], dim=GLA_K_DIM, pad=GLA_KP),
                w[:, 2 * kw:2 * kw + 2 * tw],
                jnp.pad(w[:, 2 * kw + 2 * tw:2 * kw + 2 * tw + GLA_GATE_RANK],
                        ((0, 0), (0, LANES - GLA_GATE_RANK))),
                w[:, 2 * kw + 2 * tw + GLA_GATE_RANK:]], axis=1).astype(BF16)
            gw = jnp.pad(hp(gla_gate_w[j], dim=GLA_K_DIM, pad=GLA_KP),
                         ((0, LANES - GLA_GATE_RANK), (0, 0))).astype(BF16)
            gb = _pad_heads(gla_gate_b[j].astype(F32)[None, :], GLA_HEADS, GLA_K_DIM, GLA_KP, 1)
            q, k, v, r, gc, mq = _pre_gla(x2, attn_norm[i][None, :], w_p, gsum, mq_gain, gw, gb,
                                          _chunk_tri_matrix(MXU_DIM))
            gain = _pad_heads(jnp.tile(gla_out_norm[j].astype(F32), GLA_HEADS)[None, :],
                              GLA_HEADS, GLA_V_DIM, GLA_VP, 1)
            sh = lambda a: a.reshape(b, s, a.shape[1])
            mix = _gla(sh(q), sh(k), sh(v), sh(r), sh(gc), gain)

        kn, vm = _mem_kv(mem2, mem_norm[i][None, :], w_mem_kv, i, gsum,
                         _tile_gain(mem_qk_norm[i, 1], MEM_HEADS))
        kbd, vbd = _mem_block_diag(kn, vm, b)
        x = _mix_out(x, mix.reshape(b, s, -1), mq.reshape(b, s, MEM_WIDTH), kbd, vbd,
                     w_out, _layer_spec(w_out, i, rows=(0, tw)), w_out, _layer_spec(w_out, i, rows=(tw, MEM_WIDTH)))
        x = _ffn(x, ffn_norm[i][None, :], w_up, conv_w[i].astype(F32), conv_b[i].astype(F32)[None, :], w_down, i)
    return x
```

```python
import functools
import math

import jax
import jax.numpy as jnp
from jax import lax
from jax.experimental import pallas as pl
from jax.experimental.pallas import tpu as pltpu

F32 = jnp.float32
BF16 = jnp.bfloat16

D_MODEL = 1024
DEPTH = 2
CHUNK = 64
MEM_WIDTH = D_MODEL // 4
MEM_HEADS = 4
MEM_HEAD_DIM = MEM_WIDTH // MEM_HEADS
TOKEN_WIDTH = D_MODEL - MEM_WIDTH
DIFF_HEAD_DIM = 64
DIFF_HEADS = TOKEN_WIDTH // (2 * DIFF_HEAD_DIM)
GLA_HEADS = 4
GLA_V_DIM = TOKEN_WIDTH // GLA_HEADS
GLA_K_DIM = GLA_V_DIM // 2
GLA_GATE_RANK = 16
GLA_GATE_TAU = 16.0
REL_BUCKETS = 32
REL_MAX_DIST = 128
D_FF = ((8 * D_MODEL // 3 + 127) // 128) * 128
EPS = 1e-6
LOG2E = math.log2(math.e)

LANES = 128
MXU_DIM = 256
VMEM_LIMIT_BYTES = 56 * 1024 * 1024

ROW_TILE = 512
ATT_BLK = 512
ATT_STREAMS = 4
ATT_MAX_EXP2_SPAN = 100.0
ATT_ROUNDING_SLACK = 1.02
ATT_V_ROWS = 2 * DIFF_HEAD_DIM + 16
GLA_TILE = 256
FFN_TILE = 512
FFN_COLS = 256
GLA_KP = 128
GLA_VP = 256
CARRY_ROWS = 8
CAST_BLOCK_BYTES = 4 * 1024 * 1024


def _cparams(*sem):
    return pltpu.CompilerParams(dimension_semantics=sem, vmem_limit_bytes=VMEM_LIMIT_BYTES)


def _const_spec(shape):
    n = len(shape)
    return pl.BlockSpec(shape, lambda *_: (0,) * n)


def _layer_spec(w_all, layer, rows=None, single=False):
    first, n_rows = (0, w_all.shape[1]) if rows is None else rows
    assert first % n_rows == 0
    mode = dict(pipeline_mode=pl.Buffered(1)) if single else {}
    return pl.BlockSpec((None, n_rows, w_all.shape[2]), lambda *_: (layer, first // n_rows, 0), **mode)


def _cast_kernel(w_ref, o_ref):
    o_ref[...] = w_ref[...].astype(BF16)


def _to_bf16(w):
    n, r, c = w.shape
    rows = max(rb for rb in range(16, r + 1, 16) if r % rb == 0 and rb * c * 4 <= CAST_BLOCK_BYTES)
    spec = pl.BlockSpec((1, rows, c), lambda a, i: (a, i, 0))
    return pl.pallas_call(
        _cast_kernel, grid=(n, r // rows), in_specs=[spec], out_specs=spec,
        out_shape=jax.ShapeDtypeStruct(w.shape, BF16),
        compiler_params=_cparams("parallel", "parallel"), name="to_bf16",
    )(w.astype(F32))


def _rms_rows(x, g):
    ms = jnp.mean(x * x, axis=-1, keepdims=True)
    return x * lax.rsqrt(ms + EPS) * g


def _group_rms(t, gsum, gain):
    cols = []
    for c in range(t.shape[1] // MXU_DIM):
        blk = t[:, c * MXU_DIM:(c + 1) * MXU_DIM]
        ss = jnp.dot((blk * blk).astype(BF16), gsum, preferred_element_type=F32)
        cols.append(blk * lax.rsqrt(ss * (1.0 / 64) + EPS))
    out = cols[0] if len(cols) == 1 else jnp.concatenate(cols, axis=1)
    return out * gain


def _dot(a, b):
    return jnp.dot(a, b, preferred_element_type=F32)


def _dot_nt(a, b):
    return lax.dot_general(a, b, (((1,), (1,)), ((), ())), preferred_element_type=F32)


def _pre_diff_kernel(x_ref, g_ref, w_ref, gsum_ref, qg_ref, kg_ref, mg_ref,
                     qt_ref, k_ref, vt_ref, mq_ref):
    tw = TOKEN_WIDTH
    hw = 2 * DIFF_HEAD_DIM
    h = _rms_rows(x_ref[...], g_ref[...]).astype(BF16)
    gsum = gsum_ref[...]
    q = _group_rms(_dot(h, w_ref[:, 0:tw]), gsum, qg_ref[...])
    k_ref[...] = _group_rms(_dot(h, w_ref[:, tw:2 * tw]), gsum, kg_ref[...]).astype(BF16)
    v = _dot(h, w_ref[:, 2 * tw:3 * tw])
    mq_ref[...] = _group_rms(_dot(h, w_ref[:, 3 * tw:]), gsum, mg_ref[...]).astype(BF16)
    ones = jnp.ones((ATT_V_ROWS - hw, ROW_TILE), BF16)
    for n in range(DIFF_HEADS):
        qt_ref[0, n, 0] = q[:, n * hw:(n + 1) * hw].T.astype(BF16)
        vt_ref[0, n, 0, 0:hw, :] = v[:, n * hw:(n + 1) * hw].T.astype(BF16)
        vt_ref[0, n, 0, hw:, :] = ones


def _pre_diff(x2, g, w_all, layer, gsum, qg, kg, mg, b):
    t = x2.shape[0]
    tw = TOKEN_WIDTH
    hw = 2 * DIFF_HEAD_DIM
    assert ROW_TILE == ATT_BLK
    nq = t // b // ATT_BLK
    row = lambda n: pl.BlockSpec((ROW_TILE, n), lambda i: (i, 0))
    per_head = lambda r: pl.BlockSpec((1, DIFF_HEADS, 1, r, ATT_BLK), lambda i: (i // nq, 0, i % nq, 0, 0))
    return pl.pallas_call(
        _pre_diff_kernel,
        grid=(t // ROW_TILE,),
        in_specs=[row(D_MODEL), _const_spec((1, D_MODEL)), _layer_spec(w_all, layer),
                  _const_spec(gsum.shape), _const_spec((1, tw)), _const_spec((1, tw)),
                  _const_spec((1, MEM_WIDTH))],
        out_specs=[per_head(hw), row(tw), per_head(ATT_V_ROWS), row(MEM_WIDTH)],
        out_shape=[jax.ShapeDtypeStruct((b, DIFF_HEADS, nq, hw, ATT_BLK), BF16),
                   jax.ShapeDtypeStruct((t, tw), BF16),
                   jax.ShapeDtypeStruct((b, DIFF_HEADS, nq, ATT_V_ROWS, ATT_BLK), BF16),
                   jax.ShapeDtypeStruct((t, MEM_WIDTH), BF16)],
        compiler_params=_cparams("parallel"),
        name="pre_diff",
    )(x2, g, w_all, gsum, qg, kg, mg)


def _diff_attn_kernel(lam_ref, qt_ref, k_ref, vt_ref, bvec_ref, g_ref, o_ref,
                      bias_scr, *scratch, out_scale):
    blk = ATT_BLK
    hd = DIFF_HEAD_DIM
    ns = ATT_STREAMS
    qp_scr, s_scr, cm_scr, p_scr, a_scr, m_scr, acc_scr = (scratch[i * ns:(i + 1) * ns] for i in range(7))
    w = 2 * blk // ns
    qi = pl.program_id(2)
    qt = qt_ref[0, 0, 0]
    row = lax.broadcasted_iota(jnp.int32, qt.shape, 0)
    zero = jnp.zeros_like(qt)
    q_maps = (jnp.where(row < hd, qt, zero), jnp.where(row >= hd, qt, zero))
    cols = [slice((x % (ns // 2)) * w, (x % (ns // 2) + 1) * w) for x in range(ns)]
    for x in range(ns):
        qp_scr[x][...] = q_maps[x // (ns // 2)][:, cols[x]]
        m_scr[x][...] = jnp.full(m_scr[x].shape, -jnp.inf, F32)
        acc_scr[x][...] = jnp.zeros(acc_scr[x].shape, F32)

    @pl.when(qi == 0)
    def _():
        kk = lax.broadcasted_iota(jnp.int32, (blk, blk), 0)
        qq = lax.broadcasted_iota(jnp.int32, (blk, blk), 1)
        visible = (kk // CHUNK) <= (qq // CHUNK)
        for i in range(2):
            rows = jnp.broadcast_to(bvec_ref[0, i], (blk, 2 * blk))
            tile = pltpu.roll(rows, 0, 1, stride=1, stride_axis=0)[:, :blk]
            bias_scr[i] = jnp.where(visible, tile, -jnp.inf) if i == 0 else tile

    def key_block(t):
        return jnp.maximum(qi - t, 0)

    def logits(x, t, bias=None):
        start = pl.multiple_of(key_block(t) * blk, blk)
        s = _dot(k_ref[0, pl.ds(start, blk), :], qp_scr[x][...])
        if bias is not None:
            s = s + bias[:, cols[x]]
        s_scr[x][...] = s
        part = s[0:8]
        for r in range(8, blk, 8):
            part = jnp.maximum(part, s[r:r + 8])
        cm_scr[x][...] = part

    def softmax(x):
        m_old = m_scr[x][...]
        m_new = jnp.maximum(m_old, jnp.max(cm_scr[x][...], axis=0, keepdims=True))
        a_scr[x][...] = jnp.exp2(m_old - m_new)
        m_scr[x][...] = m_new
        for r in range(0, blk, 16):
            p_scr[x][r:r + 16, :] = jnp.exp2((s_scr[x][r:r + 16, :] - m_new).astype(BF16))

    def values(x, t):
        acc_scr[x][...] = a_scr[x][...] * acc_scr[x][...] + _dot(vt_ref[0, 0, key_block(t)], p_scr[x][...])

    def step(t, bias=None):
        for x in range(ns):
            logits(x, t + 1, bias)
            values(x, t)
            softmax((x + 1) % ns)

    for x in range(ns):
        logits(x, 0, bias_scr[0])
    softmax(0)
    step(0, bias_scr[1] + jnp.where(qi == 0, -jnp.inf, 0.0).astype(F32))

    t_last = jnp.maximum(qi, 1)

    def far_pair(u, c):
        step(1 + 2 * u)
        step(2 + 2 * u)
        return c

    lax.fori_loop(0, (t_last - 1) // 2, far_pair, 0)

    @pl.when((t_last - 1) % 2 == 1)
    def _():
        step(t_last - 1)

    for x in range(ns):
        values(x, t_last)
        if x + 1 < ns:
            softmax(x + 1)

    lam = lam_ref[0, 0]
    half = ns // 2
    outs = []
    for x in range(half):
        o0 = acc_scr[x][0:2 * hd, :] * (1.0 / acc_scr[x][2 * hd:2 * hd + 1, :])
        o1 = acc_scr[half + x][0:2 * hd, :] * (1.0 / acc_scr[half + x][2 * hd:2 * hd + 1, :])
        outs.append(o0 - lam * o1)
    ot = outs[0] if half == 1 else jnp.concatenate(outs, axis=1)
    ms = jnp.mean(ot * ot, axis=0, keepdims=True)
    y = (ot * lax.rsqrt(ms + EPS)).T * (g_ref[...] * out_scale)
    o_ref[0] = y.astype(BF16)


def _diff_attn_bounded_kernel(lam_ref, qt_ref, k_ref, vt_ref, bvec_ref, g_ref, o_ref,
                              bias_scr, *scratch, out_scale):
    blk = ATT_BLK
    hd = DIFF_HEAD_DIM
    ns = ATT_STREAMS
    nq = qt_ref.shape[2]
    qp_scr, p0_scr, p1_scr, acc_scr, l_scr = (scratch[i * ns:(i + 1) * ns] for i in range(5))
    p_scr = (p0_scr, p1_scr)
    w = 2 * blk // ns
    lam = lam_ref[0, 0]
    m_ref = lam_ref[0, 1]
    cols = [slice((x % (ns // 2)) * w, (x % (ns // 2) + 1) * w) for x in range(ns)]

    kk = lax.broadcasted_iota(jnp.int32, (blk, blk), 0)
    qq = lax.broadcasted_iota(jnp.int32, (blk, blk), 1)
    visible = (kk // CHUNK) <= (qq // CHUNK)
    for i in range(2):
        rows = jnp.broadcast_to(bvec_ref[0, i], (blk, 2 * blk))
        tile = pltpu.roll(rows, 0, 1, stride=1, stride_axis=0)[:, :blk]
        bias_scr[i] = jnp.where(visible, tile, -jnp.inf) if i == 0 else tile

    def key_block(qi, t):
        return jnp.maximum(qi - t, 0)

    diag_keys = [(x % (ns // 2) + 1) * w for x in range(ns)]
    assert w % CHUNK == 0

    def probs(qi, x, t, bias=None, keys=blk):
        start = pl.multiple_of(key_block(qi, t) * blk, blk)
        s = _dot(k_ref[0, pl.ds(start, keys), :], qp_scr[x][...])
        if bias is not None:
            s = s + bias[0:keys, cols[x]]
        p = jnp.exp2(s - m_ref)
        l_scr[x][...] += jnp.sum(p.reshape(keys // 8, 8, w), axis=0)
        return p.astype(BF16)

    def step(qi, t, slot, bias=None, keys_t=None):
        for x in range(ns):
            p_scr[slot][x][...] = probs(qi, x, t + 1, bias)
            n = blk if keys_t is None else keys_t[x]
            acc_scr[x][...] += _dot(vt_ref[0, 0, key_block(qi, t), 0:2 * hd, 0:n], p_scr[1 - slot][x][0:n, :])

    def begin(qi):
        qt = qt_ref[0, 0, qi]
        row = lax.broadcasted_iota(jnp.int32, qt.shape, 0)
        zero = jnp.zeros_like(qt)
        q_maps = (jnp.where(row < hd, qt, zero), jnp.where(row >= hd, qt, zero))
        for x in range(ns):
            qp_scr[x][...] = q_maps[x // (ns // 2)][:, cols[x]]
            acc_scr[x][...] = jnp.zeros(acc_scr[x].shape, F32)
            l_scr[x][...] = jnp.zeros(l_scr[x].shape, F32)
        for x in range(ns):
            p0_scr[x][0:diag_keys[x], :] = probs(qi, x, 0, bias_scr[0], diag_keys[x])
        step(qi, 0, 1, bias_scr[1] + jnp.where(qi == 0, -jnp.inf, 0.0).astype(F32), diag_keys)

    def sweep(qi):
        n_far = qi - 1

        def far_steps(n):
            def body(u, c):
                for i in range(n):
                    step(qi, 1 + n * u + i, i % 2)
                return c
            return body

        n_quads = n_far // 4
        lax.fori_loop(0, n_quads, far_steps(4), 0)
        t_done = 1 + 4 * n_quads

        @pl.when(n_far % 4 >= 2)
        def _():
            step(qi, t_done, 0)
            step(qi, t_done + 1, 1)

        @pl.when(n_far % 2 == 1)
        def _():
            step(qi, n_far, 0)

    def finish(qi):
        t_last = jnp.maximum(qi, 1)
        last_in_p1 = t_last % 2 == 1
        for x in range(ns):
            p_last = jnp.where(last_in_p1, p1_scr[x][...], p0_scr[x][...])
            acc_scr[x][...] += _dot(vt_ref[0, 0, key_block(qi, t_last), 0:2 * hd, :], p_last)
        half = ns // 2
        inv = [1.0 / jnp.sum(l_scr[x][...], axis=0, keepdims=True) for x in range(ns)]
        outs = []
        for x in range(half):
            outs.append(acc_scr[x][...] * inv[x] - lam * (acc_scr[half + x][...] * inv[half + x]))
        ot = outs[0] if half == 1 else jnp.concatenate(outs, axis=1)
        ms = jnp.mean(ot * ot, axis=0, keepdims=True)
        y = (ot * lax.rsqrt(ms + EPS)).T * (g_ref[...] * out_scale)
        o_ref[0, pl.ds(pl.multiple_of(qi * blk, blk), blk), :] = y.astype(BF16)

    begin(0)

    def query_block(qi, c):
        finish(qi - 1)
        begin(qi)
        sweep(qi)
        return c

    lax.fori_loop(1, nq, query_block, 0)
    finish(nq - 1)


def _diff_attn(lam, qt, k, vt, bias, g, out_scale, bounded):
    b, nh, nq = qt.shape[0], qt.shape[1], qt.shape[2]
    s = k.shape[1]
    blk = ATT_BLK
    ns = ATT_STREAMS
    w = 2 * blk // ns
    hw = 2 * DIFF_HEAD_DIM
    if bounded:
        body = _diff_attn_bounded_kernel
        per_stream = (((hw, w), BF16), ((blk, w), BF16), ((blk, w), BF16), ((hw, w), F32), ((8, w), F32))
        grid = (b, nh)
        q_spec = pl.BlockSpec((1, 1, nq, hw, blk), lambda bi, hi: (bi, hi, 0, 0, 0))
        o_spec = pl.BlockSpec((1, s, hw), lambda bi, hi: (bi, 0, hi))
        sem = ("parallel", "parallel")
    else:
        body = _diff_attn_kernel
        per_stream = (((hw, w), BF16), ((blk, w), F32), ((8, w), F32), ((blk, w), BF16),
                      ((1, w), F32), ((1, w), F32), ((ATT_V_ROWS, w), F32))
        grid = (b, nh, nq)
        q_spec = pl.BlockSpec((1, 1, 1, hw, blk), lambda bi, hi, qi: (bi, hi, qi, 0, 0))
        o_spec = pl.BlockSpec((1, blk, hw), lambda bi, hi, qi: (bi, qi, hi))
        sem = ("parallel", "parallel", "arbitrary")
    return pl.pallas_call(
        functools.partial(body, out_scale=out_scale),
        grid=grid,
        in_specs=[
            pl.BlockSpec(memory_space=pltpu.SMEM),
            q_spec,
            pl.BlockSpec((1, s, hw), lambda bi, hi, *_: (bi, 0, hi)),
            pl.BlockSpec((1, 1, nq, ATT_V_ROWS, blk), lambda bi, hi, *_: (bi, hi, 0, 0, 0)),
            pl.BlockSpec((1, 2, 1, 2 * blk), lambda bi, hi, *_: (hi, 0, 0, 0)),
            _const_spec((1, hw)),
        ],
        out_specs=o_spec,
        out_shape=jax.ShapeDtypeStruct((b, s, nh * hw), BF16),
        scratch_shapes=[pltpu.VMEM((2, blk, blk), F32)] + [pltpu.VMEM(shape, dtype)
                                                        for shape, dtype in per_stream for _ in range(ns)],
        compiler_params=_cparams(*sem),
        name="diff_attn_bounded" if bounded else "diff_attn",
    )(lam, qt, k, vt, bias, g)


def _pre_gla_kernel(x_ref, g_ref, w_ref, gsum_ref, mg_ref, gw_ref, gb_ref, tri_ref,
                    q_ref, k_ref, v_ref, r_ref, gc_ref, mq_ref):
    kw = GLA_HEADS * GLA_KP
    vw = TOKEN_WIDTH
    h = _rms_rows(x_ref[...], g_ref[...]).astype(BF16)
    q_ref[...] = _dot(h, w_ref[:, 0:kw]).astype(BF16)
    k_ref[...] = _dot(h, w_ref[:, kw:2 * kw]).astype(BF16)
    o = 2 * kw
    v_ref[...] = _dot(h, w_ref[:, o:o + vw]).astype(BF16)
    r_ref[...] = _dot(h, w_ref[:, o + vw:o + 2 * vw]).astype(BF16)
    o = o + 2 * vw
    gate_low = _dot(h, w_ref[:, o:o + LANES]).astype(BF16)
    mq_ref[...] = _group_rms(_dot(h, w_ref[:, o + LANES:]), gsum_ref[...], mg_ref[...]).astype(BF16)
    z = _dot(gate_low, gw_ref[...]) + gb_ref[...]
    log_a = (jnp.minimum(z, 0.0) - jnp.log1p(jnp.exp(-jnp.abs(z)))) * (1.0 / GLA_GATE_TAU)
    hi = log_a.astype(BF16)
    rem = log_a - hi.astype(F32)
    mid = rem.astype(BF16)
    lo = (rem - mid.astype(F32)).astype(BF16)
    tri = tri_ref[...]
    n = tri.shape[0]
    for c in range(log_a.shape[0] // n):
        rows = slice(c * n, (c + 1) * n)
        gc_ref[rows, :] = _dot(tri, hi[rows]) + _dot(tri, mid[rows]) + _dot(tri, lo[rows])


def _pre_gla(x2, g, w, gsum, mg, gw, gb, tri):
    t = x2.shape[0]
    kw = GLA_HEADS * GLA_KP
    vw = TOKEN_WIDTH
    row = lambda n: pl.BlockSpec((ROW_TILE, n), lambda i: (i, 0))
    return pl.pallas_call(
        _pre_gla_kernel,
        grid=(t // ROW_TILE,),
        in_specs=[row(D_MODEL), _const_spec((1, D_MODEL)), _const_spec(w.shape),
                  _const_spec(gsum.shape), _const_spec((1, MEM_WIDTH)), _const_spec(gw.shape),
                  _const_spec(gb.shape), _const_spec(tri.shape)],
        out_specs=[row(kw), row(kw), row(vw), row(vw), row(kw), row(MEM_WIDTH)],
        out_shape=[jax.ShapeDtypeStruct((t, kw), BF16), jax.ShapeDtypeStruct((t, kw), BF16),
                   jax.ShapeDtypeStruct((t, vw), BF16), jax.ShapeDtypeStruct((t, vw), BF16),
                   jax.ShapeDtypeStruct((t, kw), F32), jax.ShapeDtypeStruct((t, MEM_WIDTH), BF16)],
        compiler_params=_cparams("parallel"),
        name="pre_gla",
    )(x2, g, w, gsum, mg, gw, gb, tri)


def _gla_kernel(q_ref, k_ref, v_ref, r_ref, gc_ref, gain_ref, o_ref, s_scr):
    tg = GLA_TILE
    nchunk = tg // CHUNK
    heads = range(GLA_HEADS)
    ks = [slice(h * GLA_KP, (h + 1) * GLA_KP) for h in heads]
    vs = [slice(h * GLA_VP, (h + 1) * GLA_VP) for h in heads]
    pad_lane = lax.broadcasted_iota(jnp.int32, (tg, GLA_VP), 1) >= GLA_V_DIM

    def head_cols(ref, h):
        first = h * GLA_V_DIM
        aligned = first // LANES * LANES
        win = ref[0, :, aligned:aligned + GLA_VP].astype(F32)
        if first != aligned:
            win = pltpu.roll(win, GLA_VP - (first - aligned), 1)
        return jnp.where(pad_lane, 0.0, win)

    @pl.when(pl.program_id(1) == 0)
    def _():
        s_scr[...] = jnp.zeros(s_scr.shape, F32)

    ri = lax.broadcasted_iota(jnp.int32, (tg, tg), 0)
    ci = lax.broadcasted_iota(jnp.int32, (tg, tg), 1)
    same_chunk = (ri // CHUNK) == (ci // CHUNK)
    past = ci <= ri
    row_chunk = lax.broadcasted_iota(jnp.int32, (tg, GLA_KP), 0) // CHUNK

    qe, scores, kv, decay, v_pad = [], [], [], [], []
    for h in heads:
        qh = q_ref[0, :, ks[h]].astype(F32) * (GLA_K_DIM ** -0.5)
        kh = k_ref[0, :, ks[h]].astype(F32)
        g = gc_ref[0, :, ks[h]]
        eg = jnp.exp(g)
        ieg = jnp.exp(-g)
        qe.append((qh * eg).astype(BF16))
        a_past = _dot_nt(qe[h], (kh * ieg).astype(BF16))
        a_fut = _dot_nt((qh * ieg).astype(BF16), (kh * eg).astype(BF16))
        scores.append(jnp.where(same_chunk, jnp.where(past, a_past, a_fut), 0.0).astype(BF16))
        v_pad.append(head_cols(v_ref, h))
        vt = v_pad[h].T.astype(BF16)
        g_last = [g[c * CHUNK + CHUNK - 1:c * CHUNK + CHUNK, :] for c in range(nchunk)]
        g_end = jnp.concatenate([jnp.broadcast_to(gl, (CHUNK, GLA_KP)) for gl in g_last], axis=0)
        kdec = kh * jnp.exp(g_end - g)
        kd_chunks = jnp.concatenate([jnp.where(row_chunk == c, kdec, 0.0) for c in range(nchunk)], axis=1)
        kv_all = _dot(vt, kd_chunks.astype(BF16))
        kv.append([kv_all[:, c * GLA_KP:(c + 1) * GLA_KP] for c in range(nchunk)])
        decay.append([jnp.exp(gl) for gl in g_last])

    starts = []
    for h in heads:
        st = s_scr[h]
        per_chunk = []
        for c in range(nchunk):
            per_chunk.append(st.astype(BF16))
            st = st * decay[h][c] + kv[h][c]
        s_scr[h] = st
        starts.append(per_chunk)

    gated = []
    for h in heads:
        inter = [_dot_nt(qe[h][c * CHUNK:(c + 1) * CHUNK], starts[h][c]) for c in range(nchunk)]
        o = _dot(scores[h], v_pad[h].astype(BF16)) + jnp.concatenate(inter, axis=0)
        ms = jnp.sum(o * o, axis=-1, keepdims=True) * (1.0 / GLA_V_DIM)
        y = o * lax.rsqrt(ms + EPS) * gain_ref[:, vs[h]]
        half_r = 0.5 * head_cols(r_ref, h)
        gated.append(y * half_r * (1.0 + jnp.tanh(half_r)))

    pair = 2 * GLA_V_DIM
    blank = jnp.zeros((tg, GLA_VP), F32)
    for p in range(GLA_HEADS // 2):
        even = jnp.concatenate([gated[2 * p], blank], axis=1)
        odd = pltpu.roll(jnp.concatenate([gated[2 * p + 1], blank], axis=1), GLA_V_DIM, 1)
        o_ref[0, :, p * pair:(p + 1) * pair] = (even + odd)[:, :pair].astype(BF16)


def _gla(q, k, v, r, gc, gain):
    b, s = q.shape[0], q.shape[1]
    kw = GLA_HEADS * GLA_KP
    vw = GLA_HEADS * GLA_VP
    spec = lambda n: pl.BlockSpec((1, GLA_TILE, n), lambda bi, i: (bi, i, 0))
    return pl.pallas_call(
        _gla_kernel,
        grid=(b, s // GLA_TILE),
        in_specs=[spec(kw), spec(kw), spec(TOKEN_WIDTH), spec(TOKEN_WIDTH), spec(kw), _const_spec((1, vw))],
        out_specs=spec(TOKEN_WIDTH),
        out_shape=jax.ShapeDtypeStruct((b, s, TOKEN_WIDTH), BF16),
        scratch_shapes=[pltpu.VMEM((GLA_HEADS, GLA_VP, GLA_KP), F32)],
        compiler_params=_cparams("parallel", "arbitrary"),
        name="gla",
    )(q, k, v, r, gc, gain)


def _mem_kv_kernel(mem_ref, g_ref, w_ref, gsum_ref, kg_ref, k_ref, v_ref):
    h = _rms_rows(mem_ref[...], g_ref[...]).astype(BF16)
    k_ref[...] = _group_rms(_dot(h, w_ref[:, :MEM_WIDTH]), gsum_ref[...], kg_ref[...]).astype(BF16)
    v_ref[...] = _dot(h, w_ref[:, MEM_WIDTH:]).astype(BF16)


def _mem_kv(mem2, g, w_all, layer, gsum, kg):
    n = mem2.shape[0]
    return pl.pallas_call(
        _mem_kv_kernel,
        grid=(1,),
        in_specs=[_const_spec(mem2.shape), _const_spec((1, D_MODEL)), _layer_spec(w_all, layer),
                  _const_spec(gsum.shape), _const_spec((1, MEM_WIDTH))],
        out_specs=[_const_spec((n, MEM_WIDTH)), _const_spec((n, MEM_WIDTH))],
        out_shape=[jax.ShapeDtypeStruct((n, MEM_WIDTH), BF16)] * 2,
        compiler_params=_cparams("arbitrary"),
        name="mem_kv",
    )(mem2, g, w_all, gsum, kg)


def _mix_out_kernel(x_ref, mix_ref, mq_ref, kbd_ref, vbd_ref, wa_ref, wb_ref, o_ref):
    m = kbd_ref.shape[2] // MEM_HEADS
    logits = _dot(mq_ref[0], kbd_ref[0])
    ps = []
    for h in range(MEM_HEADS):
        s = logits[:, h * m:(h + 1) * m]
        e = jnp.exp(s - jnp.max(s, axis=-1, keepdims=True))
        ps.append((e * (1.0 / jnp.sum(e, axis=-1, keepdims=True))).astype(BF16))
    cross = _dot(jnp.concatenate(ps, axis=1), vbd_ref[0])
    o_ref[0] = x_ref[0] + _dot(mix_ref[0], wa_ref[...]) + _dot(cross.astype(BF16), wb_ref[...])


def _mix_out(x, mix, mq, kbd, vbd, wa, wa_spec, wb, wb_spec):
    b, s = x.shape[0], x.shape[1]
    spec = lambda n: pl.BlockSpec((1, ROW_TILE, n), lambda bi, i: (bi, i, 0))
    per_b = lambda a: pl.BlockSpec((1,) + a.shape[1:], lambda bi, i: (bi, 0, 0))
    return pl.pallas_call(
        _mix_out_kernel,
        grid=(b, s // ROW_TILE),
        in_specs=[spec(D_MODEL), spec(mix.shape[2]), spec(MEM_WIDTH), per_b(kbd), per_b(vbd),
                  wa_spec, wb_spec],
        out_specs=spec(D_MODEL),
        out_shape=jax.ShapeDtypeStruct(x.shape, F32),
        compiler_params=_cparams("parallel", "parallel"),
        name="mix_out",
    )(x, mix, mq, kbd, vbd, wa, wb)


def _ffn_kernel(x_ref, g_ref, wu_ref, cw_ref, cb_ref, wd_ref, o_ref, carry_scr, act_scr, *shift_scr):
    tm = FFN_TILE
    cr = CARRY_ROWS

    @pl.when(pl.program_id(1) == 0)
    def _():
        carry_scr[...] = jnp.zeros(carry_scr.shape, F32)

    x = x_ref[0]
    h = _rms_rows(x, g_ref[...]).astype(BF16)

    def conv(cols, bufs):
        u = _dot(h, wu_ref[:, cols])
        prev = carry_scr[:, cols]
        for shift, buf in zip((1, 2), bufs):
            buf[shift:shift + cr, :] = prev
            buf[cr + shift:cr + shift + tm, :] = u
        carry_scr[:, cols] = u[tm - cr:tm]
        return (cw_ref[0:1, cols] * bufs[1][cr:cr + tm, :] + cw_ref[1:2, cols] * bufs[0][cr:cr + tm, :]
                + cw_ref[2:3, cols] * u + cb_ref[:, cols])

    for j in range(D_FF // FFN_COLS):
        bufs = shift_scr[4 * (j % 2):4 * (j % 2) + 4]
        a = conv(slice(j * FFN_COLS, (j + 1) * FFN_COLS), bufs[0:2])
        half_g = 0.5 * conv(slice(D_FF + j * FFN_COLS, D_FF + (j + 1) * FFN_COLS), bufs[2:4])
        act_scr[:, j * FFN_COLS:(j + 1) * FFN_COLS] = (a * half_g * (1.0 + jnp.tanh(half_g))).astype(BF16)

    o_ref[0] = x + _dot(act_scr[...], wd_ref[...])


def _ffn(x, g, wu_all, cw, cb, wd_all, layer):
    b, s = x.shape[0], x.shape[1]
    spec = pl.BlockSpec((1, FFN_TILE, D_MODEL), lambda bi, i: (bi, i, 0))
    return pl.pallas_call(
        _ffn_kernel,
        grid=(b, s // FFN_TILE),
        in_specs=[spec, _const_spec((1, D_MODEL)), _layer_spec(wu_all, layer, single=True), _const_spec(cw.shape),
                  _const_spec(cb.shape), _layer_spec(wd_all, layer, single=True)],
        out_specs=spec,
        out_shape=jax.ShapeDtypeStruct(x.shape, F32),
        scratch_shapes=[pltpu.VMEM((CARRY_ROWS, 2 * D_FF), F32), pltpu.VMEM((FFN_TILE, D_FF), BF16)]
        + [pltpu.VMEM((FFN_TILE + 2 * CARRY_ROWS, FFN_COLS), F32)] * 8,
        compiler_params=_cparams("parallel", "arbitrary"),
        name="ffn",
    )(x, g, wu_all, cw, cb, wd_all)


def _t5_bucket(rel):
    half = REL_BUCKETS // 2
    max_exact = half // 2
    ret = jnp.where(rel > 0, half, 0)
    n = jnp.abs(rel)
    nf = jnp.maximum(n, 1).astype(jnp.float32)
    large = max_exact + (jnp.log(nf / max_exact) / math.log(REL_MAX_DIST / max_exact)
                         * (half - max_exact)).astype(jnp.int32)
    large = jnp.minimum(large, half - 1)
    return ret + jnp.where(n < max_exact, n, large)


def _bias_vectors(rel_bias):
    blk = ATT_BLK
    assert blk >= REL_MAX_DIST
    table = rel_bias.astype(F32).T[:, :, None]

    def lookup(rel):
        bucket = _t5_bucket(rel)
        out = jnp.zeros((table.shape[0],) + rel.shape, F32)
        for i in range(REL_BUCKETS):
            out = jnp.where(bucket == i, table[:, i], out)
        return out

    far = lookup(jnp.full((1,), -2 * blk))
    j = jnp.arange(2 * blk)
    dist = jnp.where(j < blk, -j, 2 * blk - j)
    diag = (lookup(dist) - far) * LOG2E
    near = (lookup(dist - blk) - far) * LOG2E
    return jnp.stack([diag, near], axis=1)[:, :, None, :]


def _group_sum_matrix():
    i = jnp.arange(MXU_DIM)
    return ((i[:, None] // 64) == (i[None, :] // 64)).astype(BF16)


def _chunk_tri_matrix(n):
    i = jnp.arange(n)
    return (((i[:, None] // CHUNK) == (i[None, :] // CHUNK)) & (i[None, :] <= i[:, None])).astype(BF16)


def _pad_heads(w, heads, dim, pad, axis):
    shape = list(w.shape)
    shape[axis:axis + 1] = [heads, dim]
    w = w.reshape(shape)
    widths = [(0, 0)] * w.ndim
    widths[axis + 1] = (0, pad - dim)
    w = jnp.pad(w, widths)
    shape[axis:axis + 2] = [heads * pad]
    return w.reshape(shape)


def _tile_gain(g, reps, scale=1.0):
    return (jnp.tile(g.astype(F32), reps) * scale)[None, :]


def _mem_block_diag(kn, v, b):
    m = kn.shape[0] // b
    eye = jnp.eye(MEM_HEADS, dtype=BF16)
    knt = kn.reshape(b, m, MEM_WIDTH).transpose(0, 2, 1)
    kbd = (knt.reshape(b, MEM_HEADS, MEM_HEAD_DIM, 1, m) * eye.reshape(1, MEM_HEADS, 1, MEM_HEADS, 1))
    kbd = kbd.reshape(b, MEM_WIDTH, MEM_HEADS * m)
    vbd = (v.reshape(b, 1, m, MEM_HEADS, MEM_HEAD_DIM) * eye.reshape(1, MEM_HEADS, 1, MEM_HEADS, 1))
    vbd = vbd.reshape(b, MEM_HEADS * m, MEM_WIDTH)
    return kbd, vbd


def kernel(x, mem, rel_bias, attn_norm, ffn_norm, mem_norm, w_in_diff, diff_qk_norm, diff_lambda,
           diff_out_norm, w_in_gla, gla_gate_w, gla_gate_b, gla_out_norm, w_mem_kv, mem_qk_norm,
           w_out, w_up, conv_w, conv_b, w_down):
    b, s, d = x.shape
    t = b * s
    tw = TOKEN_WIDTH
    gsum = _group_sum_matrix()
    mem2 = mem.reshape(b * mem.shape[1], d)
    x = x.astype(F32)
    w_in_diff, w_mem_kv, w_out, w_up, w_down = map(_to_bf16, (w_in_diff, w_mem_kv, w_out, w_up, w_down))

    for i in range(DEPTH):
        j = i // 2
        x2 = x.reshape(t, d)
        mq_gain = _tile_gain(mem_qk_norm[i, 0], MEM_HEADS, MEM_HEAD_DIM ** -0.5)
        if i % 2 == 0:
            nh, hd = DIFF_HEADS, DIFF_HEAD_DIM
            qt, k, vt, mq = _pre_diff(
                x2, attn_norm[i][None, :], w_in_diff, j, gsum,
                _tile_gain(diff_qk_norm[j, 0], 2 * nh, hd ** -0.5 * LOG2E),
                _tile_gain(diff_qk_norm[j, 1], 2 * nh), mq_gain, b)
            lv = diff_lambda[j].astype(F32)
            lam_init = 0.8 - 0.6 * math.exp(-0.3 * i)
            lam = jnp.exp(jnp.sum(lv[0] * lv[1])) - jnp.exp(jnp.sum(lv[2] * lv[3])) + lam_init
            bvec = _bias_vectors(rel_bias)
            qk_bound = (hd ** 0.5 * LOG2E * ATT_ROUNDING_SLACK
                        * jnp.max(jnp.abs(diff_qk_norm[j, 0])) * jnp.max(jnp.abs(diff_qk_norm[j, 1]))).astype(F32)
            hi = qk_bound + jnp.maximum(jnp.max(bvec), 0.0)
            lo = -qk_bound + jnp.minimum(jnp.min(bvec), 0.0)
            scalars = jnp.stack([lam.astype(F32), hi]).reshape(1, 2)
            attend = lambda bounded: functools.partial(
                _diff_attn, qt=qt, k=k.reshape(b, s, tw), vt=vt, bias=bvec,
                g=diff_out_norm[j].astype(F32)[None, :], out_scale=1.0 - lam_init, bounded=bounded)
            mix = lax.cond(hi - lo <= ATT_MAX_EXP2_SPAN, attend(True), attend(False), scalars)
        else:
            kw = GLA_HEADS * GLA_K_DIM
            w = w_in_gla[j]
            hp = functools.partial(_pad_heads, heads=GLA_HEADS, axis=1)
            w_p = jnp.concatenate([
                hp(w[:, :kw], dim=GLA_K_DIM, pad=GLA_KP),
                hp(w[:, kw:2 * kw], dim=GLA_K_DIM, pad=GLA_KP),
                w[:, 2 * kw:2 * kw + 2 * tw],
                jnp.pad(w[:, 2 * kw + 2 * tw:2 * kw + 2 * tw + GLA_GATE_RANK],
                        ((0, 0), (0, LANES - GLA_GATE_RANK))),
                w[:, 2 * kw + 2 * tw + GLA_GATE_RANK:]], axis=1).astype(BF16)
            gw = jnp.pad(hp(gla_gate_w[j], dim=GLA_K_DIM, pad=GLA_KP),
                         ((0, LANES - GLA_GATE_RANK), (0, 0))).astype(BF16)
            gb = _pad_heads(gla_gate_b[j].astype(F32)[None, :], GLA_HEADS, GLA_K_DIM, GLA_KP, 1)
            q, k, v, r, gc, mq = _pre_gla(x2, attn_norm[i][None, :], w_p, gsum, mq_gain, gw, gb,
                                          _chunk_tri_matrix(MXU_DIM))
            gain = _pad_heads(jnp.tile(gla_out_norm[j].astype(F32), GLA_HEADS)[None, :],
                              GLA_HEADS, GLA_V_DIM, GLA_VP, 1)
            sh = lambda a: a.reshape(b, s, a.shape[1])
            mix = _gla(sh(q), sh(k), sh(v), sh(r), sh(gc), gain)

        kn, vm = _mem_kv(mem2, mem_norm[i][None, :], w_mem_kv, i, gsum,
                         _tile_gain(mem_qk_norm[i, 1], MEM_HEADS))
        kbd, vbd = _mem_block_diag(kn, vm, b)
        x = _mix_out(x, mix.reshape(b, s, -1), mq.reshape(b, s, MEM_WIDTH), kbd, vbd,
                     w_out, _layer_spec(w_out, i, rows=(0, tw)), w_out, _layer_spec(w_out, i, rows=(tw, MEM_WIDTH)))
        x = _ffn(x, ffn_norm[i][None, :], w_up, conv_w[i].astype(F32), conv_b[i].astype(F32)[None, :], w_down, i)
    return x
```

```python
import functools
import math

import jax
import jax.numpy as jnp
from jax import lax
from jax.experimental import pallas as pl
from jax.experimental.pallas import tpu as pltpu

F32 = jnp.float32
BF16 = jnp.bfloat16

D_MODEL = 1024
DEPTH = 2
CHUNK = 64
MEM_WIDTH = D_MODEL // 4
MEM_HEADS = 4
MEM_HEAD_DIM = MEM_WIDTH // MEM_HEADS
TOKEN_WIDTH = D_MODEL - MEM_WIDTH
DIFF_HEAD_DIM = 64
DIFF_HEADS = TOKEN_WIDTH // (2 * DIFF_HEAD_DIM)
GLA_HEADS = 4
GLA_V_DIM = TOKEN_WIDTH // GLA_HEADS
GLA_K_DIM = GLA_V_DIM // 2
GLA_GATE_RANK = 16
GLA_GATE_TAU = 16.0
REL_BUCKETS = 32
REL_MAX_DIST = 128
D_FF = ((8 * D_MODEL // 3 + 127) // 128) * 128
EPS = 1e-6
LOG2E = math.log2(math.e)

LANES = 128
MXU_DIM = 256
VMEM_LIMIT_BYTES = 56 * 1024 * 1024

ROW_TILE = 1024
MIX_TILE = 1024
ATT_BLK = 512
ATT_STREAMS = 4
ATT_MAX_EXP2_SPAN = 100.0
ATT_ROUNDING_SLACK = 1.02
ATT_V_ROWS = 2 * DIFF_HEAD_DIM + 16
GLA_TILE = 256
FFN_TILE = 512
FFN_COLS = 256
GLA_KP = 128
GLA_VP = 256
CARRY_ROWS = 8
CAST_BLOCK_BYTES = 4 * 1024 * 1024


def _cparams(*sem):
    return pltpu.CompilerParams(dimension_semantics=sem, vmem_limit_bytes=VMEM_LIMIT_BYTES)


def _const_spec(shape):
    n = len(shape)
    return pl.BlockSpec(shape, lambda *_: (0,) * n)


def _layer_spec(w_all, layer, rows=None, single=False):
    first, n_rows = (0, w_all.shape[1]) if rows is None else rows
    assert first % n_rows == 0
    mode = dict(pipeline_mode=pl.Buffered(1)) if single else {}
    return pl.BlockSpec((None, n_rows, w_all.shape[2]), lambda *_: (layer, first // n_rows, 0), **mode)


def _cast_kernel(w_ref, o_ref):
    o_ref[...] = w_ref[...].astype(BF16)


def _to_bf16(w):
    n, r, c = w.shape
    rows = max(rb for rb in range(16, r + 1, 16) if r % rb == 0 and rb * c * 4 <= CAST_BLOCK_BYTES)
    spec = pl.BlockSpec((1, rows, c), lambda a, i: (a, i, 0))
    return pl.pallas_call(
        _cast_kernel, grid=(n, r // rows), in_specs=[spec], out_specs=spec,
        out_shape=jax.ShapeDtypeStruct(w.shape, BF16),
        compiler_params=_cparams("parallel", "parallel"), name="to_bf16",
    )(w.astype(F32))


def _rms_rows(x, g):
    ms = jnp.mean(x * x, axis=-1, keepdims=True)
    return x * lax.rsqrt(ms + EPS) * g


def _group_rms(t, gsum, gain):
    cols = []
    for c in range(t.shape[1] // MXU_DIM):
        blk = t[:, c * MXU_DIM:(c + 1) * MXU_DIM]
        ss = jnp.dot((blk * blk).astype(BF16), gsum, preferred_element_type=F32)
        cols.append(blk * lax.rsqrt(ss * (1.0 / 64) + EPS))
    out = cols[0] if len(cols) == 1 else jnp.concatenate(cols, axis=1)
    return out * gain


def _dot(a, b):
    return jnp.dot(a, b, preferred_element_type=F32)


def _dot_nt(a, b):
    return lax.dot_general(a, b, (((1,), (1,)), ((), ())), preferred_element_type=F32)


def _pre_diff_kernel(x_ref, g_ref, w_ref, gsum_ref, qg_ref, kg_ref, mg_ref,
                     qt_ref, k_ref, vt_ref, mq_ref):
    tw = TOKEN_WIDTH
    hw = 2 * DIFF_HEAD_DIM
    h = _rms_rows(x_ref[...], g_ref[...]).astype(BF16)
    gsum = gsum_ref[...]
    q = _group_rms(_dot(h, w_ref[:, 0:tw]), gsum, qg_ref[...])
    k_ref[...] = _group_rms(_dot(h, w_ref[:, tw:2 * tw]), gsum, kg_ref[...]).astype(BF16)
    v = _dot(h, w_ref[:, 2 * tw:3 * tw])
    mq_ref[...] = _group_rms(_dot(h, w_ref[:, 3 * tw:]), gsum, mg_ref[...]).astype(BF16)
    ones = jnp.ones((ATT_V_ROWS - hw, ATT_BLK), BF16)
    for j in range(ROW_TILE // ATT_BLK):
        rows = slice(j * ATT_BLK, (j + 1) * ATT_BLK)
        for n in range(DIFF_HEADS):
            qt_ref[0, n, j] = q[rows, n * hw:(n + 1) * hw].T.astype(BF16)
            vt_ref[0, n, j, 0:hw, :] = v[rows, n * hw:(n + 1) * hw].T.astype(BF16)
            vt_ref[0, n, j, hw:, :] = ones


def _pre_diff(x2, g, w_all, layer, gsum, qg, kg, mg, b):
    t = x2.shape[0]
    tw = TOKEN_WIDTH
    hw = 2 * DIFF_HEAD_DIM
    per_tile = ROW_TILE // ATT_BLK
    nq = t // b // ATT_BLK
    tiles = nq // per_tile
    row = lambda n: pl.BlockSpec((ROW_TILE, n), lambda i: (i, 0))
    per_head = lambda r: pl.BlockSpec((1, DIFF_HEADS, per_tile, r, ATT_BLK),
                                      lambda i: (i // tiles, 0, i % tiles, 0, 0))
    return pl.pallas_call(
        _pre_diff_kernel,
        grid=(t // ROW_TILE,),
        in_specs=[row(D_MODEL), _const_spec((1, D_MODEL)), _layer_spec(w_all, layer),
                  _const_spec(gsum.shape), _const_spec((1, tw)), _const_spec((1, tw)),
                  _const_spec((1, MEM_WIDTH))],
        out_specs=[per_head(hw), row(tw), per_head(ATT_V_ROWS), row(MEM_WIDTH)],
        out_shape=[jax.ShapeDtypeStruct((b, DIFF_HEADS, nq, hw, ATT_BLK), BF16),
                   jax.ShapeDtypeStruct((t, tw), BF16),
                   jax.ShapeDtypeStruct((b, DIFF_HEADS, nq, ATT_V_ROWS, ATT_BLK), BF16),
                   jax.ShapeDtypeStruct((t, MEM_WIDTH), BF16)],
        compiler_params=_cparams("parallel"),
        name="pre_diff",
    )(x2, g, w_all, gsum, qg, kg, mg)


def _diff_attn_kernel(lam_ref, qt_ref, k_ref, vt_ref, bvec_ref, g_ref, o_ref,
                      bias_scr, *scratch, out_scale):
    blk = ATT_BLK
    hd = DIFF_HEAD_DIM
    ns = ATT_STREAMS
    qp_scr, s_scr, cm_scr, p_scr, a_scr, m_scr, acc_scr = (scratch[i * ns:(i + 1) * ns] for i in range(7))
    w = 2 * blk // ns
    qi = pl.program_id(2)
    qt = qt_ref[0, 0, 0]
    row = lax.broadcasted_iota(jnp.int32, qt.shape, 0)
    zero = jnp.zeros_like(qt)
    q_maps = (jnp.where(row < hd, qt, zero), jnp.where(row >= hd, qt, zero))
    cols = [slice((x % (ns // 2)) * w, (x % (ns // 2) + 1) * w) for x in range(ns)]
    for x in range(ns):
        qp_scr[x][...] = q_maps[x // (ns // 2)][:, cols[x]]
        m_scr[x][...] = jnp.full(m_scr[x].shape, -jnp.inf, F32)
        acc_scr[x][...] = jnp.zeros(acc_scr[x].shape, F32)

    @pl.when(qi == 0)
    def _():
        kk = lax.broadcasted_iota(jnp.int32, (blk, blk), 0)
        qq = lax.broadcasted_iota(jnp.int32, (blk, blk), 1)
        visible = (kk // CHUNK) <= (qq // CHUNK)
        for i in range(2):
            rows = jnp.broadcast_to(bvec_ref[0, i], (blk, 2 * blk))
            tile = pltpu.roll(rows, 0, 1, stride=1, stride_axis=0)[:, :blk]
            bias_scr[i] = jnp.where(visible, tile, -jnp.inf) if i == 0 else tile

    def key_block(t):
        return jnp.maximum(qi - t, 0)

    def logits(x, t, bias=None):
        start = pl.multiple_of(key_block(t) * blk, blk)
        s = _dot(k_ref[0, pl.ds(start, blk), :], qp_scr[x][...])
        if bias is not None:
            s = s + bias[:, cols[x]]
        s_scr[x][...] = s
        part = s[0:8]
        for r in range(8, blk, 8):
            part = jnp.maximum(part, s[r:r + 8])
        cm_scr[x][...] = part

    def softmax(x):
        m_old = m_scr[x][...]
        m_new = jnp.maximum(m_old, jnp.max(cm_scr[x][...], axis=0, keepdims=True))
        a_scr[x][...] = jnp.exp2(m_old - m_new)
        m_scr[x][...] = m_new
        for r in range(0, blk, 16):
            p_scr[x][r:r + 16, :] = jnp.exp2((s_scr[x][r:r + 16, :] - m_new).astype(BF16))

    def values(x, t):
        acc_scr[x][...] = a_scr[x][...] * acc_scr[x][...] + _dot(vt_ref[0, 0, key_block(t)], p_scr[x][...])

    def step(t, bias=None):
        for x in range(ns):
            logits(x, t + 1, bias)
            values(x, t)
            softmax((x + 1) % ns)

    for x in range(ns):
        logits(x, 0, bias_scr[0])
    softmax(0)
    step(0, bias_scr[1] + jnp.where(qi == 0, -jnp.inf, 0.0).astype(F32))

    t_last = jnp.maximum(qi, 1)

    def far_pair(u, c):
        step(1 + 2 * u)
        step(2 + 2 * u)
        return c

    lax.fori_loop(0, (t_last - 1) // 2, far_pair, 0)

    @pl.when((t_last - 1) % 2 == 1)
    def _():
        step(t_last - 1)

    for x in range(ns):
        values(x, t_last)
        if x + 1 < ns:
            softmax(x + 1)

    lam = lam_ref[0, 0]
    half = ns // 2
    outs = []
    for x in range(half):
        o0 = acc_scr[x][0:2 * hd, :] * (1.0 / acc_scr[x][2 * hd:2 * hd + 1, :])
        o1 = acc_scr[half + x][0:2 * hd, :] * (1.0 / acc_scr[half + x][2 * hd:2 * hd + 1, :])
        outs.append(o0 - lam * o1)
    ot = outs[0] if half == 1 else jnp.concatenate(outs, axis=1)
    ms = jnp.mean(ot * ot, axis=0, keepdims=True)
    y = (ot * lax.rsqrt(ms + EPS)).T * (g_ref[...] * out_scale)
    o_ref[0] = y.astype(BF16)


def _diff_attn_bounded_kernel(lam_ref, qt_ref, k_ref, vt_ref, bvec_ref, g_ref, o_ref,
                              bias_scr, *scratch, out_scale):
    blk = ATT_BLK
    hd = DIFF_HEAD_DIM
    ns = ATT_STREAMS
    nq = qt_ref.shape[2]
    qp_scr, p0_scr, p1_scr, acc_scr, l_scr = (scratch[i * ns:(i + 1) * ns] for i in range(5))
    p_scr = (p0_scr, p1_scr)
    w = 2 * blk // ns
    lam = lam_ref[0, 0]
    m_ref = lam_ref[0, 1]
    cols = [slice((x % (ns // 2)) * w, (x % (ns // 2) + 1) * w) for x in range(ns)]

    kk = lax.broadcasted_iota(jnp.int32, (blk, blk), 0)
    qq = lax.broadcasted_iota(jnp.int32, (blk, blk), 1)
    visible = (kk // CHUNK) <= (qq // CHUNK)
    for i in range(2):
        rows = jnp.broadcast_to(bvec_ref[0, i], (blk, 2 * blk))
        tile = pltpu.roll(rows, 0, 1, stride=1, stride_axis=0)[:, :blk]
        bias_scr[i] = jnp.where(visible, tile, -jnp.inf) if i == 0 else tile

    def key_block(qi, t):
        return jnp.maximum(qi - t, 0)

    diag_keys = [(x % (ns // 2) + 1) * w for x in range(ns)]
    assert w % CHUNK == 0

    def probs(qi, x, t, bias=None, keys=blk):
        start = pl.multiple_of(key_block(qi, t) * blk, blk)
        s = _dot(k_ref[0, pl.ds(start, keys), :], qp_scr[x][...])
        if bias is not None:
            s = s + bias[0:keys, cols[x]]
        p = jnp.exp2(s - m_ref)
        l_scr[x][...] += jnp.sum(p.reshape(keys // 8, 8, w), axis=0)
        return p.astype(BF16)

    def step(qi, t, slot, bias=None, keys_t=None):
        for x in range(ns):
            p_scr[slot][x][...] = probs(qi, x, t + 1, bias)
            n = blk if keys_t is None else keys_t[x]
            acc_scr[x][...] += _dot(vt_ref[0, 0, key_block(qi, t), 0:2 * hd, 0:n], p_scr[1 - slot][x][0:n, :])

    def begin(qi):
        qt = qt_ref[0, 0, qi]
        row = lax.broadcasted_iota(jnp.int32, qt.shape, 0)
        zero = jnp.zeros_like(qt)
        q_maps = (jnp.where(row < hd, qt, zero), jnp.where(row >= hd, qt, zero))
        for x in range(ns):
            qp_scr[x][...] = q_maps[x // (ns // 2)][:, cols[x]]
            acc_scr[x][...] = jnp.zeros(acc_scr[x].shape, F32)
            l_scr[x][...] = jnp.zeros(l_scr[x].shape, F32)
        for x in range(ns):
            p0_scr[x][0:diag_keys[x], :] = probs(qi, x, 0, bias_scr[0], diag_keys[x])
        step(qi, 0, 1, bias_scr[1] + jnp.where(qi == 0, -jnp.inf, 0.0).astype(F32), diag_keys)

    def sweep(qi):
        n_far = qi - 1

        def far_steps(n):
            def body(u, c):
                for i in range(n):
                    step(qi, 1 + n * u + i, i % 2)
                return c
            return body

        n_quads = n_far // 4
        lax.fori_loop(0, n_quads, far_steps(4), 0)
        t_done = 1 + 4 * n_quads

        @pl.when(n_far % 4 >= 2)
        def _():
            step(qi, t_done, 0)
            step(qi, t_done + 1, 1)

        @pl.when(n_far % 2 == 1)
        def _():
            step(qi, n_far, 0)

    def finish(qi):
        t_last = jnp.maximum(qi, 1)
        last_in_p1 = t_last % 2 == 1
        for x in range(ns):
            p_last = jnp.where(last_in_p1, p1_scr[x][...], p0_scr[x][...])
            acc_scr[x][...] += _dot(vt_ref[0, 0, key_block(qi, t_last), 0:2 * hd, :], p_last)
        half = ns // 2
        inv = [1.0 / jnp.sum(l_scr[x][...], axis=0, keepdims=True) for x in range(ns)]
        outs = []
        for x in range(half):
            outs.append(acc_scr[x][...] * inv[x] - lam * (acc_scr[half + x][...] * inv[half + x]))
        ot = outs[0] if half == 1 else jnp.concatenate(outs, axis=1)
        ms = jnp.mean(ot * ot, axis=0, keepdims=True)
        y = (ot * lax.rsqrt(ms + EPS)).T * (g_ref[...] * out_scale)
        o_ref[0, pl.ds(pl.multiple_of(qi * blk, blk), blk), :] = y.astype(BF16)

    begin(0)

    def query_block(qi, c):
        finish(qi - 1)
        begin(qi)
        sweep(qi)
        return c

    lax.fori_loop(1, nq, query_block, 0)
    finish(nq - 1)


def _diff_attn(lam, qt, k, vt, bias, g, out_scale, bounded):
    b, nh, nq = qt.shape[0], qt.shape[1], qt.shape[2]
    s = k.shape[1]
    blk = ATT_BLK
    ns = ATT_STREAMS
    w = 2 * blk // ns
    hw = 2 * DIFF_HEAD_DIM
    if bounded:
        body = _diff_attn_bounded_kernel
        per_stream = (((hw, w), BF16), ((blk, w), BF16), ((blk, w), BF16), ((hw, w), F32), ((8, w), F32))
        grid = (b, nh)
        q_spec = pl.BlockSpec((1, 1, nq, hw, blk), lambda bi, hi: (bi, hi, 0, 0, 0))
        o_spec = pl.BlockSpec((1, s, hw), lambda bi, hi: (bi, 0, hi))
        sem = ("parallel", "parallel")
    else:
        body = _diff_attn_kernel
        per_stream = (((hw, w), BF16), ((blk, w), F32), ((8, w), F32), ((blk, w), BF16),
                      ((1, w), F32), ((1, w), F32), ((ATT_V_ROWS, w), F32))
        grid = (b, nh, nq)
        q_spec = pl.BlockSpec((1, 1, 1, hw, blk), lambda bi, hi, qi: (bi, hi, qi, 0, 0))
        o_spec = pl.BlockSpec((1, blk, hw), lambda bi, hi, qi: (bi, qi, hi))
        sem = ("parallel", "parallel", "arbitrary")
    return pl.pallas_call(
        functools.partial(body, out_scale=out_scale),
        grid=grid,
        in_specs=[
            pl.BlockSpec(memory_space=pltpu.SMEM),
            q_spec,
            pl.BlockSpec((1, s, hw), lambda bi, hi, *_: (bi, 0, hi)),
            pl.BlockSpec((1, 1, nq, ATT_V_ROWS, blk), lambda bi, hi, *_: (bi, hi, 0, 0, 0)),
            pl.BlockSpec((1, 2, 1, 2 * blk), lambda bi, hi, *_: (hi, 0, 0, 0)),
            _const_spec((1, hw)),
        ],
        out_specs=o_spec,
        out_shape=jax.ShapeDtypeStruct((b, s, nh * hw), BF16),
        scratch_shapes=[pltpu.VMEM((2, blk, blk), F32)] + [pltpu.VMEM(shape, dtype)
                                                        for shape, dtype in per_stream for _ in range(ns)],
        compiler_params=_cparams(*sem),
        name="diff_attn_bounded" if bounded else "diff_attn",
    )(lam, qt, k, vt, bias, g)


def _pre_gla_kernel(x_ref, g_ref, w_ref, gsum_ref, mg_ref, gw_ref, gb_ref, tri_ref,
                    q_ref, k_ref, v_ref, r_ref, gc_ref, mq_ref):
    kw = GLA_HEADS * GLA_KP
    vw = TOKEN_WIDTH
    h = _rms_rows(x_ref[...], g_ref[...]).astype(BF16)
    q_ref[...] = _dot(h, w_ref[:, 0:kw]).astype(BF16)
    k_ref[...] = _dot(h, w_ref[:, kw:2 * kw]).astype(BF16)
    o = 2 * kw
    v_ref[...] = _dot(h, w_ref[:, o:o + vw]).astype(BF16)
    r_ref[...] = _dot(h, w_ref[:, o + vw:o + 2 * vw]).astype(BF16)
    o = o + 2 * vw
    gate_low = _dot(h, w_ref[:, o:o + LANES]).astype(BF16)
    mq_ref[...] = _group_rms(_dot(h, w_ref[:, o + LANES:]), gsum_ref[...], mg_ref[...]).astype(BF16)
    z = _dot(gate_low, gw_ref[...]) + gb_ref[...]
    log_a = (jnp.minimum(z, 0.0) - jnp.log1p(jnp.exp(-jnp.abs(z)))) * (1.0 / GLA_GATE_TAU)
    hi = log_a.astype(BF16)
    rem = log_a - hi.astype(F32)
    mid = rem.astype(BF16)
    lo = (rem - mid.astype(F32)).astype(BF16)
    tri = tri_ref[...]
    n = tri.shape[0]
    for c in range(log_a.shape[0] // n):
        rows = slice(c * n, (c + 1) * n)
        gc_ref[rows, :] = _dot(tri, hi[rows]) + _dot(tri, mid[rows]) + _dot(tri, lo[rows])


def _pre_gla(x2, g, w, gsum, mg, gw, gb, tri):
    t = x2.shape[0]
    kw = GLA_HEADS * GLA_KP
    vw = TOKEN_WIDTH
    row = lambda n: pl.BlockSpec((ROW_TILE, n), lambda i: (i, 0))
    return pl.pallas_call(
        _pre_gla_kernel,
        grid=(t // ROW_TILE,),
        in_specs=[row(D_MODEL), _const_spec((1, D_MODEL)), _const_spec(w.shape),
                  _const_spec(gsum.shape), _const_spec((1, MEM_WIDTH)), _const_spec(gw.shape),
                  _const_spec(gb.shape), _const_spec(tri.shape)],
        out_specs=[row(kw), row(kw), row(vw), row(vw), row(kw), row(MEM_WIDTH)],
        out_shape=[jax.ShapeDtypeStruct((t, kw), BF16), jax.ShapeDtypeStruct((t, kw), BF16),
                   jax.ShapeDtypeStruct((t, vw), BF16), jax.ShapeDtypeStruct((t, vw), BF16),
                   jax.ShapeDtypeStruct((t, kw), F32), jax.ShapeDtypeStruct((t, MEM_WIDTH), BF16)],
        compiler_params=_cparams("parallel"),
        name="pre_gla",
    )(x2, g, w, gsum, mg, gw, gb, tri)


def _gla_kernel(q_ref, k_ref, v_ref, r_ref, gc_ref, gain_ref, o_ref, s_scr):
    tg = GLA_TILE
    nchunk = tg // CHUNK
    heads = range(GLA_HEADS)
    ks = [slice(h * GLA_KP, (h + 1) * GLA_KP) for h in heads]
    vs = [slice(h * GLA_VP, (h + 1) * GLA_VP) for h in heads]
    pad_lane = lax.broadcasted_iota(jnp.int32, (tg, GLA_VP), 1) >= GLA_V_DIM

    def head_cols(ref, h):
        first = h * GLA_V_DIM
        aligned = first // LANES * LANES
        win = ref[0, :, aligned:aligned + GLA_VP].astype(F32)
        if first != aligned:
            win = pltpu.roll(win, GLA_VP - (first - aligned), 1)
        return jnp.where(pad_lane, 0.0, win)

    @pl.when(pl.program_id(1) == 0)
    def _():
        s_scr[...] = jnp.zeros(s_scr.shape, F32)

    ri = lax.broadcasted_iota(jnp.int32, (tg, tg), 0)
    ci = lax.broadcasted_iota(jnp.int32, (tg, tg), 1)
    same_chunk = (ri // CHUNK) == (ci // CHUNK)
    past = ci <= ri
    row_chunk = lax.broadcasted_iota(jnp.int32, (tg, GLA_KP), 0) // CHUNK

    qe, scores, kv, decay, v_pad = [], [], [], [], []
    for h in heads:
        qh = q_ref[0, :, ks[h]].astype(F32) * (GLA_K_DIM ** -0.5)
        kh = k_ref[0, :, ks[h]].astype(F32)
        g = gc_ref[0, :, ks[h]]
        eg = jnp.exp(g)
        ieg = jnp.exp(-g)
        qe.append((qh * eg).astype(BF16))
        a_past = _dot_nt(qe[h], (kh * ieg).astype(BF16))
        a_fut = _dot_nt((qh * ieg).astype(BF16), (kh * eg).astype(BF16))
        scores.append(jnp.where(same_chunk, jnp.where(past, a_past, a_fut), 0.0).astype(BF16))
        v_pad.append(head_cols(v_ref, h))
        vt = v_pad[h].T.astype(BF16)
        g_last = [g[c * CHUNK + CHUNK - 1:c * CHUNK + CHUNK, :] for c in range(nchunk)]
        g_end = jnp.concatenate([jnp.broadcast_to(gl, (CHUNK, GLA_KP)) for gl in g_last], axis=0)
        kdec = kh * jnp.exp(g_end - g)
        kd_chunks = jnp.concatenate([jnp.where(row_chunk == c, kdec, 0.0) for c in range(nchunk)], axis=1)
        kv_all = _dot(vt, kd_chunks.astype(BF16))
        kv.append([kv_all[:, c * GLA_KP:(c + 1) * GLA_KP] for c in range(nchunk)])
        decay.append([jnp.exp(gl) for gl in g_last])

    starts = []
    for h in heads:
        st = s_scr[h]
        per_chunk = []
        for c in range(nchunk):
            per_chunk.append(st.astype(BF16))
            st = st * decay[h][c] + kv[h][c]
        s_scr[h] = st
        starts.append(per_chunk)

    gated = []
    for h in heads:
        inter = [_dot_nt(qe[h][c * CHUNK:(c + 1) * CHUNK], starts[h][c]) for c in range(nchunk)]
        o = _dot(scores[h], v_pad[h].astype(BF16)) + jnp.concatenate(inter, axis=0)
        ms = jnp.sum(o * o, axis=-1, keepdims=True) * (1.0 / GLA_V_DIM)
        y = o * lax.rsqrt(ms + EPS) * gain_ref[:, vs[h]]
        half_r = 0.5 * head_cols(r_ref, h)
        gated.append(y * half_r * (1.0 + jnp.tanh(half_r)))

    pair = 2 * GLA_V_DIM
    blank = jnp.zeros((tg, GLA_VP), F32)
    for p in range(GLA_HEADS // 2):
        even = jnp.concatenate([gated[2 * p], blank], axis=1)
        odd = pltpu.roll(jnp.concatenate([gated[2 * p + 1], blank], axis=1), GLA_V_DIM, 1)
        o_ref[0, :, p * pair:(p + 1) * pair] = (even + odd)[:, :pair].astype(BF16)


def _gla(q, k, v, r, gc, gain):
    b, s = q.shape[0], q.shape[1]
    kw = GLA_HEADS * GLA_KP
    vw = GLA_HEADS * GLA_VP
    spec = lambda n: pl.BlockSpec((1, GLA_TILE, n), lambda bi, i: (bi, i, 0))
    return pl.pallas_call(
        _gla_kernel,
        grid=(b, s // GLA_TILE),
        in_specs=[spec(kw), spec(kw), spec(TOKEN_WIDTH), spec(TOKEN_WIDTH), spec(kw), _const_spec((1, vw))],
        out_specs=spec(TOKEN_WIDTH),
        out_shape=jax.ShapeDtypeStruct((b, s, TOKEN_WIDTH), BF16),
        scratch_shapes=[pltpu.VMEM((GLA_HEADS, GLA_VP, GLA_KP), F32)],
        compiler_params=_cparams("parallel", "arbitrary"),
        name="gla",
    )(q, k, v, r, gc, gain)


def _mem_kv_kernel(mem_ref, g_ref, w_ref, gsum_ref, kg_ref, k_ref, v_ref):
    h = _rms_rows(mem_ref[...], g_ref[...]).astype(BF16)
    k_ref[...] = _group_rms(_dot(h, w_ref[:, :MEM_WIDTH]), gsum_ref[...], kg_ref[...]).astype(BF16)
    v_ref[...] = _dot(h, w_ref[:, MEM_WIDTH:]).astype(BF16)


def _mem_kv(mem2, g, w_all, layer, gsum, kg):
    n = mem2.shape[0]
    return pl.pallas_call(
        _mem_kv_kernel,
        grid=(1,),
        in_specs=[_const_spec(mem2.shape), _const_spec((1, D_MODEL)), _layer_spec(w_all, layer),
                  _const_spec(gsum.shape), _const_spec((1, MEM_WIDTH))],
        out_specs=[_const_spec((n, MEM_WIDTH)), _const_spec((n, MEM_WIDTH))],
        out_shape=[jax.ShapeDtypeStruct((n, MEM_WIDTH), BF16)] * 2,
        compiler_params=_cparams("arbitrary"),
        name="mem_kv",
    )(mem2, g, w_all, gsum, kg)


def _mix_out_kernel(x_ref, mix_ref, mq_ref, kbd_ref, vbd_ref, wa_ref, wb_ref, o_ref):
    m = kbd_ref.shape[2] // MEM_HEADS
    logits = _dot(mq_ref[0], kbd_ref[0])
    ps = []
    for h in range(MEM_HEADS):
        s = logits[:, h * m:(h + 1) * m]
        e = jnp.exp(s - jnp.max(s, axis=-1, keepdims=True))
        ps.append((e * (1.0 / jnp.sum(e, axis=-1, keepdims=True))).astype(BF16))
    cross = _dot(jnp.concatenate(ps, axis=1), vbd_ref[0])
    o_ref[0] = x_ref[0] + _dot(mix_ref[0], wa_ref[...]) + _dot(cross.astype(BF16), wb_ref[...])


def _mix_out(x, mix, mq, kbd, vbd, wa, wa_spec, wb, wb_spec):
    b, s = x.shape[0], x.shape[1]
    spec = lambda n: pl.BlockSpec((1, MIX_TILE, n), lambda bi, i: (bi, i, 0))
    per_b = lambda a: pl.BlockSpec((1,) + a.shape[1:], lambda bi, i: (bi, 0, 0))
    return pl.pallas_call(
        _mix_out_kernel,
        grid=(b, s // MIX_TILE),
        in_specs=[spec(D_MODEL), spec(mix.shape[2]), spec(MEM_WIDTH), per_b(kbd), per_b(vbd),
                  wa_spec, wb_spec],
        out_specs=spec(D_MODEL),
        out_shape=jax.ShapeDtypeStruct(x.shape, F32),
        compiler_params=_cparams("parallel", "parallel"),
        name="mix_out",
    )(x, mix, mq, kbd, vbd, wa, wb)


def _ffn_kernel(x_ref, g_ref, wu_ref, cw_ref, cb_ref, wd_ref, o_ref, carry_scr, act_scr, *shift_scr):
    tm = FFN_TILE
    cr = CARRY_ROWS

    @pl.when(pl.program_id(1) == 0)
    def _():
        carry_scr[...] = jnp.zeros(carry_scr.shape, F32)

    x = x_ref[0]
    h = _rms_rows(x, g_ref[...]).astype(BF16)

    def conv(cols, bufs):
        u = _dot(h, wu_ref[:, cols])
        prev = carry_scr[:, cols]
        for shift, buf in zip((1, 2), bufs):
            buf[shift:shift + cr, :] = prev
            buf[cr + shift:cr + shift + tm, :] = u
        carry_scr[:, cols] = u[tm - cr:tm]
        return (cw_ref[0:1, cols] * bufs[1][cr:cr + tm, :] + cw_ref[1:2, cols] * bufs[0][cr:cr + tm, :]
                + cw_ref[2:3, cols] * u + cb_ref[:, cols])

    for j in range(D_FF // FFN_COLS):
        bufs = shift_scr[4 * (j % 2):4 * (j % 2) + 4]
        a = conv(slice(j * FFN_COLS, (j + 1) * FFN_COLS), bufs[0:2])
        half_g = 0.5 * conv(slice(D_FF + j * FFN_COLS, D_FF + (j + 1) * FFN_COLS), bufs[2:4])
        act_scr[:, j * FFN_COLS:(j + 1) * FFN_COLS] = (a * half_g * (1.0 + jnp.tanh(half_g))).astype(BF16)

    o_ref[0] = x + _dot(act_scr[...], wd_ref[...])


def _ffn(x, g, wu_all, cw, cb, wd_all, layer):
    b, s = x.shape[0], x.shape[1]
    spec = pl.BlockSpec((1, FFN_TILE, D_MODEL), lambda bi, i: (bi, i, 0))
    return pl.pallas_call(
        _ffn_kernel,
        grid=(b, s // FFN_TILE),
        in_specs=[spec, _const_spec((1, D_MODEL)), _layer_spec(wu_all, layer, single=True), _const_spec(cw.shape),
                  _const_spec(cb.shape), _layer_spec(wd_all, layer, single=True)],
        out_specs=spec,
        out_shape=jax.ShapeDtypeStruct(x.shape, F32),
        scratch_shapes=[pltpu.VMEM((CARRY_ROWS, 2 * D_FF), F32), pltpu.VMEM((FFN_TILE, D_FF), BF16)]
        + [pltpu.VMEM((FFN_TILE + 2 * CARRY_ROWS, FFN_COLS), F32)] * 8,
        compiler_params=_cparams("parallel", "arbitrary"),
        name="ffn",
    )(x, g, wu_all, cw, cb, wd_all)


def _t5_bucket(rel):
    half = REL_BUCKETS // 2
    max_exact = half // 2
    ret = jnp.where(rel > 0, half, 0)
    n = jnp.abs(rel)
    nf = jnp.maximum(n, 1).astype(jnp.float32)
    large = max_exact + (jnp.log(nf / max_exact) / math.log(REL_MAX_DIST / max_exact)
                         * (half - max_exact)).astype(jnp.int32)
    large = jnp.minimum(large, half - 1)
    return ret + jnp.where(n < max_exact, n, large)


def _bias_vectors(rel_bias):
    blk = ATT_BLK
    assert blk >= REL_MAX_DIST
    table = rel_bias.astype(F32).T[:, :, None]

    def lookup(rel):
        bucket = _t5_bucket(rel)
        out = jnp.zeros((table.shape[0],) + rel.shape, F32)
        for i in range(REL_BUCKETS):
            out = jnp.where(bucket == i, table[:, i], out)
        return out

    far = lookup(jnp.full((1,), -2 * blk))
    j = jnp.arange(2 * blk)
    dist = jnp.where(j < blk, -j, 2 * blk - j)
    diag = (lookup(dist) - far) * LOG2E
    near = (lookup(dist - blk) - far) * LOG2E
    return jnp.stack([diag, near], axis=1)[:, :, None, :]


def _group_sum_matrix():
    i = jnp.arange(MXU_DIM)
    return ((i[:, None] // 64) == (i[None, :] // 64)).astype(BF16)


def _chunk_tri_matrix(n):
    i = jnp.arange(n)
    return (((i[:, None] // CHUNK) == (i[None, :] // CHUNK)) & (i[None, :] <= i[:, None])).astype(BF16)


def _pad_heads(w, heads, dim, pad, axis):
    shape = list(w.shape)
    shape[axis:axis + 1] = [heads, dim]
    w = w.reshape(shape)
    widths = [(0, 0)] * w.ndim
    widths[axis + 1] = (0, pad - dim)
    w = jnp.pad(w, widths)
    shape[axis:axis + 2] = [heads * pad]
    return w.reshape(shape)


def _tile_gain(g, reps, scale=1.0):
    return (jnp.tile(g.astype(F32), reps) * scale)[None, :]


def _mem_block_diag(kn, v, b):
    m = kn.shape[0] // b
    eye = jnp.eye(MEM_HEADS, dtype=BF16)
    knt = kn.reshape(b, m, MEM_WIDTH).transpose(0, 2, 1)
    kbd = (knt.reshape(b, MEM_HEADS, MEM_HEAD_DIM, 1, m) * eye.reshape(1, MEM_HEADS, 1, MEM_HEADS, 1))
    kbd = kbd.reshape(b, MEM_WIDTH, MEM_HEADS * m)
    vbd = (v.reshape(b, 1, m, MEM_HEADS, MEM_HEAD_DIM) * eye.reshape(1, MEM_HEADS, 1, MEM_HEADS, 1))
    vbd = vbd.reshape(b, MEM_HEADS * m, MEM_WIDTH)
    return kbd, vbd


def kernel(x, mem, rel_bias, attn_norm, ffn_norm, mem_norm, w_in_diff, diff_qk_norm, diff_lambda,
           diff_out_norm, w_in_gla, gla_gate_w, gla_gate_b, gla_out_norm, w_mem_kv, mem_qk_norm,
           w_out, w_up, conv_w, conv_b, w_down):
    b, s, d = x.shape
    t = b * s
    tw = TOKEN_WIDTH
    gsum = _group_sum_matrix()
    mem2 = mem.reshape(b * mem.shape[1], d)
    x = x.astype(F32)
    w_in_diff, w_mem_kv, w_out, w_up, w_down = map(_to_bf16, (w_in_diff, w_mem_kv, w_out, w_up, w_down))

    for i in range(DEPTH):
        j = i // 2
        x2 = x.reshape(t, d)
        mq_gain = _tile_gain(mem_qk_norm[i, 0], MEM_HEADS, MEM_HEAD_DIM ** -0.5)
        if i % 2 == 0:
            nh, hd = DIFF_HEADS, DIFF_HEAD_DIM
            qt, k, vt, mq = _pre_diff(
                x2, attn_norm[i][None, :], w_in_diff, j, gsum,
                _tile_gain(diff_qk_norm[j, 0], 2 * nh, hd ** -0.5 * LOG2E),
                _tile_gain(diff_qk_norm[j, 1], 2 * nh), mq_gain, b)
            lv = diff_lambda[j].astype(F32)
            lam_init = 0.8 - 0.6 * math.exp(-0.3 * i)
            lam = jnp.exp(jnp.sum(lv[0] * lv[1])) - jnp.exp(jnp.sum(lv[2] * lv[3])) + lam_init
            bvec = _bias_vectors(rel_bias)
            qk_bound = (hd ** 0.5 * LOG2E * ATT_ROUNDING_SLACK
                        * jnp.max(jnp.abs(diff_qk_norm[j, 0])) * jnp.max(jnp.abs(diff_qk_norm[j, 1]))).astype(F32)
            hi = qk_bound + jnp.maximum(jnp.max(bvec), 0.0)
            lo = -qk_bound + jnp.minimum(jnp.min(bvec), 0.0)
            scalars = jnp.stack([lam.astype(F32), hi]).reshape(1, 2)
            attend = lambda bounded: functools.partial(
                _diff_attn, qt=qt, k=k.reshape(b, s, tw), vt=vt, bias=bvec,
                g=diff_out_norm[j].astype(F32)[None, :], out_scale=1.0 - lam_init, bounded=bounded)
            mix = lax.cond(hi - lo <= ATT_MAX_EXP2_SPAN, attend(True), attend(False), scalars)
        else:
            kw = GLA_HEADS * GLA_K_DIM
            w = w_in_gla[j]
            hp = functools.partial(_pad_heads, heads=GLA_HEADS, axis=1)
            w_p = jnp.concatenate([
                hp(w[:, :kw], dim=GLA_K_DIM, pad=GLA_KP),
                hp(w[:, kw:2 * kw], dim=GLA_K_DIM, pad=GLA_KP),
                w[:, 2 * kw:2 * kw + 2 * tw],
                jnp.pad(w[:, 2 * kw + 2 * tw:2 * kw + 2 * tw + GLA_GATE_RANK],
                        ((0, 0), (0, LANES - GLA_GATE_RANK))),
                w[:, 2 * kw + 2 * tw + GLA_GATE_RANK:]], axis=1).astype(BF16)
            gw = jnp.pad(hp(gla_gate_w[j], dim=GLA_K_DIM, pad=GLA_KP),
                         ((0, LANES - GLA_GATE_RANK), (0, 0))).astype(BF16)
            gb = _pad_heads(gla_gate_b[j].astype(F32)[None, :], GLA_HEADS, GLA_K_DIM, GLA_KP, 1)
            q, k, v, r, gc, mq = _pre_gla(x2, attn_norm[i][None, :], w_p, gsum, mq_gain, gw, gb,
                                          _chunk_tri_matrix(MXU_DIM))
            gain = _pad_heads(jnp.tile(gla_out_norm[j].astype(F32), GLA_HEADS)[None, :],
                              GLA_HEADS, GLA_V_DIM, GLA_VP, 1)
            sh = lambda a: a.reshape(b, s, a.shape[1])
            mix = _gla(sh(q), sh(k), sh(v), sh(r), sh(gc), gain)

        kn, vm = _mem_kv(mem2, mem_norm[i][None, :], w_mem_kv, i, gsum,
                         _tile_gain(mem_qk_norm[i, 1], MEM_HEADS))
        kbd, vbd = _mem_block_diag(kn, vm, b)
        x = _mix_out(x, mix.reshape(b, s, -1), mq.reshape(b, s, MEM_WIDTH), kbd, vbd,
                     w_out, _layer_spec(w_out, i, rows=(0, tw)), w_out, _layer_spec(w_out, i, rows=(tw, MEM_WIDTH)))
        x = _ffn(x, ffn_norm[i][None, :], w_up, conv_w[i].astype(F32), conv_b[i].astype(F32)[None, :], w_down, i)
    return x
```

```python
import functools
import math

import jax
import jax.numpy as jnp
from jax import lax
from jax.experimental import pallas as pl
from jax.experimental.pallas import tpu as pltpu

F32 = jnp.float32
BF16 = jnp.bfloat16

D_MODEL = 1024
DEPTH = 2
CHUNK = 64
MEM_WIDTH = D_MODEL // 4
MEM_HEADS = 4
MEM_HEAD_DIM = MEM_WIDTH // MEM_HEADS
TOKEN_WIDTH = D_MODEL - MEM_WIDTH
DIFF_HEAD_DIM = 64
DIFF_HEADS = TOKEN_WIDTH // (2 * DIFF_HEAD_DIM)
GLA_HEADS = 4
GLA_V_DIM = TOKEN_WIDTH // GLA_HEADS
GLA_K_DIM = GLA_V_DIM // 2
GLA_GATE_RANK = 16
GLA_GATE_TAU = 16.0
REL_BUCKETS = 32
REL_MAX_DIST = 128
D_FF = ((8 * D_MODEL // 3 + 127) // 128) * 128
EPS = 1e-6
LOG2E = math.log2(math.e)

LANES = 128
MXU_DIM = 256
VMEM_LIMIT_BYTES = 56 * 1024 * 1024

ROW_TILE = 1024
MIX_TILE = 1024
ATT_BLK = 512
ATT_STREAMS = 4
ATT_MAX_EXP2_SPAN = 100.0
ATT_ROUNDING_SLACK = 1.02
ATT_V_ROWS = 2 * DIFF_HEAD_DIM + 16
GLA_TILE = 256
FFN_TILE = 1024
FFN_COLS = 256
GLA_KP = 128
GLA_VP = 256
CARRY_ROWS = 8
CAST_BLOCK_BYTES = 4 * 1024 * 1024


def _cparams(*sem):
    return pltpu.CompilerParams(dimension_semantics=sem, vmem_limit_bytes=VMEM_LIMIT_BYTES)


def _const_spec(shape):
    n = len(shape)
    return pl.BlockSpec(shape, lambda *_: (0,) * n)


def _layer_spec(w_all, layer, rows=None, single=False):
    first, n_rows = (0, w_all.shape[1]) if rows is None else rows
    assert first % n_rows == 0
    mode = dict(pipeline_mode=pl.Buffered(1)) if single else {}
    return pl.BlockSpec((None, n_rows, w_all.shape[2]), lambda *_: (layer, first // n_rows, 0), **mode)


def _cast_kernel(w_ref, o_ref):
    o_ref[...] = w_ref[...].astype(BF16)


def _to_bf16(w):
    n, r, c = w.shape
    rows = max(rb for rb in range(16, r + 1, 16) if r % rb == 0 and rb * c * 4 <= CAST_BLOCK_BYTES)
    spec = pl.BlockSpec((1, rows, c), lambda a, i: (a, i, 0))
    return pl.pallas_call(
        _cast_kernel, grid=(n, r // rows), in_specs=[spec], out_specs=spec,
        out_shape=jax.ShapeDtypeStruct(w.shape, BF16),
        compiler_params=_cparams("parallel", "parallel"), name="to_bf16",
    )(w.astype(F32))


def _rms_rows(x, g):
    ms = jnp.mean(x * x, axis=-1, keepdims=True)
    return x * lax.rsqrt(ms + EPS) * g


def _group_rms(t, gsum, gain):
    cols = []
    for c in range(t.shape[1] // MXU_DIM):
        blk = t[:, c * MXU_DIM:(c + 1) * MXU_DIM]
        ss = jnp.dot((blk * blk).astype(BF16), gsum, preferred_element_type=F32)
        cols.append(blk * lax.rsqrt(ss * (1.0 / 64) + EPS))
    out = cols[0] if len(cols) == 1 else jnp.concatenate(cols, axis=1)
    return out * gain


def _dot(a, b):
    return jnp.dot(a, b, preferred_element_type=F32)


def _dot_nt(a, b):
    return lax.dot_general(a, b, (((1,), (1,)), ((), ())), preferred_element_type=F32)


def _pre_diff_kernel(x_ref, g_ref, w_ref, gsum_ref, qg_ref, kg_ref, mg_ref,
                     qt_ref, k_ref, vt_ref, mq_ref):
    tw = TOKEN_WIDTH
    hw = 2 * DIFF_HEAD_DIM
    h = _rms_rows(x_ref[...], g_ref[...]).astype(BF16)
    gsum = gsum_ref[...]
    q = _group_rms(_dot(h, w_ref[:, 0:tw]), gsum, qg_ref[...])
    k_ref[...] = _group_rms(_dot(h, w_ref[:, tw:2 * tw]), gsum, kg_ref[...]).astype(BF16)
    v = _dot(h, w_ref[:, 2 * tw:3 * tw])
    mq_ref[...] = _group_rms(_dot(h, w_ref[:, 3 * tw:]), gsum, mg_ref[...]).astype(BF16)
    ones = jnp.ones((ATT_V_ROWS - hw, ATT_BLK), BF16)
    for j in range(ROW_TILE // ATT_BLK):
        rows = slice(j * ATT_BLK, (j + 1) * ATT_BLK)
        for n in range(DIFF_HEADS):
            qt_ref[0, n, j] = q[rows, n * hw:(n + 1) * hw].T.astype(BF16)
            vt_ref[0, n, j, 0:hw, :] = v[rows, n * hw:(n + 1) * hw].T.astype(BF16)
            vt_ref[0, n, j, hw:, :] = ones


def _pre_diff(x2, g, w_all, layer, gsum, qg, kg, mg, b):
    t = x2.shape[0]
    tw = TOKEN_WIDTH
    hw = 2 * DIFF_HEAD_DIM
    per_tile = ROW_TILE // ATT_BLK
    nq = t // b // ATT_BLK
    tiles = nq // per_tile
    row = lambda n: pl.BlockSpec((ROW_TILE, n), lambda i: (i, 0))
    per_head = lambda r: pl.BlockSpec((1, DIFF_HEADS, per_tile, r, ATT_BLK),
                                      lambda i: (i // tiles, 0, i % tiles, 0, 0))
    return pl.pallas_call(
        _pre_diff_kernel,
        grid=(t // ROW_TILE,),
        in_specs=[row(D_MODEL), _const_spec((1, D_MODEL)), _layer_spec(w_all, layer),
                  _const_spec(gsum.shape), _const_spec((1, tw)), _const_spec((1, tw)),
                  _const_spec((1, MEM_WIDTH))],
        out_specs=[per_head(hw), row(tw), per_head(ATT_V_ROWS), row(MEM_WIDTH)],
        out_shape=[jax.ShapeDtypeStruct((b, DIFF_HEADS, nq, hw, ATT_BLK), BF16),
                   jax.ShapeDtypeStruct((t, tw), BF16),
                   jax.ShapeDtypeStruct((b, DIFF_HEADS, nq, ATT_V_ROWS, ATT_BLK), BF16),
                   jax.ShapeDtypeStruct((t, MEM_WIDTH), BF16)],
        compiler_params=_cparams("parallel"),
        name="pre_diff",
    )(x2, g, w_all, gsum, qg, kg, mg)


def _diff_attn_kernel(lam_ref, qt_ref, k_ref, vt_ref, bvec_ref, g_ref, o_ref,
                      bias_scr, *scratch, out_scale):
    blk = ATT_BLK
    hd = DIFF_HEAD_DIM
    ns = ATT_STREAMS
    qp_scr, s_scr, cm_scr, p_scr, a_scr, m_scr, acc_scr = (scratch[i * ns:(i + 1) * ns] for i in range(7))
    w = 2 * blk // ns
    qi = pl.program_id(2)
    qt = qt_ref[0, 0, 0]
    row = lax.broadcasted_iota(jnp.int32, qt.shape, 0)
    zero = jnp.zeros_like(qt)
    q_maps = (jnp.where(row < hd, qt, zero), jnp.where(row >= hd, qt, zero))
    cols = [slice((x % (ns // 2)) * w, (x % (ns // 2) + 1) * w) for x in range(ns)]
    for x in range(ns):
        qp_scr[x][...] = q_maps[x // (ns // 2)][:, cols[x]]
        m_scr[x][...] = jnp.full(m_scr[x].shape, -jnp.inf, F32)
        acc_scr[x][...] = jnp.zeros(acc_scr[x].shape, F32)

    @pl.when(qi == 0)
    def _():
        kk = lax.broadcasted_iota(jnp.int32, (blk, blk), 0)
        qq = lax.broadcasted_iota(jnp.int32, (blk, blk), 1)
        visible = (kk // CHUNK) <= (qq // CHUNK)
        for i in range(2):
            rows = jnp.broadcast_to(bvec_ref[0, i], (blk, 2 * blk))
            tile = pltpu.roll(rows, 0, 1, stride=1, stride_axis=0)[:, :blk]
            bias_scr[i] = jnp.where(visible, tile, -jnp.inf) if i == 0 else tile

    def key_block(t):
        return jnp.maximum(qi - t, 0)

    def logits(x, t, bias=None):
        start = pl.multiple_of(key_block(t) * blk, blk)
        s = _dot(k_ref[0, pl.ds(start, blk), :], qp_scr[x][...])
        if bias is not None:
            s = s + bias[:, cols[x]]
        s_scr[x][...] = s
        part = s[0:8]
        for r in range(8, blk, 8):
            part = jnp.maximum(part, s[r:r + 8])
        cm_scr[x][...] = part

    def softmax(x):
        m_old = m_scr[x][...]
        m_new = jnp.maximum(m_old, jnp.max(cm_scr[x][...], axis=0, keepdims=True))
        a_scr[x][...] = jnp.exp2(m_old - m_new)
        m_scr[x][...] = m_new
        for r in range(0, blk, 16):
            p_scr[x][r:r + 16, :] = jnp.exp2((s_scr[x][r:r + 16, :] - m_new).astype(BF16))

    def values(x, t):
        acc_scr[x][...] = a_scr[x][...] * acc_scr[x][...] + _dot(vt_ref[0, 0, key_block(t)], p_scr[x][...])

    def step(t, bias=None):
        for x in range(ns):
            logits(x, t + 1, bias)
            values(x, t)
            softmax((x + 1) % ns)

    for x in range(ns):
        logits(x, 0, bias_scr[0])
    softmax(0)
    step(0, bias_scr[1] + jnp.where(qi == 0, -jnp.inf, 0.0).astype(F32))

    t_last = jnp.maximum(qi, 1)

    def far_pair(u, c):
        step(1 + 2 * u)
        step(2 + 2 * u)
        return c

    lax.fori_loop(0, (t_last - 1) // 2, far_pair, 0)

    @pl.when((t_last - 1) % 2 == 1)
    def _():
        step(t_last - 1)

    for x in range(ns):
        values(x, t_last)
        if x + 1 < ns:
            softmax(x + 1)

    lam = lam_ref[0, 0]
    half = ns // 2
    outs = []
    for x in range(half):
        o0 = acc_scr[x][0:2 * hd, :] * (1.0 / acc_scr[x][2 * hd:2 * hd + 1, :])
        o1 = acc_scr[half + x][0:2 * hd, :] * (1.0 / acc_scr[half + x][2 * hd:2 * hd + 1, :])
        outs.append(o0 - lam * o1)
    ot = outs[0] if half == 1 else jnp.concatenate(outs, axis=1)
    ms = jnp.mean(ot * ot, axis=0, keepdims=True)
    y = (ot * lax.rsqrt(ms + EPS)).T * (g_ref[...] * out_scale)
    o_ref[0] = y.astype(BF16)


def _diff_attn_bounded_kernel(lam_ref, qt_ref, k_ref, vt_ref, bvec_ref, g_ref, o_ref,
                              bias_scr, *scratch, out_scale):
    blk = ATT_BLK
    hd = DIFF_HEAD_DIM
    ns = ATT_STREAMS
    nq = qt_ref.shape[2]
    qp_scr, p0_scr, p1_scr, acc_scr, l_scr = (scratch[i * ns:(i + 1) * ns] for i in range(5))
    p_scr = (p0_scr, p1_scr)
    w = 2 * blk // ns
    lam = lam_ref[0, 0]
    m_ref = lam_ref[0, 1]
    cols = [slice((x % (ns // 2)) * w, (x % (ns // 2) + 1) * w) for x in range(ns)]

    kk = lax.broadcasted_iota(jnp.int32, (blk, blk), 0)
    qq = lax.broadcasted_iota(jnp.int32, (blk, blk), 1)
    visible = (kk // CHUNK) <= (qq // CHUNK)
    for i in range(2):
        rows = jnp.broadcast_to(bvec_ref[0, i], (blk, 2 * blk))
        tile = pltpu.roll(rows, 0, 1, stride=1, stride_axis=0)[:, :blk]
        bias_scr[i] = jnp.where(visible, tile, -jnp.inf) if i == 0 else tile

    def key_block(qi, t):
        return jnp.maximum(qi - t, 0)

    diag_keys = [(x % (ns // 2) + 1) * w for x in range(ns)]
    assert w % CHUNK == 0

    def probs(qi, x, t, bias=None, keys=blk):
        start = pl.multiple_of(key_block(qi, t) * blk, blk)
        s = _dot(k_ref[0, pl.ds(start, keys), :], qp_scr[x][...])
        if bias is not None:
            s = s + bias[0:keys, cols[x]]
        p = jnp.exp2(s - m_ref)
        l_scr[x][...] += jnp.sum(p.reshape(keys // 8, 8, w), axis=0)
        return p.astype(BF16)

    def step(qi, t, slot, bias=None, keys_t=None):
        for x in range(ns):
            p_scr[slot][x][...] = probs(qi, x, t + 1, bias)
            n = blk if keys_t is None else keys_t[x]
            acc_scr[x][...] += _dot(vt_ref[0, 0, key_block(qi, t), 0:2 * hd, 0:n], p_scr[1 - slot][x][0:n, :])

    def begin(qi):
        qt = qt_ref[0, 0, qi]
        row = lax.broadcasted_iota(jnp.int32, qt.shape, 0)
        zero = jnp.zeros_like(qt)
        q_maps = (jnp.where(row < hd, qt, zero), jnp.where(row >= hd, qt, zero))
        for x in range(ns):
            qp_scr[x][...] = q_maps[x // (ns // 2)][:, cols[x]]
            acc_scr[x][...] = jnp.zeros(acc_scr[x].shape, F32)
            l_scr[x][...] = jnp.zeros(l_scr[x].shape, F32)
        for x in range(ns):
            p0_scr[x][0:diag_keys[x], :] = probs(qi, x, 0, bias_scr[0], diag_keys[x])
        step(qi, 0, 1, bias_scr[1] + jnp.where(qi == 0, -jnp.inf, 0.0).astype(F32), diag_keys)

    def sweep(qi):
        n_far = qi - 1

        def far_steps(n):
            def body(u, c):
                for i in range(n):
                    step(qi, 1 + n * u + i, i % 2)
                return c
            return body

        n_quads = n_far // 4
        lax.fori_loop(0, n_quads, far_steps(4), 0)
        t_done = 1 + 4 * n_quads

        @pl.when(n_far % 4 >= 2)
        def _():
            step(qi, t_done, 0)
            step(qi, t_done + 1, 1)

        @pl.when(n_far % 2 == 1)
        def _():
            step(qi, n_far, 0)

    def finish(qi):
        t_last = jnp.maximum(qi, 1)
        last_in_p1 = t_last % 2 == 1
        for x in range(ns):
            p_last = jnp.where(last_in_p1, p1_scr[x][...], p0_scr[x][...])
            acc_scr[x][...] += _dot(vt_ref[0, 0, key_block(qi, t_last), 0:2 * hd, :], p_last)
        half = ns // 2
        inv = [1.0 / jnp.sum(l_scr[x][...], axis=0, keepdims=True) for x in range(ns)]
        outs = []
        for x in range(half):
            outs.append(acc_scr[x][...] * inv[x] - lam * (acc_scr[half + x][...] * inv[half + x]))
        ot = outs[0] if half == 1 else jnp.concatenate(outs, axis=1)
        ms = jnp.mean(ot * ot, axis=0, keepdims=True)
        y = (ot * lax.rsqrt(ms + EPS)).T * (g_ref[...] * out_scale)
        o_ref[0, pl.ds(pl.multiple_of(qi * blk, blk), blk), :] = y.astype(BF16)

    begin(0)

    def query_block(qi, c):
        finish(qi - 1)
        begin(qi)
        sweep(qi)
        return c

    lax.fori_loop(1, nq, query_block, 0)
    finish(nq - 1)


def _diff_attn(lam, qt, k, vt, bias, g, out_scale, bounded):
    b, nh, nq = qt.shape[0], qt.shape[1], qt.shape[2]
    s = k.shape[1]
    blk = ATT_BLK
    ns = ATT_STREAMS
    w = 2 * blk // ns
    hw = 2 * DIFF_HEAD_DIM
    if bounded:
        body = _diff_attn_bounded_kernel
        per_stream = (((hw, w), BF16), ((blk, w), BF16), ((blk, w), BF16), ((hw, w), F32), ((8, w), F32))
        grid = (b, nh)
        q_spec = pl.BlockSpec((1, 1, nq, hw, blk), lambda bi, hi: (bi, hi, 0, 0, 0))
        o_spec = pl.BlockSpec((1, s, hw), lambda bi, hi: (bi, 0, hi))
        sem = ("parallel", "parallel")
    else:
        body = _diff_attn_kernel
        per_stream = (((hw, w), BF16), ((blk, w), F32), ((8, w), F32), ((blk, w), BF16),
                      ((1, w), F32), ((1, w), F32), ((ATT_V_ROWS, w), F32))
        grid = (b, nh, nq)
        q_spec = pl.BlockSpec((1, 1, 1, hw, blk), lambda bi, hi, qi: (bi, hi, qi, 0, 0))
        o_spec = pl.BlockSpec((1, blk, hw), lambda bi, hi, qi: (bi, qi, hi))
        sem = ("parallel", "parallel", "arbitrary")
    return pl.pallas_call(
        functools.partial(body, out_scale=out_scale),
        grid=grid,
        in_specs=[
            pl.BlockSpec(memory_space=pltpu.SMEM),
            q_spec,
            pl.BlockSpec((1, s, hw), lambda bi, hi, *_: (bi, 0, hi)),
            pl.BlockSpec((1, 1, nq, ATT_V_ROWS, blk), lambda bi, hi, *_: (bi, hi, 0, 0, 0)),
            pl.BlockSpec((1, 2, 1, 2 * blk), lambda bi, hi, *_: (hi, 0, 0, 0)),
            _const_spec((1, hw)),
        ],
        out_specs=o_spec,
        out_shape=jax.ShapeDtypeStruct((b, s, nh * hw), BF16),
        scratch_shapes=[pltpu.VMEM((2, blk, blk), F32)] + [pltpu.VMEM(shape, dtype)
                                                        for shape, dtype in per_stream for _ in range(ns)],
        compiler_params=_cparams(*sem),
        name="diff_attn_bounded" if bounded else "diff_attn",
    )(lam, qt, k, vt, bias, g)


def _pre_gla_kernel(x_ref, g_ref, w_ref, gsum_ref, mg_ref, gw_ref, gb_ref, tri_ref,
                    q_ref, k_ref, v_ref, r_ref, gc_ref, mq_ref):
    kw = GLA_HEADS * GLA_KP
    vw = TOKEN_WIDTH
    h = _rms_rows(x_ref[...], g_ref[...]).astype(BF16)
    q_ref[...] = _dot(h, w_ref[:, 0:kw]).astype(BF16)
    k_ref[...] = _dot(h, w_ref[:, kw:2 * kw]).astype(BF16)
    o = 2 * kw
    v_ref[...] = _dot(h, w_ref[:, o:o + vw]).astype(BF16)
    r_ref[...] = _dot(h, w_ref[:, o + vw:o + 2 * vw]).astype(BF16)
    o = o + 2 * vw
    gate_low = _dot(h, w_ref[:, o:o + LANES]).astype(BF16)
    mq_ref[...] = _group_rms(_dot(h, w_ref[:, o + LANES:]), gsum_ref[...], mg_ref[...]).astype(BF16)
    z = _dot(gate_low, gw_ref[...]) + gb_ref[...]
    log_a = (jnp.minimum(z, 0.0) - jnp.log1p(jnp.exp(-jnp.abs(z)))) * (1.0 / GLA_GATE_TAU)
    hi = log_a.astype(BF16)
    rem = log_a - hi.astype(F32)
    mid = rem.astype(BF16)
    lo = (rem - mid.astype(F32)).astype(BF16)
    tri = tri_ref[...]
    n = tri.shape[0]
    for c in range(log_a.shape[0] // n):
        rows = slice(c * n, (c + 1) * n)
        gc_ref[rows, :] = _dot(tri, hi[rows]) + _dot(tri, mid[rows]) + _dot(tri, lo[rows])


def _pre_gla(x2, g, w, gsum, mg, gw, gb, tri):
    t = x2.shape[0]
    kw = GLA_HEADS * GLA_KP
    vw = TOKEN_WIDTH
    row = lambda n: pl.BlockSpec((ROW_TILE, n), lambda i: (i, 0))
    return pl.pallas_call(
        _pre_gla_kernel,
        grid=(t // ROW_TILE,),
        in_specs=[row(D_MODEL), _const_spec((1, D_MODEL)), _const_spec(w.shape),
                  _const_spec(gsum.shape), _const_spec((1, MEM_WIDTH)), _const_spec(gw.shape),
                  _const_spec(gb.shape), _const_spec(tri.shape)],
        out_specs=[row(kw), row(kw), row(vw), row(vw), row(kw), row(MEM_WIDTH)],
        out_shape=[jax.ShapeDtypeStruct((t, kw), BF16), jax.ShapeDtypeStruct((t, kw), BF16),
                   jax.ShapeDtypeStruct((t, vw), BF16), jax.ShapeDtypeStruct((t, vw), BF16),
                   jax.ShapeDtypeStruct((t, kw), F32), jax.ShapeDtypeStruct((t, MEM_WIDTH), BF16)],
        compiler_params=_cparams("parallel"),
        name="pre_gla",
    )(x2, g, w, gsum, mg, gw, gb, tri)


def _gla_kernel(q_ref, k_ref, v_ref, r_ref, gc_ref, gain_ref, o_ref, s_scr):
    tg = GLA_TILE
    nchunk = tg // CHUNK
    heads = range(GLA_HEADS)
    ks = [slice(h * GLA_KP, (h + 1) * GLA_KP) for h in heads]
    vs = [slice(h * GLA_VP, (h + 1) * GLA_VP) for h in heads]
    pad_lane = lax.broadcasted_iota(jnp.int32, (tg, GLA_VP), 1) >= GLA_V_DIM

    def head_cols(ref, h):
        first = h * GLA_V_DIM
        aligned = first // LANES * LANES
        win = ref[0, :, aligned:aligned + GLA_VP].astype(F32)
        if first != aligned:
            win = pltpu.roll(win, GLA_VP - (first - aligned), 1)
        return jnp.where(pad_lane, 0.0, win)

    @pl.when(pl.program_id(1) == 0)
    def _():
        s_scr[...] = jnp.zeros(s_scr.shape, F32)

    ri = lax.broadcasted_iota(jnp.int32, (tg, tg), 0)
    ci = lax.broadcasted_iota(jnp.int32, (tg, tg), 1)
    same_chunk = (ri // CHUNK) == (ci // CHUNK)
    past = ci <= ri
    row_chunk = lax.broadcasted_iota(jnp.int32, (tg, GLA_KP), 0) // CHUNK

    qe, scores, kv, decay, v_pad = [], [], [], [], []
    for h in heads:
        qh = q_ref[0, :, ks[h]].astype(F32) * (GLA_K_DIM ** -0.5)
        kh = k_ref[0, :, ks[h]].astype(F32)
        g = gc_ref[0, :, ks[h]]
        eg = jnp.exp(g)
        ieg = jnp.exp(-g)
        qe.append((qh * eg).astype(BF16))
        a_past = _dot_nt(qe[h], (kh * ieg).astype(BF16))
        a_fut = _dot_nt((qh * ieg).astype(BF16), (kh * eg).astype(BF16))
        scores.append(jnp.where(same_chunk, jnp.where(past, a_past, a_fut), 0.0).astype(BF16))
        v_pad.append(head_cols(v_ref, h))
        vt = v_pad[h].T.astype(BF16)
        g_last = [g[c * CHUNK + CHUNK - 1:c * CHUNK + CHUNK, :] for c in range(nchunk)]
        g_end = jnp.concatenate([jnp.broadcast_to(gl, (CHUNK, GLA_KP)) for gl in g_last], axis=0)
        kdec = kh * jnp.exp(g_end - g)
        kd_chunks = jnp.concatenate([jnp.where(row_chunk == c, kdec, 0.0) for c in range(nchunk)], axis=1)
        kv_all = _dot(vt, kd_chunks.astype(BF16))
        kv.append([kv_all[:, c * GLA_KP:(c + 1) * GLA_KP] for c in range(nchunk)])
        decay.append([jnp.exp(gl) for gl in g_last])

    starts = []
    for h in heads:
        st = s_scr[h]
        per_chunk = []
        for c in range(nchunk):
            per_chunk.append(st.astype(BF16))
            st = st * decay[h][c] + kv[h][c]
        s_scr[h] = st
        starts.append(per_chunk)

    gated = []
    for h in heads:
        inter = [_dot_nt(qe[h][c * CHUNK:(c + 1) * CHUNK], starts[h][c]) for c in range(nchunk)]
        o = _dot(scores[h], v_pad[h].astype(BF16)) + jnp.concatenate(inter, axis=0)
        ms = jnp.sum(o * o, axis=-1, keepdims=True) * (1.0 / GLA_V_DIM)
        y = o * lax.rsqrt(ms + EPS) * gain_ref[:, vs[h]]
        half_r = 0.5 * head_cols(r_ref, h)
        gated.append(y * half_r * (1.0 + jnp.tanh(half_r)))

    pair = 2 * GLA_V_DIM
    blank = jnp.zeros((tg, GLA_VP), F32)
    for p in range(GLA_HEADS // 2):
        even = jnp.concatenate([gated[2 * p], blank], axis=1)
        odd = pltpu.roll(jnp.concatenate([gated[2 * p + 1], blank], axis=1), GLA_V_DIM, 1)
        o_ref[0, :, p * pair:(p + 1) * pair] = (even + odd)[:, :pair].astype(BF16)


def _gla(q, k, v, r, gc, gain):
    b, s = q.shape[0], q.shape[1]
    kw = GLA_HEADS * GLA_KP
    vw = GLA_HEADS * GLA_VP
    spec = lambda n: pl.BlockSpec((1, GLA_TILE, n), lambda bi, i: (bi, i, 0))
    return pl.pallas_call(
        _gla_kernel,
        grid=(b, s // GLA_TILE),
        in_specs=[spec(kw), spec(kw), spec(TOKEN_WIDTH), spec(TOKEN_WIDTH), spec(kw), _const_spec((1, vw))],
        out_specs=spec(TOKEN_WIDTH),
        out_shape=jax.ShapeDtypeStruct((b, s, TOKEN_WIDTH), BF16),
        scratch_shapes=[pltpu.VMEM((GLA_HEADS, GLA_VP, GLA_KP), F32)],
        compiler_params=_cparams("parallel", "arbitrary"),
        name="gla",
    )(q, k, v, r, gc, gain)


def _mem_kv_kernel(mem_ref, g_ref, w_ref, gsum_ref, kg_ref, k_ref, v_ref):
    h = _rms_rows(mem_ref[...], g_ref[...]).astype(BF16)
    k_ref[...] = _group_rms(_dot(h, w_ref[:, :MEM_WIDTH]), gsum_ref[...], kg_ref[...]).astype(BF16)
    v_ref[...] = _dot(h, w_ref[:, MEM_WIDTH:]).astype(BF16)


def _mem_kv(mem2, g, w_all, layer, gsum, kg):
    n = mem2.shape[0]
    return pl.pallas_call(
        _mem_kv_kernel,
        grid=(1,),
        in_specs=[_const_spec(mem2.shape), _const_spec((1, D_MODEL)), _layer_spec(w_all, layer),
                  _const_spec(gsum.shape), _const_spec((1, MEM_WIDTH))],
        out_specs=[_const_spec((n, MEM_WIDTH)), _const_spec((n, MEM_WIDTH))],
        out_shape=[jax.ShapeDtypeStruct((n, MEM_WIDTH), BF16)] * 2,
        compiler_params=_cparams("arbitrary"),
        name="mem_kv",
    )(mem2, g, w_all, gsum, kg)


def _mix_out_kernel(x_ref, mix_ref, mq_ref, kbd_ref, vbd_ref, wa_ref, wb_ref, o_ref):
    m = kbd_ref.shape[2] // MEM_HEADS
    logits = _dot(mq_ref[0], kbd_ref[0])
    ps = []
    for h in range(MEM_HEADS):
        s = logits[:, h * m:(h + 1) * m]
        e = jnp.exp(s - jnp.max(s, axis=-1, keepdims=True))
        ps.append((e * (1.0 / jnp.sum(e, axis=-1, keepdims=True))).astype(BF16))
    cross = _dot(jnp.concatenate(ps, axis=1), vbd_ref[0])
    o_ref[0] = x_ref[0] + _dot(mix_ref[0], wa_ref[...]) + _dot(cross.astype(BF16), wb_ref[...])


def _mix_out(x, mix, mq, kbd, vbd, wa, wa_spec, wb, wb_spec):
    b, s = x.shape[0], x.shape[1]
    spec = lambda n: pl.BlockSpec((1, MIX_TILE, n), lambda bi, i: (bi, i, 0))
    per_b = lambda a: pl.BlockSpec((1,) + a.shape[1:], lambda bi, i: (bi, 0, 0))
    return pl.pallas_call(
        _mix_out_kernel,
        grid=(b, s // MIX_TILE),
        in_specs=[spec(D_MODEL), spec(mix.shape[2]), spec(MEM_WIDTH), per_b(kbd), per_b(vbd),
                  wa_spec, wb_spec],
        out_specs=spec(D_MODEL),
        out_shape=jax.ShapeDtypeStruct(x.shape, F32),
        compiler_params=_cparams("parallel", "parallel"),
        name="mix_out",
    )(x, mix, mq, kbd, vbd, wa, wb)


def _ffn_kernel(x_ref, g_ref, wu_ref, cw_ref, cb_ref, wd_ref, o_ref, carry_scr, act_scr, *shift_scr):
    tm = FFN_TILE
    cr = CARRY_ROWS

    @pl.when(pl.program_id(1) == 0)
    def _():
        carry_scr[...] = jnp.zeros(carry_scr.shape, F32)

    x = x_ref[0]
    h = _rms_rows(x, g_ref[...]).astype(BF16)

    def conv(cols, bufs):
        u = _dot(h, wu_ref[:, cols])
        prev = carry_scr[:, cols]
        for shift, buf in zip((1, 2), bufs):
            buf[shift:shift + cr, :] = prev
            buf[cr + shift:cr + shift + tm, :] = u
        carry_scr[:, cols] = u[tm - cr:tm]
        return (cw_ref[0:1, cols] * bufs[1][cr:cr + tm, :] + cw_ref[1:2, cols] * bufs[0][cr:cr + tm, :]
                + cw_ref[2:3, cols] * u + cb_ref[:, cols])

    for j in range(D_FF // FFN_COLS):
        bufs = shift_scr[4 * (j % 2):4 * (j % 2) + 4]
        a = conv(slice(j * FFN_COLS, (j + 1) * FFN_COLS), bufs[0:2])
        half_g = 0.5 * conv(slice(D_FF + j * FFN_COLS, D_FF + (j + 1) * FFN_COLS), bufs[2:4])
        act_scr[:, j * FFN_COLS:(j + 1) * FFN_COLS] = (a * half_g * (1.0 + jnp.tanh(half_g))).astype(BF16)

    o_ref[0] = x + _dot(act_scr[...], wd_ref[...])


def _ffn(x, g, wu_all, cw, cb, wd_all, layer):
    b, s = x.shape[0], x.shape[1]
    spec = pl.BlockSpec((1, FFN_TILE, D_MODEL), lambda bi, i: (bi, i, 0))
    return pl.pallas_call(
        _ffn_kernel,
        grid=(b, s // FFN_TILE),
        in_specs=[spec, _const_spec((1, D_MODEL)), _layer_spec(wu_all, layer, single=True), _const_spec(cw.shape),
                  _const_spec(cb.shape), _layer_spec(wd_all, layer, single=True)],
        out_specs=spec,
        out_shape=jax.ShapeDtypeStruct(x.shape, F32),
        scratch_shapes=[pltpu.VMEM((CARRY_ROWS, 2 * D_FF), F32), pltpu.VMEM((FFN_TILE, D_FF), BF16)]
        + [pltpu.VMEM((FFN_TILE + 2 * CARRY_ROWS, FFN_COLS), F32)] * 8,
        compiler_params=_cparams("parallel", "arbitrary"),
        name="ffn",
    )(x, g, wu_all, cw, cb, wd_all)


def _t5_bucket(rel):
    half = REL_BUCKETS // 2
    max_exact = half // 2
    ret = jnp.where(rel > 0, half, 0)
    n = jnp.abs(rel)
    nf = jnp.maximum(n, 1).astype(jnp.float32)
    large = max_exact + (jnp.log(nf / max_exact) / math.log(REL_MAX_DIST / max_exact)
                         * (half - max_exact)).astype(jnp.int32)
    large = jnp.minimum(large, half - 1)
    return ret + jnp.where(n < max_exact, n, large)


def _bias_vectors(rel_bias):
    blk = ATT_BLK
    assert blk >= REL_MAX_DIST
    table = rel_bias.astype(F32).T[:, :, None]

    def lookup(rel):
        bucket = _t5_bucket(rel)
        out = jnp.zeros((table.shape[0],) + rel.shape, F32)
        for i in range(REL_BUCKETS):
            out = jnp.where(bucket == i, table[:, i], out)
        return out

    far = lookup(jnp.full((1,), -2 * blk))
    j = jnp.arange(2 * blk)
    dist = jnp.where(j < blk, -j, 2 * blk - j)
    diag = (lookup(dist) - far) * LOG2E
    near = (lookup(dist - blk) - far) * LOG2E
    return jnp.stack([diag, near], axis=1)[:, :, None, :]


def _group_sum_matrix():
    i = jnp.arange(MXU_DIM)
    return ((i[:, None] // 64) == (i[None, :] // 64)).astype(BF16)


def _chunk_tri_matrix(n):
    i = jnp.arange(n)
    return (((i[:, None] // CHUNK) == (i[None, :] // CHUNK)) & (i[None, :] <= i[:, None])).astype(BF16)


def _pad_heads(w, heads, dim, pad, axis):
    shape = list(w.shape)
    shape[axis:axis + 1] = [heads, dim]
    w = w.reshape(shape)
    widths = [(0, 0)] * w.ndim
    widths[axis + 1] = (0, pad - dim)
    w = jnp.pad(w, widths)
    shape[axis:axis + 2] = [heads * pad]
    return w.reshape(shape)


def _tile_gain(g, reps, scale=1.0):
    return (jnp.tile(g.astype(F32), reps) * scale)[None, :]


def _mem_block_diag(kn, v, b):
    m = kn.shape[0] // b
    eye = jnp.eye(MEM_HEADS, dtype=BF16)
    knt = kn.reshape(b, m, MEM_WIDTH).transpose(0, 2, 1)
    kbd = (knt.reshape(b, MEM_HEADS, MEM_HEAD_DIM, 1, m) * eye.reshape(1, MEM_HEADS, 1, MEM_HEADS, 1))
    kbd = kbd.reshape(b, MEM_WIDTH, MEM_HEADS * m)
    vbd = (v.reshape(b, 1, m, MEM_HEADS, MEM_HEAD_DIM) * eye.reshape(1, MEM_HEADS, 1, MEM_HEADS, 1))
    vbd = vbd.reshape(b, MEM_HEADS * m, MEM_WIDTH)
    return kbd, vbd


def kernel(x, mem, rel_bias, attn_norm, ffn_norm, mem_norm, w_in_diff, diff_qk_norm, diff_lambda,
           diff_out_norm, w_in_gla, gla_gate_w, gla_gate_b, gla_out_norm, w_mem_kv, mem_qk_norm,
           w_out, w_up, conv_w, conv_b, w_down):
    b, s, d = x.shape
    t = b * s
    tw = TOKEN_WIDTH
    gsum = _group_sum_matrix()
    mem2 = mem.reshape(b * mem.shape[1], d)
    x = x.astype(F32)
    w_in_diff, w_mem_kv, w_out, w_up, w_down = map(_to_bf16, (w_in_diff, w_mem_kv, w_out, w_up, w_down))

    for i in range(DEPTH):
        j = i // 2
        x2 = x.reshape(t, d)
        mq_gain = _tile_gain(mem_qk_norm[i, 0], MEM_HEADS, MEM_HEAD_DIM ** -0.5)
        if i % 2 == 0:
            nh, hd = DIFF_HEADS, DIFF_HEAD_DIM
            qt, k, vt, mq = _pre_diff(
                x2, attn_norm[i][None, :], w_in_diff, j, gsum,
                _tile_gain(diff_qk_norm[j, 0], 2 * nh, hd ** -0.5 * LOG2E),
                _tile_gain(diff_qk_norm[j, 1], 2 * nh), mq_gain, b)
            lv = diff_lambda[j].astype(F32)
            lam_init = 0.8 - 0.6 * math.exp(-0.3 * i)
            lam = jnp.exp(jnp.sum(lv[0] * lv[1])) - jnp.exp(jnp.sum(lv[2] * lv[3])) + lam_init
            bvec = _bias_vectors(rel_bias)
            qk_bound = (hd ** 0.5 * LOG2E * ATT_ROUNDING_SLACK
                        * jnp.max(jnp.abs(diff_qk_norm[j, 0])) * jnp.max(jnp.abs(diff_qk_norm[j, 1]))).astype(F32)
            hi = qk_bound + jnp.maximum(jnp.max(bvec), 0.0)
            lo = -qk_bound + jnp.minimum(jnp.min(bvec), 0.0)
            scalars = jnp.stack([lam.astype(F32), hi]).reshape(1, 2)
            attend = lambda bounded: functools.partial(
                _diff_attn, qt=qt, k=k.reshape(b, s, tw), vt=vt, bias=bvec,
                g=diff_out_norm[j].astype(F32)[None, :], out_scale=1.0 - lam_init, bounded=bounded)
            mix = lax.cond(hi - lo <= ATT_MAX_EXP2_SPAN, attend(True), attend(False), scalars)
        else:
            kw = GLA_HEADS * GLA_K_DIM
            w = w_in_gla[j]
            hp = functools.partial(_pad_heads, heads=GLA_HEADS, axis=1)
            w_p = jnp.concatenate([
                hp(w[:, :kw], dim=GLA_K_DIM, pad=GLA_KP),
                hp(w[:, kw:2 * kw], dim=GLA_K_DIM, pad=GLA_KP),
                w[:, 2 * kw:2 * kw + 2 * tw],
                jnp.pad(w[:, 2 * kw + 2 * tw:2 * kw + 2 * tw + GLA_GATE_RANK],
                        ((0, 0), (0, LANES - GLA_GATE_RANK))),
                w[:, 2 * kw + 2 * tw + GLA_GATE_RANK:]], axis=1).astype(BF16)
            gw = jnp.pad(hp(gla_gate_w[j], dim=GLA_K_DIM, pad=GLA_KP),
                         ((0, LANES - GLA_GATE_RANK), (0, 0))).astype(BF16)
            gb = _pad_heads(gla_gate_b[j].astype(F32)[None, :], GLA_HEADS, GLA_K_DIM, GLA_KP, 1)
            q, k, v, r, gc, mq = _pre_gla(x2, attn_norm[i][None, :], w_p, gsum, mq_gain, gw, gb,
                                          _chunk_tri_matrix(MXU_DIM))
            gain = _pad_heads(jnp.tile(gla_out_norm[j].astype(F32), GLA_HEADS)[None, :],
                              GLA_HEADS, GLA_V_DIM, GLA_VP, 1)
            sh = lambda a: a.reshape(b, s, a.shape[1])
            mix = _gla(sh(q), sh(k), sh(v), sh(r), sh(gc), gain)

        kn, vm = _mem_kv(mem2, mem_norm[i][None, :], w_mem_kv, i, gsum,
                         _tile_gain(mem_qk_norm[i, 1], MEM_HEADS))
        kbd, vbd = _mem_block_diag(kn, vm, b)
        x = _mix_out(x, mix.reshape(b, s, -1), mq.reshape(b, s, MEM_WIDTH), kbd, vbd,
                     w_out, _layer_spec(w_out, i, rows=(0, tw)), w_out, _layer_spec(w_out, i, rows=(tw, MEM_WIDTH)))
        x = _ffn(x, ffn_norm[i][None, :], w_up, conv_w[i].astype(F32), conv_b[i].astype(F32)[None, :], w_down, i)
    return x
```

```python
import functools
import math

import jax
import jax.numpy as jnp
from jax import lax
from jax.experimental import pallas as pl
from jax.experimental.pallas import tpu as pltpu

F32 = jnp.float32
BF16 = jnp.bfloat16

D_MODEL = 1024
DEPTH = 2
CHUNK = 64
MEM_WIDTH = D_MODEL // 4
MEM_HEADS = 4
MEM_HEAD_DIM = MEM_WIDTH // MEM_HEADS
TOKEN_WIDTH = D_MODEL - MEM_WIDTH
DIFF_HEAD_DIM = 64
DIFF_HEADS = TOKEN_WIDTH // (2 * DIFF_HEAD_DIM)
GLA_HEADS = 4
GLA_V_DIM = TOKEN_WIDTH // GLA_HEADS
GLA_K_DIM = GLA_V_DIM // 2
GLA_GATE_RANK = 16
GLA_GATE_TAU = 16.0
REL_BUCKETS = 32
REL_MAX_DIST = 128
D_FF = ((8 * D_MODEL // 3 + 127) // 128) * 128
EPS = 1e-6
LOG2E = math.log2(math.e)

LANES = 128
MXU_DIM = 256
VMEM_LIMIT_BYTES = 56 * 1024 * 1024

ROW_TILE = 1024
MIX_TILE = 1024
ATT_BLK = 512
ATT_STREAMS = 4
ATT_MAX_EXP2_SPAN = 100.0
ATT_ROUNDING_SLACK = 1.02
ATT_V_ROWS = 2 * DIFF_HEAD_DIM + 16
GLA_TILE = 256
FFN_TILE = 1024
FFN_COLS = 256
GLA_KP = 128
GLA_VP = 256
CARRY_ROWS = 8
CAST_BLOCK_BYTES = 4 * 1024 * 1024


def _cparams(*sem):
    return pltpu.CompilerParams(dimension_semantics=sem, vmem_limit_bytes=VMEM_LIMIT_BYTES)


def _const_spec(shape):
    n = len(shape)
    return pl.BlockSpec(shape, lambda *_: (0,) * n)


def _layer_spec(w_all, layer, rows=None, single=False):
    first, n_rows = (0, w_all.shape[1]) if rows is None else rows
    assert first % n_rows == 0
    mode = dict(pipeline_mode=pl.Buffered(1)) if single else {}
    return pl.BlockSpec((None, n_rows, w_all.shape[2]), lambda *_: (layer, first // n_rows, 0), **mode)


def _cast_kernel(w_ref, o_ref):
    o_ref[...] = w_ref[...].astype(BF16)


def _to_bf16(w):
    n, r, c = w.shape
    rows = max(rb for rb in range(16, r + 1, 16) if r % rb == 0 and rb * c * 4 <= CAST_BLOCK_BYTES)
    spec = pl.BlockSpec((1, rows, c), lambda a, i: (a, i, 0))
    return pl.pallas_call(
        _cast_kernel, grid=(n, r // rows), in_specs=[spec], out_specs=spec,
        out_shape=jax.ShapeDtypeStruct(w.shape, BF16),
        compiler_params=_cparams("parallel", "parallel"), name="to_bf16",
    )(w.astype(F32))


def _rms_rows(x, g):
    ms = jnp.mean(x * x, axis=-1, keepdims=True)
    return x * lax.rsqrt(ms + EPS) * g


def _group_rms(t, gsum, gain):
    cols = []
    for c in range(t.shape[1] // MXU_DIM):
        blk = t[:, c * MXU_DIM:(c + 1) * MXU_DIM]
        ss = jnp.dot((blk * blk).astype(BF16), gsum, preferred_element_type=F32)
        cols.append(blk * lax.rsqrt(ss * (1.0 / 64) + EPS))
    out = cols[0] if len(cols) == 1 else jnp.concatenate(cols, axis=1)
    return out * gain


def _dot(a, b):
    return jnp.dot(a, b, preferred_element_type=F32)


def _dot_nt(a, b):
    return lax.dot_general(a, b, (((1,), (1,)), ((), ())), preferred_element_type=F32)


def _pre_diff_kernel(x_ref, g_ref, w_ref, gsum_ref, qg_ref, kg_ref, mg_ref,
                     qt_ref, k_ref, vt_ref, mq_ref):
    tw = TOKEN_WIDTH
    hw = 2 * DIFF_HEAD_DIM
    h = _rms_rows(x_ref[...], g_ref[...]).astype(BF16)
    gsum = gsum_ref[...]
    q = _group_rms(_dot(h, w_ref[:, 0:tw]), gsum, qg_ref[...])
    k_ref[...] = _group_rms(_dot(h, w_ref[:, tw:2 * tw]), gsum, kg_ref[...]).astype(BF16)
    v = _dot(h, w_ref[:, 2 * tw:3 * tw])
    mq_ref[...] = _group_rms(_dot(h, w_ref[:, 3 * tw:]), gsum, mg_ref[...]).astype(BF16)
    ones = jnp.ones((ATT_V_ROWS - hw, ATT_BLK), BF16)
    for j in range(ROW_TILE // ATT_BLK):
        rows = slice(j * ATT_BLK, (j + 1) * ATT_BLK)
        for n in range(DIFF_HEADS):
            qt_ref[0, n, j] = q[rows, n * hw:(n + 1) * hw].T.astype(BF16)
            vt_ref[0, n, j, 0:hw, :] = v[rows, n * hw:(n + 1) * hw].T.astype(BF16)
            vt_ref[0, n, j, hw:, :] = ones


def _pre_diff(x2, g, w_all, layer, gsum, qg, kg, mg, b):
    t = x2.shape[0]
    tw = TOKEN_WIDTH
    hw = 2 * DIFF_HEAD_DIM
    per_tile = ROW_TILE // ATT_BLK
    nq = t // b // ATT_BLK
    tiles = nq // per_tile
    row = lambda n: pl.BlockSpec((ROW_TILE, n), lambda i: (i, 0))
    per_head = lambda r: pl.BlockSpec((1, DIFF_HEADS, per_tile, r, ATT_BLK),
                                      lambda i: (i // tiles, 0, i % tiles, 0, 0))
    return pl.pallas_call(
        _pre_diff_kernel,
        grid=(t // ROW_TILE,),
        in_specs=[row(D_MODEL), _const_spec((1, D_MODEL)), _layer_spec(w_all, layer),
                  _const_spec(gsum.shape), _const_spec((1, tw)), _const_spec((1, tw)),
                  _const_spec((1, MEM_WIDTH))],
        out_specs=[per_head(hw), row(tw), per_head(ATT_V_ROWS), row(MEM_WIDTH)],
        out_shape=[jax.ShapeDtypeStruct((b, DIFF_HEADS, nq, hw, ATT_BLK), BF16),
                   jax.ShapeDtypeStruct((t, tw), BF16),
                   jax.ShapeDtypeStruct((b, DIFF_HEADS, nq, ATT_V_ROWS, ATT_BLK), BF16),
                   jax.ShapeDtypeStruct((t, MEM_WIDTH), BF16)],
        compiler_params=_cparams("parallel"),
        name="pre_diff",
    )(x2, g, w_all, gsum, qg, kg, mg)


def _diff_attn_kernel(lam_ref, qt_ref, k_ref, vt_ref, bvec_ref, g_ref, o_ref,
                      bias_scr, *scratch, out_scale):
    blk = ATT_BLK
    hd = DIFF_HEAD_DIM
    ns = ATT_STREAMS
    qp_scr, s_scr, cm_scr, p_scr, a_scr, m_scr, acc_scr = (scratch[i * ns:(i + 1) * ns] for i in range(7))
    w = 2 * blk // ns
    qi = pl.program_id(2)
    qt = qt_ref[0, 0, 0]
    row = lax.broadcasted_iota(jnp.int32, qt.shape, 0)
    zero = jnp.zeros_like(qt)
    q_maps = (jnp.where(row < hd, qt, zero), jnp.where(row >= hd, qt, zero))
    cols = [slice((x % (ns // 2)) * w, (x % (ns // 2) + 1) * w) for x in range(ns)]
    for x in range(ns):
        qp_scr[x][...] = q_maps[x // (ns // 2)][:, cols[x]]
        m_scr[x][...] = jnp.full(m_scr[x].shape, -jnp.inf, F32)
        acc_scr[x][...] = jnp.zeros(acc_scr[x].shape, F32)

    @pl.when(qi == 0)
    def _():
        kk = lax.broadcasted_iota(jnp.int32, (blk, blk), 0)
        qq = lax.broadcasted_iota(jnp.int32, (blk, blk), 1)
        visible = (kk // CHUNK) <= (qq // CHUNK)
        for i in range(2):
            rows = jnp.broadcast_to(bvec_ref[0, i], (blk, 2 * blk))
            tile = pltpu.roll(rows, 0, 1, stride=1, stride_axis=0)[:, :blk]
            bias_scr[i] = jnp.where(visible, tile, -jnp.inf) if i == 0 else tile

    def key_block(t):
        return jnp.maximum(qi - t, 0)

    def logits(x, t, bias=None):
        start = pl.multiple_of(key_block(t) * blk, blk)
        s = _dot(k_ref[0, pl.ds(start, blk), :], qp_scr[x][...])
        if bias is not None:
            s = s + bias[:, cols[x]]
        s_scr[x][...] = s
        part = s[0:8]
        for r in range(8, blk, 8):
            part = jnp.maximum(part, s[r:r + 8])
        cm_scr[x][...] = part

    def softmax(x):
        m_old = m_scr[x][...]
        m_new = jnp.maximum(m_old, jnp.max(cm_scr[x][...], axis=0, keepdims=True))
        a_scr[x][...] = jnp.exp2(m_old - m_new)
        m_scr[x][...] = m_new
        for r in range(0, blk, 16):
            p_scr[x][r:r + 16, :] = jnp.exp2((s_scr[x][r:r + 16, :] - m_new).astype(BF16))

    def values(x, t):
        acc_scr[x][...] = a_scr[x][...] * acc_scr[x][...] + _dot(vt_ref[0, 0, key_block(t)], p_scr[x][...])

    def step(t, bias=None):
        for x in range(ns):
            logits(x, t + 1, bias)
            values(x, t)
            softmax((x + 1) % ns)

    for x in range(ns):
        logits(x, 0, bias_scr[0])
    softmax(0)
    step(0, bias_scr[1] + jnp.where(qi == 0, -jnp.inf, 0.0).astype(F32))

    t_last = jnp.maximum(qi, 1)

    def far_pair(u, c):
        step(1 + 2 * u)
        step(2 + 2 * u)
        return c

    lax.fori_loop(0, (t_last - 1) // 2, far_pair, 0)

    @pl.when((t_last - 1) % 2 == 1)
    def _():
        step(t_last - 1)

    for x in range(ns):
        values(x, t_last)
        if x + 1 < ns:
            softmax(x + 1)

    lam = lam_ref[0, 0]
    half = ns // 2
    outs = []
    for x in range(half):
        o0 = acc_scr[x][0:2 * hd, :] * (1.0 / acc_scr[x][2 * hd:2 * hd + 1, :])
        o1 = acc_scr[half + x][0:2 * hd, :] * (1.0 / acc_scr[half + x][2 * hd:2 * hd + 1, :])
        outs.append(o0 - lam * o1)
    ot = outs[0] if half == 1 else jnp.concatenate(outs, axis=1)
    ms = jnp.mean(ot * ot, axis=0, keepdims=True)
    y = (ot * lax.rsqrt(ms + EPS)).T * (g_ref[...] * out_scale)
    o_ref[0] = y.astype(BF16)


def _diff_attn_bounded_kernel(lam_ref, qt_ref, k_ref, vt_ref, bvec_ref, g_ref, o_ref,
                              bias_scr, *scratch, out_scale):
    blk = ATT_BLK
    hd = DIFF_HEAD_DIM
    ns = ATT_STREAMS
    nq = qt_ref.shape[2]
    qp_scr, p0_scr, p1_scr, acc_scr, l_scr = (scratch[i * ns:(i + 1) * ns] for i in range(5))
    p_scr = (p0_scr, p1_scr)
    w = 2 * blk // ns
    lam = lam_ref[0, 0]
    m_ref = lam_ref[0, 1]
    cols = [slice((x % (ns // 2)) * w, (x % (ns // 2) + 1) * w) for x in range(ns)]

    kk = lax.broadcasted_iota(jnp.int32, (blk, blk), 0)
    qq = lax.broadcasted_iota(jnp.int32, (blk, blk), 1)
    visible = (kk // CHUNK) <= (qq // CHUNK)
    for i in range(2):
        rows = jnp.broadcast_to(bvec_ref[0, i], (blk, 2 * blk))
        tile = pltpu.roll(rows, 0, 1, stride=1, stride_axis=0)[:, :blk]
        bias_scr[i] = jnp.where(visible, tile, -jnp.inf) if i == 0 else tile

    def key_block(qi, t):
        return jnp.maximum(qi - t, 0)

    diag_keys = [(x % (ns // 2) + 1) * w for x in range(ns)]
    assert w % CHUNK == 0

    def probs(qi, x, t, bias=None, keys=blk):
        start = pl.multiple_of(key_block(qi, t) * blk, blk)
        s = _dot(k_ref[0, pl.ds(start, keys), :], qp_scr[x][...])
        if bias is not None:
            s = s + bias[0:keys, cols[x]]
        p = jnp.exp2(s - m_ref)
        l_scr[x][...] += jnp.sum(p.reshape(keys // 8, 8, w), axis=0)
        return p.astype(BF16)

    def step(qi, t, slot, bias=None, keys_t=None):
        for x in range(ns):
            p_scr[slot][x][...] = probs(qi, x, t + 1, bias)
            n = blk if keys_t is None else keys_t[x]
            acc_scr[x][...] += _dot(vt_ref[0, 0, key_block(qi, t), 0:2 * hd, 0:n], p_scr[1 - slot][x][0:n, :])

    def begin(qi):
        qt = qt_ref[0, 0, qi]
        row = lax.broadcasted_iota(jnp.int32, qt.shape, 0)
        zero = jnp.zeros_like(qt)
        q_maps = (jnp.where(row < hd, qt, zero), jnp.where(row >= hd, qt, zero))
        for x in range(ns):
            qp_scr[x][...] = q_maps[x // (ns // 2)][:, cols[x]]
            acc_scr[x][...] = jnp.zeros(acc_scr[x].shape, F32)
            l_scr[x][...] = jnp.zeros(l_scr[x].shape, F32)
        for x in range(ns):
            p0_scr[x][0:diag_keys[x], :] = probs(qi, x, 0, bias_scr[0], diag_keys[x])
        step(qi, 0, 1, bias_scr[1] + jnp.where(qi == 0, -jnp.inf, 0.0).astype(F32), diag_keys)

    def sweep(qi):
        n_far = qi - 1

        def far_steps(n):
            def body(u, c):
                for i in range(n):
                    step(qi, 1 + n * u + i, i % 2)
                return c
            return body

        n_quads = n_far // 4
        lax.fori_loop(0, n_quads, far_steps(4), 0)
        t_done = 1 + 4 * n_quads

        @pl.when(n_far % 4 >= 2)
        def _():
            step(qi, t_done, 0)
            step(qi, t_done + 1, 1)

        @pl.when(n_far % 2 == 1)
        def _():
            step(qi, n_far, 0)

    def finish(qi):
        t_last = jnp.maximum(qi, 1)
        last_in_p1 = t_last % 2 == 1
        for x in range(ns):
            p_last = jnp.where(last_in_p1, p1_scr[x][...], p0_scr[x][...])
            acc_scr[x][...] += _dot(vt_ref[0, 0, key_block(qi, t_last), 0:2 * hd, :], p_last)
        half = ns // 2
        inv = [1.0 / jnp.sum(l_scr[x][...], axis=0, keepdims=True) for x in range(ns)]
        outs = []
        for x in range(half):
            outs.append(acc_scr[x][...] * inv[x] - lam * (acc_scr[half + x][...] * inv[half + x]))
        ot = outs[0] if half == 1 else jnp.concatenate(outs, axis=1)
        ms = jnp.mean(ot * ot, axis=0, keepdims=True)
        y = (ot * lax.rsqrt(ms + EPS)).T * (g_ref[...] * out_scale)
        o_ref[0, pl.ds(pl.multiple_of(qi * blk, blk), blk), :] = y.astype(BF16)

    begin(0)

    def query_block(qi, c):
        finish(qi - 1)
        begin(qi)
        sweep(qi)
        return c

    lax.fori_loop(1, nq, query_block, 0)
    finish(nq - 1)


def _diff_attn(lam, qt, k, vt, bias, g, out_scale, bounded):
    b, nh, nq = qt.shape[0], qt.shape[1], qt.shape[2]
    s = k.shape[1]
    blk = ATT_BLK
    ns = ATT_STREAMS
    w = 2 * blk // ns
    hw = 2 * DIFF_HEAD_DIM
    if bounded:
        body = _diff_attn_bounded_kernel
        per_stream = (((hw, w), BF16), ((blk, w), BF16), ((blk, w), BF16), ((hw, w), F32), ((8, w), F32))
        grid = (b, nh)
        q_spec = pl.BlockSpec((1, 1, nq, hw, blk), lambda bi, hi: (bi, hi, 0, 0, 0))
        o_spec = pl.BlockSpec((1, s, hw), lambda bi, hi: (bi, 0, hi))
        sem = ("parallel", "parallel")
    else:
        body = _diff_attn_kernel
        per_stream = (((hw, w), BF16), ((blk, w), F32), ((8, w), F32), ((blk, w), BF16),
                      ((1, w), F32), ((1, w), F32), ((ATT_V_ROWS, w), F32))
        grid = (b, nh, nq)
        q_spec = pl.BlockSpec((1, 1, 1, hw, blk), lambda bi, hi, qi: (bi, hi, qi, 0, 0))
        o_spec = pl.BlockSpec((1, blk, hw), lambda bi, hi, qi: (bi, qi, hi))
        sem = ("parallel", "parallel", "arbitrary")
    return pl.pallas_call(
        functools.partial(body, out_scale=out_scale),
        grid=grid,
        in_specs=[
            pl.BlockSpec(memory_space=pltpu.SMEM),
            q_spec,
            pl.BlockSpec((1, s, hw), lambda bi, hi, *_: (bi, 0, hi)),
            pl.BlockSpec((1, 1, nq, ATT_V_ROWS, blk), lambda bi, hi, *_: (bi, hi, 0, 0, 0)),
            pl.BlockSpec((1, 2, 1, 2 * blk), lambda bi, hi, *_: (hi, 0, 0, 0)),
            _const_spec((1, hw)),
        ],
        out_specs=o_spec,
        out_shape=jax.ShapeDtypeStruct((b, s, nh * hw), BF16),
        scratch_shapes=[pltpu.VMEM((2, blk, blk), F32)] + [pltpu.VMEM(shape, dtype)
                                                        for shape, dtype in per_stream for _ in range(ns)],
        compiler_params=_cparams(*sem),
        name="diff_attn_bounded" if bounded else "diff_attn",
    )(lam, qt, k, vt, bias, g)


def _pre_gla_kernel(x_ref, g_ref, w_ref, gsum_ref, mg_ref, gw_ref, gb_ref, tri_ref,
                    q_ref, k_ref, v_ref, r_ref, gc_ref, mq_ref):
    kw = GLA_HEADS * GLA_KP
    vw = TOKEN_WIDTH
    h = _rms_rows(x_ref[...], g_ref[...]).astype(BF16)
    q_ref[...] = _dot(h, w_ref[:, 0:kw]).astype(BF16)
    k_ref[...] = _dot(h, w_ref[:, kw:2 * kw]).astype(BF16)
    o = 2 * kw
    v_ref[...] = _dot(h, w_ref[:, o:o + vw]).astype(BF16)
    r_ref[...] = _dot(h, w_ref[:, o + vw:o + 2 * vw]).astype(BF16)
    o = o + 2 * vw
    gate_low = _dot(h, w_ref[:, o:o + LANES]).astype(BF16)
    mq_ref[...] = _group_rms(_dot(h, w_ref[:, o + LANES:]), gsum_ref[...], mg_ref[...]).astype(BF16)
    z = _dot(gate_low, gw_ref[...]) + gb_ref[...]
    log_a = (jnp.minimum(z, 0.0) - jnp.log1p(jnp.exp(-jnp.abs(z)))) * (1.0 / GLA_GATE_TAU)
    hi = log_a.astype(BF16)
    lo = (log_a - hi.astype(F32)).astype(BF16)
    tri = tri_ref[...]
    n = tri.shape[0]
    for c in range(log_a.shape[0] // n):
        rows = slice(c * n, (c + 1) * n)
        gc_ref[rows, :] = _dot(tri, hi[rows]) + _dot(tri, lo[rows])


def _pre_gla(x2, g, w, gsum, mg, gw, gb, tri):
    t = x2.shape[0]
    kw = GLA_HEADS * GLA_KP
    vw = TOKEN_WIDTH
    row = lambda n: pl.BlockSpec((ROW_TILE, n), lambda i: (i, 0))
    return pl.pallas_call(
        _pre_gla_kernel,
        grid=(t // ROW_TILE,),
        in_specs=[row(D_MODEL), _const_spec((1, D_MODEL)), _const_spec(w.shape),
                  _const_spec(gsum.shape), _const_spec((1, MEM_WIDTH)), _const_spec(gw.shape),
                  _const_spec(gb.shape), _const_spec(tri.shape)],
        out_specs=[row(kw), row(kw), row(vw), row(vw), row(kw), row(MEM_WIDTH)],
        out_shape=[jax.ShapeDtypeStruct((t, kw), BF16), jax.ShapeDtypeStruct((t, kw), BF16),
                   jax.ShapeDtypeStruct((t, vw), BF16), jax.ShapeDtypeStruct((t, vw), BF16),
                   jax.ShapeDtypeStruct((t, kw), F32), jax.ShapeDtypeStruct((t, MEM_WIDTH), BF16)],
        compiler_params=_cparams("parallel"),
        name="pre_gla",
    )(x2, g, w, gsum, mg, gw, gb, tri)


def _gla_kernel(q_ref, k_ref, v_ref, r_ref, gc_ref, gain_ref, o_ref, s_scr):
    tg = GLA_TILE
    nchunk = tg // CHUNK
    heads = range(GLA_HEADS)
    ks = [slice(h * GLA_KP, (h + 1) * GLA_KP) for h in heads]
    vs = [slice(h * GLA_VP, (h + 1) * GLA_VP) for h in heads]
    pad_lane = lax.broadcasted_iota(jnp.int32, (tg, GLA_VP), 1) >= GLA_V_DIM

    def head_cols(ref, h):
        first = h * GLA_V_DIM
        aligned = first // LANES * LANES
        win = ref[0, :, aligned:aligned + GLA_VP].astype(F32)
        if first != aligned:
            win = pltpu.roll(win, GLA_VP - (first - aligned), 1)
        return jnp.where(pad_lane, 0.0, win)

    @pl.when(pl.program_id(1) == 0)
    def _():
        s_scr[...] = jnp.zeros(s_scr.shape, F32)

    ri = lax.broadcasted_iota(jnp.int32, (tg, tg), 0)
    ci = lax.broadcasted_iota(jnp.int32, (tg, tg), 1)
    same_chunk = (ri // CHUNK) == (ci // CHUNK)
    past = ci <= ri
    row_chunk = lax.broadcasted_iota(jnp.int32, (tg, GLA_KP), 0) // CHUNK

    qe, scores, kv, decay, v_pad = [], [], [], [], []
    for h in heads:
        qh = q_ref[0, :, ks[h]].astype(F32) * (GLA_K_DIM ** -0.5)
        kh = k_ref[0, :, ks[h]].astype(F32)
        g = gc_ref[0, :, ks[h]]
        eg = jnp.exp(g)
        ieg = jnp.exp(-g)
        qe.append((qh * eg).astype(BF16))
        a_past = _dot_nt(qe[h], (kh * ieg).astype(BF16))
        a_fut = _dot_nt((qh * ieg).astype(BF16), (kh * eg).astype(BF16))
        scores.append(jnp.where(same_chunk, jnp.where(past, a_past, a_fut), 0.0).astype(BF16))
        v_pad.append(head_cols(v_ref, h))
        vt = v_pad[h].T.astype(BF16)
        g_last = [g[c * CHUNK + CHUNK - 1:c * CHUNK + CHUNK, :] for c in range(nchunk)]
        g_end = jnp.concatenate([jnp.broadcast_to(gl, (CHUNK, GLA_KP)) for gl in g_last], axis=0)
        kdec = kh * jnp.exp(g_end - g)
        kd_chunks = jnp.concatenate([jnp.where(row_chunk == c, kdec, 0.0) for c in range(nchunk)], axis=1)
        kv_all = _dot(vt, kd_chunks.astype(BF16))
        kv.append([kv_all[:, c * GLA_KP:(c + 1) * GLA_KP] for c in range(nchunk)])
        decay.append([jnp.exp(gl) for gl in g_last])

    starts = []
    for h in heads:
        st = s_scr[h]
        per_chunk = []
        for c in range(nchunk):
            per_chunk.append(st.astype(BF16))
            st = st * decay[h][c] + kv[h][c]
        s_scr[h] = st
        starts.append(per_chunk)

    gated = []
    for h in heads:
        inter = [_dot_nt(qe[h][c * CHUNK:(c + 1) * CHUNK], starts[h][c]) for c in range(nchunk)]
        o = _dot(scores[h], v_pad[h].astype(BF16)) + jnp.concatenate(inter, axis=0)
        ms = jnp.sum(o * o, axis=-1, keepdims=True) * (1.0 / GLA_V_DIM)
        y = o * lax.rsqrt(ms + EPS) * gain_ref[:, vs[h]]
        half_r = 0.5 * head_cols(r_ref, h)
        gated.append(y * half_r * (1.0 + jnp.tanh(half_r)))

    pair = 2 * GLA_V_DIM
    blank = jnp.zeros((tg, GLA_VP), F32)
    for p in range(GLA_HEADS // 2):
        even = jnp.concatenate([gated[2 * p], blank], axis=1)
        odd = pltpu.roll(jnp.concatenate([gated[2 * p + 1], blank], axis=1), GLA_V_DIM, 1)
        o_ref[0, :, p * pair:(p + 1) * pair] = (even + odd)[:, :pair].astype(BF16)


def _gla(q, k, v, r, gc, gain):
    b, s = q.shape[0], q.shape[1]
    kw = GLA_HEADS * GLA_KP
    vw = GLA_HEADS * GLA_VP
    spec = lambda n: pl.BlockSpec((1, GLA_TILE, n), lambda bi, i: (bi, i, 0))
    return pl.pallas_call(
        _gla_kernel,
        grid=(b, s // GLA_TILE),
        in_specs=[spec(kw), spec(kw), spec(TOKEN_WIDTH), spec(TOKEN_WIDTH), spec(kw), _const_spec((1, vw))],
        out_specs=spec(TOKEN_WIDTH),
        out_shape=jax.ShapeDtypeStruct((b, s, TOKEN_WIDTH), BF16),
        scratch_shapes=[pltpu.VMEM((GLA_HEADS, GLA_VP, GLA_KP), F32)],
        compiler_params=_cparams("parallel", "arbitrary"),
        name="gla",
    )(q, k, v, r, gc, gain)


def _mem_kv_kernel(mem_ref, g_ref, w_ref, gsum_ref, kg_ref, k_ref, v_ref):
    h = _rms_rows(mem_ref[...], g_ref[...]).astype(BF16)
    k_ref[...] = _group_rms(_dot(h, w_ref[:, :MEM_WIDTH]), gsum_ref[...], kg_ref[...]).astype(BF16)
    v_ref[...] = _dot(h, w_ref[:, MEM_WIDTH:]).astype(BF16)


def _mem_kv(mem2, g, w_all, layer, gsum, kg):
    n = mem2.shape[0]
    return pl.pallas_call(
        _mem_kv_kernel,
        grid=(1,),
        in_specs=[_const_spec(mem2.shape), _const_spec((1, D_MODEL)), _layer_spec(w_all, layer),
                  _const_spec(gsum.shape), _const_spec((1, MEM_WIDTH))],
        out_specs=[_const_spec((n, MEM_WIDTH)), _const_spec((n, MEM_WIDTH))],
        out_shape=[jax.ShapeDtypeStruct((n, MEM_WIDTH), BF16)] * 2,
        compiler_params=_cparams("arbitrary"),
        name="mem_kv",
    )(mem2, g, w_all, gsum, kg)


def _mix_out_kernel(x_ref, mix_ref, mq_ref, kbd_ref, vbd_ref, wa_ref, wb_ref, o_ref):
    m = kbd_ref.shape[2] // MEM_HEADS
    logits = _dot(mq_ref[0], kbd_ref[0])
    ps = []
    for h in range(MEM_HEADS):
        s = logits[:, h * m:(h + 1) * m]
        e = jnp.exp(s - jnp.max(s, axis=-1, keepdims=True))
        ps.append((e * (1.0 / jnp.sum(e, axis=-1, keepdims=True))).astype(BF16))
    cross = _dot(jnp.concatenate(ps, axis=1), vbd_ref[0])
    o_ref[0] = x_ref[0] + _dot(mix_ref[0], wa_ref[...]) + _dot(cross.astype(BF16), wb_ref[...])


def _mix_out(x, mix, mq, kbd, vbd, wa, wa_spec, wb, wb_spec):
    b, s = x.shape[0], x.shape[1]
    spec = lambda n: pl.BlockSpec((1, MIX_TILE, n), lambda bi, i: (bi, i, 0))
    per_b = lambda a: pl.BlockSpec((1,) + a.shape[1:], lambda bi, i: (bi, 0, 0))
    return pl.pallas_call(
        _mix_out_kernel,
        grid=(b, s // MIX_TILE),
        in_specs=[spec(D_MODEL), spec(mix.shape[2]), spec(MEM_WIDTH), per_b(kbd), per_b(vbd),
                  wa_spec, wb_spec],
        out_specs=spec(D_MODEL),
        out_shape=jax.ShapeDtypeStruct(x.shape, F32),
        compiler_params=_cparams("parallel", "parallel"),
        name="mix_out",
    )(x, mix, mq, kbd, vbd, wa, wb)


def _ffn_kernel(x_ref, g_ref, wu_ref, cw_ref, cb_ref, wd_ref, o_ref, carry_scr, act_scr, *shift_scr):
    tm = FFN_TILE
    cr = CARRY_ROWS

    @pl.when(pl.program_id(1) == 0)
    def _():
        carry_scr[...] = jnp.zeros(carry_scr.shape, F32)

    x = x_ref[0]
    h = _rms_rows(x, g_ref[...]).astype(BF16)

    def conv(cols, bufs):
        u = _dot(h, wu_ref[:, cols])
        prev = carry_scr[:, cols]
        for shift, buf in zip((1, 2), bufs):
            buf[shift:shift + cr, :] = prev
            buf[cr + shift:cr + shift + tm, :] = u
        carry_scr[:, cols] = u[tm - cr:tm]
        return (cw_ref[0:1, cols] * bufs[1][cr:cr + tm, :] + cw_ref[1:2, cols] * bufs[0][cr:cr + tm, :]
                + cw_ref[2:3, cols] * u + cb_ref[:, cols])

    for j in range(D_FF // FFN_COLS):
        bufs = shift_scr[4 * (j % 2):4 * (j % 2) + 4]
        a = conv(slice(j * FFN_COLS, (j + 1) * FFN_COLS), bufs[0:2])
        half_g = 0.5 * conv(slice(D_FF + j * FFN_COLS, D_FF + (j + 1) * FFN_COLS), bufs[2:4])
        act_scr[:, j * FFN_COLS:(j + 1) * FFN_COLS] = (a * half_g * (1.0 + jnp.tanh(half_g))).astype(BF16)

    o_ref[0] = x + _dot(act_scr[...], wd_ref[...])


def _ffn(x, g, wu_all, cw, cb, wd_all, layer):
    b, s = x.shape[0], x.shape[1]
    spec = pl.BlockSpec((1, FFN_TILE, D_MODEL), lambda bi, i: (bi, i, 0))
    return pl.pallas_call(
        _ffn_kernel,
        grid=(b, s // FFN_TILE),
        in_specs=[spec, _const_spec((1, D_MODEL)), _layer_spec(wu_all, layer, single=True), _const_spec(cw.shape),
                  _const_spec(cb.shape), _layer_spec(wd_all, layer, single=True)],
        out_specs=spec,
        out_shape=jax.ShapeDtypeStruct(x.shape, F32),
        scratch_shapes=[pltpu.VMEM((CARRY_ROWS, 2 * D_FF), F32), pltpu.VMEM((FFN_TILE, D_FF), BF16)]
        + [pltpu.VMEM((FFN_TILE + 2 * CARRY_ROWS, FFN_COLS), F32)] * 8,
        compiler_params=_cparams("parallel", "arbitrary"),
        name="ffn",
    )(x, g, wu_all, cw, cb, wd_all)


def _t5_bucket(rel):
    half = REL_BUCKETS // 2
    max_exact = half // 2
    ret = jnp.where(rel > 0, half, 0)
    n = jnp.abs(rel)
    nf = jnp.maximum(n, 1).astype(jnp.float32)
    large = max_exact + (jnp.log(nf / max_exact) / math.log(REL_MAX_DIST / max_exact)
                         * (half - max_exact)).astype(jnp.int32)
    large = jnp.minimum(large, half - 1)
    return ret + jnp.where(n < max_exact, n, large)


def _bias_vectors(rel_bias):
    blk = ATT_BLK
    assert blk >= REL_MAX_DIST
    table = rel_bias.astype(F32).T[:, :, None]

    def lookup(rel):
        bucket = _t5_bucket(rel)
        out = jnp.zeros((table.shape[0],) + rel.shape, F32)
        for i in range(REL_BUCKETS):
            out = jnp.where(bucket == i, table[:, i], out)
        return out

    j = jnp.arange(2 * blk)
    dist = jnp.where(j < blk, -j, 2 * blk - j)
    values = lookup(jnp.concatenate([dist, dist - blk, jnp.full((1,), -2 * blk)]))
    vectors = (values[:, :4 * blk] - values[:, 4 * blk:]) * LOG2E
    return vectors.reshape(-1, 2, 1, 2 * blk)


def _group_sum_matrix():
    i = jnp.arange(MXU_DIM)
    return ((i[:, None] // 64) == (i[None, :] // 64)).astype(BF16)


def _chunk_tri_matrix(n):
    i = jnp.arange(n)
    return (((i[:, None] // CHUNK) == (i[None, :] // CHUNK)) & (i[None, :] <= i[:, None])).astype(BF16)


def _pad_heads(w, heads, dim, pad, axis):
    shape = list(w.shape)
    shape[axis:axis + 1] = [heads, dim]
    w = w.reshape(shape)
    widths = [(0, 0)] * w.ndim
    widths[axis + 1] = (0, pad - dim)
    w = jnp.pad(w, widths)
    shape[axis:axis + 2] = [heads * pad]
    return w.reshape(shape)


def _tile_gain(g, reps, scale=1.0):
    return (jnp.tile(g.astype(F32), reps) * scale)[None, :]


def _mem_block_diag(kn, v, b):
    m = kn.shape[0] // b
    eye = jnp.eye(MEM_HEADS, dtype=BF16)
    knt = kn.reshape(b, m, MEM_WIDTH).transpose(0, 2, 1)
    kbd = (knt.reshape(b, MEM_HEADS, MEM_HEAD_DIM, 1, m) * eye.reshape(1, MEM_HEADS, 1, MEM_HEADS, 1))
    kbd = kbd.reshape(b, MEM_WIDTH, MEM_HEADS * m)
    vbd = (v.reshape(b, 1, m, MEM_HEADS, MEM_HEAD_DIM) * eye.reshape(1, MEM_HEADS, 1, MEM_HEADS, 1))
    vbd = vbd.reshape(b, MEM_HEADS * m, MEM_WIDTH)
    return kbd, vbd


def kernel(x, mem, rel_bias, attn_norm, ffn_norm, mem_norm, w_in_diff, diff_qk_norm, diff_lambda,
           diff_out_norm, w_in_gla, gla_gate_w, gla_gate_b, gla_out_norm, w_mem_kv, mem_qk_norm,
           w_out, w_up, conv_w, conv_b, w_down):
    b, s, d = x.shape
    t = b * s
    tw = TOKEN_WIDTH
    gsum = _group_sum_matrix()
    mem2 = mem.reshape(b * mem.shape[1], d)
    x = x.astype(F32)
    w_in_diff, w_mem_kv, w_out, w_up, w_down = map(_to_bf16, (w_in_diff, w_mem_kv, w_out, w_up, w_down))

    for i in range(DEPTH):
        j = i // 2
        x2 = x.reshape(t, d)
        mq_gain = _tile_gain(mem_qk_norm[i, 0], MEM_HEADS, MEM_HEAD_DIM ** -0.5)
        if i % 2 == 0:
            nh, hd = DIFF_HEADS, DIFF_HEAD_DIM
            qt, k, vt, mq = _pre_diff(
                x2, attn_norm[i][None, :], w_in_diff, j, gsum,
                _tile_gain(diff_qk_norm[j, 0], 2 * nh, hd ** -0.5 * LOG2E),
                _tile_gain(diff_qk_norm[j, 1], 2 * nh), mq_gain, b)
            lv = diff_lambda[j].astype(F32)
            lam_init = 0.8 - 0.6 * math.exp(-0.3 * i)
            lam = jnp.exp(jnp.sum(lv[0] * lv[1])) - jnp.exp(jnp.sum(lv[2] * lv[3])) + lam_init
            bvec = _bias_vectors(rel_bias)
            qk_bound = (hd ** 0.5 * LOG2E * ATT_ROUNDING_SLACK
                        * jnp.max(jnp.abs(diff_qk_norm[j, 0])) * jnp.max(jnp.abs(diff_qk_norm[j, 1]))).astype(F32)
            hi = qk_bound + jnp.maximum(jnp.max(bvec), 0.0)
            lo = -qk_bound + jnp.minimum(jnp.min(bvec), 0.0)
            scalars = jnp.stack([lam.astype(F32), hi]).reshape(1, 2)
            attend = lambda bounded: functools.partial(
                _diff_attn, qt=qt, k=k.reshape(b, s, tw), vt=vt, bias=bvec,
                g=diff_out_norm[j].astype(F32)[None, :], out_scale=1.0 - lam_init, bounded=bounded)
            mix = lax.cond(hi - lo <= ATT_MAX_EXP2_SPAN, attend(True), attend(False), scalars)
        else:
            kw = GLA_HEADS * GLA_K_DIM
            w = w_in_gla[j]
            hp = functools.partial(_pad_heads, heads=GLA_HEADS, axis=1)
            w_p = jnp.concatenate([
                hp(w[:, :kw], dim=GLA_K_DIM, pad=GLA_KP),
                hp(w[:, kw:2 * kw], dim=GLA_K_DIM, pad=GLA_KP),
                w[:, 2 * kw:2 * kw + 2 * tw],
                jnp.pad(w[:, 2 * kw + 2 * tw:2 * kw + 2 * tw + GLA_GATE_RANK],
                        ((0, 0), (0, LANES - GLA_GATE_RANK))),
                w[:, 2 * kw + 2 * tw + GLA_GATE_RANK:]], axis=1).astype(BF16)
            gw = jnp.pad(hp(gla_gate_w[j], dim=GLA_K_DIM, pad=GLA_KP),
                         ((0, LANES - GLA_GATE_RANK), (0, 0))).astype(BF16)
            gb = _pad_heads(gla_gate_b[j].astype(F32)[None, :], GLA_HEADS, GLA_K_DIM, GLA_KP, 1)
            q, k, v, r, gc, mq = _pre_gla(x2, attn_norm[i][None, :], w_p, gsum, mq_gain, gw, gb,
                                          _chunk_tri_matrix(MXU_DIM))
            gain = _pad_heads(jnp.tile(gla_out_norm[j].astype(F32), GLA_HEADS)[None, :],
                              GLA_HEADS, GLA_V_DIM, GLA_VP, 1)
            sh = lambda a: a.reshape(b, s, a.shape[1])
            mix = _gla(sh(q), sh(k), sh(v), sh(r), sh(gc), gain)

        kn, vm = _mem_kv(mem2, mem_norm[i][None, :], w_mem_kv, i, gsum,
                         _tile_gain(mem_qk_norm[i, 1], MEM_HEADS))
        kbd, vbd = _mem_block_diag(kn, vm, b)
        x = _mix_out(x, mix.reshape(b, s, -1), mq.reshape(b, s, MEM_WIDTH), kbd, vbd,
                     w_out, _layer_spec(w_out, i, rows=(0, tw)), w_out, _layer_spec(w_out, i, rows=(tw, MEM_WIDTH)))
        x = _ffn(x, ffn_norm[i][None, :], w_up, conv_w[i].astype(F32), conv_b[i].astype(F32)[None, :], w_down, i)
    return x
```

```python
import functools
import math

import jax
import jax.numpy as jnp
from jax import lax
from jax.experimental import pallas as pl
from jax.experimental.pallas import tpu as pltpu

F32 = jnp.float32
BF16 = jnp.bfloat16

D_MODEL = 1024
DEPTH = 2
CHUNK = 64
MEM_WIDTH = D_MODEL // 4
MEM_HEADS = 4
MEM_HEAD_DIM = MEM_WIDTH // MEM_HEADS
TOKEN_WIDTH = D_MODEL - MEM_WIDTH
DIFF_HEAD_DIM = 64
DIFF_HEADS = TOKEN_WIDTH // (2 * DIFF_HEAD_DIM)
GLA_HEADS = 4
GLA_V_DIM = TOKEN_WIDTH // GLA_HEADS
GLA_K_DIM = GLA_V_DIM // 2
GLA_GATE_RANK = 16
GLA_GATE_TAU = 16.0
REL_BUCKETS = 32
REL_MAX_DIST = 128
D_FF = ((8 * D_MODEL // 3 + 127) // 128) * 128
EPS = 1e-6
LOG2E = math.log2(math.e)
NORM_GROUP = DIFF_HEAD_DIM
assert MEM_HEAD_DIM == NORM_GROUP

LANES = 128
MXU_DIM = 256
VMEM_LIMIT_BYTES = 56 * 1024 * 1024

ROW_TILE = 1024
MIX_TILE = 1024
ATT_BLK = 512
ATT_STREAMS = 4
ATT_MAX_EXP2_SPAN = 100.0
ATT_ROUNDING_SLACK = 1.02
ATT_V_ROWS = 2 * DIFF_HEAD_DIM + 16
GLA_TILE = 256
FFN_TILE = 1024
FFN_COLS = 256
GLA_KP = 128
GLA_VP = 256
CARRY_ROWS = 8
CAST_BLOCK_BYTES = 4 * 1024 * 1024


def _cparams(*sem):
    return pltpu.CompilerParams(dimension_semantics=sem, vmem_limit_bytes=VMEM_LIMIT_BYTES)


def _const_spec(shape):
    n = len(shape)
    return pl.BlockSpec(shape, lambda *_: (0,) * n)


def _layer_spec(w_all, layer, rows=None, single=False):
    first, n_rows = (0, w_all.shape[1]) if rows is None else rows
    assert first % n_rows == 0
    mode = dict(pipeline_mode=pl.Buffered(1)) if single else {}
    return pl.BlockSpec((None, n_rows, w_all.shape[2]), lambda *_: (layer, first // n_rows, 0), **mode)


def _cast_kernel(w_ref, o_ref):
    o_ref[...] = w_ref[...].astype(BF16)


def _to_bf16(w):
    n, r, c = w.shape
    rows = max(rb for rb in range(16, r + 1, 16) if r % rb == 0 and rb * c * 4 <= CAST_BLOCK_BYTES)
    spec = pl.BlockSpec((1, rows, c), lambda a, i: (a, i, 0))
    return pl.pallas_call(
        _cast_kernel, grid=(n, r // rows), in_specs=[spec], out_specs=spec,
        out_shape=jax.ShapeDtypeStruct(w.shape, BF16),
        compiler_params=_cparams("parallel", "parallel"), name="to_bf16",
    )(w.astype(F32))


def _rms_rows(x, g):
    ms = jnp.mean(x * x, axis=-1, keepdims=True)
    return x * lax.rsqrt(ms + EPS) * g


def _group_rms(t, gsum, gain):
    cols = []
    for c in range(t.shape[1] // MXU_DIM):
        blk = t[:, c * MXU_DIM:(c + 1) * MXU_DIM]
        ss = jnp.dot((blk * blk).astype(BF16), gsum, preferred_element_type=F32)
        cols.append(blk * lax.rsqrt(ss * (1.0 / NORM_GROUP) + EPS))
    out = cols[0] if len(cols) == 1 else jnp.concatenate(cols, axis=1)
    return out * gain


def _dot(a, b):
    return jnp.dot(a, b, preferred_element_type=F32)


def _dot_nt(a, b):
    return lax.dot_general(a, b, (((1,), (1,)), ((), ())), preferred_element_type=F32)


def _pre_diff_kernel(x_ref, g_ref, w_ref, gsum_ref, qg_ref, kg_ref, mg_ref,
                     qt_ref, k_ref, vt_ref, mq_ref):
    tw = TOKEN_WIDTH
    hw = 2 * DIFF_HEAD_DIM
    h = _rms_rows(x_ref[...], g_ref[...]).astype(BF16)
    gsum = gsum_ref[...]
    q = _group_rms(_dot(h, w_ref[:, 0:tw]), gsum, qg_ref[...])
    k_ref[...] = _group_rms(_dot(h, w_ref[:, tw:2 * tw]), gsum, kg_ref[...]).astype(BF16)
    v = _dot(h, w_ref[:, 2 * tw:3 * tw])
    mq_ref[...] = _group_rms(_dot(h, w_ref[:, 3 * tw:]), gsum, mg_ref[...]).astype(BF16)
    ones = jnp.ones((ATT_V_ROWS - hw, ATT_BLK), BF16)
    for j in range(ROW_TILE // ATT_BLK):
        rows = slice(j * ATT_BLK, (j + 1) * ATT_BLK)
        for n in range(DIFF_HEADS):
            qt_ref[0, n, j] = q[rows, n * hw:(n + 1) * hw].T.astype(BF16)
            vt_ref[0, n, j, 0:hw, :] = v[rows, n * hw:(n + 1) * hw].T.astype(BF16)
            vt_ref[0, n, j, hw:, :] = ones


def _pre_diff(x2, g, w_all, layer, gsum, qg, kg, mg, b):
    t = x2.shape[0]
    tw = TOKEN_WIDTH
    hw = 2 * DIFF_HEAD_DIM
    per_tile = ROW_TILE // ATT_BLK
    nq = t // b // ATT_BLK
    tiles = nq // per_tile
    row = lambda n: pl.BlockSpec((ROW_TILE, n), lambda i: (i, 0))
    per_head = lambda r: pl.BlockSpec((1, DIFF_HEADS, per_tile, r, ATT_BLK),
                                      lambda i: (i // tiles, 0, i % tiles, 0, 0))
    return pl.pallas_call(
        _pre_diff_kernel,
        grid=(t // ROW_TILE,),
        in_specs=[row(D_MODEL), _const_spec((1, D_MODEL)), _layer_spec(w_all, layer),
                  _const_spec(gsum.shape), _const_spec((1, tw)), _const_spec((1, tw)),
                  _const_spec((1, MEM_WIDTH))],
        out_specs=[per_head(hw), row(tw), per_head(ATT_V_ROWS), row(MEM_WIDTH)],
        out_shape=[jax.ShapeDtypeStruct((b, DIFF_HEADS, nq, hw, ATT_BLK), BF16),
                   jax.ShapeDtypeStruct((t, tw), BF16),
                   jax.ShapeDtypeStruct((b, DIFF_HEADS, nq, ATT_V_ROWS, ATT_BLK), BF16),
                   jax.ShapeDtypeStruct((t, MEM_WIDTH), BF16)],
        compiler_params=_cparams("parallel"),
        name="pre_diff",
    )(x2, g, w_all, gsum, qg, kg, mg)


def _diff_attn_kernel(lam_ref, qt_ref, k_ref, vt_ref, bvec_ref, g_ref, o_ref,
                      bias_scr, *scratch, out_scale):
    blk = ATT_BLK
    hd = DIFF_HEAD_DIM
    ns = ATT_STREAMS
    qp_scr, s_scr, cm_scr, p_scr, a_scr, m_scr, acc_scr = (scratch[i * ns:(i + 1) * ns] for i in range(7))
    w = 2 * blk // ns
    qi = pl.program_id(2)
    qt = qt_ref[0, 0, 0]
    row = lax.broadcasted_iota(jnp.int32, qt.shape, 0)
    zero = jnp.zeros_like(qt)
    q_maps = (jnp.where(row < hd, qt, zero), jnp.where(row >= hd, qt, zero))
    cols = [slice((x % (ns // 2)) * w, (x % (ns // 2) + 1) * w) for x in range(ns)]
    for x in range(ns):
        qp_scr[x][...] = q_maps[x // (ns // 2)][:, cols[x]]
        m_scr[x][...] = jnp.full(m_scr[x].shape, -jnp.inf, F32)
        acc_scr[x][...] = jnp.zeros(acc_scr[x].shape, F32)

    @pl.when(qi == 0)
    def _():
        kk = lax.broadcasted_iota(jnp.int32, (blk, blk), 0)
        qq = lax.broadcasted_iota(jnp.int32, (blk, blk), 1)
        visible = (kk // CHUNK) <= (qq // CHUNK)
        for i in range(2):
            rows = jnp.broadcast_to(bvec_ref[0, i], (blk, 2 * blk))
            tile = pltpu.roll(rows, 0, 1, stride=1, stride_axis=0)[:, :blk]
            bias_scr[i] = jnp.where(visible, tile, -jnp.inf) if i == 0 else tile

    def key_block(t):
        return jnp.maximum(qi - t, 0)

    def logits(x, t, bias=None):
        start = pl.multiple_of(key_block(t) * blk, blk)
        s = _dot(k_ref[0, pl.ds(start, blk), :], qp_scr[x][...])
        if bias is not None:
            s = s + bias[:, cols[x]]
        s_scr[x][...] = s
        part = s[0:8]
        for r in range(8, blk, 8):
            part = jnp.maximum(part, s[r:r + 8])
        cm_scr[x][...] = part

    def softmax(x):
        m_old = m_scr[x][...]
        m_new = jnp.maximum(m_old, jnp.max(cm_scr[x][...], axis=0, keepdims=True))
        a_scr[x][...] = jnp.exp2(m_old - m_new)
        m_scr[x][...] = m_new
        for r in range(0, blk, 16):
            p_scr[x][r:r + 16, :] = jnp.exp2((s_scr[x][r:r + 16, :] - m_new).astype(BF16))

    def values(x, t):
        acc_scr[x][...] = a_scr[x][...] * acc_scr[x][...] + _dot(vt_ref[0, 0, key_block(t)], p_scr[x][...])

    def step(t, bias=None):
        for x in range(ns):
            logits(x, t + 1, bias)
            values(x, t)
            softmax((x + 1) % ns)

    for x in range(ns):
        logits(x, 0, bias_scr[0])
    softmax(0)
    step(0, bias_scr[1] + jnp.where(qi == 0, -jnp.inf, 0.0).astype(F32))

    t_last = jnp.maximum(qi, 1)

    def far_pair(u, c):
        step(1 + 2 * u)
        step(2 + 2 * u)
        return c

    lax.fori_loop(0, (t_last - 1) // 2, far_pair, 0)

    @pl.when((t_last - 1) % 2 == 1)
    def _():
        step(t_last - 1)

    for x in range(ns):
        values(x, t_last)
        if x + 1 < ns:
            softmax(x + 1)

    lam = lam_ref[0, 0]
    half = ns // 2
    outs = []
    for x in range(half):
        o0 = acc_scr[x][0:2 * hd, :] * (1.0 / acc_scr[x][2 * hd:2 * hd + 1, :])
        o1 = acc_scr[half + x][0:2 * hd, :] * (1.0 / acc_scr[half + x][2 * hd:2 * hd + 1, :])
        outs.append(o0 - lam * o1)
    ot = outs[0] if half == 1 else jnp.concatenate(outs, axis=1)
    ms = jnp.mean(ot * ot, axis=0, keepdims=True)
    y = (ot * lax.rsqrt(ms + EPS)).T * (g_ref[...] * out_scale)
    o_ref[0] = y.astype(BF16)


def _diff_attn_bounded_kernel(lam_ref, qt_ref, k_ref, vt_ref, bvec_ref, g_ref, o_ref,
                              bias_scr, *scratch, out_scale):
    blk = ATT_BLK
    hd = DIFF_HEAD_DIM
    ns = ATT_STREAMS
    nq = qt_ref.shape[2]
    qp_scr, p0_scr, p1_scr, acc_scr, l_scr = (scratch[i * ns:(i + 1) * ns] for i in range(5))
    p_scr = (p0_scr, p1_scr)
    w = 2 * blk // ns
    lam = lam_ref[0, 0]
    m_ref = lam_ref[0, 1]
    cols = [slice((x % (ns // 2)) * w, (x % (ns // 2) + 1) * w) for x in range(ns)]

    kk = lax.broadcasted_iota(jnp.int32, (blk, blk), 0)
    qq = lax.broadcasted_iota(jnp.int32, (blk, blk), 1)
    visible = (kk // CHUNK) <= (qq // CHUNK)
    for i in range(2):
        rows = jnp.broadcast_to(bvec_ref[0, i], (blk, 2 * blk))
        tile = pltpu.roll(rows, 0, 1, stride=1, stride_axis=0)[:, :blk]
        bias_scr[i] = jnp.where(visible, tile, -jnp.inf) if i == 0 else tile

    def key_block(qi, t):
        return jnp.maximum(qi - t, 0)

    diag_keys = [(x % (ns // 2) + 1) * w for x in range(ns)]
    assert w % CHUNK == 0

    def probs(qi, x, t, bias=None, keys=blk):
        start = pl.multiple_of(key_block(qi, t) * blk, blk)
        s = _dot(k_ref[0, pl.ds(start, keys), :], qp_scr[x][...])
        if bias is not None:
            s = s + bias[0:keys, cols[x]]
        p = jnp.exp2(s - m_ref)
        l_scr[x][...] += jnp.sum(p.reshape(keys // 8, 8, w), axis=0)
        return p.astype(BF16)

    def step(qi, t, slot, bias=None, keys_t=None):
        for x in range(ns):
            p_scr[slot][x][...] = probs(qi, x, t + 1, bias)
            n = blk if keys_t is None else keys_t[x]
            acc_scr[x][...] += _dot(vt_ref[0, 0, key_block(qi, t), 0:2 * hd, 0:n], p_scr[1 - slot][x][0:n, :])

    def begin(qi):
        qt = qt_ref[0, 0, qi]
        row = lax.broadcasted_iota(jnp.int32, qt.shape, 0)
        zero = jnp.zeros_like(qt)
        q_maps = (jnp.where(row < hd, qt, zero), jnp.where(row >= hd, qt, zero))
        for x in range(ns):
            qp_scr[x][...] = q_maps[x // (ns // 2)][:, cols[x]]
            acc_scr[x][...] = jnp.zeros(acc_scr[x].shape, F32)
            l_scr[x][...] = jnp.zeros(l_scr[x].shape, F32)
        for x in range(ns):
            p0_scr[x][0:diag_keys[x], :] = probs(qi, x, 0, bias_scr[0], diag_keys[x])
        step(qi, 0, 1, bias_scr[1] + jnp.where(qi == 0, -jnp.inf, 0.0).astype(F32), diag_keys)

    def sweep(qi):
        n_far = qi - 1

        def far_steps(n):
            def body(u, c):
                for i in range(n):
                    step(qi, 1 + n * u + i, i % 2)
                return c
            return body

        n_quads = n_far // 4
        lax.fori_loop(0, n_quads, far_steps(4), 0)
        t_done = 1 + 4 * n_quads

        @pl.when(n_far % 4 >= 2)
        def _():
            step(qi, t_done, 0)
            step(qi, t_done + 1, 1)

        @pl.when(n_far % 2 == 1)
        def _():
            step(qi, n_far, 0)

    def finish(qi):
        t_last = jnp.maximum(qi, 1)
        last_in_p1 = t_last % 2 == 1
        for x in range(ns):
            p_last = jnp.where(last_in_p1, p1_scr[x][...], p0_scr[x][...])
            acc_scr[x][...] += _dot(vt_ref[0, 0, key_block(qi, t_last), 0:2 * hd, :], p_last)
        half = ns // 2
        inv = [1.0 / jnp.sum(l_scr[x][...], axis=0, keepdims=True) for x in range(ns)]
        outs = []
        for x in range(half):
            outs.append(acc_scr[x][...] * inv[x] - lam * (acc_scr[half + x][...] * inv[half + x]))
        ot = outs[0] if half == 1 else jnp.concatenate(outs, axis=1)
        ms = jnp.mean(ot * ot, axis=0, keepdims=True)
        y = (ot * lax.rsqrt(ms + EPS)).T * (g_ref[...] * out_scale)
        o_ref[0, pl.ds(pl.multiple_of(qi * blk, blk), blk), :] = y.astype(BF16)

    begin(0)

    def query_block(qi, c):
        finish(qi - 1)
        begin(qi)
        sweep(qi)
        return c

    lax.fori_loop(1, nq, query_block, 0)
    finish(nq - 1)


def _diff_attn(lam, qt, k, vt, bias, g, out_scale, bounded):
    b, nh, nq = qt.shape[0], qt.shape[1], qt.shape[2]
    s = k.shape[1]
    blk = ATT_BLK
    ns = ATT_STREAMS
    w = 2 * blk // ns
    hw = 2 * DIFF_HEAD_DIM
    if bounded:
        body = _diff_attn_bounded_kernel
        per_stream = (((hw, w), BF16), ((blk, w), BF16), ((blk, w), BF16), ((hw, w), F32), ((8, w), F32))
        grid = (b, nh)
        q_spec = pl.BlockSpec((1, 1, nq, hw, blk), lambda bi, hi: (bi, hi, 0, 0, 0))
        o_spec = pl.BlockSpec((1, s, hw), lambda bi, hi: (bi, 0, hi))
        sem = ("parallel", "parallel")
    else:
        body = _diff_attn_kernel
        per_stream = (((hw, w), BF16), ((blk, w), F32), ((8, w), F32), ((blk, w), BF16),
                      ((1, w), F32), ((1, w), F32), ((ATT_V_ROWS, w), F32))
        grid = (b, nh, nq)
        q_spec = pl.BlockSpec((1, 1, 1, hw, blk), lambda bi, hi, qi: (bi, hi, qi, 0, 0))
        o_spec = pl.BlockSpec((1, blk, hw), lambda bi, hi, qi: (bi, qi, hi))
        sem = ("parallel", "parallel", "arbitrary")
    return pl.pallas_call(
        functools.partial(body, out_scale=out_scale),
        grid=grid,
        in_specs=[
            pl.BlockSpec(memory_space=pltpu.SMEM),
            q_spec,
            pl.BlockSpec((1, s, hw), lambda bi, hi, *_: (bi, 0, hi)),
            pl.BlockSpec((1, 1, nq, ATT_V_ROWS, blk), lambda bi, hi, *_: (bi, hi, 0, 0, 0)),
            pl.BlockSpec((1, 2, 1, 2 * blk), lambda bi, hi, *_: (hi, 0, 0, 0)),
            _const_spec((1, hw)),
        ],
        out_specs=o_spec,
        out_shape=jax.ShapeDtypeStruct((b, s, nh * hw), BF16),
        scratch_shapes=[pltpu.VMEM((2, blk, blk), F32)] + [pltpu.VMEM(shape, dtype)
                                                        for shape, dtype in per_stream for _ in range(ns)],
        compiler_params=_cparams(*sem),
        name="diff_attn_bounded" if bounded else "diff_attn",
    )(lam, qt, k, vt, bias, g)


def _pre_gla_kernel(x_ref, g_ref, w_ref, gsum_ref, mg_ref, gw_ref, gb_ref, tri_ref,
                    q_ref, k_ref, v_ref, r_ref, gc_ref, mq_ref):
    kw = GLA_HEADS * GLA_KP
    vw = TOKEN_WIDTH
    h = _rms_rows(x_ref[...], g_ref[...]).astype(BF16)
    q_ref[...] = _dot(h, w_ref[:, 0:kw]).astype(BF16)
    k_ref[...] = _dot(h, w_ref[:, kw:2 * kw]).astype(BF16)
    o = 2 * kw
    v_ref[...] = _dot(h, w_ref[:, o:o + vw]).astype(BF16)
    r_ref[...] = _dot(h, w_ref[:, o + vw:o + 2 * vw]).astype(BF16)
    o = o + 2 * vw
    gate_low = _dot(h, w_ref[:, o:o + LANES]).astype(BF16)
    mq_ref[...] = _group_rms(_dot(h, w_ref[:, o + LANES:]), gsum_ref[...], mg_ref[...]).astype(BF16)
    z = _dot(gate_low, gw_ref[...]) + gb_ref[...]
    log_a = (jnp.minimum(z, 0.0) - jnp.log1p(jnp.exp(-jnp.abs(z)))) * (1.0 / GLA_GATE_TAU)
    hi = log_a.astype(BF16)
    lo = (log_a - hi.astype(F32)).astype(BF16)
    tri = tri_ref[...]
    n = tri.shape[0]
    for c in range(log_a.shape[0] // n):
        rows = slice(c * n, (c + 1) * n)
        gc_ref[rows, :] = _dot(tri, hi[rows]) + _dot(tri, lo[rows])


def _pre_gla(x2, g, w, gsum, mg, gw, gb, tri):
    t = x2.shape[0]
    kw = GLA_HEADS * GLA_KP
    vw = TOKEN_WIDTH
    row = lambda n: pl.BlockSpec((ROW_TILE, n), lambda i: (i, 0))
    return pl.pallas_call(
        _pre_gla_kernel,
        grid=(t // ROW_TILE,),
        in_specs=[row(D_MODEL), _const_spec((1, D_MODEL)), _const_spec(w.shape),
                  _const_spec(gsum.shape), _const_spec((1, MEM_WIDTH)), _const_spec(gw.shape),
                  _const_spec(gb.shape), _const_spec(tri.shape)],
        out_specs=[row(kw), row(kw), row(vw), row(vw), row(kw), row(MEM_WIDTH)],
        out_shape=[jax.ShapeDtypeStruct((t, kw), BF16), jax.ShapeDtypeStruct((t, kw), BF16),
                   jax.ShapeDtypeStruct((t, vw), BF16), jax.ShapeDtypeStruct((t, vw), BF16),
                   jax.ShapeDtypeStruct((t, kw), F32), jax.ShapeDtypeStruct((t, MEM_WIDTH), BF16)],
        compiler_params=_cparams("parallel"),
        name="pre_gla",
    )(x2, g, w, gsum, mg, gw, gb, tri)


def _gla_kernel(q_ref, k_ref, v_ref, r_ref, gc_ref, gain_ref, o_ref, s_scr):
    tg = GLA_TILE
    nchunk = tg // CHUNK
    heads = range(GLA_HEADS)
    ks = [slice(h * GLA_KP, (h + 1) * GLA_KP) for h in heads]
    vs = [slice(h * GLA_VP, (h + 1) * GLA_VP) for h in heads]
    pad_lane = lax.broadcasted_iota(jnp.int32, (tg, GLA_VP), 1) >= GLA_V_DIM

    def head_cols(ref, h):
        first = h * GLA_V_DIM
        aligned = first // LANES * LANES
        win = ref[0, :, aligned:aligned + GLA_VP].astype(F32)
        if first != aligned:
            win = pltpu.roll(win, GLA_VP - (first - aligned), 1)
        return jnp.where(pad_lane, 0.0, win)

    @pl.when(pl.program_id(1) == 0)
    def _():
        s_scr[...] = jnp.zeros(s_scr.shape, F32)

    ri = lax.broadcasted_iota(jnp.int32, (tg, tg), 0)
    ci = lax.broadcasted_iota(jnp.int32, (tg, tg), 1)
    same_chunk = (ri // CHUNK) == (ci // CHUNK)
    past = ci <= ri
    row_chunk = lax.broadcasted_iota(jnp.int32, (tg, GLA_KP), 0) // CHUNK

    qe, scores, kv, decay, v_pad = [], [], [], [], []
    for h in heads:
        qh = q_ref[0, :, ks[h]].astype(F32) * (GLA_K_DIM ** -0.5)
        kh = k_ref[0, :, ks[h]].astype(F32)
        g = gc_ref[0, :, ks[h]]
        eg = jnp.exp(g)
        ieg = jnp.exp(-g)
        qe.append((qh * eg).astype(BF16))
        a_past = _dot_nt(qe[h], (kh * ieg).astype(BF16))
        a_fut = _dot_nt((qh * ieg).astype(BF16), (kh * eg).astype(BF16))
        scores.append(jnp.where(same_chunk, jnp.where(past, a_past, a_fut), 0.0).astype(BF16))
        v_pad.append(head_cols(v_ref, h))
        vt = v_pad[h].T.astype(BF16)
        g_last = [g[c * CHUNK + CHUNK - 1:c * CHUNK + CHUNK, :] for c in range(nchunk)]
        g_end = jnp.concatenate([jnp.broadcast_to(gl, (CHUNK, GLA_KP)) for gl in g_last], axis=0)
        kdec = kh * jnp.exp(g_end - g)
        kd_chunks = jnp.concatenate([jnp.where(row_chunk == c, kdec, 0.0) for c in range(nchunk)], axis=1)
        kv_all = _dot(vt, kd_chunks.astype(BF16))
        kv.append([kv_all[:, c * GLA_KP:(c + 1) * GLA_KP] for c in range(nchunk)])
        decay.append([jnp.exp(gl) for gl in g_last])

    starts = []
    for h in heads:
        st = s_scr[h]
        per_chunk = []
        for c in range(nchunk):
            per_chunk.append(st.astype(BF16))
            st = st * decay[h][c] + kv[h][c]
        s_scr[h] = st
        starts.append(per_chunk)

    gated = []
    for h in heads:
        inter = [_dot_nt(qe[h][c * CHUNK:(c + 1) * CHUNK], starts[h][c]) for c in range(nchunk)]
        o = _dot(scores[h], v_pad[h].astype(BF16)) + jnp.concatenate(inter, axis=0)
        ms = jnp.sum(o * o, axis=-1, keepdims=True) * (1.0 / GLA_V_DIM)
        y = o * lax.rsqrt(ms + EPS) * gain_ref[:, vs[h]]
        half_r = 0.5 * head_cols(r_ref, h)
        gated.append(y * half_r * (1.0 + jnp.tanh(half_r)))

    pair = 2 * GLA_V_DIM
    blank = jnp.zeros((tg, GLA_VP), F32)
    for p in range(GLA_HEADS // 2):
        even = jnp.concatenate([gated[2 * p], blank], axis=1)
        odd = pltpu.roll(jnp.concatenate([gated[2 * p + 1], blank], axis=1), GLA_V_DIM, 1)
        o_ref[0, :, p * pair:(p + 1) * pair] = (even + odd)[:, :pair].astype(BF16)


def _gla(q, k, v, r, gc, gain):
    b, s = q.shape[0], q.shape[1]
    kw = GLA_HEADS * GLA_KP
    vw = GLA_HEADS * GLA_VP
    spec = lambda n: pl.BlockSpec((1, GLA_TILE, n), lambda bi, i: (bi, i, 0))
    return pl.pallas_call(
        _gla_kernel,
        grid=(b, s // GLA_TILE),
        in_specs=[spec(kw), spec(kw), spec(TOKEN_WIDTH), spec(TOKEN_WIDTH), spec(kw), _const_spec((1, vw))],
        out_specs=spec(TOKEN_WIDTH),
        out_shape=jax.ShapeDtypeStruct((b, s, TOKEN_WIDTH), BF16),
        scratch_shapes=[pltpu.VMEM((GLA_HEADS, GLA_VP, GLA_KP), F32)],
        compiler_params=_cparams("parallel", "arbitrary"),
        name="gla",
    )(q, k, v, r, gc, gain)


def _mem_kv_kernel(mem_ref, g_ref, w_ref, gsum_ref, kg_ref, k_ref, v_ref):
    h = _rms_rows(mem_ref[...], g_ref[...]).astype(BF16)
    k_ref[...] = _group_rms(_dot(h, w_ref[:, :MEM_WIDTH]), gsum_ref[...], kg_ref[...]).astype(BF16)
    v_ref[...] = _dot(h, w_ref[:, MEM_WIDTH:]).astype(BF16)


def _mem_kv(mem2, g, w_all, layer, gsum, kg):
    n = mem2.shape[0]
    return pl.pallas_call(
        _mem_kv_kernel,
        grid=(1,),
        in_specs=[_const_spec(mem2.shape), _const_spec((1, D_MODEL)), _layer_spec(w_all, layer),
                  _const_spec(gsum.shape), _const_spec((1, MEM_WIDTH))],
        out_specs=[_const_spec((n, MEM_WIDTH)), _const_spec((n, MEM_WIDTH))],
        out_shape=[jax.ShapeDtypeStruct((n, MEM_WIDTH), BF16)] * 2,
        compiler_params=_cparams("arbitrary"),
        name="mem_kv",
    )(mem2, g, w_all, gsum, kg)


def _mix_out_kernel(x_ref, mix_ref, mq_ref, kbd_ref, vbd_ref, wa_ref, wb_ref, o_ref):
    m = kbd_ref.shape[2] // MEM_HEADS
    logits = _dot(mq_ref[0], kbd_ref[0])
    ps = []
    for h in range(MEM_HEADS):
        s = logits[:, h * m:(h + 1) * m]
        e = jnp.exp(s - jnp.max(s, axis=-1, keepdims=True))
        ps.append((e * (1.0 / jnp.sum(e, axis=-1, keepdims=True))).astype(BF16))
    cross = _dot(jnp.concatenate(ps, axis=1), vbd_ref[0])
    o_ref[0] = x_ref[0] + _dot(mix_ref[0], wa_ref[...]) + _dot(cross.astype(BF16), wb_ref[...])


def _mix_out(x, mix, mq, kbd, vbd, wa, wa_spec, wb, wb_spec):
    b, s = x.shape[0], x.shape[1]
    spec = lambda n: pl.BlockSpec((1, MIX_TILE, n), lambda bi, i: (bi, i, 0))
    per_b = lambda a: pl.BlockSpec((1,) + a.shape[1:], lambda bi, i: (bi, 0, 0))
    return pl.pallas_call(
        _mix_out_kernel,
        grid=(b, s // MIX_TILE),
        in_specs=[spec(D_MODEL), spec(mix.shape[2]), spec(MEM_WIDTH), per_b(kbd), per_b(vbd),
                  wa_spec, wb_spec],
        out_specs=spec(D_MODEL),
        out_shape=jax.ShapeDtypeStruct(x.shape, F32),
        compiler_params=_cparams("parallel", "parallel"),
        name="mix_out",
    )(x, mix, mq, kbd, vbd, wa, wb)


def _ffn_kernel(x_ref, g_ref, wu_ref, cw_ref, cb_ref, wd_ref, o_ref, carry_scr, act_scr, *shift_scr):
    tm = FFN_TILE
    cr = CARRY_ROWS

    @pl.when(pl.program_id(1) == 0)
    def _():
        carry_scr[...] = jnp.zeros(carry_scr.shape, F32)

    x = x_ref[0]
    h = _rms_rows(x, g_ref[...]).astype(BF16)

    def conv(cols, bufs):
        u = _dot(h, wu_ref[:, cols])
        prev = carry_scr[:, cols]
        for shift, buf in zip((1, 2), bufs):
            buf[shift:shift + cr, :] = prev
            buf[cr + shift:cr + shift + tm, :] = u
        carry_scr[:, cols] = u[tm - cr:tm]
        return (cw_ref[0:1, cols] * bufs[1][cr:cr + tm, :] + cw_ref[1:2, cols] * bufs[0][cr:cr + tm, :]
                + cw_ref[2:3, cols] * u + cb_ref[:, cols])

    for j in range(D_FF // FFN_COLS):
        bufs = shift_scr[4 * (j % 2):4 * (j % 2) + 4]
        a = conv(slice(j * FFN_COLS, (j + 1) * FFN_COLS), bufs[0:2])
        half_g = 0.5 * conv(slice(D_FF + j * FFN_COLS, D_FF + (j + 1) * FFN_COLS), bufs[2:4])
        act_scr[:, j * FFN_COLS:(j + 1) * FFN_COLS] = (a * half_g * (1.0 + jnp.tanh(half_g))).astype(BF16)

    o_ref[0] = x + _dot(act_scr[...], wd_ref[...])


def _ffn(x, g, wu_all, cw, cb, wd_all, layer):
    b, s = x.shape[0], x.shape[1]
    spec = pl.BlockSpec((1, FFN_TILE, D_MODEL), lambda bi, i: (bi, i, 0))
    return pl.pallas_call(
        _ffn_kernel,
        grid=(b, s // FFN_TILE),
        in_specs=[spec, _const_spec((1, D_MODEL)), _layer_spec(wu_all, layer, single=True), _const_spec(cw.shape),
                  _const_spec(cb.shape), _layer_spec(wd_all, layer, single=True)],
        out_specs=spec,
        out_shape=jax.ShapeDtypeStruct(x.shape, F32),
        scratch_shapes=[pltpu.VMEM((CARRY_ROWS, 2 * D_FF), F32), pltpu.VMEM((FFN_TILE, D_FF), BF16)]
        + [pltpu.VMEM((FFN_TILE + 2 * CARRY_ROWS, FFN_COLS), F32)] * 8,
        compiler_params=_cparams("parallel", "arbitrary"),
        name="ffn",
    )(x, g, wu_all, cw, cb, wd_all)


def _t5_bucket(rel):
    half = REL_BUCKETS // 2
    max_exact = half // 2
    ret = jnp.where(rel > 0, half, 0)
    n = jnp.abs(rel)
    nf = jnp.maximum(n, 1).astype(jnp.float32)
    large = max_exact + (jnp.log(nf / max_exact) / math.log(REL_MAX_DIST / max_exact)
                         * (half - max_exact)).astype(jnp.int32)
    large = jnp.minimum(large, half - 1)
    return ret + jnp.where(n < max_exact, n, large)


def _bias_vectors(rel_bias):
    blk = ATT_BLK
    assert blk >= REL_MAX_DIST
    table = rel_bias.astype(F32).T[:, :, None]

    def lookup(rel):
        bucket = _t5_bucket(rel)
        out = jnp.zeros((table.shape[0],) + rel.shape, F32)
        for i in range(REL_BUCKETS):
            out = jnp.where(bucket == i, table[:, i], out)
        return out

    j = jnp.arange(2 * blk)
    dist = jnp.where(j < blk, -j, 2 * blk - j)
    values = lookup(jnp.concatenate([dist, dist - blk, jnp.full((1,), -2 * blk)]))
    vectors = (values[:, :4 * blk] - values[:, 4 * blk:]) * LOG2E
    return vectors.reshape(-1, 2, 1, 2 * blk)


def _group_sum_matrix():
    i = jnp.arange(MXU_DIM)
    return ((i[:, None] // NORM_GROUP) == (i[None, :] // NORM_GROUP)).astype(BF16)


def _chunk_tri_matrix(n):
    i = jnp.arange(n)
    return (((i[:, None] // CHUNK) == (i[None, :] // CHUNK)) & (i[None, :] <= i[:, None])).astype(BF16)


def _pad_heads(w, heads, dim, pad, axis):
    shape = list(w.shape)
    shape[axis:axis + 1] = [heads, dim]
    w = w.reshape(shape)
    widths = [(0, 0)] * w.ndim
    widths[axis + 1] = (0, pad - dim)
    w = jnp.pad(w, widths)
    shape[axis:axis + 2] = [heads * pad]
    return w.reshape(shape)


def _tile_gain(g, reps, scale=1.0):
    return (jnp.tile(g.astype(F32), reps) * scale)[None, :]


def _mem_block_diag(kn, v, b):
    m = kn.shape[0] // b
    eye = jnp.eye(MEM_HEADS, dtype=BF16)
    knt = kn.reshape(b, m, MEM_WIDTH).transpose(0, 2, 1)
    kbd = (knt.reshape(b, MEM_HEADS, MEM_HEAD_DIM, 1, m) * eye.reshape(1, MEM_HEADS, 1, MEM_HEADS, 1))
    kbd = kbd.reshape(b, MEM_WIDTH, MEM_HEADS * m)
    vbd = (v.reshape(b, 1, m, MEM_HEADS, MEM_HEAD_DIM) * eye.reshape(1, MEM_HEADS, 1, MEM_HEADS, 1))
    vbd = vbd.reshape(b, MEM_HEADS * m, MEM_WIDTH)
    return kbd, vbd


def kernel(x, mem, rel_bias, attn_norm, ffn_norm, mem_norm, w_in_diff, diff_qk_norm, diff_lambda,
           diff_out_norm, w_in_gla, gla_gate_w, gla_gate_b, gla_out_norm, w_mem_kv, mem_qk_norm,
           w_out, w_up, conv_w, conv_b, w_down):
    b, s, d = x.shape
    t = b * s
    tw = TOKEN_WIDTH
    gsum = _group_sum_matrix()
    mem2 = mem.reshape(b * mem.shape[1], d)
    x = x.astype(F32)
    w_in_diff, w_mem_kv, w_out, w_up, w_down = map(_to_bf16, (w_in_diff, w_mem_kv, w_out, w_up, w_down))

    for i in range(DEPTH):
        j = i // 2
        x2 = x.reshape(t, d)
        mq_gain = _tile_gain(mem_qk_norm[i, 0], MEM_HEADS, MEM_HEAD_DIM ** -0.5)
        if i % 2 == 0:
            nh, hd = DIFF_HEADS, DIFF_HEAD_DIM
            qt, k, vt, mq = _pre_diff(
                x2, attn_norm[i][None, :], w_in_diff, j, gsum,
                _tile_gain(diff_qk_norm[j, 0], 2 * nh, hd ** -0.5 * LOG2E),
                _tile_gain(diff_qk_norm[j, 1], 2 * nh), mq_gain, b)
            lv = diff_lambda[j].astype(F32)
            lam_init = 0.8 - 0.6 * math.exp(-0.3 * i)
            lam = jnp.exp(jnp.sum(lv[0] * lv[1])) - jnp.exp(jnp.sum(lv[2] * lv[3])) + lam_init
            bvec = _bias_vectors(rel_bias)
            qk_bound = (hd ** 0.5 * LOG2E * ATT_ROUNDING_SLACK
                        * jnp.max(jnp.abs(diff_qk_norm[j, 0])) * jnp.max(jnp.abs(diff_qk_norm[j, 1]))).astype(F32)
            hi = qk_bound + jnp.maximum(jnp.max(bvec), 0.0)
            lo = -qk_bound + jnp.minimum(jnp.min(bvec), 0.0)
            scalars = jnp.stack([lam.astype(F32), hi]).reshape(1, 2)
            attend = lambda bounded: functools.partial(
                _diff_attn, qt=qt, k=k.reshape(b, s, tw), vt=vt, bias=bvec,
                g=diff_out_norm[j].astype(F32)[None, :], out_scale=1.0 - lam_init, bounded=bounded)
            mix = lax.cond(hi - lo <= ATT_MAX_EXP2_SPAN, attend(True), attend(False), scalars)
        else:
            kw = GLA_HEADS * GLA_K_DIM
            w = w_in_gla[j]
            hp = functools.partial(_pad_heads, heads=GLA_HEADS, axis=1)
            w_p = jnp.concatenate([
                hp(w[:, :kw], dim=GLA_K_DIM, pad=GLA_KP),
                hp(w[:, kw:2 * kw], dim=GLA_K_DIM, pad=GLA_KP),
                w[:, 2 * kw:2 * kw + 2 * tw],
                jnp.pad(w[:, 2 * kw + 2 * tw:2 * kw + 2 * tw + GLA_GATE_RANK],
                        ((0, 0), (0, LANES - GLA_GATE_RANK))),
                w[:, 2 * kw + 2 * tw + GLA_GATE_RANK:]], axis=1).astype(BF16)
            gw = jnp.pad(hp(gla_gate_w[j], dim=GLA_K_DIM, pad=GLA_KP),
                         ((0, LANES - GLA_GATE_RANK), (0, 0))).astype(BF16)
            gb = _pad_heads(gla_gate_b[j].astype(F32)[None, :], GLA_HEADS, GLA_K_DIM, GLA_KP, 1)
            q, k, v, r, gc, mq = _pre_gla(x2, attn_norm[i][None, :], w_p, gsum, mq_gain, gw, gb,
                                          _chunk_tri_matrix(MXU_DIM))
            gain = _pad_heads(jnp.tile(gla_out_norm[j].astype(F32), GLA_HEADS)[None, :],
                              GLA_HEADS, GLA_V_DIM, GLA_VP, 1)
            sh = lambda a: a.reshape(b, s, a.shape[1])
            mix = _gla(sh(q), sh(k), sh(v), sh(r), sh(gc), gain)

        kn, vm = _mem_kv(mem2, mem_norm[i][None, :], w_mem_kv, i, gsum,
                         _tile_gain(mem_qk_norm[i, 1], MEM_HEADS))
        kbd, vbd = _mem_block_diag(kn, vm, b)
        x = _mix_out(x, mix.reshape(b, s, -1), mq.reshape(b, s, MEM_WIDTH), kbd, vbd,
                     w_out, _layer_spec(w_out, i, rows=(0, tw)), w_out, _layer_spec(w_out, i, rows=(tw, MEM_WIDTH)))
        x = _ffn(x, ffn_norm[i][None, :], w_up, conv_w[i].astype(F32), conv_b[i].astype(F32)[None, :], w_down, i)
    return x
```

```python
import functools
import math

import jax
import jax.numpy as jnp
from jax import lax
from jax.experimental import pallas as pl
from jax.experimental.pallas import tpu as pltpu

F32 = jnp.float32
BF16 = jnp.bfloat16

D_MODEL = 1024
DEPTH = 2
CHUNK = 64
MEM_WIDTH = D_MODEL // 4
MEM_HEADS = 4
MEM_HEAD_DIM = MEM_WIDTH // MEM_HEADS
TOKEN_WIDTH = D_MODEL - MEM_WIDTH
DIFF_HEAD_DIM = 64
DIFF_HEADS = TOKEN_WIDTH // (2 * DIFF_HEAD_DIM)
GLA_HEADS = 4
GLA_V_DIM = TOKEN_WIDTH // GLA_HEADS
GLA_K_DIM = GLA_V_DIM // 2
GLA_GATE_RANK = 16
GLA_GATE_TAU = 16.0
REL_BUCKETS = 32
REL_MAX_DIST = 128
D_FF = ((8 * D_MODEL // 3 + 127) // 128) * 128
EPS = 1e-6
LOG2E = math.log2(math.e)
NORM_GROUP = DIFF_HEAD_DIM
assert MEM_HEAD_DIM == NORM_GROUP

LANES = 128
MXU_DIM = 256
VMEM_LIMIT_BYTES = 56 * 1024 * 1024

ROW_TILE = 1024
MIX_TILE = 1024
ATT_BLK = 512
ATT_STREAMS = 4
ATT_MAX_EXP2_SPAN = 100.0
ATT_ROUNDING_SLACK = 1.02
ATT_V_ROWS = 2 * DIFF_HEAD_DIM + 16
GLA_TILE = 256
FFN_TILE = 1024
FFN_COLS = 256
GLA_KP = 128
GLA_VP = 256
CARRY_ROWS = 8
CAST_BLOCK_BYTES = 4 * 1024 * 1024


def _cparams(*sem):
    return pltpu.CompilerParams(dimension_semantics=sem, vmem_limit_bytes=VMEM_LIMIT_BYTES)


def _const_spec(shape):
    n = len(shape)
    return pl.BlockSpec(shape, lambda *_: (0,) * n)


def _layer_spec(w_all, layer, rows=None, single=False):
    first, n_rows = (0, w_all.shape[1]) if rows is None else rows
    assert first % n_rows == 0
    mode = dict(pipeline_mode=pl.Buffered(1)) if single else {}
    return pl.BlockSpec((None, n_rows, w_all.shape[2]), lambda *_: (layer, first // n_rows, 0), **mode)


def _cast_kernel(w_ref, o_ref):
    o_ref[...] = w_ref[...].astype(BF16)


def _to_bf16(w):
    n, r, c = w.shape
    rows = max(rb for rb in range(16, r + 1, 16) if r % rb == 0 and rb * c * 4 <= CAST_BLOCK_BYTES)
    spec = pl.BlockSpec((1, rows, c), lambda a, i: (a, i, 0))
    return pl.pallas_call(
        _cast_kernel, grid=(n, r // rows), in_specs=[spec], out_specs=spec,
        out_shape=jax.ShapeDtypeStruct(w.shape, BF16),
        compiler_params=_cparams("parallel", "parallel"), name="to_bf16",
    )(w.astype(F32))


def _rms_rows(x, g):
    ms = jnp.mean(x * x, axis=-1, keepdims=True)
    return x * lax.rsqrt(ms + EPS) * g


def _group_rms(t, gsum, gain):
    cols = []
    for c in range(t.shape[1] // MXU_DIM):
        blk = t[:, c * MXU_DIM:(c + 1) * MXU_DIM]
        ss = jnp.dot((blk * blk).astype(BF16), gsum, preferred_element_type=F32)
        cols.append(blk * lax.rsqrt(ss * (1.0 / NORM_GROUP) + EPS))
    out = cols[0] if len(cols) == 1 else jnp.concatenate(cols, axis=1)
    return out * gain


def _dot(a, b):
    return jnp.dot(a, b, preferred_element_type=F32)


def _dot_nt(a, b):
    return lax.dot_general(a, b, (((1,), (1,)), ((), ())), preferred_element_type=F32)


def _pre_diff_kernel(x_ref, g_ref, w_ref, gsum_ref, qg_ref, kg_ref, mg_ref,
                     qt_ref, k_ref, vt_ref, mq_ref):
    tw = TOKEN_WIDTH
    hw = 2 * DIFF_HEAD_DIM
    h = _rms_rows(x_ref[...], g_ref[...]).astype(BF16)
    gsum = gsum_ref[...]
    q = _group_rms(_dot(h, w_ref[:, 0:tw]), gsum, qg_ref[...])
    k_ref[...] = _group_rms(_dot(h, w_ref[:, tw:2 * tw]), gsum, kg_ref[...]).astype(BF16)
    v = _dot(h, w_ref[:, 2 * tw:3 * tw])
    mq_ref[...] = _group_rms(_dot(h, w_ref[:, 3 * tw:]), gsum, mg_ref[...]).astype(BF16)
    ones = jnp.ones((ATT_V_ROWS - hw, ATT_BLK), BF16)
    for j in range(ROW_TILE // ATT_BLK):
        rows = slice(j * ATT_BLK, (j + 1) * ATT_BLK)
        for n in range(DIFF_HEADS):
            qt_ref[0, n, j] = q[rows, n * hw:(n + 1) * hw].T.astype(BF16)
            vt_ref[0, n, j, 0:hw, :] = v[rows, n * hw:(n + 1) * hw].T.astype(BF16)
            vt_ref[0, n, j, hw:, :] = ones


def _pre_diff(x2, g, w_all, layer, gsum, qg, kg, mg, b):
    t = x2.shape[0]
    tw = TOKEN_WIDTH
    hw = 2 * DIFF_HEAD_DIM
    per_tile = ROW_TILE // ATT_BLK
    nq = t // b // ATT_BLK
    tiles = nq // per_tile
    row = lambda n: pl.BlockSpec((ROW_TILE, n), lambda i: (i, 0))
    per_head = lambda r: pl.BlockSpec((1, DIFF_HEADS, per_tile, r, ATT_BLK),
                                      lambda i: (i // tiles, 0, i % tiles, 0, 0))
    return pl.pallas_call(
        _pre_diff_kernel,
        grid=(t // ROW_TILE,),
        in_specs=[row(D_MODEL), _const_spec((1, D_MODEL)), _layer_spec(w_all, layer),
                  _const_spec(gsum.shape), _const_spec((1, tw)), _const_spec((1, tw)),
                  _const_spec((1, MEM_WIDTH))],
        out_specs=[per_head(hw), row(tw), per_head(ATT_V_ROWS), row(MEM_WIDTH)],
        out_shape=[jax.ShapeDtypeStruct((b, DIFF_HEADS, nq, hw, ATT_BLK), BF16),
                   jax.ShapeDtypeStruct((t, tw), BF16),
                   jax.ShapeDtypeStruct((b, DIFF_HEADS, nq, ATT_V_ROWS, ATT_BLK), BF16),
                   jax.ShapeDtypeStruct((t, MEM_WIDTH), BF16)],
        compiler_params=_cparams("parallel"),
        name="pre_diff",
    )(x2, g, w_all, gsum, qg, kg, mg)


def _diff_attn_kernel(lam_ref, qt_ref, k_ref, vt_ref, bvec_ref, g_ref, o_ref,
                      bias_scr, *scratch, out_scale):
    blk = ATT_BLK
    hd = DIFF_HEAD_DIM
    ns = ATT_STREAMS
    qp_scr, s_scr, cm_scr, p_scr, a_scr, m_scr, acc_scr = (scratch[i * ns:(i + 1) * ns] for i in range(7))
    w = 2 * blk // ns
    qi = pl.program_id(2)
    qt = qt_ref[0, 0, 0]
    row = lax.broadcasted_iota(jnp.int32, qt.shape, 0)
    zero = jnp.zeros_like(qt)
    q_maps = (jnp.where(row < hd, qt, zero), jnp.where(row >= hd, qt, zero))
    cols = [slice((x % (ns // 2)) * w, (x % (ns // 2) + 1) * w) for x in range(ns)]
    for x in range(ns):
        qp_scr[x][...] = q_maps[x // (ns // 2)][:, cols[x]]
        m_scr[x][...] = jnp.full(m_scr[x].shape, -jnp.inf, F32)
        acc_scr[x][...] = jnp.zeros(acc_scr[x].shape, F32)

    @pl.when(qi == 0)
    def _():
        kk = lax.broadcasted_iota(jnp.int32, (blk, blk), 0)
        qq = lax.broadcasted_iota(jnp.int32, (blk, blk), 1)
        visible = (kk // CHUNK) <= (qq // CHUNK)
        for i in range(2):
            rows = jnp.broadcast_to(bvec_ref[0, i], (blk, 2 * blk))
            tile = pltpu.roll(rows, 0, 1, stride=1, stride_axis=0)[:, :blk]
            bias_scr[i] = jnp.where(visible, tile, -jnp.inf) if i == 0 else tile

    def key_block(t):
        return jnp.maximum(qi - t, 0)

    def logits(x, t, bias=None):
        start = pl.multiple_of(key_block(t) * blk, blk)
        s = _dot(k_ref[0, pl.ds(start, blk), :], qp_scr[x][...])
        if bias is not None:
            s = s + bias[:, cols[x]]
        s_scr[x][...] = s
        part = s[0:8]
        for r in range(8, blk, 8):
            part = jnp.maximum(part, s[r:r + 8])
        cm_scr[x][...] = part

    def softmax(x):
        m_old = m_scr[x][...]
        m_new = jnp.maximum(m_old, jnp.max(cm_scr[x][...], axis=0, keepdims=True))
        a_scr[x][...] = jnp.exp2(m_old - m_new)
        m_scr[x][...] = m_new
        for r in range(0, blk, 16):
            p_scr[x][r:r + 16, :] = jnp.exp2((s_scr[x][r:r + 16, :] - m_new).astype(BF16))

    def values(x, t):
        acc_scr[x][...] = a_scr[x][...] * acc_scr[x][...] + _dot(vt_ref[0, 0, key_block(t)], p_scr[x][...])

    def step(t, bias=None):
        for x in range(ns):
            logits(x, t + 1, bias)
            values(x, t)
            softmax((x + 1) % ns)

    for x in range(ns):
        logits(x, 0, bias_scr[0])
    softmax(0)
    step(0, bias_scr[1] + jnp.where(qi == 0, -jnp.inf, 0.0).astype(F32))

    t_last = jnp.maximum(qi, 1)

    def far_pair(u, c):
        step(1 + 2 * u)
        step(2 + 2 * u)
        return c

    lax.fori_loop(0, (t_last - 1) // 2, far_pair, 0)

    @pl.when((t_last - 1) % 2 == 1)
    def _():
        step(t_last - 1)

    for x in range(ns):
        values(x, t_last)
        if x + 1 < ns:
            softmax(x + 1)

    lam = lam_ref[0, 0]
    half = ns // 2
    outs = []
    for x in range(half):
        o0 = acc_scr[x][0:2 * hd, :] * (1.0 / acc_scr[x][2 * hd:2 * hd + 1, :])
        o1 = acc_scr[half + x][0:2 * hd, :] * (1.0 / acc_scr[half + x][2 * hd:2 * hd + 1, :])
        outs.append(o0 - lam * o1)
    ot = outs[0] if half == 1 else jnp.concatenate(outs, axis=1)
    ms = jnp.mean(ot * ot, axis=0, keepdims=True)
    y = (ot * lax.rsqrt(ms + EPS)).T * (g_ref[...] * out_scale)
    o_ref[0] = y.astype(BF16)


def _diff_attn_bounded_kernel(lam_ref, qt_ref, k_ref, vt_ref, bvec_ref, g_ref, o_ref,
                              bias_scr, *scratch, out_scale):
    blk = ATT_BLK
    hd = DIFF_HEAD_DIM
    ns = ATT_STREAMS
    nq = qt_ref.shape[2]
    qp_scr, p0_scr, p1_scr, acc_scr, l_scr = (scratch[i * ns:(i + 1) * ns] for i in range(5))
    p_scr = (p0_scr, p1_scr)
    w = 2 * blk // ns
    lam = lam_ref[0, 0]
    m_ref = lam_ref[0, 1]
    cols = [slice((x % (ns // 2)) * w, (x % (ns // 2) + 1) * w) for x in range(ns)]

    kk = lax.broadcasted_iota(jnp.int32, (blk, blk), 0)
    qq = lax.broadcasted_iota(jnp.int32, (blk, blk), 1)
    visible = (kk // CHUNK) <= (qq // CHUNK)
    for i in range(2):
        rows = jnp.broadcast_to(bvec_ref[0, i], (blk, 2 * blk))
        tile = pltpu.roll(rows, 0, 1, stride=1, stride_axis=0)[:, :blk]
        bias_scr[i] = jnp.where(visible, tile, -jnp.inf) if i == 0 else tile

    def key_block(qi, t):
        return jnp.maximum(qi - t, 0)

    diag_keys = [(x % (ns // 2) + 1) * w for x in range(ns)]
    assert w % CHUNK == 0

    def probs(qi, x, t, bias=None, keys=blk):
        start = pl.multiple_of(key_block(qi, t) * blk, blk)
        s = _dot(k_ref[0, pl.ds(start, keys), :], qp_scr[x][...])
        if bias is not None:
            s = s + bias[0:keys, cols[x]]
        p = jnp.exp2(s - m_ref)
        l_scr[x][...] += jnp.sum(p.reshape(keys // 8, 8, w), axis=0)
        return p.astype(BF16)

    def step(qi, t, slot, bias=None, keys_t=None):
        for x in range(ns):
            p_scr[slot][x][...] = probs(qi, x, t + 1, bias)
            n = blk if keys_t is None else keys_t[x]
            acc_scr[x][...] += _dot(vt_ref[0, 0, key_block(qi, t), 0:2 * hd, 0:n], p_scr[1 - slot][x][0:n, :])

    def begin(qi):
        qt = qt_ref[0, 0, qi]
        row = lax.broadcasted_iota(jnp.int32, qt.shape, 0)
        zero = jnp.zeros_like(qt)
        q_maps = (jnp.where(row < hd, qt, zero), jnp.where(row >= hd, qt, zero))
        for x in range(ns):
            qp_scr[x][...] = q_maps[x // (ns // 2)][:, cols[x]]
            acc_scr[x][...] = jnp.zeros(acc_scr[x].shape, F32)
            l_scr[x][...] = jnp.zeros(l_scr[x].shape, F32)
        for x in range(ns):
            p0_scr[x][0:diag_keys[x], :] = probs(qi, x, 0, bias_scr[0], diag_keys[x])
        step(qi, 0, 1, bias_scr[1] + jnp.where(qi == 0, -jnp.inf, 0.0).astype(F32), diag_keys)

    def sweep(qi):
        n_far = qi - 1

        def far_steps(n):
            def body(u, c):
                for i in range(n):
                    step(qi, 1 + n * u + i, i % 2)
                return c
            return body

        n_octs = n_far // 8
        lax.fori_loop(0, n_octs, far_steps(8), 0)
        t_oct = 1 + 8 * n_octs

        @pl.when(n_far % 8 >= 4)
        def _():
            for i in range(4):
                step(qi, t_oct + i, i % 2)

        t_done = 1 + 4 * (n_far // 4)

        @pl.when(n_far % 4 >= 2)
        def _():
            step(qi, t_done, 0)
            step(qi, t_done + 1, 1)

        @pl.when(n_far % 2 == 1)
        def _():
            step(qi, n_far, 0)

    def finish(qi):
        t_last = jnp.maximum(qi, 1)
        last_in_p1 = t_last % 2 == 1
        for x in range(ns):
            p_last = jnp.where(last_in_p1, p1_scr[x][...], p0_scr[x][...])
            acc_scr[x][...] += _dot(vt_ref[0, 0, key_block(qi, t_last), 0:2 * hd, :], p_last)
        half = ns // 2
        inv = [1.0 / jnp.sum(l_scr[x][...], axis=0, keepdims=True) for x in range(ns)]
        outs = []
        for x in range(half):
            outs.append(acc_scr[x][...] * inv[x] - lam * (acc_scr[half + x][...] * inv[half + x]))
        ot = outs[0] if half == 1 else jnp.concatenate(outs, axis=1)
        ms = jnp.mean(ot * ot, axis=0, keepdims=True)
        y = (ot * lax.rsqrt(ms + EPS)).T * (g_ref[...] * out_scale)
        o_ref[0, pl.ds(pl.multiple_of(qi * blk, blk), blk), :] = y.astype(BF16)

    begin(0)

    def query_block(qi, c):
        finish(qi - 1)
        begin(qi)
        sweep(qi)
        return c

    lax.fori_loop(1, nq, query_block, 0)
    finish(nq - 1)


def _diff_attn(lam, qt, k, vt, bias, g, out_scale, bounded):
    b, nh, nq = qt.shape[0], qt.shape[1], qt.shape[2]
    s = k.shape[1]
    blk = ATT_BLK
    ns = ATT_STREAMS
    w = 2 * blk // ns
    hw = 2 * DIFF_HEAD_DIM
    if bounded:
        body = _diff_attn_bounded_kernel
        per_stream = (((hw, w), BF16), ((blk, w), BF16), ((blk, w), BF16), ((hw, w), F32), ((8, w), F32))
        grid = (b, nh)
        q_spec = pl.BlockSpec((1, 1, nq, hw, blk), lambda bi, hi: (bi, hi, 0, 0, 0))
        o_spec = pl.BlockSpec((1, s, hw), lambda bi, hi: (bi, 0, hi))
        sem = ("parallel", "parallel")
    else:
        body = _diff_attn_kernel
        per_stream = (((hw, w), BF16), ((blk, w), F32), ((8, w), F32), ((blk, w), BF16),
                      ((1, w), F32), ((1, w), F32), ((ATT_V_ROWS, w), F32))
        grid = (b, nh, nq)
        q_spec = pl.BlockSpec((1, 1, 1, hw, blk), lambda bi, hi, qi: (bi, hi, qi, 0, 0))
        o_spec = pl.BlockSpec((1, blk, hw), lambda bi, hi, qi: (bi, qi, hi))
        sem = ("parallel", "parallel", "arbitrary")
    return pl.pallas_call(
        functools.partial(body, out_scale=out_scale),
        grid=grid,
        in_specs=[
            pl.BlockSpec(memory_space=pltpu.SMEM),
            q_spec,
            pl.BlockSpec((1, s, hw), lambda bi, hi, *_: (bi, 0, hi)),
            pl.BlockSpec((1, 1, nq, ATT_V_ROWS, blk), lambda bi, hi, *_: (bi, hi, 0, 0, 0)),
            pl.BlockSpec((1, 2, 1, 2 * blk), lambda bi, hi, *_: (hi, 0, 0, 0)),
            _const_spec((1, hw)),
        ],
        out_specs=o_spec,
        out_shape=jax.ShapeDtypeStruct((b, s, nh * hw), BF16),
        scratch_shapes=[pltpu.VMEM((2, blk, blk), F32)] + [pltpu.VMEM(shape, dtype)
                                                        for shape, dtype in per_stream for _ in range(ns)],
        compiler_params=_cparams(*sem),
        name="diff_attn_bounded" if bounded else "diff_attn",
    )(lam, qt, k, vt, bias, g)


def _pre_gla_kernel(x_ref, g_ref, w_ref, gsum_ref, mg_ref, gw_ref, gb_ref, tri_ref,
                    q_ref, k_ref, v_ref, r_ref, gc_ref, mq_ref):
    kw = GLA_HEADS * GLA_KP
    vw = TOKEN_WIDTH
    h = _rms_rows(x_ref[...], g_ref[...]).astype(BF16)
    q_ref[...] = _dot(h, w_ref[:, 0:kw]).astype(BF16)
    k_ref[...] = _dot(h, w_ref[:, kw:2 * kw]).astype(BF16)
    o = 2 * kw
    v_ref[...] = _dot(h, w_ref[:, o:o + vw]).astype(BF16)
    r_ref[...] = _dot(h, w_ref[:, o + vw:o + 2 * vw]).astype(BF16)
    o = o + 2 * vw
    gate_low = _dot(h, w_ref[:, o:o + LANES]).astype(BF16)
    mq_ref[...] = _group_rms(_dot(h, w_ref[:, o + LANES:]), gsum_ref[...], mg_ref[...]).astype(BF16)
    z = _dot(gate_low, gw_ref[...]) + gb_ref[...]
    log_a = (jnp.minimum(z, 0.0) - jnp.log1p(jnp.exp(-jnp.abs(z)))) * (1.0 / GLA_GATE_TAU)
    hi = log_a.astype(BF16)
    lo = (log_a - hi.astype(F32)).astype(BF16)
    tri = tri_ref[...]
    n = tri.shape[0]
    for c in range(log_a.shape[0] // n):
        rows = slice(c * n, (c + 1) * n)
        gc_ref[rows, :] = _dot(tri, hi[rows]) + _dot(tri, lo[rows])


def _pre_gla(x2, g, w, gsum, mg, gw, gb, tri):
    t = x2.shape[0]
    kw = GLA_HEADS * GLA_KP
    vw = TOKEN_WIDTH
    row = lambda n: pl.BlockSpec((ROW_TILE, n), lambda i: (i, 0))
    return pl.pallas_call(
        _pre_gla_kernel,
        grid=(t // ROW_TILE,),
        in_specs=[row(D_MODEL), _const_spec((1, D_MODEL)), _const_spec(w.shape),
                  _const_spec(gsum.shape), _const_spec((1, MEM_WIDTH)), _const_spec(gw.shape),
                  _const_spec(gb.shape), _const_spec(tri.shape)],
        out_specs=[row(kw), row(kw), row(vw), row(vw), row(kw), row(MEM_WIDTH)],
        out_shape=[jax.ShapeDtypeStruct((t, kw), BF16), jax.ShapeDtypeStruct((t, kw), BF16),
                   jax.ShapeDtypeStruct((t, vw), BF16), jax.ShapeDtypeStruct((t, vw), BF16),
                   jax.ShapeDtypeStruct((t, kw), F32), jax.ShapeDtypeStruct((t, MEM_WIDTH), BF16)],
        compiler_params=_cparams("parallel"),
        name="pre_gla",
    )(x2, g, w, gsum, mg, gw, gb, tri)


def _gla_kernel(q_ref, k_ref, v_ref, r_ref, gc_ref, gain_ref, o_ref, s_scr):
    tg = GLA_TILE
    nchunk = tg // CHUNK
    heads = range(GLA_HEADS)
    ks = [slice(h * GLA_KP, (h + 1) * GLA_KP) for h in heads]
    vs = [slice(h * GLA_VP, (h + 1) * GLA_VP) for h in heads]
    pad_lane = lax.broadcasted_iota(jnp.int32, (tg, GLA_VP), 1) >= GLA_V_DIM

    def head_cols(ref, h):
        first = h * GLA_V_DIM
        aligned = first // LANES * LANES
        win = ref[0, :, aligned:aligned + GLA_VP].astype(F32)
        if first != aligned:
            win = pltpu.roll(win, GLA_VP - (first - aligned), 1)
        return jnp.where(pad_lane, 0.0, win)

    @pl.when(pl.program_id(1) == 0)
    def _():
        s_scr[...] = jnp.zeros(s_scr.shape, F32)

    ri = lax.broadcasted_iota(jnp.int32, (tg, tg), 0)
    ci = lax.broadcasted_iota(jnp.int32, (tg, tg), 1)
    same_chunk = (ri // CHUNK) == (ci // CHUNK)
    past = ci <= ri
    row_chunk = lax.broadcasted_iota(jnp.int32, (tg, GLA_KP), 0) // CHUNK

    qe, scores, kv, decay, v_pad = [], [], [], [], []
    for h in heads:
        qh = q_ref[0, :, ks[h]].astype(F32) * (GLA_K_DIM ** -0.5)
        kh = k_ref[0, :, ks[h]].astype(F32)
        g = gc_ref[0, :, ks[h]]
        eg = jnp.exp(g)
        ieg = jnp.exp(-g)
        qe.append((qh * eg).astype(BF16))
        a_past = _dot_nt(qe[h], (kh * ieg).astype(BF16))
        a_fut = _dot_nt((qh * ieg).astype(BF16), (kh * eg).astype(BF16))
        scores.append(jnp.where(same_chunk, jnp.where(past, a_past, a_fut), 0.0).astype(BF16))
        v_pad.append(head_cols(v_ref, h))
        vt = v_pad[h].T.astype(BF16)
        g_last = [g[c * CHUNK + CHUNK - 1:c * CHUNK + CHUNK, :] for c in range(nchunk)]
        g_end = jnp.concatenate([jnp.broadcast_to(gl, (CHUNK, GLA_KP)) for gl in g_last], axis=0)
        kdec = kh * jnp.exp(g_end - g)
        kd_chunks = jnp.concatenate([jnp.where(row_chunk == c, kdec, 0.0) for c in range(nchunk)], axis=1)
        kv_all = _dot(vt, kd_chunks.astype(BF16))
        kv.append([kv_all[:, c * GLA_KP:(c + 1) * GLA_KP] for c in range(nchunk)])
        decay.append([jnp.exp(gl) for gl in g_last])

    starts = []
    for h in heads:
        st = s_scr[h]
        per_chunk = []
        for c in range(nchunk):
            per_chunk.append(st.astype(BF16))
            st = st * decay[h][c] + kv[h][c]
        s_scr[h] = st
        starts.append(per_chunk)

    gated = []
    for h in heads:
        inter = [_dot_nt(qe[h][c * CHUNK:(c + 1) * CHUNK], starts[h][c]) for c in range(nchunk)]
        o = _dot(scores[h], v_pad[h].astype(BF16)) + jnp.concatenate(inter, axis=0)
        ms = jnp.sum(o * o, axis=-1, keepdims=True) * (1.0 / GLA_V_DIM)
        y = o * lax.rsqrt(ms + EPS) * gain_ref[:, vs[h]]
        half_r = 0.5 * head_cols(r_ref, h)
        gated.append(y * half_r * (1.0 + jnp.tanh(half_r)))

    pair = 2 * GLA_V_DIM
    blank = jnp.zeros((tg, GLA_VP), F32)
    for p in range(GLA_HEADS // 2):
        even = jnp.concatenate([gated[2 * p], blank], axis=1)
        odd = pltpu.roll(jnp.concatenate([gated[2 * p + 1], blank], axis=1), GLA_V_DIM, 1)
        o_ref[0, :, p * pair:(p + 1) * pair] = (even + odd)[:, :pair].astype(BF16)


def _gla(q, k, v, r, gc, gain):
    b, s = q.shape[0], q.shape[1]
    kw = GLA_HEADS * GLA_KP
    vw = GLA_HEADS * GLA_VP
    spec = lambda n: pl.BlockSpec((1, GLA_TILE, n), lambda bi, i: (bi, i, 0))
    return pl.pallas_call(
        _gla_kernel,
        grid=(b, s // GLA_TILE),
        in_specs=[spec(kw), spec(kw), spec(TOKEN_WIDTH), spec(TOKEN_WIDTH), spec(kw), _const_spec((1, vw))],
        out_specs=spec(TOKEN_WIDTH),
        out_shape=jax.ShapeDtypeStruct((b, s, TOKEN_WIDTH), BF16),
        scratch_shapes=[pltpu.VMEM((GLA_HEADS, GLA_VP, GLA_KP), F32)],
        compiler_params=_cparams("parallel", "arbitrary"),
        name="gla",
    )(q, k, v, r, gc, gain)


def _mem_kv_kernel(mem_ref, g_ref, w_ref, gsum_ref, kg_ref, k_ref, v_ref):
    h = _rms_rows(mem_ref[...], g_ref[...]).astype(BF16)
    k_ref[...] = _group_rms(_dot(h, w_ref[:, :MEM_WIDTH]), gsum_ref[...], kg_ref[...]).astype(BF16)
    v_ref[...] = _dot(h, w_ref[:, MEM_WIDTH:]).astype(BF16)


def _mem_kv(mem2, g, w_all, layer, gsum, kg):
    n = mem2.shape[0]
    return pl.pallas_call(
        _mem_kv_kernel,
        grid=(1,),
        in_specs=[_const_spec(mem2.shape), _const_spec((1, D_MODEL)), _layer_spec(w_all, layer),
                  _const_spec(gsum.shape), _const_spec((1, MEM_WIDTH))],
        out_specs=[_const_spec((n, MEM_WIDTH)), _const_spec((n, MEM_WIDTH))],
        out_shape=[jax.ShapeDtypeStruct((n, MEM_WIDTH), BF16)] * 2,
        compiler_params=_cparams("arbitrary"),
        name="mem_kv",
    )(mem2, g, w_all, gsum, kg)


def _mix_out_kernel(x_ref, mix_ref, mq_ref, kbd_ref, vbd_ref, wa_ref, wb_ref, o_ref):
    m = kbd_ref.shape[2] // MEM_HEADS
    logits = _dot(mq_ref[0], kbd_ref[0])
    ps = []
    for h in range(MEM_HEADS):
        s = logits[:, h * m:(h + 1) * m]
        e = jnp.exp(s - jnp.max(s, axis=-1, keepdims=True))
        ps.append((e * (1.0 / jnp.sum(e, axis=-1, keepdims=True))).astype(BF16))
    cross = _dot(jnp.concatenate(ps, axis=1), vbd_ref[0])
    o_ref[0] = x_ref[0] + _dot(mix_ref[0], wa_ref[...]) + _dot(cross.astype(BF16), wb_ref[...])


def _mix_out(x, mix, mq, kbd, vbd, wa, wa_spec, wb, wb_spec):
    b, s = x.shape[0], x.shape[1]
    spec = lambda n: pl.BlockSpec((1, MIX_TILE, n), lambda bi, i: (bi, i, 0))
    per_b = lambda a: pl.BlockSpec((1,) + a.shape[1:], lambda bi, i: (bi, 0, 0))
    return pl.pallas_call(
        _mix_out_kernel,
        grid=(b, s // MIX_TILE),
        in_specs=[spec(D_MODEL), spec(mix.shape[2]), spec(MEM_WIDTH), per_b(kbd), per_b(vbd),
                  wa_spec, wb_spec],
        out_specs=spec(D_MODEL),
        out_shape=jax.ShapeDtypeStruct(x.shape, F32),
        compiler_params=_cparams("parallel", "parallel"),
        name="mix_out",
    )(x, mix, mq, kbd, vbd, wa, wb)


def _ffn_kernel(x_ref, g_ref, wu_ref, cw_ref, cb_ref, wd_ref, o_ref, carry_scr, act_scr, *shift_scr):
    tm = FFN_TILE
    cr = CARRY_ROWS

    @pl.when(pl.program_id(1) == 0)
    def _():
        carry_scr[...] = jnp.zeros(carry_scr.shape, F32)

    x = x_ref[0]
    h = _rms_rows(x, g_ref[...]).astype(BF16)

    def conv(cols, bufs):
        u = _dot(h, wu_ref[:, cols])
        prev = carry_scr[:, cols]
        for shift, buf in zip((1, 2), bufs):
            buf[shift:shift + cr, :] = prev
            buf[cr + shift:cr + shift + tm, :] = u
        carry_scr[:, cols] = u[tm - cr:tm]
        return (cw_ref[0:1, cols] * bufs[1][cr:cr + tm, :] + cw_ref[1:2, cols] * bufs[0][cr:cr + tm, :]
                + cw_ref[2:3, cols] * u + cb_ref[:, cols])

    for j in range(D_FF // FFN_COLS):
        bufs = shift_scr[4 * (j % 2):4 * (j % 2) + 4]
        a = conv(slice(j * FFN_COLS, (j + 1) * FFN_COLS), bufs[0:2])
        half_g = 0.5 * conv(slice(D_FF + j * FFN_COLS, D_FF + (j + 1) * FFN_COLS), bufs[2:4])
        act_scr[:, j * FFN_COLS:(j + 1) * FFN_COLS] = (a * half_g * (1.0 + jnp.tanh(half_g))).astype(BF16)

    o_ref[0] = x + _dot(act_scr[...], wd_ref[...])


def _ffn(x, g, wu_all, cw, cb, wd_all, layer):
    b, s = x.shape[0], x.shape[1]
    spec = pl.BlockSpec((1, FFN_TILE, D_MODEL), lambda bi, i: (bi, i, 0))
    return pl.pallas_call(
        _ffn_kernel,
        grid=(b, s // FFN_TILE),
        in_specs=[spec, _const_spec((1, D_MODEL)), _layer_spec(wu_all, layer, single=True), _const_spec(cw.shape),
                  _const_spec(cb.shape), _layer_spec(wd_all, layer, single=True)],
        out_specs=spec,
        out_shape=jax.ShapeDtypeStruct(x.shape, F32),
        scratch_shapes=[pltpu.VMEM((CARRY_ROWS, 2 * D_FF), F32), pltpu.VMEM((FFN_TILE, D_FF), BF16)]
        + [pltpu.VMEM((FFN_TILE + 2 * CARRY_ROWS, FFN_COLS), F32)] * 8,
        compiler_params=_cparams("parallel", "arbitrary"),
        name="ffn",
    )(x, g, wu_all, cw, cb, wd_all)


def _t5_bucket(rel):
    half = REL_BUCKETS // 2
    max_exact = half // 2
    ret = jnp.where(rel > 0, half, 0)
    n = jnp.abs(rel)
    nf = jnp.maximum(n, 1).astype(jnp.float32)
    large = max_exact + (jnp.log(nf / max_exact) / math.log(REL_MAX_DIST / max_exact)
                         * (half - max_exact)).astype(jnp.int32)
    large = jnp.minimum(large, half - 1)
    return ret + jnp.where(n < max_exact, n, large)


def _bias_vectors(rel_bias):
    blk = ATT_BLK
    assert blk >= REL_MAX_DIST
    table = rel_bias.astype(F32).T[:, :, None]

    def lookup(rel):
        bucket = _t5_bucket(rel)
        out = jnp.zeros((table.shape[0],) + rel.shape, F32)
        for i in range(REL_BUCKETS):
            out = jnp.where(bucket == i, table[:, i], out)
        return out

    j = jnp.arange(2 * blk)
    dist = jnp.where(j < blk, -j, 2 * blk - j)
    values = lookup(jnp.concatenate([dist, dist - blk, jnp.full((1,), -2 * blk)]))
    vectors = (values[:, :4 * blk] - values[:, 4 * blk:]) * LOG2E
    return vectors.reshape(-1, 2, 1, 2 * blk)


def _group_sum_matrix():
    i = jnp.arange(MXU_DIM)
    return ((i[:, None] // NORM_GROUP) == (i[None, :] // NORM_GROUP)).astype(BF16)


def _chunk_tri_matrix(n):
    i = jnp.arange(n)
    return (((i[:, None] // CHUNK) == (i[None, :] // CHUNK)) & (i[None, :] <= i[:, None])).astype(BF16)


def _pad_heads(w, heads, dim, pad, axis):
    shape = list(w.shape)
    shape[axis:axis + 1] = [heads, dim]
    w = w.reshape(shape)
    widths = [(0, 0)] * w.ndim
    widths[axis + 1] = (0, pad - dim)
    w = jnp.pad(w, widths)
    shape[axis:axis + 2] = [heads * pad]
    return w.reshape(shape)


def _tile_gain(g, reps, scale=1.0):
    return (jnp.tile(g.astype(F32), reps) * scale)[None, :]


def _mem_block_diag(kn, v, b):
    m = kn.shape[0] // b
    eye = jnp.eye(MEM_HEADS, dtype=BF16)
    knt = kn.reshape(b, m, MEM_WIDTH).transpose(0, 2, 1)
    kbd = (knt.reshape(b, MEM_HEADS, MEM_HEAD_DIM, 1, m) * eye.reshape(1, MEM_HEADS, 1, MEM_HEADS, 1))
    kbd = kbd.reshape(b, MEM_WIDTH, MEM_HEADS * m)
    vbd = (v.reshape(b, 1, m, MEM_HEADS, MEM_HEAD_DIM) * eye.reshape(1, MEM_HEADS, 1, MEM_HEADS, 1))
    vbd = vbd.reshape(b, MEM_HEADS * m, MEM_WIDTH)
    return kbd, vbd


def kernel(x, mem, rel_bias, attn_norm, ffn_norm, mem_norm, w_in_diff, diff_qk_norm, diff_lambda,
           diff_out_norm, w_in_gla, gla_gate_w, gla_gate_b, gla_out_norm, w_mem_kv, mem_qk_norm,
           w_out, w_up, conv_w, conv_b, w_down):
    b, s, d = x.shape
    t = b * s
    tw = TOKEN_WIDTH
    gsum = _group_sum_matrix()
    mem2 = mem.reshape(b * mem.shape[1], d)
    x = x.astype(F32)
    w_in_diff, w_mem_kv, w_out, w_up, w_down = map(_to_bf16, (w_in_diff, w_mem_kv, w_out, w_up, w_down))

    for i in range(DEPTH):
        j = i // 2
        x2 = x.reshape(t, d)
        mq_gain = _tile_gain(mem_qk_norm[i, 0], MEM_HEADS, MEM_HEAD_DIM ** -0.5)
        if i % 2 == 0:
            nh, hd = DIFF_HEADS, DIFF_HEAD_DIM
            qt, k, vt, mq = _pre_diff(
                x2, attn_norm[i][None, :], w_in_diff, j, gsum,
                _tile_gain(diff_qk_norm[j, 0], 2 * nh, hd ** -0.5 * LOG2E),
                _tile_gain(diff_qk_norm[j, 1], 2 * nh), mq_gain, b)
            lv = diff_lambda[j].astype(F32)
            lam_init = 0.8 - 0.6 * math.exp(-0.3 * i)
            lam = jnp.exp(jnp.sum(lv[0] * lv[1])) - jnp.exp(jnp.sum(lv[2] * lv[3])) + lam_init
            bvec = _bias_vectors(rel_bias)
            qk_bound = (hd ** 0.5 * LOG2E * ATT_ROUNDING_SLACK
                        * jnp.max(jnp.abs(diff_qk_norm[j, 0])) * jnp.max(jnp.abs(diff_qk_norm[j, 1]))).astype(F32)
            hi = qk_bound + jnp.maximum(jnp.max(bvec), 0.0)
            lo = -qk_bound + jnp.minimum(jnp.min(bvec), 0.0)
            scalars = jnp.stack([lam.astype(F32), hi]).reshape(1, 2)
            attend = lambda bounded: functools.partial(
                _diff_attn, qt=qt, k=k.reshape(b, s, tw), vt=vt, bias=bvec,
                g=diff_out_norm[j].astype(F32)[None, :], out_scale=1.0 - lam_init, bounded=bounded)
            mix = lax.cond(hi - lo <= ATT_MAX_EXP2_SPAN, attend(True), attend(False), scalars)
        else:
            kw = GLA_HEADS * GLA_K_DIM
            w = w_in_gla[j]
            hp = functools.partial(_pad_heads, heads=GLA_HEADS, axis=1)
            w_p = jnp.concatenate([
                hp(w[:, :kw], dim=GLA_K_DIM, pad=GLA_KP),
                hp(w[:, kw:2 * kw], dim=GLA_K_DIM, pad=GLA_KP),
                w[:, 2 * kw:2 * kw + 2 * tw],
                jnp.pad(w[:, 2 * kw + 2 * tw:2 * kw + 2 * tw + GLA_GATE_RANK],
                        ((0, 0), (0, LANES - GLA_GATE_RANK))),
                w[:, 2 * kw + 2 * tw + GLA_GATE_RANK:]], axis=1).astype(BF16)
            gw = jnp.pad(hp(gla_gate_w[j], dim=GLA_K_DIM, pad=GLA_KP),
                         ((0, LANES - GLA_GATE_RANK), (0, 0))).astype(BF16)
            gb = _pad_heads(gla_gate_b[j].astype(F32)[None, :], GLA_HEADS, GLA_K_DIM, GLA_KP, 1)
            q, k, v, r, gc, mq = _pre_gla(x2, attn_norm[i][None, :], w_p, gsum, mq_gain, gw, gb,
                                          _chunk_tri_matrix(MXU_DIM))
            gain = _pad_heads(jnp.tile(gla_out_norm[j].astype(F32), GLA_HEADS)[None, :],
                              GLA_HEADS, GLA_V_DIM, GLA_VP, 1)
            sh = lambda a: a.reshape(b, s, a.shape[1])
            mix = _gla(sh(q), sh(k), sh(v), sh(r), sh(gc), gain)

        kn, vm = _mem_kv(mem2, mem_norm[i][None, :], w_mem_kv, i, gsum,
                         _tile_gain(mem_qk_norm[i, 1], MEM_HEADS))
        kbd, vbd = _mem_block_diag(kn, vm, b)
        x = _mix_out(x, mix.reshape(b, s, -1), mq.reshape(b, s, MEM_WIDTH), kbd, vbd,
                     w_out, _layer_spec(w_out, i, rows=(0, tw)), w_out, _layer_spec(w_out, i, rows=(tw, MEM_WIDTH)))
        x = _ffn(x, ffn_norm[i][None, :], w_up, conv_w[i].astype(F32), conv_b[i].astype(F32)[None, :], w_down, i)
    return x
```

```python
import functools
import math

import jax
import jax.numpy as jnp
from jax import lax
from jax.experimental import pallas as pl
from jax.experimental.pallas import tpu as pltpu

F32 = jnp.float32
BF16 = jnp.bfloat16

D_MODEL = 1024
DEPTH = 2
CHUNK = 64
MEM_WIDTH = D_MODEL // 4
MEM_HEADS = 4
MEM_HEAD_DIM = MEM_WIDTH // MEM_HEADS
TOKEN_WIDTH = D_MODEL - MEM_WIDTH
DIFF_HEAD_DIM = 64
DIFF_HEADS = TOKEN_WIDTH // (2 * DIFF_HEAD_DIM)
GLA_HEADS = 4
GLA_V_DIM = TOKEN_WIDTH // GLA_HEADS
GLA_K_DIM = GLA_V_DIM // 2
GLA_GATE_RANK = 16
GLA_GATE_TAU = 16.0
REL_BUCKETS = 32
REL_MAX_DIST = 128
D_FF = ((8 * D_MODEL // 3 + 127) // 128) * 128
EPS = 1e-6
LOG2E = math.log2(math.e)
NORM_GROUP = DIFF_HEAD_DIM
assert MEM_HEAD_DIM == NORM_GROUP

LANES = 128
SUBLANES = 8
MXU_DIM = 256
VMEM_LIMIT_BYTES = 56 * 1024 * 1024

ROW_TILE = 1024
MIX_TILE = 1024
ATT_BLK = 512
ATT_STREAMS = 4
ATT_MAX_EXP2_SPAN = 100.0
ATT_ROUNDING_SLACK = 1.02
ATT_V_ROWS = 2 * DIFF_HEAD_DIM + 2 * SUBLANES
GLA_TILE = 256
FFN_TILE = 1024
FFN_COLS = 256
GLA_KP = 128
GLA_VP = 256
CARRY_ROWS = 8
CAST_BLOCK_BYTES = 4 * 1024 * 1024
CAST_BUFFERS = 3


def _cparams(*sem):
    return pltpu.CompilerParams(dimension_semantics=sem, vmem_limit_bytes=VMEM_LIMIT_BYTES)


def _const_spec(shape):
    n = len(shape)
    return pl.BlockSpec(shape, lambda *_: (0,) * n)


def _layer_spec(w_all, layer, rows=None, single=False):
    first, n_rows = (0, w_all.shape[1]) if rows is None else rows
    assert first % n_rows == 0
    mode = dict(pipeline_mode=pl.Buffered(1)) if single else {}
    return pl.BlockSpec((None, n_rows, w_all.shape[2]), lambda *_: (layer, first // n_rows, 0), **mode)


def _cast_kernel(w_hbm, o_hbm, in_buf, out_buf, in_sem, out_sem):
    nblk = w_hbm.shape[0]
    nb = min(CAST_BUFFERS, nblk)

    def read(i, slot):
        return pltpu.make_async_copy(w_hbm.at[i], in_buf.at[slot], in_sem.at[slot])

    def write(i, slot):
        return pltpu.make_async_copy(out_buf.at[slot], o_hbm.at[i], out_sem.at[slot])

    for s in range(nb):
        read(s, s).start()

    def band(i, c):
        slot = i % nb
        read(i, slot).wait()

        @pl.when(i >= nb)
        def _():
            write(i - nb, slot).wait()

        out_buf[slot] = in_buf[slot].astype(BF16)
        write(i, slot).start()

        @pl.when(i + nb < nblk)
        def _():
            read(i + nb, slot).start()
        return c

    lax.fori_loop(0, nblk, band, 0)
    for i in range(nblk - nb, nblk):
        write(i, i % nb).wait()


def _to_bf16(w):
    n, r, c = w.shape
    rows = max(rb for rb in range(16, r + 1, 16) if r % rb == 0 and rb * c * 4 <= CAST_BLOCK_BYTES)
    bands = n * r // rows
    out = pl.pallas_call(
        _cast_kernel,
        in_specs=[pl.BlockSpec(memory_space=pl.ANY)], out_specs=pl.BlockSpec(memory_space=pl.ANY),
        out_shape=jax.ShapeDtypeStruct((bands, rows, c), BF16),
        scratch_shapes=[pltpu.VMEM((CAST_BUFFERS, rows, c), F32), pltpu.VMEM((CAST_BUFFERS, rows, c), BF16),
                        pltpu.SemaphoreType.DMA((CAST_BUFFERS,)), pltpu.SemaphoreType.DMA((CAST_BUFFERS,))],
        compiler_params=pltpu.CompilerParams(vmem_limit_bytes=VMEM_LIMIT_BYTES), name="to_bf16",
    )(w.astype(F32).reshape(bands, rows, c))
    return out.reshape(n, r, c)


def _rms_rows(x, g):
    ms = jnp.mean(x * x, axis=-1, keepdims=True)
    return x * lax.rsqrt(ms + EPS) * g


def _group_rms(t, gsum, gain):
    cols = []
    for c in range(t.shape[1] // MXU_DIM):
        blk = t[:, c * MXU_DIM:(c + 1) * MXU_DIM]
        ss = jnp.dot((blk * blk).astype(BF16), gsum, preferred_element_type=F32)
        cols.append(blk * lax.rsqrt(ss * (1.0 / NORM_GROUP) + EPS))
    out = cols[0] if len(cols) == 1 else jnp.concatenate(cols, axis=1)
    return out * gain


def _dot(a, b):
    return jnp.dot(a, b, preferred_element_type=F32)


def _dot_nt(a, b):
    return lax.dot_general(a, b, (((1,), (1,)), ((), ())), preferred_element_type=F32)


def _pre_diff_kernel(x_ref, g_ref, w_ref, gsum_ref, qg_ref, kg_ref, mg_ref,
                     qt_ref, k_ref, vt_ref, mq_ref):
    tw = TOKEN_WIDTH
    hw = 2 * DIFF_HEAD_DIM
    h = _rms_rows(x_ref[...], g_ref[...]).astype(BF16)
    gsum = gsum_ref[...]
    q = _group_rms(_dot(h, w_ref[:, 0:tw]), gsum, qg_ref[...])
    k_ref[...] = _group_rms(_dot(h, w_ref[:, tw:2 * tw]), gsum, kg_ref[...]).astype(BF16)
    v = _dot(h, w_ref[:, 2 * tw:3 * tw])
    mq_ref[...] = _group_rms(_dot(h, w_ref[:, 3 * tw:]), gsum, mg_ref[...]).astype(BF16)
    ones = jnp.ones((ATT_V_ROWS - hw, ATT_BLK), BF16)
    for j in range(ROW_TILE // ATT_BLK):
        rows = slice(j * ATT_BLK, (j + 1) * ATT_BLK)
        for n in range(DIFF_HEADS):
            qt_ref[0, n, j] = q[rows, n * hw:(n + 1) * hw].T.astype(BF16)
            vt_ref[0, n, j, 0:hw, :] = v[rows, n * hw:(n + 1) * hw].T.astype(BF16)
            vt_ref[0, n, j, hw:, :] = ones


def _pre_diff(x2, g, w_all, layer, gsum, qg, kg, mg, b):
    t = x2.shape[0]
    tw = TOKEN_WIDTH
    hw = 2 * DIFF_HEAD_DIM
    per_tile = ROW_TILE // ATT_BLK
    nq = t // b // ATT_BLK
    tiles = nq // per_tile
    row = lambda n: pl.BlockSpec((ROW_TILE, n), lambda i: (i, 0))
    per_head = lambda r: pl.BlockSpec((1, DIFF_HEADS, per_tile, r, ATT_BLK),
                                      lambda i: (i // tiles, 0, i % tiles, 0, 0))
    return pl.pallas_call(
        _pre_diff_kernel,
        grid=(t // ROW_TILE,),
        in_specs=[row(D_MODEL), _const_spec((1, D_MODEL)), _layer_spec(w_all, layer),
                  _const_spec(gsum.shape), _const_spec((1, tw)), _const_spec((1, tw)),
                  _const_spec((1, MEM_WIDTH))],
        out_specs=[per_head(hw), row(tw), per_head(ATT_V_ROWS), row(MEM_WIDTH)],
        out_shape=[jax.ShapeDtypeStruct((b, DIFF_HEADS, nq, hw, ATT_BLK), BF16),
                   jax.ShapeDtypeStruct((t, tw), BF16),
                   jax.ShapeDtypeStruct((b, DIFF_HEADS, nq, ATT_V_ROWS, ATT_BLK), BF16),
                   jax.ShapeDtypeStruct((t, MEM_WIDTH), BF16)],
        compiler_params=_cparams("parallel"),
        name="pre_diff",
    )(x2, g, w_all, gsum, qg, kg, mg)


def _diff_attn_kernel(lam_ref, qt_ref, k_ref, vt_ref, bvec_ref, g_ref, o_ref,
                      bias_scr, *scratch, out_scale):
    blk = ATT_BLK
    hd = DIFF_HEAD_DIM
    ns = ATT_STREAMS
    qp_scr, s_scr, cm_scr, p_scr, a_scr, m_scr, acc_scr = (scratch[i * ns:(i + 1) * ns] for i in range(7))
    w = 2 * blk // ns
    qi = pl.program_id(2)
    qt = qt_ref[0, 0, 0]
    row = lax.broadcasted_iota(jnp.int32, qt.shape, 0)
    zero = jnp.zeros_like(qt)
    q_maps = (jnp.where(row < hd, qt, zero), jnp.where(row >= hd, qt, zero))
    cols = [slice((x % (ns // 2)) * w, (x % (ns // 2) + 1) * w) for x in range(ns)]
    for x in range(ns):
        qp_scr[x][...] = q_maps[x // (ns // 2)][:, cols[x]]
        m_scr[x][...] = jnp.full(m_scr[x].shape, -jnp.inf, F32)
        acc_scr[x][...] = jnp.zeros(acc_scr[x].shape, F32)

    @pl.when(qi == 0)
    def _():
        kk = lax.broadcasted_iota(jnp.int32, (blk, blk), 0)
        qq = lax.broadcasted_iota(jnp.int32, (blk, blk), 1)
        visible = (kk // CHUNK) <= (qq // CHUNK)
        for i in range(2):
            rows = jnp.broadcast_to(bvec_ref[0, i], (blk, 2 * blk))
            tile = pltpu.roll(rows, 0, 1, stride=1, stride_axis=0)[:, :blk]
            bias_scr[i] = jnp.where(visible, tile, -jnp.inf) if i == 0 else tile

    def key_block(t):
        return jnp.maximum(qi - t, 0)

    def logits(x, t, bias=None):
        start = pl.multiple_of(key_block(t) * blk, blk)
        s = _dot(k_ref[0, pl.ds(start, blk), :], qp_scr[x][...])
        if bias is not None:
            s = s + bias[:, cols[x]]
        s_scr[x][...] = s
        part = s[0:SUBLANES]
        for r in range(SUBLANES, blk, SUBLANES):
            part = jnp.maximum(part, s[r:r + SUBLANES])
        cm_scr[x][...] = part

    def softmax(x):
        m_old = m_scr[x][...]
        m_new = jnp.maximum(m_old, jnp.max(cm_scr[x][...], axis=0, keepdims=True))
        a_scr[x][...] = jnp.exp2(m_old - m_new)
        m_scr[x][...] = m_new
        for r in range(0, blk, 16):
            p_scr[x][r:r + 16, :] = jnp.exp2((s_scr[x][r:r + 16, :] - m_new).astype(BF16))

    def values(x, t):
        acc_scr[x][...] = a_scr[x][...] * acc_scr[x][...] + _dot(vt_ref[0, 0, key_block(t)], p_scr[x][...])

    def step(t, bias=None):
        for x in range(ns):
            logits(x, t + 1, bias)
            values(x, t)
            softmax((x + 1) % ns)

    for x in range(ns):
        logits(x, 0, bias_scr[0])
    softmax(0)
    step(0, bias_scr[1] + jnp.where(qi == 0, -jnp.inf, 0.0).astype(F32))

    t_last = jnp.maximum(qi, 1)

    def far_pair(u, c):
        step(1 + 2 * u)
        step(2 + 2 * u)
        return c

    lax.fori_loop(0, (t_last - 1) // 2, far_pair, 0)

    @pl.when((t_last - 1) % 2 == 1)
    def _():
        step(t_last - 1)

    for x in range(ns):
        values(x, t_last)
        if x + 1 < ns:
            softmax(x + 1)

    lam = lam_ref[0, 0]
    half = ns // 2
    outs = []
    for x in range(half):
        o0 = acc_scr[x][0:2 * hd, :] * (1.0 / acc_scr[x][2 * hd:2 * hd + 1, :])
        o1 = acc_scr[half + x][0:2 * hd, :] * (1.0 / acc_scr[half + x][2 * hd:2 * hd + 1, :])
        outs.append(o0 - lam * o1)
    ot = outs[0] if half == 1 else jnp.concatenate(outs, axis=1)
    ms = jnp.mean(ot * ot, axis=0, keepdims=True)
    y = (ot * lax.rsqrt(ms + EPS)).T * (g_ref[...] * out_scale)
    o_ref[0] = y.astype(BF16)


def _diff_attn_bounded_kernel(lam_ref, qt_ref, k_ref, vt_ref, bvec_ref, g_ref, o_ref,
                              bias_scr, *scratch, out_scale):
    blk = ATT_BLK
    hd = DIFF_HEAD_DIM
    ns = ATT_STREAMS
    nq = qt_ref.shape[2]
    qp_scr, p0_scr, p1_scr, acc_scr, l_scr = (scratch[i * ns:(i + 1) * ns] for i in range(5))
    p_scr = (p0_scr, p1_scr)
    w = 2 * blk // ns
    lam = lam_ref[0, 0]
    m_ref = lam_ref[0, 1]
    cols = [slice((x % (ns // 2)) * w, (x % (ns // 2) + 1) * w) for x in range(ns)]

    kk = lax.broadcasted_iota(jnp.int32, (blk, blk), 0)
    qq = lax.broadcasted_iota(jnp.int32, (blk, blk), 1)
    visible = (kk // CHUNK) <= (qq // CHUNK)
    for i in range(2):
        rows = jnp.broadcast_to(bvec_ref[0, i], (blk, 2 * blk))
        tile = pltpu.roll(rows, 0, 1, stride=1, stride_axis=0)[:, :blk]
        bias_scr[i] = jnp.where(visible, tile, -jnp.inf) if i == 0 else tile

    def key_block(qi, t):
        return jnp.maximum(qi - t, 0)

    diag_keys = [(x % (ns // 2) + 1) * w for x in range(ns)]
    assert w % CHUNK == 0

    def probs(qi, x, t, bias=None, keys=blk):
        start = pl.multiple_of(key_block(qi, t) * blk, blk)
        s = _dot(k_ref[0, pl.ds(start, keys), :], qp_scr[x][...])
        if bias is not None:
            s = s + bias[0:keys, cols[x]]
        p = jnp.exp2(s - m_ref)
        l_scr[x][...] += jnp.sum(p.reshape(keys // SUBLANES, SUBLANES, w), axis=0)
        return p.astype(BF16)

    def step(qi, t, slot, bias=None, keys_t=None):
        for x in range(ns):
            p_scr[slot][x][...] = probs(qi, x, t + 1, bias)
            n = blk if keys_t is None else keys_t[x]
            acc_scr[x][...] += _dot(vt_ref[0, 0, key_block(qi, t), 0:2 * hd, 0:n], p_scr[1 - slot][x][0:n, :])

    def begin(qi):
        qt = qt_ref[0, 0, qi]
        row = lax.broadcasted_iota(jnp.int32, qt.shape, 0)
        zero = jnp.zeros_like(qt)
        q_maps = (jnp.where(row < hd, qt, zero), jnp.where(row >= hd, qt, zero))
        for x in range(ns):
            qp_scr[x][...] = q_maps[x // (ns // 2)][:, cols[x]]
            acc_scr[x][...] = jnp.zeros(acc_scr[x].shape, F32)
            l_scr[x][...] = jnp.zeros(l_scr[x].shape, F32)
        for x in range(ns):
            p0_scr[x][0:diag_keys[x], :] = probs(qi, x, 0, bias_scr[0], diag_keys[x])
        step(qi, 0, 1, bias_scr[1] + jnp.where(qi == 0, -jnp.inf, 0.0).astype(F32), diag_keys)

    def sweep(qi):
        n_far = qi - 1

        def far_steps(n):
            def body(u, c):
                for i in range(n):
                    step(qi, 1 + n * u + i, i % 2)
                return c
            return body

        n_octs = n_far // 8
        lax.fori_loop(0, n_octs, far_steps(8), 0)
        t_oct = 1 + 8 * n_octs

        @pl.when(n_far % 8 >= 4)
        def _():
            for i in range(4):
                step(qi, t_oct + i, i % 2)

        t_done = 1 + 4 * (n_far // 4)

        @pl.when(n_far % 4 >= 2)
        def _():
            step(qi, t_done, 0)
            step(qi, t_done + 1, 1)

        @pl.when(n_far % 2 == 1)
        def _():
            step(qi, n_far, 0)

    def finish(qi):
        t_last = jnp.maximum(qi, 1)
        last_in_p1 = t_last % 2 == 1
        for x in range(ns):
            p_last = jnp.where(last_in_p1, p1_scr[x][...], p0_scr[x][...])
            acc_scr[x][...] += _dot(vt_ref[0, 0, key_block(qi, t_last), 0:2 * hd, :], p_last)
        half = ns // 2
        inv = [1.0 / jnp.sum(l_scr[x][...], axis=0, keepdims=True) for x in range(ns)]
        outs = []
        for x in range(half):
            outs.append(acc_scr[x][...] * inv[x] - lam * (acc_scr[half + x][...] * inv[half + x]))
        ot = outs[0] if half == 1 else jnp.concatenate(outs, axis=1)
        ms = jnp.mean(ot * ot, axis=0, keepdims=True)
        y = (ot * lax.rsqrt(ms + EPS)).T * (g_ref[...] * out_scale)
        o_ref[0, pl.ds(pl.multiple_of(qi * blk, blk), blk), :] = y.astype(BF16)

    begin(0)

    def query_block(qi, c):
        finish(qi - 1)
        begin(qi)
        sweep(qi)
        return c

    lax.fori_loop(1, nq, query_block, 0)
    finish(nq - 1)


def _diff_attn(lam, qt, k, vt, bias, g, out_scale, bounded):
    b, nh, nq = qt.shape[0], qt.shape[1], qt.shape[2]
    s = k.shape[1]
    blk = ATT_BLK
    ns = ATT_STREAMS
    w = 2 * blk // ns
    hw = 2 * DIFF_HEAD_DIM
    if bounded:
        body = _diff_attn_bounded_kernel
        per_stream = (((hw, w), BF16), ((blk, w), BF16), ((blk, w), BF16), ((hw, w), F32), ((SUBLANES, w), F32))
        grid = (b, nh)
        q_spec = pl.BlockSpec((1, 1, nq, hw, blk), lambda bi, hi: (bi, hi, 0, 0, 0))
        o_spec = pl.BlockSpec((1, s, hw), lambda bi, hi: (bi, 0, hi))
        sem = ("parallel", "parallel")
    else:
        body = _diff_attn_kernel
        per_stream = (((hw, w), BF16), ((blk, w), F32), ((SUBLANES, w), F32), ((blk, w), BF16),
                      ((1, w), F32), ((1, w), F32), ((ATT_V_ROWS, w), F32))
        grid = (b, nh, nq)
        q_spec = pl.BlockSpec((1, 1, 1, hw, blk), lambda bi, hi, qi: (bi, hi, qi, 0, 0))
        o_spec = pl.BlockSpec((1, blk, hw), lambda bi, hi, qi: (bi, qi, hi))
        sem = ("parallel", "parallel", "arbitrary")
    return pl.pallas_call(
        functools.partial(body, out_scale=out_scale),
        grid=grid,
        in_specs=[
            pl.BlockSpec(memory_space=pltpu.SMEM),
            q_spec,
            pl.BlockSpec((1, s, hw), lambda bi, hi, *_: (bi, 0, hi)),
            pl.BlockSpec((1, 1, nq, ATT_V_ROWS, blk), lambda bi, hi, *_: (bi, hi, 0, 0, 0)),
            pl.BlockSpec((1, 2, 1, 2 * blk), lambda bi, hi, *_: (hi, 0, 0, 0)),
            _const_spec((1, hw)),
        ],
        out_specs=o_spec,
        out_shape=jax.ShapeDtypeStruct((b, s, nh * hw), BF16),
        scratch_shapes=[pltpu.VMEM((2, blk, blk), F32)] + [pltpu.VMEM(shape, dtype)
                                                        for shape, dtype in per_stream for _ in range(ns)],
        compiler_params=_cparams(*sem),
        name="diff_attn_bounded" if bounded else "diff_attn",
    )(lam, qt, k, vt, bias, g)


def _pre_gla_kernel(x_ref, g_ref, w_ref, gsum_ref, mg_ref, gw_ref, gb_ref, tri_ref,
                    q_ref, k_ref, v_ref, r_ref, gc_ref, mq_ref):
    kw = GLA_HEADS * GLA_KP
    vw = TOKEN_WIDTH
    h = _rms_rows(x_ref[...], g_ref[...]).astype(BF16)
    q_ref[...] = _dot(h, w_ref[:, 0:kw]).astype(BF16)
    k_ref[...] = _dot(h, w_ref[:, kw:2 * kw]).astype(BF16)
    o = 2 * kw
    v_ref[...] = _dot(h, w_ref[:, o:o + vw]).astype(BF16)
    r_ref[...] = _dot(h, w_ref[:, o + vw:o + 2 * vw]).astype(BF16)
    o = o + 2 * vw
    gate_low = _dot(h, w_ref[:, o:o + LANES]).astype(BF16)
    mq_ref[...] = _group_rms(_dot(h, w_ref[:, o + LANES:]), gsum_ref[...], mg_ref[...]).astype(BF16)
    z = _dot(gate_low, gw_ref[...]) + gb_ref[...]
    log_a = (jnp.minimum(z, 0.0) - jnp.log(1.0 + jnp.exp(-jnp.abs(z)))) * (1.0 / GLA_GATE_TAU)
    hi = log_a.astype(BF16)
    lo = (log_a - hi.astype(F32)).astype(BF16)
    tri = tri_ref[...]
    n = tri.shape[0]
    for c in range(log_a.shape[0] // n):
        rows = slice(c * n, (c + 1) * n)
        gc_ref[rows, :] = _dot(tri, hi[rows]) + _dot(tri, lo[rows])


def _pre_gla(x2, g, w, gsum, mg, gw, gb, tri):
    t = x2.shape[0]
    kw = GLA_HEADS * GLA_KP
    vw = TOKEN_WIDTH
    row = lambda n: pl.BlockSpec((ROW_TILE, n), lambda i: (i, 0))
    return pl.pallas_call(
        _pre_gla_kernel,
        grid=(t // ROW_TILE,),
        in_specs=[row(D_MODEL), _const_spec((1, D_MODEL)), _const_spec(w.shape),
                  _const_spec(gsum.shape), _const_spec((1, MEM_WIDTH)), _const_spec(gw.shape),
                  _const_spec(gb.shape), _const_spec(tri.shape)],
        out_specs=[row(kw), row(kw), row(vw), row(vw), row(kw), row(MEM_WIDTH)],
        out_shape=[jax.ShapeDtypeStruct((t, kw), BF16), jax.ShapeDtypeStruct((t, kw), BF16),
                   jax.ShapeDtypeStruct((t, vw), BF16), jax.ShapeDtypeStruct((t, vw), BF16),
                   jax.ShapeDtypeStruct((t, kw), F32), jax.ShapeDtypeStruct((t, MEM_WIDTH), BF16)],
        compiler_params=_cparams("parallel"),
        name="pre_gla",
    )(x2, g, w, gsum, mg, gw, gb, tri)


def _gla_kernel(q_ref, k_ref, v_ref, r_ref, gc_ref, gain_ref, o_ref, s_scr):
    tg = GLA_TILE
    nchunk = tg // CHUNK
    heads = range(GLA_HEADS)
    ks = [slice(h * GLA_KP, (h + 1) * GLA_KP) for h in heads]
    vs = [slice(h * GLA_VP, (h + 1) * GLA_VP) for h in heads]
    pad_lane = lax.broadcasted_iota(jnp.int32, (tg, GLA_VP), 1) >= GLA_V_DIM

    def head_cols(ref, h):
        first = h * GLA_V_DIM
        aligned = first // LANES * LANES
        win = ref[0, :, aligned:aligned + GLA_VP].astype(F32)
        if first != aligned:
            win = pltpu.roll(win, GLA_VP - (first - aligned), 1)
        return jnp.where(pad_lane, 0.0, win)

    @pl.when(pl.program_id(1) == 0)
    def _():
        s_scr[...] = jnp.zeros(s_scr.shape, F32)

    ri = lax.broadcasted_iota(jnp.int32, (tg, tg), 0)
    ci = lax.broadcasted_iota(jnp.int32, (tg, tg), 1)
    same_chunk = (ri // CHUNK) == (ci // CHUNK)
    past = ci <= ri
    row_chunk = lax.broadcasted_iota(jnp.int32, (tg, GLA_KP), 0) // CHUNK

    qe, scores, kv, decay, v_pad = [], [], [], [], []
    for h in heads:
        qh = q_ref[0, :, ks[h]].astype(F32) * (GLA_K_DIM ** -0.5)
        kh = k_ref[0, :, ks[h]].astype(F32)
        g = gc_ref[0, :, ks[h]]
        eg = jnp.exp(g)
        ieg = jnp.exp(-g)
        qe.append((qh * eg).astype(BF16))
        a_past = _dot_nt(qe[h], (kh * ieg).astype(BF16))
        a_fut = _dot_nt((qh * ieg).astype(BF16), (kh * eg).astype(BF16))
        scores.append(jnp.where(same_chunk, jnp.where(past, a_past, a_fut), 0.0).astype(BF16))
        v_pad.append(head_cols(v_ref, h))
        vt = v_pad[h].T.astype(BF16)
        g_last = [g[c * CHUNK + CHUNK - 1:c * CHUNK + CHUNK, :] for c in range(nchunk)]
        g_end = jnp.concatenate([jnp.broadcast_to(gl, (CHUNK, GLA_KP)) for gl in g_last], axis=0)
        kdec = kh * jnp.exp(g_end - g)
        kd_chunks = jnp.concatenate([jnp.where(row_chunk == c, kdec, 0.0) for c in range(nchunk)], axis=1)
        kv_all = _dot(vt, kd_chunks.astype(BF16))
        kv.append([kv_all[:, c * GLA_KP:(c + 1) * GLA_KP] for c in range(nchunk)])
        decay.append([jnp.exp(gl) for gl in g_last])

    starts = []
    for h in heads:
        st = s_scr[h]
        per_chunk = []
        for c in range(nchunk):
            per_chunk.append(st.astype(BF16))
            st = st * decay[h][c] + kv[h][c]
        s_scr[h] = st
        starts.append(per_chunk)

    gated = []
    for h in heads:
        inter = [_dot_nt(qe[h][c * CHUNK:(c + 1) * CHUNK], starts[h][c]) for c in range(nchunk)]
        o = _dot(scores[h], v_pad[h].astype(BF16)) + jnp.concatenate(inter, axis=0)
        ms = jnp.sum(o * o, axis=-1, keepdims=True) * (1.0 / GLA_V_DIM)
        y = o * lax.rsqrt(ms + EPS) * gain_ref[:, vs[h]]
        half_r = 0.5 * head_cols(r_ref, h)
        gated.append(y * half_r * (1.0 + jnp.tanh(half_r)))

    pair = 2 * GLA_V_DIM
    blank = jnp.zeros((tg, GLA_VP), F32)
    for p in range(GLA_HEADS // 2):
        even = jnp.concatenate([gated[2 * p], blank], axis=1)
        odd = pltpu.roll(jnp.concatenate([gated[2 * p + 1], blank], axis=1), GLA_V_DIM, 1)
        o_ref[0, :, p * pair:(p + 1) * pair] = (even + odd)[:, :pair].astype(BF16)


def _gla(q, k, v, r, gc, gain):
    b, s = q.shape[0], q.shape[1]
    kw = GLA_HEADS * GLA_KP
    vw = GLA_HEADS * GLA_VP
    spec = lambda n: pl.BlockSpec((1, GLA_TILE, n), lambda bi, i: (bi, i, 0))
    return pl.pallas_call(
        _gla_kernel,
        grid=(b, s // GLA_TILE),
        in_specs=[spec(kw), spec(kw), spec(TOKEN_WIDTH), spec(TOKEN_WIDTH), spec(kw), _const_spec((1, vw))],
        out_specs=spec(TOKEN_WIDTH),
        out_shape=jax.ShapeDtypeStruct((b, s, TOKEN_WIDTH), BF16),
        scratch_shapes=[pltpu.VMEM((GLA_HEADS, GLA_VP, GLA_KP), F32)],
        compiler_params=_cparams("parallel", "arbitrary"),
        name="gla",
    )(q, k, v, r, gc, gain)


def _mem_kv_kernel(mem_ref, g_ref, w_ref, gsum_ref, kg_ref, k_ref, v_ref):
    h = _rms_rows(mem_ref[...], g_ref[...]).astype(BF16)
    k_ref[...] = _group_rms(_dot(h, w_ref[:, :MEM_WIDTH]), gsum_ref[...], kg_ref[...]).astype(BF16)
    v_ref[...] = _dot(h, w_ref[:, MEM_WIDTH:]).astype(BF16)


def _mem_kv(mem2, g, w_all, layer, gsum, kg):
    n = mem2.shape[0]
    return pl.pallas_call(
        _mem_kv_kernel,
        grid=(1,),
        in_specs=[_const_spec(mem2.shape), _const_spec((1, D_MODEL)), _layer_spec(w_all, layer),
                  _const_spec(gsum.shape), _const_spec((1, MEM_WIDTH))],
        out_specs=[_const_spec((n, MEM_WIDTH)), _const_spec((n, MEM_WIDTH))],
        out_shape=[jax.ShapeDtypeStruct((n, MEM_WIDTH), BF16)] * 2,
        compiler_params=_cparams("arbitrary"),
        name="mem_kv",
    )(mem2, g, w_all, gsum, kg)


def _mix_out_kernel(x_ref, mix_ref, mq_ref, kbd_ref, vbd_ref, wa_ref, wb_ref, o_ref):
    m = kbd_ref.shape[2] // MEM_HEADS
    logits = _dot(mq_ref[0], kbd_ref[0])
    ps = []
    for h in range(MEM_HEADS):
        s = logits[:, h * m:(h + 1) * m]
        e = jnp.exp(s - jnp.max(s, axis=-1, keepdims=True))
        ps.append((e * (1.0 / jnp.sum(e, axis=-1, keepdims=True))).astype(BF16))
    cross = _dot(jnp.concatenate(ps, axis=1), vbd_ref[0])
    o_ref[0] = x_ref[0] + _dot(mix_ref[0], wa_ref[...]) + _dot(cross.astype(BF16), wb_ref[...])


def _mix_out(x, mix, mq, kbd, vbd, wa, wa_spec, wb, wb_spec):
    b, s = x.shape[0], x.shape[1]
    spec = lambda n: pl.BlockSpec((1, MIX_TILE, n), lambda bi, i: (bi, i, 0))
    per_b = lambda a: pl.BlockSpec((1,) + a.shape[1:], lambda bi, i: (bi, 0, 0))
    return pl.pallas_call(
        _mix_out_kernel,
        grid=(b, s // MIX_TILE),
        in_specs=[spec(D_MODEL), spec(mix.shape[2]), spec(MEM_WIDTH), per_b(kbd), per_b(vbd),
                  wa_spec, wb_spec],
        out_specs=spec(D_MODEL),
        out_shape=jax.ShapeDtypeStruct(x.shape, F32),
        compiler_params=_cparams("parallel", "parallel"),
        name="mix_out",
    )(x, mix, mq, kbd, vbd, wa, wb)


def _ffn_kernel(x_ref, g_ref, wu_ref, cw_ref, cb_ref, wd_ref, o_ref, carry_scr, act_scr, *shift_scr):
    tm = FFN_TILE
    cr = CARRY_ROWS

    @pl.when(pl.program_id(1) == 0)
    def _():
        carry_scr[...] = jnp.zeros(carry_scr.shape, F32)

    x = x_ref[0]
    h = _rms_rows(x, g_ref[...]).astype(BF16)

    def conv(cols, bufs):
        u = _dot(h, wu_ref[:, cols])
        prev = carry_scr[:, cols]
        for shift, buf in zip((1, 2), bufs):
            buf[shift:shift + cr, :] = prev
            buf[cr + shift:cr + shift + tm, :] = u
        carry_scr[:, cols] = u[tm - cr:tm]
        return (cw_ref[0:1, cols] * bufs[1][cr:cr + tm, :] + cw_ref[1:2, cols] * bufs[0][cr:cr + tm, :]
                + cw_ref[2:3, cols] * u + cb_ref[:, cols])

    for j in range(D_FF // FFN_COLS):
        bufs = shift_scr[4 * (j % 2):4 * (j % 2) + 4]
        a = conv(slice(j * FFN_COLS, (j + 1) * FFN_COLS), bufs[0:2])
        half_g = 0.5 * conv(slice(D_FF + j * FFN_COLS, D_FF + (j + 1) * FFN_COLS), bufs[2:4])
        act_scr[:, j * FFN_COLS:(j + 1) * FFN_COLS] = (a * half_g * (1.0 + jnp.tanh(half_g))).astype(BF16)

    o_ref[0] = x + _dot(act_scr[...], wd_ref[...])


def _ffn(x, g, wu_all, cw, cb, wd_all, layer):
    b, s = x.shape[0], x.shape[1]
    spec = pl.BlockSpec((1, FFN_TILE, D_MODEL), lambda bi, i: (bi, i, 0))
    return pl.pallas_call(
        _ffn_kernel,
        grid=(b, s // FFN_TILE),
        in_specs=[spec, _const_spec((1, D_MODEL)), _layer_spec(wu_all, layer, single=True), _const_spec(cw.shape),
                  _const_spec(cb.shape), _layer_spec(wd_all, layer, single=True)],
        out_specs=spec,
        out_shape=jax.ShapeDtypeStruct(x.shape, F32),
        scratch_shapes=[pltpu.VMEM((CARRY_ROWS, 2 * D_FF), F32), pltpu.VMEM((FFN_TILE, D_FF), BF16)]
        + [pltpu.VMEM((FFN_TILE + 2 * CARRY_ROWS, FFN_COLS), F32)] * 8,
        compiler_params=_cparams("parallel", "arbitrary"),
        name="ffn",
    )(x, g, wu_all, cw, cb, wd_all)


def _t5_bucket(rel):
    half = REL_BUCKETS // 2
    max_exact = half // 2
    ret = jnp.where(rel > 0, half, 0)
    n = jnp.abs(rel)
    nf = jnp.maximum(n, 1).astype(jnp.float32)
    large = max_exact + (jnp.log(nf / max_exact) / math.log(REL_MAX_DIST / max_exact)
                         * (half - max_exact)).astype(jnp.int32)
    large = jnp.minimum(large, half - 1)
    return ret + jnp.where(n < max_exact, n, large)


def _bias_vectors(rel_bias):
    blk = ATT_BLK
    assert blk >= REL_MAX_DIST
    table = rel_bias.astype(F32).T[:, :, None]

    def lookup(rel):
        bucket = _t5_bucket(rel)
        out = jnp.zeros((table.shape[0],) + rel.shape, F32)
        for i in range(REL_BUCKETS):
            out = jnp.where(bucket == i, table[:, i], out)
        return out

    j = jnp.arange(2 * blk)
    dist = jnp.where(j < blk, -j, 2 * blk - j)
    values = lookup(jnp.concatenate([dist, dist - blk, jnp.full((1,), -2 * blk)]))
    vectors = (values[:, :4 * blk] - values[:, 4 * blk:]) * LOG2E
    return vectors.reshape(-1, 2, 1, 2 * blk)


def _group_sum_matrix():
    i = jnp.arange(MXU_DIM)
    return ((i[:, None] // NORM_GROUP) == (i[None, :] // NORM_GROUP)).astype(BF16)


def _chunk_tri_matrix(n):
    i = jnp.arange(n)
    return (((i[:, None] // CHUNK) == (i[None, :] // CHUNK)) & (i[None, :] <= i[:, None])).astype(BF16)


def _pad_heads(w, heads, dim, pad, axis):
    shape = list(w.shape)
    shape[axis:axis + 1] = [heads, dim]
    w = w.reshape(shape)
    widths = [(0, 0)] * w.ndim
    widths[axis + 1] = (0, pad - dim)
    w = jnp.pad(w, widths)
    shape[axis:axis + 2] = [heads * pad]
    return w.reshape(shape)


def _tile_gain(g, reps, scale=1.0):
    return (jnp.tile(g.astype(F32), reps) * scale)[None, :]


def _mem_block_diag(kn, v, b):
    m = kn.shape[0] // b
    eye = jnp.eye(MEM_HEADS, dtype=BF16)
    knt = kn.reshape(b, m, MEM_WIDTH).transpose(0, 2, 1)
    kbd = (knt.reshape(b, MEM_HEADS, MEM_HEAD_DIM, 1, m) * eye.reshape(1, MEM_HEADS, 1, MEM_HEADS, 1))
    kbd = kbd.reshape(b, MEM_WIDTH, MEM_HEADS * m)
    vbd = (v.reshape(b, 1, m, MEM_HEADS, MEM_HEAD_DIM) * eye.reshape(1, MEM_HEADS, 1, MEM_HEADS, 1))
    vbd = vbd.reshape(b, MEM_HEADS * m, MEM_WIDTH)
    return kbd, vbd


def kernel(x, mem, rel_bias, attn_norm, ffn_norm, mem_norm, w_in_diff, diff_qk_norm, diff_lambda,
           diff_out_norm, w_in_gla, gla_gate_w, gla_gate_b, gla_out_norm, w_mem_kv, mem_qk_norm,
           w_out, w_up, conv_w, conv_b, w_down):
    b, s, d = x.shape
    t = b * s
    tw = TOKEN_WIDTH
    gsum = _group_sum_matrix()
    mem2 = mem.reshape(b * mem.shape[1], d)
    x = x.astype(F32)
    w_in_diff, w_mem_kv, w_out, w_up, w_down = map(_to_bf16, (w_in_diff, w_mem_kv, w_out, w_up, w_down))

    for i in range(DEPTH):
        j = i // 2
        x2 = x.reshape(t, d)
        mq_gain = _tile_gain(mem_qk_norm[i, 0], MEM_HEADS, MEM_HEAD_DIM ** -0.5)
        if i % 2 == 0:
            nh, hd = DIFF_HEADS, DIFF_HEAD_DIM
            qt, k, vt, mq = _pre_diff(
                x2, attn_norm[i][None, :], w_in_diff, j, gsum,
                _tile_gain(diff_qk_norm[j, 0], 2 * nh, hd ** -0.5 * LOG2E),
                _tile_gain(diff_qk_norm[j, 1], 2 * nh), mq_gain, b)
            lv = diff_lambda[j].astype(F32)
            lam_init = 0.8 - 0.6 * math.exp(-0.3 * i)
            lam = jnp.exp(jnp.sum(lv[0] * lv[1])) - jnp.exp(jnp.sum(lv[2] * lv[3])) + lam_init
            bvec = _bias_vectors(rel_bias)
            qk_bound = (hd ** 0.5 * LOG2E * ATT_ROUNDING_SLACK
                        * jnp.max(jnp.abs(diff_qk_norm[j, 0])) * jnp.max(jnp.abs(diff_qk_norm[j, 1]))).astype(F32)
            hi = qk_bound + jnp.maximum(jnp.max(bvec), 0.0)
            lo = -qk_bound + jnp.minimum(jnp.min(bvec), 0.0)
            scalars = jnp.stack([lam.astype(F32), hi]).reshape(1, 2)
            attend = lambda bounded: functools.partial(
                _diff_attn, qt=qt, k=k.reshape(b, s, tw), vt=vt, bias=bvec,
                g=diff_out_norm[j].astype(F32)[None, :], out_scale=1.0 - lam_init, bounded=bounded)
            mix = lax.cond(hi - lo <= ATT_MAX_EXP2_SPAN, attend(True), attend(False), scalars)
        else:
            kw = GLA_HEADS * GLA_K_DIM
            w = w_in_gla[j]
            hp = functools.partial(_pad_heads, heads=GLA_HEADS, axis=1)
            w_p = jnp.concatenate([
                hp(w[:, :kw], dim=GLA_K_DIM, pad=GLA_KP),
                hp(w[:, kw:2 * kw], dim=GLA_K_DIM, pad=GLA_KP),
                w[:, 2 * kw:2 * kw + 2 * tw],
                jnp.pad(w[:, 2 * kw + 2 * tw:2 * kw + 2 * tw + GLA_GATE_RANK],
                        ((0, 0), (0, LANES - GLA_GATE_RANK))),
                w[:, 2 * kw + 2 * tw + GLA_GATE_RANK:]], axis=1).astype(BF16)
            gw = jnp.pad(hp(gla_gate_w[j], dim=GLA_K_DIM, pad=GLA_KP),
                         ((0, LANES - GLA_GATE_RANK), (0, 0))).astype(BF16)
            gb = _pad_heads(gla_gate_b[j].astype(F32)[None, :], GLA_HEADS, GLA_K_DIM, GLA_KP, 1)
            q, k, v, r, gc, mq = _pre_gla(x2, attn_norm[i][None, :], w_p, gsum, mq_gain, gw, gb,
                                          _chunk_tri_matrix(MXU_DIM))
            gain = _pad_heads(jnp.tile(gla_out_norm[j].astype(F32), GLA_HEADS)[None, :],
                              GLA_HEADS, GLA_V_DIM, GLA_VP, 1)
            sh = lambda a: a.reshape(b, s, a.shape[1])
            mix = _gla(sh(q), sh(k), sh(v), sh(r), sh(gc), gain)

        kn, vm = _mem_kv(mem2, mem_norm[i][None, :], w_mem_kv, i, gsum,
                         _tile_gain(mem_qk_norm[i, 1], MEM_HEADS))
        kbd, vbd = _mem_block_diag(kn, vm, b)
        x = _mix_out(x, mix.reshape(b, s, -1), mq.reshape(b, s, MEM_WIDTH), kbd, vbd,
                     w_out, _layer_spec(w_out, i, rows=(0, tw)), w_out, _layer_spec(w_out, i, rows=(tw, MEM_WIDTH)))
        x = _ffn(x, ffn_norm[i][None, :], w_up, conv_w[i].astype(F32), conv_b[i].astype(F32)[None, :], w_down, i)
    return x
```
